```python
import math
import jax
import jax.numpy as jnp
from jax import lax
import numpy as np

D_MODEL = 1024
BATCH = 8
SEQ = 2048
DEPTH = 1
DEC_BATCH = 128
DEC_SEQ = 1
PAST_LEN = 16384
PAGE_SIZE = 128

D_POOL = D_MODEL // 2
D_MLSTM = D_MODEL - D_POOL
POOL_WINDOWS = (2, 4, 8, 16)
POOL_GROUPS = len(POOL_WINDOWS)
POOL_GC = D_POOL // POOL_GROUPS
POOL_BUF = max(POOL_WINDOWS) - 1
NH = 4
DH = D_MLSTM // NH
MLSTM_CHUNK = 64
D_FF = 4 * D_MODEL
IN_DIM = D_POOL + 4 * D_MLSTM + 2 * NH
ALPHA = (2.0 * DEPTH) ** 0.25
BETA = (8.0 * DEPTH) ** -0.25
LN_EPS = 1e-5

kernel_name = "hymba_pool_mlstm_deepnorm_step"


def _layer_norm(x, g, b):
    xf = x.astype(jnp.float32)
    mu = jnp.mean(xf, axis=-1, keepdims=True)
    xc = xf - mu
    var = jnp.mean(xc * xc, axis=-1, keepdims=True)
    return (xc * lax.rsqrt(var + LN_EPS) * g.astype(jnp.float32) + b.astype(jnp.float32)).astype(x.dtype)


def _pool_mixer(u, buf, pos0, w_pool, pool_scale):
    bsz, t_len, _ = u.shape
    full_in = jnp.concatenate([buf.astype(u.dtype), u], axis=1)
    full = full_in.astype(jnp.float32)
    cs = jnp.cumsum(full, axis=1)
    cs = jnp.concatenate([jnp.zeros_like(cs[:, :1]), cs], axis=1)
    end = cs[:, POOL_BUF + 1:]
    pos = pos0 + jnp.arange(t_len)
    means = []
    for g, w in enumerate(POOL_WINDOWS):
        lo, hi = g * POOL_GC, (g + 1) * POOL_GC
        s0 = POOL_BUF + 1 - w
        start = cs[:, s0:s0 + t_len, lo:hi]
        cnt = jnp.minimum(pos + 1, w).astype(jnp.float32)[None, :, None]
        means.append((end[..., lo:hi] - start) / cnt)
    pooled = jnp.concatenate(means, axis=-1) - full[:, POOL_BUF:]
    mixed = jnp.einsum('btgc,gcd->btgd', pooled.reshape(bsz, t_len, POOL_GROUPS, POOL_GC),
                       w_pool.astype(jnp.float32)).reshape(bsz, t_len, D_POOL)
    mixed = mixed * pool_scale.astype(jnp.float32)
    new_buf = full_in[:, -POOL_BUF:]
    return mixed, new_buf


def _mlstm(q, k, v, ig, fg, C0, n0, m0):
    bsz, t_len = q.shape[0], q.shape[1]
    L = math.gcd(t_len, MLSTM_CHUNK)
    nc = t_len // L
    f32 = jnp.float32
    qf = q.astype(f32)
    kf = k.astype(f32) * (DH ** -0.5)
    vf = v.astype(f32)
    logf = jax.nn.log_sigmoid(fg.astype(f32))
    igf = ig.astype(f32)

    def chunks(a):
        return a.reshape(bsz, nc, L, NH, DH).transpose(1, 0, 3, 2, 4)

    def gchunks(a):
        return a.reshape(bsz, nc, L, NH).transpose(1, 0, 3, 2)

    causal = jnp.tril(jnp.ones((L, L), dtype=bool))

    def step(carry, xs):
        C, n, m = carry
        qc, kc, vc, ic, lfc = xs
        b = jnp.cumsum(lfc, axis=-1)
        D = b[..., :, None] - b[..., None, :] + ic[..., None, :]
        D = jnp.where(causal, D, -jnp.inf)
        inter = b + m[..., None]
        m_t = jnp.maximum(inter, jnp.max(D, axis=-1))
        Dw = jnp.exp(D - m_t[..., None])
        iw = jnp.exp(inter - m_t)
        s = jnp.einsum('bhtd,bhsd->bhts', qc, kc) * Dw
        num = jnp.einsum('bhts,bhse->bhte', s, vc) + iw[..., None] * jnp.einsum('bhtd,bhde->bhte', qc, C)
        den = jnp.sum(s, axis=-1) + iw * jnp.einsum('bhtd,bhd->bht', qc, n)
        h = num / jnp.maximum(jnp.abs(den), jnp.exp(-m_t))[..., None]
        m_new = m_t[..., -1]
        g_state = jnp.exp(b[..., -1] + m - m_new)
        w_s = jnp.exp(b[..., -1:] - b + ic - m_new[..., None])
        C_new = g_state[..., None, None] * C + jnp.einsum('bhs,bhsd,bhse->bhde', w_s, kc, vc)
        n_new = g_state[..., None] * n + jnp.einsum('bhs,bhsd->bhd', w_s, kc)
        return (C_new, n_new, m_new), h

    carry0 = (C0.astype(f32), n0.astype(f32), m0.astype(f32))
    (C1, n1, m1), hs = lax.scan(step, carry0, (chunks(qf), chunks(kf), chunks(vf), gchunks(igf), gchunks(logf)))
    h = hs.transpose(1, 0, 3, 2, 4).reshape(bsz, t_len, NH * DH)
    return h, C1, n1, m1


def _layer(x, pool_buf, C, n, m, pos0, w_in, b_gate, w_pool, pool_scale, w_out,
           ln1_g, ln1_b, w_ff1, b_ff1, w_ff2, b_ff2, ln2_g, ln2_b):
    bsz, t_len, _ = x.shape
    proj = x @ w_in
    cuts = [D_POOL, D_POOL + D_MLSTM, D_POOL + 2 * D_MLSTM, D_POOL + 3 * D_MLSTM, D_POOL + 4 * D_MLSTM]
    u_pool, q, k, v, o_pre, gates = jnp.split(proj, cuts, axis=-1)
    gates = gates.astype(jnp.float32) + b_gate.astype(jnp.float32)
    ig, fg = gates[..., :NH], gates[..., NH:]
    y_pool, new_buf = _pool_mixer(u_pool, pool_buf, pos0, w_pool, pool_scale)
    hd = (bsz, t_len, NH, DH)
    h, C1, n1, m1 = _mlstm(q.reshape(hd), k.reshape(hd), v.reshape(hd), ig, fg, C, n, m)
    y_ml = jax.nn.sigmoid(o_pre.astype(jnp.float32)) * h
    mix = jnp.concatenate([y_pool, y_ml], axis=-1).astype(x.dtype) @ w_out
    x1 = _layer_norm(ALPHA * x + mix, ln1_g, ln1_b)
    ff = jnp.square(jax.nn.relu(x1 @ w_ff1 + b_ff1)) @ w_ff2 + b_ff2
    x2 = _layer_norm(ALPHA * x1 + ff, ln2_g, ln2_b)
    dt = x.dtype
    return x2, new_buf.astype(dt), C1.astype(dt), n1.astype(dt), m1.astype(dt)


def setup_inputs(seed: int = 0) -> dict:
    key = jax.random.key(seed)
    ks = jax.random.split(key, 24)
    nrm = jax.random.normal
    f32 = jnp.float32
    x_prompt = nrm(ks[0], (BATCH, SEQ, D_MODEL), f32)
    x_sample = nrm(ks[1], (DEC_BATCH, DEC_SEQ, D_MODEL), f32)
    state_pool = nrm(ks[2], (DEPTH, DEC_BATCH, POOL_BUF, D_POOL), f32)
    state_C = 0.5 * nrm(ks[3], (DEPTH, DEC_BATCH, NH, DH, DH), f32)
    state_n = 0.5 * nrm(ks[4], (DEPTH, DEC_BATCH, NH, DH), f32)
    state_m = nrm(ks[5], (DEPTH, DEC_BATCH, NH), f32)
    w_in = nrm(ks[6], (DEPTH, D_MODEL, IN_DIM), f32) * D_MODEL ** -0.5
    i_bias = 0.1 * nrm(ks[7], (DEPTH, NH), f32)
    f_bias = jnp.broadcast_to(jnp.linspace(3.0, 6.0, NH, dtype=f32), (DEPTH, NH)) + 0.1 * nrm(ks[8], (DEPTH, NH), f32)
    b_gate = jnp.concatenate([i_bias, f_bias], axis=-1)
    w_pool = nrm(ks[9], (DEPTH, POOL_GROUPS, POOL_GC, POOL_GC), f32) * POOL_GC ** -0.5
    pool_scale = 1.0 + 0.02 * nrm(ks[10], (DEPTH, D_POOL), f32)
    w_out = nrm(ks[11], (DEPTH, D_MODEL, D_MODEL), f32) * (D_MODEL ** -0.5) * BETA
    ln1_g = 1.0 + 0.02 * nrm(ks[12], (DEPTH, D_MODEL), f32)
    ln1_b = 0.02 * nrm(ks[13], (DEPTH, D_MODEL), f32)
    w_ff1 = nrm(ks[14], (DEPTH, D_MODEL, D_FF), f32) * D_MODEL ** -0.5
    b_ff1 = 0.02 * nrm(ks[15], (DEPTH, D_FF), f32)
    w_ff2 = nrm(ks[16], (DEPTH, D_FF, D_MODEL), f32) * (D_FF ** -0.5) * BETA
    b_ff2 = 0.02 * nrm(ks[17], (DEPTH, D_MODEL), f32)
    ln2_g = 1.0 + 0.02 * nrm(ks[18], (DEPTH, D_MODEL), f32)
    ln2_b = 0.02 * nrm(ks[19], (DEPTH, D_MODEL), f32)
    return {"x_prompt": x_prompt, "x_sample": x_sample,
            "state_pool": state_pool, "state_C": state_C, "state_n": state_n, "state_m": state_m,
            "w_in": w_in, "b_gate": b_gate, "w_pool": w_pool, "pool_scale": pool_scale,
            "w_out": w_out, "ln1_g": ln1_g, "ln1_b": ln1_b, "w_ff1": w_ff1, "b_ff1": b_ff1,
            "w_ff2": w_ff2, "b_ff2": b_ff2, "ln2_g": ln2_g, "ln2_b": ln2_b}


def reference(x_prompt, x_sample, state_pool, state_C, state_n, state_m,
              w_in, b_gate, w_pool, pool_scale, w_out, ln1_g, ln1_b,
              w_ff1, b_ff1, w_ff2, b_ff2, ln2_g, ln2_b):
    f32 = jnp.float32
    bp = x_prompt.shape[0]
    yp, ys = x_prompt, x_sample
    pp, Cp, np_, mp = [], [], [], []
    ps, Cs, ns, ms = [], [], [], []
    for l in range(DEPTH):
        wts = (w_in[l], b_gate[l], w_pool[l], pool_scale[l], w_out[l], ln1_g[l], ln1_b[l],
               w_ff1[l], b_ff1[l], w_ff2[l], b_ff2[l], ln2_g[l], ln2_b[l])
        buf0 = jnp.zeros((bp, POOL_BUF, D_POOL), yp.dtype)
        C0 = jnp.zeros((bp, NH, DH, DH), f32)
        n0 = jnp.zeros((bp, NH, DH), f32)
        m0 = jnp.zeros((bp, NH), f32)
        yp, b1, c1, n1, m1 = _layer(yp, buf0, C0, n0, m0, 0, *wts)
        pp.append(b1); Cp.append(c1); np_.append(n1); mp.append(m1)
        ys, b2, c2, n2, m2 = _layer(ys, state_pool[l], state_C[l], state_n[l], state_m[l], PAST_LEN, *wts)
        ps.append(b2); Cs.append(c2); ns.append(n2); ms.append(m2)
    return (yp, ys,
            jnp.stack(pp), jnp.stack(Cp), jnp.stack(np_), jnp.stack(mp),
            jnp.stack(ps), jnp.stack(Cs), jnp.stack(ns), jnp.stack(ms))
```

```python
import functools

import jax
import jax.numpy as jnp
from jax import lax
from jax.experimental import pallas as pl
from jax.experimental.pallas import tpu as pltpu

F32 = jnp.float32
BF16 = jnp.bfloat16

D_MODEL = 1024
D_POOL = D_MODEL // 2
D_MLSTM = D_MODEL - D_POOL
POOL_WINDOWS = (2, 4, 8, 16)
POOL_GC = D_POOL // len(POOL_WINDOWS)
POOL_BUF = max(POOL_WINDOWS) - 1
POOL_PAD = POOL_BUF + 1
NH = 4
DH = D_MLSTM // NH
D_FF = 4 * D_MODEL
DEPTH = 1
PAST_LEN = 16384
ALPHA = (2.0 * DEPTH) ** 0.25
LN_EPS = 1e-5
K_SCALE = DH ** -0.5

MLSTM_CHUNK = 128
MIX_TILE = 512
FFN_TILE = 512
FF_CHUNK = 1024
SAMPLE_BLOCK = 8
GATE_ROWS = 16
V7X_VMEM_LIMIT = 56 * 1024 * 1024


def _layer_norm(y, g, b):
    mu = jnp.mean(y, axis=-1, keepdims=True)
    yc = y - mu
    var = jnp.mean(yc * yc, axis=-1, keepdims=True)
    return yc * lax.rsqrt(var + LN_EPS) * g + b


def _dot(a, b):
    return jnp.dot(a, b, preferred_element_type=F32)


def _twice(row):
    return jnp.concatenate([row, row], axis=-1)


def _mixer_kernel(x_ref, wm_ref, wkg_ref, bg_ref, wp_ref, ps_ref, wo_ref, g1_ref, b1_ref,
                  x1_ref, pool_ref, c_ref, n_ref, m_ref,
                  pm_ref, pt_ref, ext_ref, mix_ref, caug_ref, mst_ref):
    tq = MIX_TILE
    t_idx = pl.program_id(1)
    last_t = pl.num_programs(1) - 1

    @pl.when(t_idx == 0)
    def _init():
        ext_ref[0:POOL_PAD, :] = jnp.zeros((POOL_PAD, D_POOL), F32)
        caug_ref[...] = jnp.zeros(caug_ref.shape, F32)
        mst_ref[...] = jnp.zeros(mst_ref.shape, F32)

    xb = x_ref[0].astype(BF16)
    pm_ref[...] = _dot(xb, wm_ref[...])
    pt_ref[...] = lax.dot_general(wkg_ref[...], xb, (((1,), (1,)), ((), ())),
                                  preferred_element_type=F32)

    ext_ref[POOL_PAD:POOL_PAD + tq, :] = pm_ref[:, 0:D_POOL]
    pos = t_idx * tq + lax.broadcasted_iota(jnp.int32, (tq, POOL_GC), 0)
    for g, w in enumerate(POOL_WINDOWS):
        lo = g * POOL_GC
        s = ext_ref[:, lo:lo + POOL_GC]
        k = 1
        while k < w:
            s = s + pltpu.roll(s, k, axis=0)
            k *= 2
        cnt = jnp.minimum(pos + 1, w).astype(F32)
        pooled = s[POOL_PAD:, :] / cnt - ext_ref[POOL_PAD:POOL_PAD + tq, lo:lo + POOL_GC]
        mixed = _dot(pooled.astype(BF16), wp_ref[g]) * ps_ref[:, lo:lo + POOL_GC]
        mix_ref[:, lo:lo + POOL_GC] = mixed.astype(BF16)

    @pl.when(t_idx == last_t)
    def _pool_state():
        pool_ref[0, 0] = ext_ref[tq + 1:tq + POOL_PAD, :]

    ext_ref[0:POOL_PAD, :] = ext_ref[tq:tq + POOL_PAD, :]

    L = MLSTM_CHUNK
    gates = pt_ref[0:2 * NH, :] + bg_ref[...]
    ig_all = gates[0:NH]
    lf_all = jax.nn.log_sigmoid(gates[NH:2 * NH])
    lane = lax.broadcasted_iota(jnp.int32, (NH, L), 1)
    tt = lax.broadcasted_iota(jnp.int32, (L, L), 0)
    ss = lax.broadcasted_iota(jnp.int32, (L, L), 1)
    causal = ss <= tt
    diag = ss == tt
    ones_blk = jnp.ones((L, DH), BF16)
    m_prev = mst_ref[...]
    for c in range(tq // L):
        rows = slice(c * L, (c + 1) * L)
        lf = lf_all[:, rows]
        b = lf
        k = 1
        while k < L:
            b = b + jnp.where(lane >= k, pltpu.roll(b, k, axis=1), 0.0)
            k *= 2
        a = ig_all[:, rows] - b
        amax = jnp.max(a, axis=-1, keepdims=True)
        b_last = jnp.sum(lf, axis=-1, keepdims=True)
        mm = jnp.maximum(m_prev, amax)
        g_state = jnp.exp(m_prev - mm)
        f_state = jnp.exp(amax - mm)
        w_loc = jnp.exp(a - amax)
        for h in range(NH):
            col = slice(h * DH, (h + 1) * DH)
            a_row = a[h:h + 1]
            mp = m_prev[h:h + 1]
            dm = jnp.where(causal, a_row, -jnp.inf)
            big_m = jnp.maximum(jnp.max(dm, axis=-1, keepdims=True), mp)
            p = jnp.exp(dm - big_m)
            iw = jnp.exp(mp - big_m)
            b_col = jnp.sum(jnp.where(diag, b[h:h + 1], 0.0), axis=-1, keepdims=True)
            floor = jnp.exp(-(b_col + big_m))
            qb = pm_ref[rows, D_POOL + h * DH:D_POOL + (h + 1) * DH].astype(BF16)
            v = pm_ref[rows, D_POOL + D_MLSTM + h * DH:D_POOL + D_MLSTM + (h + 1) * DH]
            o = pm_ref[rows, D_POOL + 2 * D_MLSTM + h * DH:D_POOL + 2 * D_MLSTM + (h + 1) * DH]
            kt = pt_ref[GATE_ROWS + h * DH:GATE_ROWS + (h + 1) * DH, rows] * K_SCALE
            v_aug = jnp.concatenate([v.astype(BF16), ones_blk], axis=1)
            caug = caug_ref[h]
            s_loc = (_dot(qb, kt.astype(BF16)) * p).astype(BF16)
            comb = _dot(s_loc, v_aug) + _twice(iw) * _dot(qb, caug.astype(BF16))
            hh = comb[:, :DH] / jnp.maximum(jnp.abs(comb[:, DH:]), floor)
            mix_ref[rows, D_POOL + h * DH:D_POOL + (h + 1) * DH] = (
                jax.nn.sigmoid(o) * hh).astype(BF16)
            kv = _dot((kt * w_loc[h:h + 1]).astype(BF16), v_aug)
            caug_ref[h] = _twice(g_state[h:h + 1]) * caug + _twice(f_state[h:h + 1]) * kv
        m_prev = b_last + mm
    mst_ref[...] = m_prev

    @pl.when(t_idx == last_t)
    def _mlstm_state():
        for h in range(NH):
            caug = caug_ref[h]
            c_ref[0, 0, h] = caug[:, :DH]
            n_ref[0, 0, h:h + 1, :] = jnp.sum(jnp.where(diag, caug[:, DH:], 0.0),
                                              axis=0, keepdims=True)
        m_ref[0] = mst_ref[...]

    mix = _dot(mix_ref[...], wo_ref[...])
    x1_ref[0] = _layer_norm(ALPHA * x_ref[0] + mix, g1_ref[...], b1_ref[...])


def _prompt_mixer(x, w_main, w_kgt, bg_col, w_pool, pool_scale, w_out, ln_g, ln_b):
    bsz, t_len, _ = x.shape
    tq = MIX_TILE
    assert t_len % tq == 0 and tq % MLSTM_CHUNK == 0 and tq >= POOL_PAD
    const2 = lambda b, t: (0, 0)
    return pl.pallas_call(
        _mixer_kernel,
        name="prompt_mixer",
        grid=(bsz, t_len // tq),
        in_specs=[
            pl.BlockSpec((1, tq, D_MODEL), lambda b, t: (b, t, 0)),
            pl.BlockSpec(w_main.shape, const2),
            pl.BlockSpec(w_kgt.shape, const2),
            pl.BlockSpec(bg_col.shape, const2),
            pl.BlockSpec(w_pool.shape, lambda b, t: (0, 0, 0)),
            pl.BlockSpec(pool_scale.shape, const2),
            pl.BlockSpec(w_out.shape, const2),
            pl.BlockSpec(ln_g.shape, const2),
            pl.BlockSpec(ln_b.shape, const2),
        ],
        out_specs=[
            pl.BlockSpec((1, tq, D_MODEL), lambda b, t: (b, t, 0)),
            pl.BlockSpec((1, 1, POOL_BUF, D_POOL), lambda b, t: (0, b, 0, 0)),
            pl.BlockSpec((1, 1, NH, DH, DH), lambda b, t: (0, b, 0, 0, 0)),
            pl.BlockSpec((1, 1, NH, DH), lambda b, t: (0, b, 0, 0)),
            pl.BlockSpec((1, NH, MLSTM_CHUNK), lambda b, t: (b, 0, 0)),
        ],
        out_shape=[
            jax.ShapeDtypeStruct((bsz, t_len, D_MODEL), F32),
            jax.ShapeDtypeStruct((DEPTH, bsz, POOL_BUF, D_POOL), F32),
            jax.ShapeDtypeStruct((DEPTH, bsz, NH, DH, DH), F32),
            jax.ShapeDtypeStruct((DEPTH, bsz, NH, DH), F32),
            jax.ShapeDtypeStruct((bsz, NH, MLSTM_CHUNK), F32),
        ],
        scratch_shapes=[
            pltpu.VMEM((tq, D_POOL + 3 * D_MLSTM), F32),
            pltpu.VMEM((GATE_ROWS + D_MLSTM, tq), F32),
            pltpu.VMEM((POOL_PAD + tq, D_POOL), F32),
            pltpu.VMEM((tq, D_MODEL), BF16),
            pltpu.VMEM((NH, DH, 2 * DH), F32),
            pltpu.VMEM((NH, MLSTM_CHUNK), F32),
        ],
        compiler_params=pltpu.CompilerParams(
            dimension_semantics=("arbitrary", "arbitrary"),
            vmem_limit_bytes=V7X_VMEM_LIMIT),
    )(x, w_main, w_kgt, bg_col, w_pool, pool_scale, w_out, ln_g, ln_b)


def _ffn_kernel(x1_ref, w1_ref, b1_ref, w2_ref, b2_ref, g_ref, be_ref, o_ref):
    xb = x1_ref[...].astype(BF16)
    acc = None
    for c in range(D_FF // FF_CHUNK):
        cols = slice(c * FF_CHUNK, (c + 1) * FF_CHUNK)
        hid = jnp.maximum(_dot(xb, w1_ref[:, cols]) + b1_ref[:, cols], 0.0)
        part = _dot((hid * hid).astype(BF16), w2_ref[cols, :])
        acc = part if acc is None else acc + part
    y = ALPHA * x1_ref[...] + (acc + b2_ref[...])
    o_ref[...] = _layer_norm(y, g_ref[...], be_ref[...])


def _ffn(x1, w1, b1, w2, b2, ln_g, ln_b, tile):
    n_tok = x1.shape[0]
    assert n_tok % tile == 0
    const2 = lambda i: (0, 0)
    return pl.pallas_call(
        _ffn_kernel,
        name="ffn_ln2",
        grid=(n_tok // tile,),
        in_specs=[
            pl.BlockSpec((tile, D_MODEL), lambda i: (i, 0)),
            pl.BlockSpec(w1.shape, const2),
            pl.BlockSpec(b1.shape, const2),
            pl.BlockSpec(w2.shape, const2),
            pl.BlockSpec(b2.shape, const2),
            pl.BlockSpec(ln_g.shape, const2),
            pl.BlockSpec(ln_b.shape, const2),
        ],
        out_specs=pl.BlockSpec((tile, D_MODEL), lambda i: (i, 0)),
        out_shape=jax.ShapeDtypeStruct((n_tok, D_MODEL), F32),
        compiler_params=pltpu.CompilerParams(
            dimension_semantics=("arbitrary",),
            vmem_limit_bytes=V7X_VMEM_LIMIT),
    )(x1, w1, b1, w2, b2, ln_g, ln_b)


def _sample_proj_kernel(x_ref, wm_ref, wk_ref, wg_ref, bg_ref, sp_ref, n_ref, m_ref, wp_ref, ps_ref,
                        ypool_ref, pool_ref, q_ref, kd_ref, v_ref, iw_ref, sv_ref, den_ref,
                        floor_ref, osig_ref, nout_ref, mout_ref):
    xb = x_ref[...].astype(BF16)
    pm = _dot(xb, wm_ref[...])
    kk = _dot(xb, wk_ref[...]) * K_SCALE
    gr = _dot(xb, wg_ref[...]) + bg_ref[...]
    u = pm[:, 0:D_POOL]

    for g, w in enumerate(POOL_WINDOWS):
        lo = g * POOL_GC
        u_g = u[:, lo:lo + POOL_GC]
        acc = u_g
        for r in range(POOL_PAD - w, POOL_BUF):
            acc = acc + sp_ref[:, r * D_POOL + lo:r * D_POOL + lo + POOL_GC]
        pooled = acc / float(min(PAST_LEN + 1, w)) - u_g
        ypool_ref[:, lo:lo + POOL_GC] = _dot(pooled.astype(BF16), wp_ref[g]) * ps_ref[:, lo:lo + POOL_GC]
    pool_ref[:, 0:(POOL_BUF - 1) * D_POOL] = sp_ref[:, D_POOL:POOL_BUF * D_POOL]
    pool_ref[:, (POOL_BUF - 1) * D_POOL:POOL_BUF * D_POOL] = u

    for h in range(NH):
        col = slice(h * DH, (h + 1) * DH)
        ig = gr[:, h * DH:(h + 1) * DH]
        lf = jax.nn.log_sigmoid(gr[:, D_MLSTM + h * DH:D_MLSTM + (h + 1) * DH])
        m_old = jnp.broadcast_to(m_ref[:, h:h + 1], ig.shape)
        inter = lf + m_old
        m_t = jnp.maximum(inter, ig)
        dw = jnp.exp(ig - m_t)
        iw = jnp.exp(inter - m_t)
        q = pm[:, D_POOL + h * DH:D_POOL + (h + 1) * DH]
        k = kk[:, col]
        v = pm[:, D_POOL + D_MLSTM + h * DH:D_POOL + D_MLSTM + (h + 1) * DH]
        o = pm[:, D_POOL + 2 * D_MLSTM + h * DH:D_POOL + 2 * D_MLSTM + (h + 1) * DH]
        n_old = n_ref[:, col]
        s = jnp.sum(q * k, axis=-1, keepdims=True) * dw
        q_ref[:, col] = q
        kd_ref[:, col] = dw * k
        v_ref[:, col] = v
        iw_ref[:, col] = iw
        sv_ref[:, col] = s * v
        den_ref[:, col] = s + iw * jnp.sum(q * n_old, axis=-1, keepdims=True)
        floor_ref[:, col] = jnp.exp(-m_t)
        osig_ref[:, col] = jax.nn.sigmoid(o)
        nout_ref[:, col] = iw * n_old + dw * k
        mout_ref[:, col] = m_t


def _sample_proj(x, w_main, w_k, w_grep, b_grep, pool2d, n2d, m2d, w_pool, pool_scale):
    nb = x.shape[0]
    wide = jax.ShapeDtypeStruct((nb, D_MLSTM), F32)
    return pl.pallas_call(
        _sample_proj_kernel,
        name="sample_proj",
        out_shape=[wide, jax.ShapeDtypeStruct((nb, POOL_BUF * D_POOL), F32)] + [wide] * 10,
        compiler_params=pltpu.CompilerParams(vmem_limit_bytes=V7X_VMEM_LIMIT),
    )(x, w_main, w_k, w_grep, b_grep, pool2d, n2d, m2d, w_pool, pool_scale)


def _sample_state_kernel(c_ref, q_ref, kd_ref, v_ref, iw_ref, cout_ref, inter_ref):
    nb = SAMPLE_BLOCK
    pad = jnp.zeros((DH - nb, DH), F32)
    for h in range(NH):
        col = slice(h * DH, (h + 1) * DH)
        q_t = jnp.concatenate([q_ref[:, col], pad], axis=0).T
        kd_t = jnp.concatenate([kd_ref[:, col], pad], axis=0).T
        for j in range(nb):
            c_old = c_ref[0, j, h]
            inter_ref[j:j + 1, col] = jnp.sum(q_t[:, j:j + 1] * c_old, axis=0, keepdims=True)
            cout_ref[0, j, h] = iw_ref[j:j + 1, col] * c_old + kd_t[:, j:j + 1] * v_ref[j:j + 1, col]


def _sample_state(state_c, q, kd, v, iw):
    nb = q.shape[0]
    blk = SAMPLE_BLOCK
    assert nb % blk == 0
    row_spec = pl.BlockSpec((blk, D_MLSTM), lambda i: (i, 0))
    c_spec = pl.BlockSpec((1, blk, NH, DH, DH), lambda i: (0, i, 0, 0, 0))
    return pl.pallas_call(
        _sample_state_kernel,
        name="sample_state",
        grid=(nb // blk,),
        in_specs=[c_spec, row_spec, row_spec, row_spec, row_spec],
        out_specs=[c_spec, row_spec],
        out_shape=[jax.ShapeDtypeStruct(state_c.shape, F32),
                   jax.ShapeDtypeStruct((nb, D_MLSTM), F32)],
        compiler_params=pltpu.CompilerParams(
            dimension_semantics=("arbitrary",),
            vmem_limit_bytes=V7X_VMEM_LIMIT),
    )(state_c, q, kd, v, iw)


def _sample_out_kernel(x_ref, ypool_ref, sv_ref, iw_ref, inter_ref, den_ref, floor_ref, osig_ref,
                       wo_ref, g1_ref, b1_ref, x1_ref):
    hh = (sv_ref[...] + iw_ref[...] * inter_ref[...]) / jnp.maximum(jnp.abs(den_ref[...]), floor_ref[...])
    mixin = jnp.concatenate([ypool_ref[...], osig_ref[...] * hh], axis=-1).astype(BF16)
    x1_ref[...] = _layer_norm(ALPHA * x_ref[...] + _dot(mixin, wo_ref[...]), g1_ref[...], b1_ref[...])


def _sample_out(x, ypool, sv, iw, inter, den, floor, osig, w_out, ln_g, ln_b):
    return pl.pallas_call(
        _sample_out_kernel,
        name="sample_out",
        out_shape=jax.ShapeDtypeStruct(x.shape, F32),
        compiler_params=pltpu.CompilerParams(vmem_limit_bytes=V7X_VMEM_LIMIT),
    )(x, ypool, sv, iw, inter, den, floor, osig, w_out, ln_g, ln_b)


def kernel(x_prompt, x_sample, state_pool, state_C, state_n, state_m, w_in, b_gate, w_pool, pool_scale,
           w_out, ln1_g, ln1_b, w_ff1, b_ff1, w_ff2, b_ff2, ln2_g, ln2_b):
    assert w_in.shape[0] == DEPTH == 1
    bp, t_len, _ = x_prompt.shape
    bs = x_sample.shape[0]
    assert x_sample.shape[1] == 1

    w = w_in[0]
    cut_q, cut_k, cut_v, cut_o, cut_g = (D_POOL, D_POOL + D_MLSTM, D_POOL + 2 * D_MLSTM,
                                         D_POOL + 3 * D_MLSTM, D_POOL + 4 * D_MLSTM)
    w_main = jnp.concatenate([w[:, :cut_k], w[:, cut_v:cut_g]], axis=1).astype(BF16)
    w_k = w[:, cut_k:cut_v].astype(BF16)
    w_g = w[:, cut_g:]
    w_kgt = jnp.concatenate([w_g.T, jnp.zeros((GATE_ROWS - 2 * NH, D_MODEL), F32), w[:, cut_k:cut_v].T],
                            axis=0).astype(BF16)
    bg_col = b_gate[0].reshape(2 * NH, 1)
    w_grep = jnp.repeat(w_g, DH, axis=1).astype(BF16)
    b_grep = jnp.repeat(b_gate[0], DH).reshape(1, 2 * NH * DH)
    wp = w_pool[0].astype(BF16)
    ps = pool_scale[0].reshape(1, D_POOL)
    wo = w_out[0].astype(BF16)
    g1, b1 = ln1_g[0].reshape(1, D_MODEL), ln1_b[0].reshape(1, D_MODEL)
    g2, b2 = ln2_g[0].reshape(1, D_MODEL), ln2_b[0].reshape(1, D_MODEL)
    wf1, wf2 = w_ff1[0].astype(BF16), w_ff2[0].astype(BF16)
    bf1, bf2 = b_ff1[0].reshape(1, D_FF), b_ff2[0].reshape(1, D_MODEL)

    x1_p, pool_p, c_p, n_p, m_rep = _prompt_mixer(x_prompt, w_main, w_kgt, bg_col, wp, ps, wo, g1, b1)
    y_p = _ffn(x1_p.reshape(bp * t_len, D_MODEL), wf1, bf1, wf2, bf2, g2, b2, FFN_TILE)
    m_p = m_rep[:, :, 0].reshape(DEPTH, bp, NH)

    xs = x_sample.reshape(bs, D_MODEL)
    (ypool, pool_s, q_s, kd_s, v_s, iw_s, sv_s, den_s, floor_s, osig_s, n_s, m_s_rep) = _sample_proj(
        xs, w_main, w_k, w_grep, b_grep, state_pool[0].reshape(bs, POOL_BUF * D_POOL),
        state_n[0].reshape(bs, D_MLSTM), state_m[0], wp, ps)
    c_s, inter_s = _sample_state(state_C, q_s, kd_s, v_s, iw_s)
    x1_s = _sample_out(xs, ypool, sv_s, iw_s, inter_s, den_s, floor_s, osig_s, wo, g1, b1)
    y_s = _ffn(x1_s, wf1, bf1, wf2, bf2, g2, b2, bs)

    return (y_p.reshape(bp, t_len, D_MODEL), y_s.reshape(bs, 1, D_MODEL),
            pool_p, c_p, n_p, m_p,
            pool_s.reshape(DEPTH, bs, POOL_BUF, D_POOL), c_s,
            n_s.reshape(DEPTH, bs, NH, DH), m_s_rep[:, ::DH].reshape(DEPTH, bs, NH))
```

```python
import functools

import jax
import jax.numpy as jnp
from jax import lax
from jax.experimental import pallas as pl
from jax.experimental.pallas import tpu as pltpu

F32 = jnp.float32
BF16 = jnp.bfloat16

D_MODEL = 1024
D_POOL = D_MODEL // 2
D_MLSTM = D_MODEL - D_POOL
POOL_WINDOWS = (2, 4, 8, 16)
POOL_GC = D_POOL // len(POOL_WINDOWS)
POOL_BUF = max(POOL_WINDOWS) - 1
POOL_PAD = POOL_BUF + 1
NH = 4
DH = D_MLSTM // NH
D_FF = 4 * D_MODEL
DEPTH = 1
PAST_LEN = 16384
ALPHA = (2.0 * DEPTH) ** 0.25
LN_EPS = 1e-5
K_SCALE = DH ** -0.5

MLSTM_CHUNK = 128
MIX_TILE = 512
FFN_TILE = 512
FF_CHUNK = 1024
SAMPLE_BLOCK = 8
GATE_ROWS = 16
PROJ_COLS = 256
V7X_VMEM_LIMIT = 56 * 1024 * 1024


def _layer_norm(y, g, b):
    mu = jnp.mean(y, axis=-1, keepdims=True)
    yc = y - mu
    var = jnp.mean(yc * yc, axis=-1, keepdims=True)
    return yc * lax.rsqrt(var + LN_EPS) * g + b


def _dot(a, b):
    return jnp.dot(a, b, preferred_element_type=F32)


def _twice(row):
    return jnp.concatenate([row, row], axis=-1)


def _mixer_kernel(xn_ref, xc_ref, wm_ref, wkg_ref, bg_ref, wp_ref, ps_ref, wo_ref, g1_ref, b1_ref,
                  x1_ref, pool_ref, c_ref, n_ref, m_ref,
                  pm_a, pt_a, pm_b, pt_b, xb_ref, ext_ref, mix_ref, caug_ref, mst_ref, *, tiles_per_seq):
    tq = MIX_TILE
    L = MLSTM_CHUNK
    step = pl.program_id(0)
    t_idx = lax.rem(step + tiles_per_seq - 1, tiles_per_seq)
    last_t = tiles_per_seq - 1
    parity = lax.rem(step, 2)

    @pl.when(step == 0)
    def _no_tile_yet():
        pm_b[...] = jnp.zeros(pm_b.shape, F32)
        pt_b[...] = jnp.zeros(pt_b.shape, F32)

    @pl.when((t_idx == 0) | (step == 0))
    def _init():
        ext_ref[0:POOL_PAD, :] = jnp.zeros((POOL_PAD, D_POOL), F32)
        caug_ref[...] = jnp.zeros(caug_ref.shape, F32)
        mst_ref[...] = jnp.zeros(mst_ref.shape, F32)

    tt = lax.broadcasted_iota(jnp.int32, (L, L), 0)
    ss = lax.broadcasted_iota(jnp.int32, (L, L), 1)
    causal = ss <= tt
    diag = ss == tt

    def run(pm_next, pt_next, pm_cur, pt_cur):
        xb_ref[...] = xn_ref[0].astype(BF16)

        def main_piece(j):
            cols = slice(j * PROJ_COLS, (j + 1) * PROJ_COLS)
            pm_next[:, cols] = _dot(xb_ref[...], wm_ref[:, cols])

        def t_piece(r0, r1):
            pt_next[r0:r1, :] = lax.dot_general(wkg_ref[r0:r1, :], xb_ref[...], (((1,), (1,)), ((), ())),
                                                preferred_element_type=F32)

        pieces = [functools.partial(main_piece, j) for j in range(pm_next.shape[1] // PROJ_COLS)]
        pieces += [functools.partial(t_piece, 0, GATE_ROWS + DH)]
        pieces += [functools.partial(t_piece, GATE_ROWS + h * DH, GATE_ROWS + (h + 1) * DH)
                   for h in range(1, NH)]
        n_slots = len(POOL_WINDOWS) + (tq // L) * NH
        slot_of = [(k * n_slots) // len(pieces) for k in range(len(pieces))]
        slot = [0]

        def next_slot():
            for k, piece in enumerate(pieces):
                if slot_of[k] == slot[0]:
                    piece()
            slot[0] += 1

        ext_ref[POOL_PAD:POOL_PAD + tq, :] = pm_cur[:, 0:D_POOL]
        pos = t_idx * tq + lax.broadcasted_iota(jnp.int32, (tq, POOL_GC), 0)
        for g, w in enumerate(POOL_WINDOWS):
            lo = g * POOL_GC
            s = ext_ref[:, lo:lo + POOL_GC]
            k = 1
            while k < w:
                s = s + pltpu.roll(s, k, axis=0)
                k *= 2
            cnt = jnp.minimum(pos + 1, w).astype(F32)
            pooled = s[POOL_PAD:, :] / cnt - ext_ref[POOL_PAD:POOL_PAD + tq, lo:lo + POOL_GC]
            mixed = _dot(pooled.astype(BF16), wp_ref[g]) * ps_ref[:, lo:lo + POOL_GC]
            mix_ref[:, lo:lo + POOL_GC] = mixed.astype(BF16)
            next_slot()
        ext_ref[0:POOL_PAD, :] = ext_ref[tq:tq + POOL_PAD, :]

        gates = pt_cur[0:2 * NH, :] + bg_ref[...]
        ig_all = gates[0:NH]
        lf_all = jax.nn.log_sigmoid(gates[NH:2 * NH])
        lane = lax.broadcasted_iota(jnp.int32, (NH, L), 1)
        ones_blk = jnp.ones((L, DH), BF16)
        m_prev = mst_ref[...]
        for c in range(tq // L):
            rows = slice(c * L, (c + 1) * L)
            lf = lf_all[:, rows]
            b = lf
            k = 1
            while k < L:
                b = b + jnp.where(lane >= k, pltpu.roll(b, k, axis=1), 0.0)
                k *= 2
            a = ig_all[:, rows] - b
            amax = jnp.max(a, axis=-1, keepdims=True)
            b_last = jnp.sum(lf, axis=-1, keepdims=True)
            mm = jnp.maximum(m_prev, amax)
            g_state = jnp.exp(m_prev - mm)
            f_state = jnp.exp(amax - mm)
            w_loc = jnp.exp(a - amax)
            for h in range(NH):
                a_row = a[h:h + 1]
                mp = m_prev[h:h + 1]
                dm = jnp.where(causal, a_row, -jnp.inf)
                big_m = jnp.maximum(jnp.max(dm, axis=-1, keepdims=True), mp)
                p = jnp.exp(dm - big_m)
                iw = jnp.exp(mp - big_m)
                b_col = jnp.sum(jnp.where(diag, b[h:h + 1], 0.0), axis=-1, keepdims=True)
                floor = jnp.exp(-(b_col + big_m))
                qb = pm_cur[rows, D_POOL + h * DH:D_POOL + (h + 1) * DH].astype(BF16)
                v = pm_cur[rows, D_POOL + D_MLSTM + h * DH:D_POOL + D_MLSTM + (h + 1) * DH]
                o = pm_cur[rows, D_POOL + 2 * D_MLSTM + h * DH:D_POOL + 2 * D_MLSTM + (h + 1) * DH]
                kt = pt_cur[GATE_ROWS + h * DH:GATE_ROWS + (h + 1) * DH, rows] * K_SCALE
                v_aug = jnp.concatenate([v.astype(BF16), ones_blk], axis=1)
                caug = caug_ref[h]
                s_loc = (_dot(qb, kt.astype(BF16)) * p).astype(BF16)
                comb = _dot(s_loc, v_aug) + _twice(iw) * _dot(qb, caug.astype(BF16))
                hh = comb[:, :DH] / jnp.maximum(jnp.abs(comb[:, DH:]), floor)
                mix_ref[rows, D_POOL + h * DH:D_POOL + (h + 1) * DH] = (
                    jax.nn.sigmoid(o) * hh).astype(BF16)
                kv = _dot((kt * w_loc[h:h + 1]).astype(BF16), v_aug)
                caug_ref[h] = _twice(g_state[h:h + 1]) * caug + _twice(f_state[h:h + 1]) * kv
                next_slot()
            m_prev = b_last + mm
        mst_ref[...] = m_prev

        mix = _dot(mix_ref[...], wo_ref[...])
        x1_ref[0] = _layer_norm(ALPHA * xc_ref[0] + mix, g1_ref[...], b1_ref[...])

    @pl.when(parity == 0)
    def _even():
        run(pm_a, pt_a, pm_b, pt_b)

    @pl.when(parity == 1)
    def _odd():
        run(pm_b, pt_b, pm_a, pt_a)

    @pl.when((t_idx == last_t) & (step > 0))
    def _final_state():
        pool_ref[0, 0] = ext_ref[tq + 1:tq + POOL_PAD, :]
        for h in range(NH):
            caug = caug_ref[h]
            c_ref[0, 0, h] = caug[:, :DH]
            n_ref[0, 0, h:h + 1, :] = jnp.sum(jnp.where(diag, caug[:, DH:], 0.0),
                                              axis=0, keepdims=True)
        m_ref[0] = mst_ref[...]


def _prompt_mixer(x, w_main, w_kgt, bg_col, w_pool, pool_scale, w_out, ln_g, ln_b):
    bsz, t_len, _ = x.shape
    tq = MIX_TILE
    assert t_len % tq == 0 and tq % MLSTM_CHUNK == 0 and tq >= POOL_PAD
    nt = t_len // tq
    n_tiles = bsz * nt
    const2 = lambda i: (0, 0)
    nxt = lambda i: jnp.minimum(i, n_tiles - 1)
    cur = lambda i: jnp.maximum(i - 1, 0)
    return pl.pallas_call(
        functools.partial(_mixer_kernel, tiles_per_seq=nt),
        name="prompt_mixer",
        grid=(n_tiles + 1,),
        in_specs=[
            pl.BlockSpec((1, tq, D_MODEL), lambda i: (nxt(i) // nt, nxt(i) % nt, 0)),
            pl.BlockSpec((1, tq, D_MODEL), lambda i: (cur(i) // nt, cur(i) % nt, 0)),
            pl.BlockSpec(w_main.shape, const2),
            pl.BlockSpec(w_kgt.shape, const2),
            pl.BlockSpec(bg_col.shape, const2),
            pl.BlockSpec(w_pool.shape, lambda i: (0, 0, 0)),
            pl.BlockSpec(pool_scale.shape, const2),
            pl.BlockSpec(w_out.shape, const2),
            pl.BlockSpec(ln_g.shape, const2),
            pl.BlockSpec(ln_b.shape, const2),
        ],
        out_specs=[
            pl.BlockSpec((1, tq, D_MODEL), lambda i: (cur(i) // nt, cur(i) % nt, 0)),
            pl.BlockSpec((1, 1, POOL_BUF, D_POOL), lambda i: (0, cur(i) // nt, 0, 0)),
            pl.BlockSpec((1, 1, NH, DH, DH), lambda i: (0, cur(i) // nt, 0, 0, 0)),
            pl.BlockSpec((1, 1, NH, DH), lambda i: (0, cur(i) // nt, 0, 0)),
            pl.BlockSpec((1, NH, MLSTM_CHUNK), lambda i: (cur(i) // nt, 0, 0)),
        ],
        out_shape=[
            jax.ShapeDtypeStruct((bsz, t_len, D_MODEL), F32),
            jax.ShapeDtypeStruct((DEPTH, bsz, POOL_BUF, D_POOL), F32),
            jax.ShapeDtypeStruct((DEPTH, bsz, NH, DH, DH), F32),
            jax.ShapeDtypeStruct((DEPTH, bsz, NH, DH), F32),
            jax.ShapeDtypeStruct((bsz, NH, MLSTM_CHUNK), F32),
        ],
        scratch_shapes=[
            pltpu.VMEM((tq, D_POOL + 3 * D_MLSTM), F32),
            pltpu.VMEM((GATE_ROWS + D_MLSTM, tq), F32),
            pltpu.VMEM((tq, D_POOL + 3 * D_MLSTM), F32),
            pltpu.VMEM((GATE_ROWS + D_MLSTM, tq), F32),
            pltpu.VMEM((tq, D_MODEL), BF16),
            pltpu.VMEM((POOL_PAD + tq, D_POOL), F32),
            pltpu.VMEM((tq, D_MODEL), BF16),
            pltpu.VMEM((NH, DH, 2 * DH), F32),
            pltpu.VMEM((NH, MLSTM_CHUNK), F32),
        ],
        compiler_params=pltpu.CompilerParams(
            dimension_semantics=("arbitrary",),
            vmem_limit_bytes=V7X_VMEM_LIMIT),
    )(x, x, w_main, w_kgt, bg_col, w_pool, pool_scale, w_out, ln_g, ln_b)


def _ffn_kernel(x1_ref, w1_ref, b1_ref, w2_ref, b2_ref, g_ref, be_ref, o_ref):
    xb = x1_ref[...].astype(BF16)
    acc = None
    for c in range(D_FF // FF_CHUNK):
        cols = slice(c * FF_CHUNK, (c + 1) * FF_CHUNK)
        hid = jnp.maximum(_dot(xb, w1_ref[:, cols]) + b1_ref[:, cols], 0.0)
        part = _dot((hid * hid).astype(BF16), w2_ref[cols, :])
        acc = part if acc is None else acc + part
    y = ALPHA * x1_ref[...] + (acc + b2_ref[...])
    o_ref[...] = _layer_norm(y, g_ref[...], be_ref[...])


def _ffn(x1, w1, b1, w2, b2, ln_g, ln_b, tile):
    n_tok = x1.shape[0]
    assert n_tok % tile == 0
    const2 = lambda i: (0, 0)
    return pl.pallas_call(
        _ffn_kernel,
        name="ffn_ln2",
        grid=(n_tok // tile,),
        in_specs=[
            pl.BlockSpec((tile, D_MODEL), lambda i: (i, 0)),
            pl.BlockSpec(w1.shape, const2),
            pl.BlockSpec(b1.shape, const2),
            pl.BlockSpec(w2.shape, const2),
            pl.BlockSpec(b2.shape, const2),
            pl.BlockSpec(ln_g.shape, const2),
            pl.BlockSpec(ln_b.shape, const2),
        ],
        out_specs=pl.BlockSpec((tile, D_MODEL), lambda i: (i, 0)),
        out_shape=jax.ShapeDtypeStruct((n_tok, D_MODEL), F32),
        compiler_params=pltpu.CompilerParams(
            dimension_semantics=("arbitrary",),
            vmem_limit_bytes=V7X_VMEM_LIMIT),
    )(x1, w1, b1, w2, b2, ln_g, ln_b)


def _sample_proj_kernel(x_ref, wm_ref, wk_ref, wg_ref, bg_ref, sp_ref, n_ref, m_ref, wp_ref, ps_ref,
                        ypool_ref, pool_ref, q_ref, kd_ref, v_ref, iw_ref, sv_ref, den_ref,
                        floor_ref, osig_ref, nout_ref, mout_ref):
    xb = x_ref[...].astype(BF16)
    pm = _dot(xb, wm_ref[...])
    kk = _dot(xb, wk_ref[...]) * K_SCALE
    gr = _dot(xb, wg_ref[...]) + bg_ref[...]
    u = pm[:, 0:D_POOL]

    for g, w in enumerate(POOL_WINDOWS):
        lo = g * POOL_GC
        u_g = u[:, lo:lo + POOL_GC]
        acc = u_g
        for r in range(POOL_PAD - w, POOL_BUF):
            acc = acc + sp_ref[r, :, lo:lo + POOL_GC]
        pooled = acc / float(min(PAST_LEN + 1, w)) - u_g
        ypool_ref[:, lo:lo + POOL_GC] = _dot(pooled.astype(BF16), wp_ref[g]) * ps_ref[:, lo:lo + POOL_GC]
    pool_ref[0:POOL_BUF - 1] = sp_ref[1:POOL_BUF]
    pool_ref[POOL_BUF - 1] = u

    for h in range(NH):
        col = slice(h * DH, (h + 1) * DH)
        ig = gr[:, h * DH:(h + 1) * DH]
        lf = jax.nn.log_sigmoid(gr[:, D_MLSTM + h * DH:D_MLSTM + (h + 1) * DH])
        m_old = jnp.broadcast_to(m_ref[:, h:h + 1], ig.shape)
        inter = lf + m_old
        m_t = jnp.maximum(inter, ig)
        dw = jnp.exp(ig - m_t)
        iw = jnp.exp(inter - m_t)
        q = pm[:, D_POOL + h * DH:D_POOL + (h + 1) * DH]
        k = kk[:, col]
        v = pm[:, D_POOL + D_MLSTM + h * DH:D_POOL + D_MLSTM + (h + 1) * DH]
        o = pm[:, D_POOL + 2 * D_MLSTM + h * DH:D_POOL + 2 * D_MLSTM + (h + 1) * DH]
        n_old = n_ref[:, col]
        s = jnp.sum(q * k, axis=-1, keepdims=True) * dw
        q_ref[:, col] = q
        kd_ref[:, col] = dw * k
        v_ref[:, col] = v
        iw_ref[:, col] = iw
        sv_ref[:, col] = s * v
        den_ref[:, col] = s + iw * jnp.sum(q * n_old, axis=-1, keepdims=True)
        floor_ref[:, col] = jnp.exp(-m_t)
        osig_ref[:, col] = jax.nn.sigmoid(o)
        nout_ref[:, col] = iw * n_old + dw * k
        mout_ref[:, col] = m_t


def _sample_proj(x, w_main, w_k, w_grep, b_grep, pool_rows, n2d, m2d, w_pool, pool_scale):
    nb = x.shape[0]
    wide = jax.ShapeDtypeStruct((nb, D_MLSTM), F32)
    return pl.pallas_call(
        _sample_proj_kernel,
        name="sample_proj",
        out_shape=[wide, jax.ShapeDtypeStruct(pool_rows.shape, F32)] + [wide] * 10,
        compiler_params=pltpu.CompilerParams(vmem_limit_bytes=V7X_VMEM_LIMIT),
    )(x, w_main, w_k, w_grep, b_grep, pool_rows, n2d, m2d, w_pool, pool_scale)


def _sample_state_kernel(c_ref, q_ref, kd_ref, v_ref, iw_ref, cout_ref, inter_ref):
    nb = SAMPLE_BLOCK
    pad = jnp.zeros((DH - nb, DH), F32)
    for h in range(NH):
        col = slice(h * DH, (h + 1) * DH)
        q_t = jnp.concatenate([q_ref[:, col], pad], axis=0).T
        kd_t = jnp.concatenate([kd_ref[:, col], pad], axis=0).T
        for j in range(nb):
            c_old = c_ref[0, j, h]
            inter_ref[j:j + 1, col] = jnp.sum(q_t[:, j:j + 1] * c_old, axis=0, keepdims=True)
            cout_ref[0, j, h] = iw_ref[j:j + 1, col] * c_old + kd_t[:, j:j + 1] * v_ref[j:j + 1, col]


def _sample_state(state_c, q, kd, v, iw):
    nb = q.shape[0]
    blk = SAMPLE_BLOCK
    assert nb % blk == 0
    row_spec = pl.BlockSpec((blk, D_MLSTM), lambda i: (i, 0))
    c_spec = pl.BlockSpec((1, blk, NH, DH, DH), lambda i: (0, i, 0, 0, 0))
    return pl.pallas_call(
        _sample_state_kernel,
        name="sample_state",
        grid=(nb // blk,),
        in_specs=[c_spec, row_spec, row_spec, row_spec, row_spec],
        out_specs=[c_spec, row_spec],
        out_shape=[jax.ShapeDtypeStruct(state_c.shape, F32),
                   jax.ShapeDtypeStruct((nb, D_MLSTM), F32)],
        compiler_params=pltpu.CompilerParams(
            dimension_semantics=("arbitrary",),
            vmem_limit_bytes=V7X_VMEM_LIMIT),
    )(state_c, q, kd, v, iw)


def _sample_out_kernel(x_ref, ypool_ref, sv_ref, iw_ref, inter_ref, den_ref, floor_ref, osig_ref,
                       wo_ref, g1_ref, b1_ref, x1_ref):
    hh = (sv_ref[...] + iw_ref[...] * inter_ref[...]) / jnp.maximum(jnp.abs(den_ref[...]), floor_ref[...])
    mixin = jnp.concatenate([ypool_ref[...], osig_ref[...] * hh], axis=-1).astype(BF16)
    x1_ref[...] = _layer_norm(ALPHA * x_ref[...] + _dot(mixin, wo_ref[...]), g1_ref[...], b1_ref[...])


def _sample_out(x, ypool, sv, iw, inter, den, floor, osig, w_out, ln_g, ln_b):
    return pl.pallas_call(
        _sample_out_kernel,
        name="sample_out",
        out_shape=jax.ShapeDtypeStruct(x.shape, F32),
        compiler_params=pltpu.CompilerParams(vmem_limit_bytes=V7X_VMEM_LIMIT),
    )(x, ypool, sv, iw, inter, den, floor, osig, w_out, ln_g, ln_b)


def kernel(x_prompt, x_sample, state_pool, state_C, state_n, state_m, w_in, b_gate, w_pool, pool_scale,
           w_out, ln1_g, ln1_b, w_ff1, b_ff1, w_ff2, b_ff2, ln2_g, ln2_b):
    assert w_in.shape[0] == DEPTH == 1
    bp, t_len, _ = x_prompt.shape
    bs = x_sample.shape[0]
    assert x_sample.shape[1] == 1

    w = w_in[0]
    cut_q, cut_k, cut_v, cut_o, cut_g = (D_POOL, D_POOL + D_MLSTM, D_POOL + 2 * D_MLSTM,
                                         D_POOL + 3 * D_MLSTM, D_POOL + 4 * D_MLSTM)
    w_main = jnp.concatenate([w[:, :cut_k], w[:, cut_v:cut_g]], axis=1).astype(BF16)
    w_k = w[:, cut_k:cut_v].astype(BF16)
    w_g = w[:, cut_g:]
    w_kgt = jnp.concatenate([w_g.T, jnp.zeros((GATE_ROWS - 2 * NH, D_MODEL), F32), w[:, cut_k:cut_v].T],
                            axis=0).astype(BF16)
    bg_col = b_gate[0].reshape(2 * NH, 1)
    w_grep = jnp.repeat(w_g, DH, axis=1).astype(BF16)
    b_grep = jnp.repeat(b_gate[0], DH).reshape(1, 2 * NH * DH)
    wp = w_pool[0].astype(BF16)
    ps = pool_scale[0].reshape(1, D_POOL)
    wo = w_out[0].astype(BF16)
    g1, b1 = ln1_g[0].reshape(1, D_MODEL), ln1_b[0].reshape(1, D_MODEL)
    g2, b2 = ln2_g[0].reshape(1, D_MODEL), ln2_b[0].reshape(1, D_MODEL)
    wf1, wf2 = w_ff1[0].astype(BF16), w_ff2[0].astype(BF16)
    bf1, bf2 = b_ff1[0].reshape(1, D_FF), b_ff2[0].reshape(1, D_MODEL)

    x1_p, pool_p, c_p, n_p, m_rep = _prompt_mixer(x_prompt, w_main, w_kgt, bg_col, wp, ps, wo, g1, b1)
    y_p = _ffn(x1_p.reshape(bp * t_len, D_MODEL), wf1, bf1, wf2, bf2, g2, b2, FFN_TILE)
    m_p = m_rep[:, :, 0].reshape(DEPTH, bp, NH)

    xs = x_sample.reshape(bs, D_MODEL)
    (ypool, pool_s, q_s, kd_s, v_s, iw_s, sv_s, den_s, floor_s, osig_s, n_s, m_s_rep) = _sample_proj(
        xs, w_main, w_k, w_grep, b_grep, jnp.transpose(state_pool[0], (1, 0, 2)),
        state_n[0].reshape(bs, D_MLSTM), state_m[0], wp, ps)
    c_s, inter_s = _sample_state(state_C, q_s, kd_s, v_s, iw_s)
    x1_s = _sample_out(xs, ypool, sv_s, iw_s, inter_s, den_s, floor_s, osig_s, wo, g1, b1)
    y_s = _ffn(x1_s, wf1, bf1, wf2, bf2, g2, b2, bs)

    return (y_p.reshape(bp, t_len, D_MODEL), y_s.reshape(bs, 1, D_MODEL),
            pool_p, c_p, n_p, m_p,
            jnp.transpose(pool_s, (1, 0, 2))[None], c_s,
            n_s.reshape(DEPTH, bs, NH, DH), m_s_rep[:, ::DH].reshape(DEPTH, bs, NH))
```

```python
import functools

import jax
import jax.numpy as jnp
from jax import lax
from jax.experimental import pallas as pl
from jax.experimental.pallas import tpu as pltpu

F32 = jnp.float32
BF16 = jnp.bfloat16

D_MODEL = 1024
D_POOL = D_MODEL // 2
D_MLSTM = D_MODEL - D_POOL
POOL_WINDOWS = (2, 4, 8, 16)
POOL_GC = D_POOL // len(POOL_WINDOWS)
POOL_BUF = max(POOL_WINDOWS) - 1
POOL_PAD = POOL_BUF + 1
NH = 4
DH = D_MLSTM // NH
D_FF = 4 * D_MODEL
DEPTH = 1
PAST_LEN = 16384
ALPHA = (2.0 * DEPTH) ** 0.25
LN_EPS = 1e-5
K_SCALE = DH ** -0.5

MLSTM_CHUNK = 128
MIX_TILE = 512
FFN_TILE = 512
FF_CHUNK = 1024
SAMPLE_BLOCK = 8
GATE_ROWS = 16
PROJ_COLS = 256
LANES = 128
assert POOL_GC == LANES and DH == LANES
LB_Q = D_POOL // LANES
LB_V = LB_Q + NH
LB_O = LB_V + NH
V7X_VMEM_LIMIT = 56 * 1024 * 1024


def _layer_norm(y, g, b):
    mu = jnp.mean(y, axis=-1, keepdims=True)
    yc = y - mu
    var = jnp.mean(yc * yc, axis=-1, keepdims=True)
    return yc * lax.rsqrt(var + LN_EPS) * g + b


def _dot(a, b):
    return jnp.dot(a, b, preferred_element_type=F32)


def _twice(row):
    return jnp.concatenate([row, row], axis=-1)


def _mixer_kernel(xn_ref, xc_ref, wm_ref, wkg_ref, bg_ref, wp_ref, ps_ref, wo_ref, g1_ref, b1_ref,
                  x1_ref, pool_ref, c_ref, n_ref, m_ref,
                  pm_a, pt_a, pm_b, pt_b, xb_ref, ext_ref, mix_ref, caug_ref, mst_ref,
                  sloc_ref, vaug_ref, kv_ref, qiw_ref, floor_ref, *, tiles_per_seq):
    tq = MIX_TILE
    L = MLSTM_CHUNK
    step = pl.program_id(0)
    t_idx = lax.rem(step + tiles_per_seq - 1, tiles_per_seq)
    last_t = tiles_per_seq - 1
    parity = lax.rem(step, 2)

    @pl.when(step == 0)
    def _no_tile_yet():
        pm_b[...] = jnp.zeros(pm_b.shape, F32)
        pt_b[...] = jnp.zeros(pt_b.shape, F32)

    @pl.when((t_idx == 0) | (step == 0))
    def _init():
        ext_ref[:, 0:POOL_PAD, :] = jnp.zeros((len(POOL_WINDOWS), POOL_PAD, POOL_GC), F32)
        caug_ref[...] = jnp.zeros(caug_ref.shape, F32)
        mst_ref[...] = jnp.zeros(mst_ref.shape, F32)

    tt = lax.broadcasted_iota(jnp.int32, (L, L), 0)
    ss = lax.broadcasted_iota(jnp.int32, (L, L), 1)
    causal = ss <= tt
    diag = ss == tt

    def run(pm_next, pt_next, pm_cur, pt_cur):
        xb_ref[...] = xn_ref[0].astype(BF16)

        def main_piece(j):
            cols = slice(j * PROJ_COLS, (j + 1) * PROJ_COLS)
            res = _dot(xb_ref[...], wm_ref[:, cols])
            for i in range(PROJ_COLS // LANES):
                pm_next[j * (PROJ_COLS // LANES) + i] = res[:, i * LANES:(i + 1) * LANES]

        def t_piece(r0, r1):
            pt_next[r0:r1, :] = lax.dot_general(wkg_ref[r0:r1, :], xb_ref[...], (((1,), (1,)), ((), ())),
                                                preferred_element_type=F32)

        pieces = [functools.partial(main_piece, j) for j in range(wm_ref.shape[1] // PROJ_COLS)]
        pieces += [functools.partial(t_piece, 0, GATE_ROWS + DH)]
        pieces += [functools.partial(t_piece, GATE_ROWS + h * DH, GATE_ROWS + (h + 1) * DH)
                   for h in range(1, NH)]
        n_chunks = tq // L
        n_slots = len(POOL_WINDOWS) + 2 * n_chunks * NH
        slot_of = [(k * n_slots) // len(pieces) for k in range(len(pieces))]
        slot = [0]

        def next_slot():
            for k, piece in enumerate(pieces):
                if slot_of[k] == slot[0]:
                    piece()
            slot[0] += 1

        pos = t_idx * tq + lax.broadcasted_iota(jnp.int32, (tq, POOL_GC), 0)
        for g, w in enumerate(POOL_WINDOWS):
            lo = g * POOL_GC
            u_g = pm_cur[g]
            ext_ref[g, POOL_PAD:POOL_PAD + tq, :] = u_g
            s = ext_ref[g]
            k = 1
            while k < w:
                s = s + pltpu.roll(s, k, axis=0)
                k *= 2
            cnt = jnp.minimum(pos + 1, w).astype(F32)
            pooled = s[POOL_PAD:, :] / cnt - u_g
            mixed = _dot(pooled.astype(BF16), wp_ref[g]) * ps_ref[:, lo:lo + POOL_GC]
            mix_ref[:, lo:lo + POOL_GC] = mixed.astype(BF16)
            ext_ref[g, 0:POOL_PAD, :] = ext_ref[g, tq:tq + POOL_PAD, :]
            next_slot()

        gates = pt_cur[0:2 * NH, :] + bg_ref[...]
        ig_all = gates[0:NH]
        lf_all = jax.nn.log_sigmoid(gates[NH:2 * NH])
        lane = lax.broadcasted_iota(jnp.int32, (NH, L), 1)
        m_prev = mst_ref[...]
        chunk_rows = []
        for c in range(n_chunks):
            rows = slice(c * L, (c + 1) * L)
            lf = lf_all[:, rows]
            b = lf
            k = 1
            while k < L:
                b = b + jnp.where(lane >= k, pltpu.roll(b, k, axis=1), 0.0)
                k *= 2
            a = ig_all[:, rows] - b
            cmax = a
            k = 1
            while k < L:
                cmax = jnp.maximum(cmax, jnp.where(lane >= k, pltpu.roll(cmax, k, axis=1), -jnp.inf))
                k *= 2
            amax = jnp.max(a, axis=-1, keepdims=True)
            big_m = jnp.maximum(cmax, m_prev)
            mm = jnp.maximum(m_prev, amax)
            chunk_rows.append(dict(
                a=a, big_m=big_m,
                iw=jnp.exp(m_prev - big_m),
                floor=jnp.exp(-(b + big_m)),
                w_loc=jnp.exp(a - amax),
                g_state=jnp.exp(m_prev - mm),
                f_state=jnp.exp(amax - mm)))
            m_prev = jnp.sum(lf, axis=-1, keepdims=True) + mm
        mst_ref[...] = m_prev

        def as_column(row):
            return jnp.broadcast_to(row, (L, L)).T

        ones_blk = jnp.ones((L, DH), BF16)
        for c in range(n_chunks):
            rows = slice(c * L, (c + 1) * L)
            cr = chunk_rows[c]
            for h in range(NH):
                i = c * NH + h
                q = pm_cur[LB_Q + h, rows, :]
                qb = q.astype(BF16)
                kt = pt_cur[GATE_ROWS + h * DH:GATE_ROWS + (h + 1) * DH, rows] * K_SCALE
                v_aug = jnp.concatenate([pm_cur[LB_V + h, rows, :].astype(BF16), ones_blk], axis=1)
                p = jnp.exp(jnp.where(causal, cr["a"][h:h + 1], -jnp.inf) - as_column(cr["big_m"][h:h + 1]))
                sloc_ref[i] = (_dot(qb, kt.astype(BF16)) * p).astype(BF16)
                vaug_ref[i] = v_aug
                kv_ref[i] = _dot((kt * cr["w_loc"][h:h + 1]).astype(BF16), v_aug)
                qiw_ref[i] = (q * as_column(cr["iw"][h:h + 1])).astype(BF16)
                floor_ref[i] = as_column(cr["floor"][h:h + 1])
                next_slot()

        for c in range(n_chunks):
            rows = slice(c * L, (c + 1) * L)
            cr = chunk_rows[c]
            for h in range(NH):
                i = c * NH + h
                caug = caug_ref[h]
                comb = _dot(sloc_ref[i], vaug_ref[i]) + _dot(qiw_ref[i], caug.astype(BF16))
                hh = comb[:, :DH] / jnp.maximum(jnp.abs(comb[:, DH:]), floor_ref[i])
                o = pm_cur[LB_O + h, rows, :]
                mix_ref[rows, D_POOL + h * DH:D_POOL + (h + 1) * DH] = (
                    jax.nn.sigmoid(o) * hh).astype(BF16)
                caug_ref[h] = (_twice(cr["g_state"][h:h + 1]) * caug
                               + _twice(cr["f_state"][h:h + 1]) * kv_ref[i])
                next_slot()

        mix = _dot(mix_ref[...], wo_ref[...])
        x1_ref[0] = _layer_norm(ALPHA * xc_ref[0] + mix, g1_ref[...], b1_ref[...])

    @pl.when(parity == 0)
    def _even():
        run(pm_a, pt_a, pm_b, pt_b)

    @pl.when(parity == 1)
    def _odd():
        run(pm_b, pt_b, pm_a, pt_a)

    @pl.when((t_idx == last_t) & (step > 0))
    def _final_state():
        for g in range(len(POOL_WINDOWS)):
            pool_ref[0, 0, :, g * POOL_GC:(g + 1) * POOL_GC] = ext_ref[g, tq + 1:tq + POOL_PAD, :]
        for h in range(NH):
            caug = caug_ref[h]
            c_ref[0, 0, h] = caug[:, :DH]
            n_ref[0, 0, h:h + 1, :] = jnp.sum(jnp.where(diag, caug[:, DH:], 0.0),
                                              axis=0, keepdims=True)
        m_ref[0] = mst_ref[...]


def _prompt_mixer(x, w_main, w_kgt, bg_col, w_pool, pool_scale, w_out, ln_g, ln_b):
    bsz, t_len, _ = x.shape
    tq = MIX_TILE
    assert t_len % tq == 0 and tq % MLSTM_CHUNK == 0 and tq >= POOL_PAD
    nt = t_len // tq
    n_tiles = bsz * nt
    n_items = (tq // MLSTM_CHUNK) * NH
    const2 = lambda i: (0, 0)
    nxt = lambda i: jnp.minimum(i, n_tiles - 1)
    cur = lambda i: jnp.maximum(i - 1, 0)
    return pl.pallas_call(
        functools.partial(_mixer_kernel, tiles_per_seq=nt),
        name="prompt_mixer",
        grid=(n_tiles + 1,),
        in_specs=[
            pl.BlockSpec((1, tq, D_MODEL), lambda i: (nxt(i) // nt, nxt(i) % nt, 0)),
            pl.BlockSpec((1, tq, D_MODEL), lambda i: (cur(i) // nt, cur(i) % nt, 0)),
            pl.BlockSpec(w_main.shape, const2),
            pl.BlockSpec(w_kgt.shape, const2),
            pl.BlockSpec(bg_col.shape, const2),
            pl.BlockSpec(w_pool.shape, lambda i: (0, 0, 0)),
            pl.BlockSpec(pool_scale.shape, const2),
            pl.BlockSpec(w_out.shape, const2),
            pl.BlockSpec(ln_g.shape, const2),
            pl.BlockSpec(ln_b.shape, const2),
        ],
        out_specs=[
            pl.BlockSpec((1, tq, D_MODEL), lambda i: (cur(i) // nt, cur(i) % nt, 0)),
            pl.BlockSpec((1, 1, POOL_BUF, D_POOL), lambda i: (0, cur(i) // nt, 0, 0)),
            pl.BlockSpec((1, 1, NH, DH, DH), lambda i: (0, cur(i) // nt, 0, 0, 0)),
            pl.BlockSpec((1, 1, NH, DH), lambda i: (0, cur(i) // nt, 0, 0)),
            pl.BlockSpec((1, NH, MLSTM_CHUNK), lambda i: (cur(i) // nt, 0, 0)),
        ],
        out_shape=[
            jax.ShapeDtypeStruct((bsz, t_len, D_MODEL), F32),
            jax.ShapeDtypeStruct((DEPTH, bsz, POOL_BUF, D_POOL), F32),
            jax.ShapeDtypeStruct((DEPTH, bsz, NH, DH, DH), F32),
            jax.ShapeDtypeStruct((DEPTH, bsz, NH, DH), F32),
            jax.ShapeDtypeStruct((bsz, NH, MLSTM_CHUNK), F32),
        ],
        scratch_shapes=[
            pltpu.VMEM((LB_O + NH, tq, LANES), F32),
            pltpu.VMEM((GATE_ROWS + D_MLSTM, tq), F32),
            pltpu.VMEM((LB_O + NH, tq, LANES), F32),
            pltpu.VMEM((GATE_ROWS + D_MLSTM, tq), F32),
            pltpu.VMEM((tq, D_MODEL), BF16),
            pltpu.VMEM((len(POOL_WINDOWS), POOL_PAD + tq, POOL_GC), F32),
            pltpu.VMEM((tq, D_MODEL), BF16),
            pltpu.VMEM((NH, DH, 2 * DH), F32),
            pltpu.VMEM((NH, MLSTM_CHUNK), F32),
            pltpu.VMEM((n_items, MLSTM_CHUNK, MLSTM_CHUNK), BF16),
            pltpu.VMEM((n_items, MLSTM_CHUNK, 2 * DH), BF16),
            pltpu.VMEM((n_items, DH, 2 * DH), F32),
            pltpu.VMEM((n_items, MLSTM_CHUNK, DH), BF16),
            pltpu.VMEM((n_items, MLSTM_CHUNK, DH), F32),
        ],
        compiler_params=pltpu.CompilerParams(
            dimension_semantics=("arbitrary",),
            vmem_limit_bytes=V7X_VMEM_LIMIT),
    )(x, x, w_main, w_kgt, bg_col, w_pool, pool_scale, w_out, ln_g, ln_b)


def _ffn_kernel(x1_ref, w1_ref, b1_ref, w2_ref, b2_ref, g_ref, be_ref, o_ref):
    xb = x1_ref[...].astype(BF16)
    acc = None
    for c in range(D_FF // FF_CHUNK):
        cols = slice(c * FF_CHUNK, (c + 1) * FF_CHUNK)
        hid = jnp.maximum(_dot(xb, w1_ref[:, cols]) + b1_ref[:, cols], 0.0)
        part = _dot((hid * hid).astype(BF16), w2_ref[cols, :])
        acc = part if acc is None else acc + part
    y = ALPHA * x1_ref[...] + (acc + b2_ref[...])
    o_ref[...] = _layer_norm(y, g_ref[...], be_ref[...])


def _ffn(x1, w1, b1, w2, b2, ln_g, ln_b, tile):
    n_tok = x1.shape[0]
    assert n_tok % tile == 0
    const2 = lambda i: (0, 0)
    return pl.pallas_call(
        _ffn_kernel,
        name="ffn_ln2",
        grid=(n_tok // tile,),
        in_specs=[
            pl.BlockSpec((tile, D_MODEL), lambda i: (i, 0)),
            pl.BlockSpec(w1.shape, const2),
            pl.BlockSpec(b1.shape, const2),
            pl.BlockSpec(w2.shape, const2),
            pl.BlockSpec(b2.shape, const2),
            pl.BlockSpec(ln_g.shape, const2),
            pl.BlockSpec(ln_b.shape, const2),
        ],
        out_specs=pl.BlockSpec((tile, D_MODEL), lambda i: (i, 0)),
        out_shape=jax.ShapeDtypeStruct((n_tok, D_MODEL), F32),
        compiler_params=pltpu.CompilerParams(
            dimension_semantics=("arbitrary",),
            vmem_limit_bytes=V7X_VMEM_LIMIT),
    )(x1, w1, b1, w2, b2, ln_g, ln_b)


def _sample_proj_kernel(x_ref, wm_ref, wk_ref, wg_ref, bg_ref, sp_ref, n_ref, m_ref, wp_ref, ps_ref,
                        ypool_ref, pool_ref, q_ref, kd_ref, v_ref, iw_ref, sv_ref, den_ref,
                        floor_ref, osig_ref, nout_ref, mout_ref):
    xb = x_ref[...].astype(BF16)
    pm = _dot(xb, wm_ref[...])
    kk = _dot(xb, wk_ref[...]) * K_SCALE
    gr = _dot(xb, wg_ref[...]) + bg_ref[...]
    u = pm[:, 0:D_POOL]

    for g, w in enumerate(POOL_WINDOWS):
        lo = g * POOL_GC
        u_g = u[:, lo:lo + POOL_GC]
        acc = u_g
        for r in range(POOL_PAD - w, POOL_BUF):
            acc = acc + sp_ref[r, :, lo:lo + POOL_GC]
        pooled = acc / float(min(PAST_LEN + 1, w)) - u_g
        ypool_ref[:, lo:lo + POOL_GC] = _dot(pooled.astype(BF16), wp_ref[g]) * ps_ref[:, lo:lo + POOL_GC]
    pool_ref[0:POOL_BUF - 1] = sp_ref[1:POOL_BUF]
    pool_ref[POOL_BUF - 1] = u

    for h in range(NH):
        col = slice(h * DH, (h + 1) * DH)
        ig = gr[:, h * DH:(h + 1) * DH]
        lf = jax.nn.log_sigmoid(gr[:, D_MLSTM + h * DH:D_MLSTM + (h + 1) * DH])
        m_old = jnp.broadcast_to(m_ref[:, h:h + 1], ig.shape)
        inter = lf + m_old
        m_t = jnp.maximum(inter, ig)
        dw = jnp.exp(ig - m_t)
        iw = jnp.exp(inter - m_t)
        q = pm[:, D_POOL + h * DH:D_POOL + (h + 1) * DH]
        k = kk[:, col]
        v = pm[:, D_POOL + D_MLSTM + h * DH:D_POOL + D_MLSTM + (h + 1) * DH]
        o = pm[:, D_POOL + 2 * D_MLSTM + h * DH:D_POOL + 2 * D_MLSTM + (h + 1) * DH]
        n_old = n_ref[:, col]
        s = jnp.sum(q * k, axis=-1, keepdims=True) * dw
        q_ref[:, col] = q
        kd_ref[:, col] = dw * k
        v_ref[:, col] = v
        iw_ref[:, col] = iw
        sv_ref[:, col] = s * v
        den_ref[:, col] = s + iw * jnp.sum(q * n_old, axis=-1, keepdims=True)
        floor_ref[:, col] = jnp.exp(-m_t)
        osig_ref[:, col] = jax.nn.sigmoid(o)
        nout_ref[:, col] = iw * n_old + dw * k
        mout_ref[:, col] = m_t


def _sample_proj(x, w_main, w_k, w_grep, b_grep, pool_rows, n2d, m2d, w_pool, pool_scale):
    nb = x.shape[0]
    wide = jax.ShapeDtypeStruct((nb, D_MLSTM), F32)
    return pl.pallas_call(
        _sample_proj_kernel,
        name="sample_proj",
        out_shape=[wide, jax.ShapeDtypeStruct(pool_rows.shape, F32)] + [wide] * 10,
        compiler_params=pltpu.CompilerParams(vmem_limit_bytes=V7X_VMEM_LIMIT),
    )(x, w_main, w_k, w_grep, b_grep, pool_rows, n2d, m2d, w_pool, pool_scale)


def _sample_state_kernel(c_ref, q_ref, kd_ref, v_ref, iw_ref, cout_ref, inter_ref):
    nb = SAMPLE_BLOCK
    pad = jnp.zeros((DH - nb, DH), F32)
    for h in range(NH):
        col = slice(h * DH, (h + 1) * DH)
        q_t = jnp.concatenate([q_ref[:, col], pad], axis=0).T
        kd_t = jnp.concatenate([kd_ref[:, col], pad], axis=0).T
        for j in range(nb):
            c_old = c_ref[0, j, h]
            inter_ref[j:j + 1, col] = jnp.sum(q_t[:, j:j + 1] * c_old, axis=0, keepdims=True)
            cout_ref[0, j, h] = iw_ref[j:j + 1, col] * c_old + kd_t[:, j:j + 1] * v_ref[j:j + 1, col]


def _sample_state(state_c, q, kd, v, iw):
    nb = q.shape[0]
    blk = SAMPLE_BLOCK
    assert nb % blk == 0
    row_spec = pl.BlockSpec((blk, D_MLSTM), lambda i: (i, 0))
    c_spec = pl.BlockSpec((1, blk, NH, DH, DH), lambda i: (0, i, 0, 0, 0))
    return pl.pallas_call(
        _sample_state_kernel,
        name="sample_state",
        grid=(nb // blk,),
        in_specs=[c_spec, row_spec, row_spec, row_spec, row_spec],
        out_specs=[c_spec, row_spec],
        out_shape=[jax.ShapeDtypeStruct(state_c.shape, F32),
                   jax.ShapeDtypeStruct((nb, D_MLSTM), F32)],
        compiler_params=pltpu.CompilerParams(
            dimension_semantics=("arbitrary",),
            vmem_limit_bytes=V7X_VMEM_LIMIT),
    )(state_c, q, kd, v, iw)


def _sample_out_kernel(x_ref, ypool_ref, sv_ref, iw_ref, inter_ref, den_ref, floor_ref, osig_ref,
                       wo_ref, g1_ref, b1_ref, x1_ref):
    hh = (sv_ref[...] + iw_ref[...] * inter_ref[...]) / jnp.maximum(jnp.abs(den_ref[...]), floor_ref[...])
    mixin = jnp.concatenate([ypool_ref[...], osig_ref[...] * hh], axis=-1).astype(BF16)
    x1_ref[...] = _layer_norm(ALPHA * x_ref[...] + _dot(mixin, wo_ref[...]), g1_ref[...], b1_ref[...])


def _sample_out(x, ypool, sv, iw, inter, den, floor, osig, w_out, ln_g, ln_b):
    return pl.pallas_call(
        _sample_out_kernel,
        name="sample_out",
        out_shape=jax.ShapeDtypeStruct(x.shape, F32),
        compiler_params=pltpu.CompilerParams(vmem_limit_bytes=V7X_VMEM_LIMIT),
    )(x, ypool, sv, iw, inter, den, floor, osig, w_out, ln_g, ln_b)


def kernel(x_prompt, x_sample, state_pool, state_C, state_n, state_m, w_in, b_gate, w_pool, pool_scale,
           w_out, ln1_g, ln1_b, w_ff1, b_ff1, w_ff2, b_ff2, ln2_g, ln2_b):
    assert w_in.shape[0] == DEPTH == 1
    bp, t_len, _ = x_prompt.shape
    bs = x_sample.shape[0]
    assert x_sample.shape[1] == 1

    w = w_in[0]
    cut_q, cut_k, cut_v, cut_o, cut_g = (D_POOL, D_POOL + D_MLSTM, D_POOL + 2 * D_MLSTM,
                                         D_POOL + 3 * D_MLSTM, D_POOL + 4 * D_MLSTM)
    w_main = jnp.concatenate([w[:, :cut_k], w[:, cut_v:cut_g]], axis=1).astype(BF16)
    w_k = w[:, cut_k:cut_v].astype(BF16)
    w_g = w[:, cut_g:]
    w_kgt = jnp.concatenate([w_g.T, jnp.zeros((GATE_ROWS - 2 * NH, D_MODEL), F32), w[:, cut_k:cut_v].T],
                            axis=0).astype(BF16)
    bg_col = b_gate[0].reshape(2 * NH, 1)
    w_grep = jnp.repeat(w_g, DH, axis=1).astype(BF16)
    b_grep = jnp.repeat(b_gate[0], DH).reshape(1, 2 * NH * DH)
    wp = w_pool[0].astype(BF16)
    ps = pool_scale[0].reshape(1, D_POOL)
    wo = w_out[0].astype(BF16)
    g1, b1 = ln1_g[0].reshape(1, D_MODEL), ln1_b[0].reshape(1, D_MODEL)
    g2, b2 = ln2_g[0].reshape(1, D_MODEL), ln2_b[0].reshape(1, D_MODEL)
    wf1, wf2 = w_ff1[0].astype(BF16), w_ff2[0].astype(BF16)
    bf1, bf2 = b_ff1[0].reshape(1, D_FF), b_ff2[0].reshape(1, D_MODEL)

    x1_p, pool_p, c_p, n_p, m_rep = _prompt_mixer(x_prompt, w_main, w_kgt, bg_col, wp, ps, wo, g1, b1)
    y_p = _ffn(x1_p.reshape(bp * t_len, D_MODEL), wf1, bf1, wf2, bf2, g2, b2, FFN_TILE)
    m_p = m_rep[:, :, 0].reshape(DEPTH, bp, NH)

    xs = x_sample.reshape(bs, D_MODEL)
    (ypool, pool_s, q_s, kd_s, v_s, iw_s, sv_s, den_s, floor_s, osig_s, n_s, m_s_rep) = _sample_proj(
        xs, w_main, w_k, w_grep, b_grep, jnp.transpose(state_pool[0], (1, 0, 2)),
        state_n[0].reshape(bs, D_MLSTM), state_m[0], wp, ps)
    c_s, inter_s = _sample_state(state_C, q_s, kd_s, v_s, iw_s)
    x1_s = _sample_out(xs, ypool, sv_s, iw_s, inter_s, den_s, floor_s, osig_s, wo, g1, b1)
    y_s = _ffn(x1_s, wf1, bf1, wf2, bf2, g2, b2, bs)

    return (y_p.reshape(bp, t_len, D_MODEL), y_s.reshape(bs, 1, D_MODEL),
            pool_p, c_p, n_p, m_p,
            jnp.transpose(pool_s, (1, 0, 2))[None], c_s,
            n_s.reshape(DEPTH, bs, NH, DH), m_s_rep[:, ::DH].reshape(DEPTH, bs, NH))
```

```python
import functools

import jax
import jax.numpy as jnp
from jax import lax
from jax.experimental import pallas as pl
from jax.experimental.pallas import tpu as pltpu

F32 = jnp.float32
BF16 = jnp.bfloat16

D_MODEL = 1024
D_POOL = D_MODEL // 2
D_MLSTM = D_MODEL - D_POOL
POOL_WINDOWS = (2, 4, 8, 16)
POOL_GC = D_POOL // len(POOL_WINDOWS)
POOL_BUF = max(POOL_WINDOWS) - 1
POOL_PAD = POOL_BUF + 1
NH = 4
DH = D_MLSTM // NH
D_FF = 4 * D_MODEL
DEPTH = 1
PAST_LEN = 16384
ALPHA = (2.0 * DEPTH) ** 0.25
LN_EPS = 1e-5
K_SCALE = DH ** -0.5

MLSTM_CHUNK = 128
MIX_TILE = 512
FFN_TILE = 512
FF_CHUNK = 1024
SAMPLE_BLOCK = 8
GATE_ROWS = 16
PROJ_COLS = 256
LANES = 128
assert POOL_GC == LANES and DH == LANES
CUT_K = D_POOL + D_MLSTM
CUT_V = CUT_K + D_MLSTM
CUT_G = D_POOL + 4 * D_MLSTM
LB_Q = D_POOL // LANES
LB_V = LB_Q + NH
LB_O = LB_V + NH
V7X_VMEM_LIMIT = 56 * 1024 * 1024


def _layer_norm(y, g, b):
    mu = jnp.mean(y, axis=-1, keepdims=True)
    yc = y - mu
    var = jnp.mean(yc * yc, axis=-1, keepdims=True)
    return yc * lax.rsqrt(var + LN_EPS) * g + b


def _dot(a, b):
    return jnp.dot(a, b, preferred_element_type=F32)


def _twice(row):
    return jnp.concatenate([row, row], axis=-1)


def _mixer_kernel(xn_ref, xc_ref, win_ref, wkg_ref, bg_ref, wp_ref, ps_ref, wo_ref, g1_ref, b1_ref,
                  x1_ref, pool_ref, c_ref, n_ref, m_ref,
                  pm_a, pt_a, pm_b, pt_b, ext_ref, mix_ref, caug_ref, mst_ref,
                  sloc_ref, vaug_ref, kv_ref, qiw_ref, floor_ref, *, tiles_per_seq):
    tq = MIX_TILE
    L = MLSTM_CHUNK
    step = pl.program_id(0)
    t_idx = lax.rem(step + tiles_per_seq - 1, tiles_per_seq)
    last_t = tiles_per_seq - 1
    parity = lax.rem(step, 2)

    @pl.when(step == 0)
    def _no_tile_yet():
        pm_b[...] = jnp.zeros(pm_b.shape, F32)
        pt_b[...] = jnp.zeros(pt_b.shape, F32)

    @pl.when((t_idx == 0) | (step == 0))
    def _init():
        ext_ref[:, 0:POOL_PAD, :] = jnp.zeros((len(POOL_WINDOWS), POOL_PAD, POOL_GC), F32)
        caug_ref[...] = jnp.zeros(caug_ref.shape, F32)
        mst_ref[...] = jnp.zeros(mst_ref.shape, F32)

    tt = lax.broadcasted_iota(jnp.int32, (L, L), 0)
    ss = lax.broadcasted_iota(jnp.int32, (L, L), 1)
    causal = ss <= tt
    diag = ss == tt

    def run(pm_next, pt_next, pm_cur, pt_cur):
        def main_piece(j):
            c0 = j * PROJ_COLS if j * PROJ_COLS < CUT_K else CUT_V + j * PROJ_COLS - CUT_K
            res = _dot(xn_ref[0], win_ref[:, c0:c0 + PROJ_COLS])
            for i in range(PROJ_COLS // LANES):
                pm_next[j * (PROJ_COLS // LANES) + i] = res[:, i * LANES:(i + 1) * LANES]

        def t_piece(r0, r1):
            pt_next[r0:r1, :] = lax.dot_general(wkg_ref[r0:r1, :], xn_ref[0], (((1,), (1,)), ((), ())),
                                                preferred_element_type=F32)

        pieces = [functools.partial(main_piece, j) for j in range((LB_O + NH) * LANES // PROJ_COLS)]
        pieces += [functools.partial(t_piece, 0, GATE_ROWS + DH)]
        pieces += [functools.partial(t_piece, GATE_ROWS + h * DH, GATE_ROWS + (h + 1) * DH)
                   for h in range(1, NH)]
        n_chunks = tq // L
        n_slots = len(POOL_WINDOWS) + 2 * n_chunks * NH
        slot_of = [(k * n_slots) // len(pieces) for k in range(len(pieces))]
        slot = [0]

        def next_slot():
            for k, piece in enumerate(pieces):
                if slot_of[k] == slot[0]:
                    piece()
            slot[0] += 1

        pos = t_idx * tq + lax.broadcasted_iota(jnp.int32, (tq, POOL_GC), 0)
        for g, w in enumerate(POOL_WINDOWS):
            lo = g * POOL_GC
            u_g = pm_cur[g]
            ext_ref[g, POOL_PAD:POOL_PAD + tq, :] = u_g
            s = ext_ref[g]
            k = 1
            while k < w:
                s = s + pltpu.roll(s, k, axis=0)
                k *= 2
            cnt = jnp.minimum(pos + 1, w).astype(F32)
            pooled = s[POOL_PAD:, :] / cnt - u_g
            mix_ref[:, lo:lo + POOL_GC] = _dot(pooled, wp_ref[g]) * ps_ref[:, lo:lo + POOL_GC]
            ext_ref[g, 0:POOL_PAD, :] = ext_ref[g, tq:tq + POOL_PAD, :]
            next_slot()

        gates = pt_cur[0:2 * NH, :] + bg_ref[...]
        ig_all = gates[0:NH]
        lf_all = jax.nn.log_sigmoid(gates[NH:2 * NH])
        lane = lax.broadcasted_iota(jnp.int32, (NH, L), 1)
        m_prev = mst_ref[...]
        chunk_rows = []
        for c in range(n_chunks):
            rows = slice(c * L, (c + 1) * L)
            lf = lf_all[:, rows]
            b = lf
            k = 1
            while k < L:
                b = b + jnp.where(lane >= k, pltpu.roll(b, k, axis=1), 0.0)
                k *= 2
            a = ig_all[:, rows] - b
            cmax = a
            k = 1
            while k < L:
                cmax = jnp.maximum(cmax, jnp.where(lane >= k, pltpu.roll(cmax, k, axis=1), -jnp.inf))
                k *= 2
            amax = jnp.max(a, axis=-1, keepdims=True)
            big_m = jnp.maximum(cmax, m_prev)
            mm = jnp.maximum(m_prev, amax)
            chunk_rows.append(dict(
                a=a, big_m=big_m,
                iw=jnp.exp(m_prev - big_m),
                floor=jnp.exp(-(b + big_m)),
                w_loc=jnp.exp(a - amax),
                g_state=jnp.exp(m_prev - mm),
                f_state=jnp.exp(amax - mm)))
            m_prev = jnp.sum(lf, axis=-1, keepdims=True) + mm
        mst_ref[...] = m_prev

        def as_column(row):
            return jnp.broadcast_to(row, (L, L)).T

        ones_blk = jnp.ones((L, DH), BF16)
        for c in range(n_chunks):
            rows = slice(c * L, (c + 1) * L)
            cr = chunk_rows[c]
            for h in range(NH):
                i = c * NH + h
                q = pm_cur[LB_Q + h, rows, :]
                qb = q.astype(BF16)
                kt = pt_cur[GATE_ROWS + h * DH:GATE_ROWS + (h + 1) * DH, rows] * K_SCALE
                v_aug = jnp.concatenate([pm_cur[LB_V + h, rows, :].astype(BF16), ones_blk], axis=1)
                p = jnp.exp(jnp.where(causal, cr["a"][h:h + 1], -jnp.inf) - as_column(cr["big_m"][h:h + 1]))
                sloc_ref[i] = (_dot(qb, kt.astype(BF16)) * p).astype(BF16)
                vaug_ref[i] = v_aug
                kv_ref[i] = _dot((kt * cr["w_loc"][h:h + 1]).astype(BF16), v_aug)
                qiw_ref[i] = (q * as_column(cr["iw"][h:h + 1])).astype(BF16)
                floor_ref[i] = as_column(cr["floor"][h:h + 1])
                next_slot()

        for c in range(n_chunks):
            rows = slice(c * L, (c + 1) * L)
            cr = chunk_rows[c]
            for h in range(NH):
                i = c * NH + h
                caug = caug_ref[h]
                comb = _dot(sloc_ref[i], vaug_ref[i]) + _dot(qiw_ref[i], caug.astype(BF16))
                hh = comb[:, :DH] / jnp.maximum(jnp.abs(comb[:, DH:]), floor_ref[i])
                o = pm_cur[LB_O + h, rows, :]
                mix_ref[rows, D_POOL + h * DH:D_POOL + (h + 1) * DH] = jax.nn.sigmoid(o) * hh
                caug_ref[h] = (_twice(cr["g_state"][h:h + 1]) * caug
                               + _twice(cr["f_state"][h:h + 1]) * kv_ref[i])
                next_slot()

        mix = _dot(mix_ref[...], wo_ref[...])
        x1_ref[0] = _layer_norm(ALPHA * xc_ref[0] + mix, g1_ref[...], b1_ref[...])

    @pl.when(parity == 0)
    def _even():
        run(pm_a, pt_a, pm_b, pt_b)

    @pl.when(parity == 1)
    def _odd():
        run(pm_b, pt_b, pm_a, pt_a)

    @pl.when((t_idx == last_t) & (step > 0))
    def _final_state():
        for g in range(len(POOL_WINDOWS)):
            pool_ref[0, 0, :, g * POOL_GC:(g + 1) * POOL_GC] = ext_ref[g, tq + 1:tq + POOL_PAD, :]
        for h in range(NH):
            caug = caug_ref[h]
            c_ref[0, 0, h] = caug[:, :DH]
            n_ref[0, 0, h:h + 1, :] = jnp.sum(jnp.where(diag, caug[:, DH:], 0.0),
                                              axis=0, keepdims=True)
        m_ref[0] = mst_ref[...]


def _prompt_mixer(x, w_in2d, w_kgt, bg_col, w_pool, pool_scale, w_out, ln_g, ln_b):
    bsz, t_len, _ = x.shape
    tq = MIX_TILE
    assert t_len % tq == 0 and tq % MLSTM_CHUNK == 0 and tq >= POOL_PAD
    nt = t_len // tq
    n_tiles = bsz * nt
    n_items = (tq // MLSTM_CHUNK) * NH
    const2 = lambda i: (0, 0)
    nxt = lambda i: jnp.minimum(i, n_tiles - 1)
    cur = lambda i: jnp.maximum(i - 1, 0)
    return pl.pallas_call(
        functools.partial(_mixer_kernel, tiles_per_seq=nt),
        name="prompt_mixer",
        grid=(n_tiles + 1,),
        in_specs=[
            pl.BlockSpec((1, tq, D_MODEL), lambda i: (nxt(i) // nt, nxt(i) % nt, 0)),
            pl.BlockSpec((1, tq, D_MODEL), lambda i: (cur(i) // nt, cur(i) % nt, 0)),
            pl.BlockSpec(w_in2d.shape, const2, pipeline_mode=pl.Buffered(1)),
            pl.BlockSpec(w_kgt.shape, const2, pipeline_mode=pl.Buffered(1)),
            pl.BlockSpec(bg_col.shape, const2),
            pl.BlockSpec(w_pool.shape, lambda i: (0, 0, 0)),
            pl.BlockSpec(pool_scale.shape, const2),
            pl.BlockSpec(w_out.shape, const2, pipeline_mode=pl.Buffered(1)),
            pl.BlockSpec(ln_g.shape, const2),
            pl.BlockSpec(ln_b.shape, const2),
        ],
        out_specs=[
            pl.BlockSpec((1, tq, D_MODEL), lambda i: (cur(i) // nt, cur(i) % nt, 0)),
            pl.BlockSpec((1, 1, POOL_BUF, D_POOL), lambda i: (0, cur(i) // nt, 0, 0)),
            pl.BlockSpec((1, 1, NH, DH, DH), lambda i: (0, cur(i) // nt, 0, 0, 0)),
            pl.BlockSpec((1, 1, NH, DH), lambda i: (0, cur(i) // nt, 0, 0)),
            pl.BlockSpec((1, NH, MLSTM_CHUNK), lambda i: (cur(i) // nt, 0, 0)),
        ],
        out_shape=[
            jax.ShapeDtypeStruct((bsz, t_len, D_MODEL), F32),
            jax.ShapeDtypeStruct((DEPTH, bsz, POOL_BUF, D_POOL), F32),
            jax.ShapeDtypeStruct((DEPTH, bsz, NH, DH, DH), F32),
            jax.ShapeDtypeStruct((DEPTH, bsz, NH, DH), F32),
            jax.ShapeDtypeStruct((bsz, NH, MLSTM_CHUNK), F32),
        ],
        scratch_shapes=[
            pltpu.VMEM((LB_O + NH, tq, LANES), F32),
            pltpu.VMEM((GATE_ROWS + D_MLSTM, tq), F32),
            pltpu.VMEM((LB_O + NH, tq, LANES), F32),
            pltpu.VMEM((GATE_ROWS + D_MLSTM, tq), F32),
            pltpu.VMEM((len(POOL_WINDOWS), POOL_PAD + tq, POOL_GC), F32),
            pltpu.VMEM((tq, D_MODEL), F32),
            pltpu.VMEM((NH, DH, 2 * DH), F32),
            pltpu.VMEM((NH, MLSTM_CHUNK), F32),
            pltpu.VMEM((n_items, MLSTM_CHUNK, MLSTM_CHUNK), BF16),
            pltpu.VMEM((n_items, MLSTM_CHUNK, 2 * DH), BF16),
            pltpu.VMEM((n_items, DH, 2 * DH), F32),
            pltpu.VMEM((n_items, MLSTM_CHUNK, DH), BF16),
            pltpu.VMEM((n_items, MLSTM_CHUNK, DH), F32),
        ],
        compiler_params=pltpu.CompilerParams(
            dimension_semantics=("arbitrary",),
            vmem_limit_bytes=V7X_VMEM_LIMIT),
    )(x, x, w_in2d, w_kgt, bg_col, w_pool, pool_scale, w_out, ln_g, ln_b)


def _ffn_kernel(x1_ref, w1_ref, b1_ref, w2_ref, b2_ref, g_ref, be_ref, o_ref):
    xb = x1_ref[...]
    acc = None
    for c in range(D_FF // FF_CHUNK):
        cols = slice(c * FF_CHUNK, (c + 1) * FF_CHUNK)
        hid = jnp.maximum(_dot(xb, w1_ref[:, cols]) + b1_ref[:, cols], 0.0)
        part = _dot(hid * hid, w2_ref[cols, :])
        acc = part if acc is None else acc + part
    y = ALPHA * x1_ref[...] + (acc + b2_ref[...])
    o_ref[...] = _layer_norm(y, g_ref[...], be_ref[...])


def _ffn(x1, w1, b1, w2, b2, ln_g, ln_b, tile):
    n_tok = x1.shape[0]
    assert n_tok % tile == 0
    const2 = lambda i: (0, 0)
    return pl.pallas_call(
        _ffn_kernel,
        name="ffn_ln2",
        grid=(n_tok // tile,),
        in_specs=[
            pl.BlockSpec((tile, D_MODEL), lambda i: (i, 0)),
            pl.BlockSpec(w1.shape, const2, pipeline_mode=pl.Buffered(1)),
            pl.BlockSpec(b1.shape, const2),
            pl.BlockSpec(w2.shape, const2, pipeline_mode=pl.Buffered(1)),
            pl.BlockSpec(b2.shape, const2),
            pl.BlockSpec(ln_g.shape, const2),
            pl.BlockSpec(ln_b.shape, const2),
        ],
        out_specs=pl.BlockSpec((tile, D_MODEL), lambda i: (i, 0)),
        out_shape=jax.ShapeDtypeStruct((n_tok, D_MODEL), F32),
        compiler_params=pltpu.CompilerParams(
            dimension_semantics=("arbitrary",),
            vmem_limit_bytes=V7X_VMEM_LIMIT),
    )(x1, w1, b1, w2, b2, ln_g, ln_b)


def _sample_proj_kernel(x_ref, win_ref, wg_ref, bg_ref, sp_ref, n_ref, m_ref, wp_ref, ps_ref,
                        ypool_ref, pool_ref, q_ref, kd_ref, v_ref, iw_ref, sv_ref, den_ref,
                        floor_ref, osig_ref, nout_ref, mout_ref):
    x = x_ref[...]
    pm = jnp.concatenate([_dot(x, win_ref[:, 0:CUT_K]), _dot(x, win_ref[:, CUT_V:CUT_G])], axis=1)
    kk = _dot(x, win_ref[:, CUT_K:CUT_V]) * K_SCALE
    gr = _dot(x, wg_ref[...]) + bg_ref[...]
    u = pm[:, 0:D_POOL]

    for g, w in enumerate(POOL_WINDOWS):
        lo = g * POOL_GC
        u_g = u[:, lo:lo + POOL_GC]
        acc = u_g
        for r in range(POOL_PAD - w, POOL_BUF):
            acc = acc + sp_ref[r, :, lo:lo + POOL_GC]
        pooled = acc / float(min(PAST_LEN + 1, w)) - u_g
        ypool_ref[:, lo:lo + POOL_GC] = _dot(pooled, wp_ref[g]) * ps_ref[:, lo:lo + POOL_GC]
    pool_ref[0:POOL_BUF - 1] = sp_ref[1:POOL_BUF]
    pool_ref[POOL_BUF - 1] = u

    for h in range(NH):
        col = slice(h * DH, (h + 1) * DH)
        ig = gr[:, h * DH:(h + 1) * DH]
        lf = jax.nn.log_sigmoid(gr[:, D_MLSTM + h * DH:D_MLSTM + (h + 1) * DH])
        m_old = jnp.broadcast_to(m_ref[:, h:h + 1], ig.shape)
        inter = lf + m_old
        m_t = jnp.maximum(inter, ig)
        dw = jnp.exp(ig - m_t)
        iw = jnp.exp(inter - m_t)
        q = pm[:, D_POOL + h * DH:D_POOL + (h + 1) * DH]
        k = kk[:, col]
        v = pm[:, D_POOL + D_MLSTM + h * DH:D_POOL + D_MLSTM + (h + 1) * DH]
        o = pm[:, D_POOL + 2 * D_MLSTM + h * DH:D_POOL + 2 * D_MLSTM + (h + 1) * DH]
        n_old = n_ref[:, col]
        s = jnp.sum(q * k, axis=-1, keepdims=True) * dw
        q_ref[:, col] = q
        kd_ref[:, col] = dw * k
        v_ref[:, col] = v
        iw_ref[:, col] = iw
        sv_ref[:, col] = s * v
        den_ref[:, col] = s + iw * jnp.sum(q * n_old, axis=-1, keepdims=True)
        floor_ref[:, col] = jnp.exp(-m_t)
        osig_ref[:, col] = jax.nn.sigmoid(o)
        nout_ref[:, col] = iw * n_old + dw * k
        mout_ref[:, col] = m_t


def _sample_proj(x, w_in2d, w_grep, b_grep, pool_rows, n2d, m2d, w_pool, pool_scale):
    nb = x.shape[0]
    wide = jax.ShapeDtypeStruct((nb, D_MLSTM), F32)
    return pl.pallas_call(
        _sample_proj_kernel,
        name="sample_proj",
        out_shape=[wide, jax.ShapeDtypeStruct(pool_rows.shape, F32)] + [wide] * 10,
        compiler_params=pltpu.CompilerParams(vmem_limit_bytes=V7X_VMEM_LIMIT),
    )(x, w_in2d, w_grep, b_grep, pool_rows, n2d, m2d, w_pool, pool_scale)


def _sample_state_kernel(c_ref, q_ref, kd_ref, v_ref, iw_ref, cout_ref, inter_ref):
    nb = SAMPLE_BLOCK
    pad = jnp.zeros((DH - nb, DH), F32)
    for h in range(NH):
        col = slice(h * DH, (h + 1) * DH)
        q_t = jnp.concatenate([q_ref[:, col], pad], axis=0).T
        kd_t = jnp.concatenate([kd_ref[:, col], pad], axis=0).T
        for j in range(nb):
            c_old = c_ref[0, j, h]
            inter_ref[j:j + 1, col] = jnp.sum(q_t[:, j:j + 1] * c_old, axis=0, keepdims=True)
            cout_ref[0, j, h] = iw_ref[j:j + 1, col] * c_old + kd_t[:, j:j + 1] * v_ref[j:j + 1, col]


def _sample_state(state_c, q, kd, v, iw):
    nb = q.shape[0]
    blk = SAMPLE_BLOCK
    assert nb % blk == 0
    row_spec = pl.BlockSpec((blk, D_MLSTM), lambda i: (i, 0))
    c_spec = pl.BlockSpec((1, blk, NH, DH, DH), lambda i: (0, i, 0, 0, 0))
    return pl.pallas_call(
        _sample_state_kernel,
        name="sample_state",
        grid=(nb // blk,),
        in_specs=[c_spec, row_spec, row_spec, row_spec, row_spec],
        out_specs=[c_spec, row_spec],
        out_shape=[jax.ShapeDtypeStruct(state_c.shape, F32),
                   jax.ShapeDtypeStruct((nb, D_MLSTM), F32)],
        compiler_params=pltpu.CompilerParams(
            dimension_semantics=("arbitrary",),
            vmem_limit_bytes=V7X_VMEM_LIMIT),
    )(state_c, q, kd, v, iw)


def _sample_out_kernel(x_ref, ypool_ref, sv_ref, iw_ref, inter_ref, den_ref, floor_ref, osig_ref,
                       wo_ref, g1_ref, b1_ref, x1_ref):
    hh = (sv_ref[...] + iw_ref[...] * inter_ref[...]) / jnp.maximum(jnp.abs(den_ref[...]), floor_ref[...])
    mixin = jnp.concatenate([ypool_ref[...], osig_ref[...] * hh], axis=-1)
    x1_ref[...] = _layer_norm(ALPHA * x_ref[...] + _dot(mixin, wo_ref[...]), g1_ref[...], b1_ref[...])


def _sample_out(x, ypool, sv, iw, inter, den, floor, osig, w_out, ln_g, ln_b):
    return pl.pallas_call(
        _sample_out_kernel,
        name="sample_out",
        out_shape=jax.ShapeDtypeStruct(x.shape, F32),
        compiler_params=pltpu.CompilerParams(vmem_limit_bytes=V7X_VMEM_LIMIT),
    )(x, ypool, sv, iw, inter, den, floor, osig, w_out, ln_g, ln_b)


def kernel(x_prompt, x_sample, state_pool, state_C, state_n, state_m, w_in, b_gate, w_pool, pool_scale,
           w_out, ln1_g, ln1_b, w_ff1, b_ff1, w_ff2, b_ff2, ln2_g, ln2_b):
    assert w_in.shape[0] == DEPTH == 1
    bp, t_len, _ = x_prompt.shape
    bs = x_sample.shape[0]
    assert x_sample.shape[1] == 1

    w = w_in[0]
    w_g = w[:, CUT_G:]
    w_kgt = jnp.concatenate([w_g.T, jnp.zeros((GATE_ROWS - 2 * NH, D_MODEL), F32), w[:, CUT_K:CUT_V].T],
                            axis=0)
    bg_col = b_gate[0].reshape(2 * NH, 1)
    w_grep = jnp.repeat(w_g, DH, axis=1)
    b_grep = jnp.repeat(b_gate[0], DH).reshape(1, 2 * NH * DH)
    wp = w_pool[0]
    ps = pool_scale[0].reshape(1, D_POOL)
    wo = w_out[0]
    g1, b1 = ln1_g[0].reshape(1, D_MODEL), ln1_b[0].reshape(1, D_MODEL)
    g2, b2 = ln2_g[0].reshape(1, D_MODEL), ln2_b[0].reshape(1, D_MODEL)
    wf1, wf2 = w_ff1[0], w_ff2[0]
    bf1, bf2 = b_ff1[0].reshape(1, D_FF), b_ff2[0].reshape(1, D_MODEL)

    x1_p, pool_p, c_p, n_p, m_rep = _prompt_mixer(x_prompt, w, w_kgt, bg_col, wp, ps, wo, g1, b1)
    y_p = _ffn(x1_p.reshape(bp * t_len, D_MODEL), wf1, bf1, wf2, bf2, g2, b2, FFN_TILE)
    m_p = m_rep[:, :, 0].reshape(DEPTH, bp, NH)

    xs = x_sample.reshape(bs, D_MODEL)
    (ypool, pool_s, q_s, kd_s, v_s, iw_s, sv_s, den_s, floor_s, osig_s, n_s, m_s_rep) = _sample_proj(
        xs, w, w_grep, b_grep, jnp.transpose(state_pool[0], (1, 0, 2)),
        state_n[0].reshape(bs, D_MLSTM), state_m[0], wp, ps)
    c_s, inter_s = _sample_state(state_C, q_s, kd_s, v_s, iw_s)
    x1_s = _sample_out(xs, ypool, sv_s, iw_s, inter_s, den_s, floor_s, osig_s, wo, g1, b1)
    y_s = _ffn(x1_s, wf1, bf1, wf2, bf2, g2, b2, bs)

    return (y_p.reshape(bp, t_len, D_MODEL), y_s.reshape(bs, 1, D_MODEL),
            pool_p, c_p, n_p, m_p,
            jnp.transpose(pool_s, (1, 0, 2))[None], c_s,
            n_s.reshape(DEPTH, bs, NH, DH), m_s_rep[:, ::DH].reshape(DEPTH, bs, NH))
```

```python
import functools

import jax
import jax.numpy as jnp
from jax import lax
from jax.experimental import pallas as pl
from jax.experimental.pallas import tpu as pltpu

F32 = jnp.float32
BF16 = jnp.bfloat16

D_MODEL = 1024
D_POOL = D_MODEL // 2
D_MLSTM = D_MODEL - D_POOL
POOL_WINDOWS = (2, 4, 8, 16)
POOL_GC = D_POOL // len(POOL_WINDOWS)
POOL_BUF = max(POOL_WINDOWS) - 1
POOL_PAD = POOL_BUF + 1
NH = 4
DH = D_MLSTM // NH
D_FF = 4 * D_MODEL
DEPTH = 1
PAST_LEN = 16384
ALPHA = (2.0 * DEPTH) ** 0.25
LN_EPS = 1e-5
K_SCALE = DH ** -0.5

MLSTM_CHUNK = 128
MIX_TILE = 512
FFN_TILE = 512
FF_CHUNK = 1024
SAMPLE_BLOCK = 8
GATE_ROWS = 16
PROJ_COLS = 256
LANES = 128
assert POOL_GC == LANES and DH == LANES
CUT_K = D_POOL + D_MLSTM
CUT_V = CUT_K + D_MLSTM
CUT_G = D_POOL + 4 * D_MLSTM
LB_Q = D_POOL // LANES
LB_V = LB_Q + NH
LB_O = LB_V + NH
V7X_VMEM_LIMIT = 56 * 1024 * 1024


def _layer_norm(y, g, b):
    mu = jnp.mean(y, axis=-1, keepdims=True)
    yc = y - mu
    var = jnp.mean(yc * yc, axis=-1, keepdims=True)
    return yc * lax.rsqrt(var + LN_EPS) * g + b


def _dot(a, b):
    return jnp.dot(a, b, preferred_element_type=F32)


def _dot_nt(a, b):
    return lax.dot_general(a, b, (((1,), (1,)), ((), ())), preferred_element_type=F32)


def _twice(row):
    return jnp.concatenate([row, row], axis=-1)


def _mixer_kernel(xn_ref, xc_ref, wt_ref, bg_ref, wp_ref, ps_ref, wo_ref, g1_ref, b1_ref,
                  x1_ref, pool_ref, c_ref, n_ref, m_ref,
                  wmain_ref, pm_a, pt_a, pm_b, pt_b, ext_ref, mix_ref, caug_ref, mst_ref,
                  sloc_ref, vaug_ref, kv_ref, qiw_ref, floor_ref, *, tiles_per_seq):
    tq = MIX_TILE
    L = MLSTM_CHUNK
    step = pl.program_id(0)
    t_idx = lax.rem(step + tiles_per_seq - 1, tiles_per_seq)
    last_t = tiles_per_seq - 1
    parity = lax.rem(step, 2)

    @pl.when(step == 0)
    def _no_tile_yet():
        pm_b[...] = jnp.zeros(pm_b.shape, F32)
        pt_b[...] = jnp.zeros(pt_b.shape, F32)
        for j in range(wmain_ref.shape[1] // PROJ_COLS):
            r0 = j * PROJ_COLS if j * PROJ_COLS < CUT_K else CUT_V + j * PROJ_COLS - CUT_K
            wmain_ref[:, j * PROJ_COLS:(j + 1) * PROJ_COLS] = wt_ref[r0:r0 + PROJ_COLS, :].T

    @pl.when((t_idx == 0) | (step == 0))
    def _init():
        ext_ref[:, 0:POOL_PAD, :] = jnp.zeros((len(POOL_WINDOWS), POOL_PAD, POOL_GC), F32)
        caug_ref[...] = jnp.zeros(caug_ref.shape, F32)
        mst_ref[...] = jnp.zeros(mst_ref.shape, F32)

    tt = lax.broadcasted_iota(jnp.int32, (L, L), 0)
    ss = lax.broadcasted_iota(jnp.int32, (L, L), 1)
    causal = ss <= tt
    diag = ss == tt

    def run(pm_next, pt_next, pm_cur, pt_cur):
        def main_piece(j):
            res = _dot(xn_ref[0], wmain_ref[:, j * PROJ_COLS:(j + 1) * PROJ_COLS])
            for i in range(PROJ_COLS // LANES):
                pm_next[j * (PROJ_COLS // LANES) + i] = res[:, i * LANES:(i + 1) * LANES]

        def t_piece(h):
            if h == 0:
                pt_next[0:2 * NH, :] = _dot_nt(wt_ref[CUT_G:CUT_G + 2 * NH, :], xn_ref[0])
            pt_next[GATE_ROWS + h * DH:GATE_ROWS + (h + 1) * DH, :] = _dot_nt(
                wt_ref[CUT_K + h * DH:CUT_K + (h + 1) * DH, :], xn_ref[0])

        pieces = [functools.partial(main_piece, j) for j in range(wmain_ref.shape[1] // PROJ_COLS)]
        pieces += [functools.partial(t_piece, h) for h in range(NH)]
        n_chunks = tq // L
        n_slots = len(POOL_WINDOWS) + 2 * n_chunks * NH
        slot_of = [(k * n_slots) // len(pieces) for k in range(len(pieces))]
        slot = [0]

        def next_slot():
            for k, piece in enumerate(pieces):
                if slot_of[k] == slot[0]:
                    piece()
            slot[0] += 1

        pos = t_idx * tq + lax.broadcasted_iota(jnp.int32, (tq, POOL_GC), 0)
        for g, w in enumerate(POOL_WINDOWS):
            lo = g * POOL_GC
            u_g = pm_cur[g]
            ext_ref[g, POOL_PAD:POOL_PAD + tq, :] = u_g
            s = ext_ref[g]
            k = 1
            while k < w:
                s = s + pltpu.roll(s, k, axis=0)
                k *= 2
            cnt = jnp.minimum(pos + 1, w).astype(F32)
            pooled = s[POOL_PAD:, :] / cnt - u_g
            mix_ref[:, lo:lo + POOL_GC] = _dot(pooled, wp_ref[g]) * ps_ref[:, lo:lo + POOL_GC]
            ext_ref[g, 0:POOL_PAD, :] = ext_ref[g, tq:tq + POOL_PAD, :]
            next_slot()

        gates = pt_cur[0:2 * NH, :] + bg_ref[...]
        ig_all = gates[0:NH]
        lf_all = jax.nn.log_sigmoid(gates[NH:2 * NH])
        lane = lax.broadcasted_iota(jnp.int32, (NH, L), 1)
        m_prev = mst_ref[...]
        chunk_rows = []
        for c in range(n_chunks):
            rows = slice(c * L, (c + 1) * L)
            lf = lf_all[:, rows]
            b = lf
            k = 1
            while k < L:
                b = b + jnp.where(lane >= k, pltpu.roll(b, k, axis=1), 0.0)
                k *= 2
            a = ig_all[:, rows] - b
            cmax = a
            k = 1
            while k < L:
                cmax = jnp.maximum(cmax, jnp.where(lane >= k, pltpu.roll(cmax, k, axis=1), -jnp.inf))
                k *= 2
            amax = jnp.max(a, axis=-1, keepdims=True)
            big_m = jnp.maximum(cmax, m_prev)
            mm = jnp.maximum(m_prev, amax)
            chunk_rows.append(dict(
                a=a, big_m=big_m,
                iw=jnp.exp(m_prev - big_m),
                floor=jnp.exp(-(b + big_m)),
                w_loc=jnp.exp(a - amax),
                g_state=jnp.exp(m_prev - mm),
                f_state=jnp.exp(amax - mm)))
            m_prev = jnp.sum(lf, axis=-1, keepdims=True) + mm
        mst_ref[...] = m_prev

        def as_column(row):
            return jnp.broadcast_to(row, (L, L)).T

        ones_blk = jnp.ones((L, DH), BF16)
        for c in range(n_chunks):
            rows = slice(c * L, (c + 1) * L)
            cr = chunk_rows[c]
            for h in range(NH):
                i = c * NH + h
                q = pm_cur[LB_Q + h, rows, :]
                qb = q.astype(BF16)
                kt = pt_cur[GATE_ROWS + h * DH:GATE_ROWS + (h + 1) * DH, rows] * K_SCALE
                v_aug = jnp.concatenate([pm_cur[LB_V + h, rows, :].astype(BF16), ones_blk], axis=1)
                p = jnp.exp(jnp.where(causal, cr["a"][h:h + 1], -jnp.inf) - as_column(cr["big_m"][h:h + 1]))
                sloc_ref[i] = (_dot(qb, kt.astype(BF16)) * p).astype(BF16)
                vaug_ref[i] = v_aug
                kv_ref[i] = _dot((kt * cr["w_loc"][h:h + 1]).astype(BF16), v_aug)
                qiw_ref[i] = (q * as_column(cr["iw"][h:h + 1])).astype(BF16)
                floor_ref[i] = as_column(cr["floor"][h:h + 1])
                next_slot()

        for c in range(n_chunks):
            rows = slice(c * L, (c + 1) * L)
            cr = chunk_rows[c]
            for h in range(NH):
                i = c * NH + h
                caug = caug_ref[h]
                comb = _dot(sloc_ref[i], vaug_ref[i]) + _dot(qiw_ref[i], caug.astype(BF16))
                hh = comb[:, :DH] / jnp.maximum(jnp.abs(comb[:, DH:]), floor_ref[i])
                o = pm_cur[LB_O + h, rows, :]
                mix_ref[rows, D_POOL + h * DH:D_POOL + (h + 1) * DH] = jax.nn.sigmoid(o) * hh
                caug_ref[h] = (_twice(cr["g_state"][h:h + 1]) * caug
                               + _twice(cr["f_state"][h:h + 1]) * kv_ref[i])
                next_slot()

        mix = _dot(mix_ref[...], wo_ref[...])
        x1_ref[0] = _layer_norm(ALPHA * xc_ref[0] + mix, g1_ref[...], b1_ref[...])

    @pl.when(parity == 0)
    def _even():
        run(pm_a, pt_a, pm_b, pt_b)

    @pl.when(parity == 1)
    def _odd():
        run(pm_b, pt_b, pm_a, pt_a)

    @pl.when((t_idx == last_t) & (step > 0))
    def _final_state():
        for g in range(len(POOL_WINDOWS)):
            pool_ref[0, 0, :, g * POOL_GC:(g + 1) * POOL_GC] = ext_ref[g, tq + 1:tq + POOL_PAD, :]
        for h in range(NH):
            caug = caug_ref[h]
            c_ref[0, 0, h] = caug[:, :DH]
            n_ref[0, 0, h:h + 1, :] = jnp.sum(jnp.where(diag, caug[:, DH:], 0.0),
                                              axis=0, keepdims=True)
        m_ref[0] = mst_ref[...]


def _prompt_mixer(x, w_in_t, bg_col, w_pool, pool_scale, w_out, ln_g, ln_b):
    bsz, t_len, _ = x.shape
    tq = MIX_TILE
    assert t_len % tq == 0 and tq % MLSTM_CHUNK == 0 and tq >= POOL_PAD
    nt = t_len // tq
    n_tiles = bsz * nt
    n_items = (tq // MLSTM_CHUNK) * NH
    const2 = lambda i: (0, 0)
    nxt = lambda i: jnp.minimum(i, n_tiles - 1)
    cur = lambda i: jnp.maximum(i - 1, 0)
    return pl.pallas_call(
        functools.partial(_mixer_kernel, tiles_per_seq=nt),
        name="prompt_mixer",
        grid=(n_tiles + 1,),
        in_specs=[
            pl.BlockSpec((1, tq, D_MODEL), lambda i: (nxt(i) // nt, nxt(i) % nt, 0)),
            pl.BlockSpec((1, tq, D_MODEL), lambda i: (cur(i) // nt, cur(i) % nt, 0)),
            pl.BlockSpec(w_in_t.shape, const2, pipeline_mode=pl.Buffered(1)),
            pl.BlockSpec(bg_col.shape, const2),
            pl.BlockSpec(w_pool.shape, lambda i: (0, 0, 0)),
            pl.BlockSpec(pool_scale.shape, const2),
            pl.BlockSpec(w_out.shape, const2, pipeline_mode=pl.Buffered(1)),
            pl.BlockSpec(ln_g.shape, const2),
            pl.BlockSpec(ln_b.shape, const2),
        ],
        out_specs=[
            pl.BlockSpec((1, tq, D_MODEL), lambda i: (cur(i) // nt, cur(i) % nt, 0)),
            pl.BlockSpec((1, 1, POOL_BUF, D_POOL), lambda i: (0, cur(i) // nt, 0, 0)),
            pl.BlockSpec((1, 1, NH, DH, DH), lambda i: (0, cur(i) // nt, 0, 0, 0)),
            pl.BlockSpec((1, 1, NH, DH), lambda i: (0, cur(i) // nt, 0, 0)),
            pl.BlockSpec((1, NH, MLSTM_CHUNK), lambda i: (cur(i) // nt, 0, 0)),
        ],
        out_shape=[
            jax.ShapeDtypeStruct((bsz, t_len, D_MODEL), F32),
            jax.ShapeDtypeStruct((DEPTH, bsz, POOL_BUF, D_POOL), F32),
            jax.ShapeDtypeStruct((DEPTH, bsz, NH, DH, DH), F32),
            jax.ShapeDtypeStruct((DEPTH, bsz, NH, DH), F32),
            jax.ShapeDtypeStruct((bsz, NH, MLSTM_CHUNK), F32),
        ],
        scratch_shapes=[
            pltpu.VMEM((D_MODEL, (LB_O + NH) * LANES), F32),
            pltpu.VMEM((LB_O + NH, tq, LANES), F32),
            pltpu.VMEM((GATE_ROWS + D_MLSTM, tq), F32),
            pltpu.VMEM((LB_O + NH, tq, LANES), F32),
            pltpu.VMEM((GATE_ROWS + D_MLSTM, tq), F32),
            pltpu.VMEM((len(POOL_WINDOWS), POOL_PAD + tq, POOL_GC), F32),
            pltpu.VMEM((tq, D_MODEL), F32),
            pltpu.VMEM((NH, DH, 2 * DH), F32),
            pltpu.VMEM((NH, MLSTM_CHUNK), F32),
            pltpu.VMEM((n_items, MLSTM_CHUNK, MLSTM_CHUNK), BF16),
            pltpu.VMEM((n_items, MLSTM_CHUNK, 2 * DH), BF16),
            pltpu.VMEM((n_items, DH, 2 * DH), F32),
            pltpu.VMEM((n_items, MLSTM_CHUNK, DH), BF16),
            pltpu.VMEM((n_items, MLSTM_CHUNK, DH), F32),
        ],
        compiler_params=pltpu.CompilerParams(
            dimension_semantics=("arbitrary",),
            vmem_limit_bytes=V7X_VMEM_LIMIT),
    )(x, x, w_in_t, bg_col, w_pool, pool_scale, w_out, ln_g, ln_b)


def _ffn_kernel(x1p_ref, x1s_ref, w1_ref, b1_ref, w2_ref, b2_ref, g_ref, be_ref, yp_ref, ys_ref,
                *, n_prompt_tiles):
    def rows(x1_ref, o_ref):
        acc = None
        for c in range(D_FF // FF_CHUNK):
            cols = slice(c * FF_CHUNK, (c + 1) * FF_CHUNK)
            hid = jnp.maximum(_dot(x1_ref[...], w1_ref[:, cols]) + b1_ref[:, cols], 0.0)
            part = _dot(hid * hid, w2_ref[cols, :])
            acc = part if acc is None else acc + part
        y = ALPHA * x1_ref[...] + (acc + b2_ref[...])
        o_ref[...] = _layer_norm(y, g_ref[...], be_ref[...])

    step = pl.program_id(0)

    @pl.when(step < n_prompt_tiles)
    def _prompt():
        rows(x1p_ref, yp_ref)

    @pl.when(step == n_prompt_tiles)
    def _sample():
        rows(x1s_ref, ys_ref)


def _ffn(x1p, x1s, w1, b1, w2, b2, ln_g, ln_b):
    tile = FFN_TILE
    n_tok = x1p.shape[0]
    assert n_tok % tile == 0
    n_tiles = n_tok // tile
    const2 = lambda i: (0, 0)
    ptile = lambda i: (jnp.minimum(i, n_tiles - 1), 0)
    return pl.pallas_call(
        functools.partial(_ffn_kernel, n_prompt_tiles=n_tiles),
        name="ffn_ln2",
        grid=(n_tiles + 1,),
        in_specs=[
            pl.BlockSpec((tile, D_MODEL), ptile),
            pl.BlockSpec(x1s.shape, const2),
            pl.BlockSpec(w1.shape, const2, pipeline_mode=pl.Buffered(1)),
            pl.BlockSpec(b1.shape, const2),
            pl.BlockSpec(w2.shape, const2, pipeline_mode=pl.Buffered(1)),
            pl.BlockSpec(b2.shape, const2),
            pl.BlockSpec(ln_g.shape, const2),
            pl.BlockSpec(ln_b.shape, const2),
        ],
        out_specs=[pl.BlockSpec((tile, D_MODEL), ptile), pl.BlockSpec(x1s.shape, const2)],
        out_shape=[jax.ShapeDtypeStruct((n_tok, D_MODEL), F32), jax.ShapeDtypeStruct(x1s.shape, F32)],
        compiler_params=pltpu.CompilerParams(
            dimension_semantics=("arbitrary",),
            vmem_limit_bytes=V7X_VMEM_LIMIT),
    )(x1p, x1s, w1, b1, w2, b2, ln_g, ln_b)


def _sample_proj_kernel(x_ref, wt_ref, bg_ref, sp_ref, n_ref, m_ref, wp_ref, ps_ref,
                        ypool_ref, pool_ref, q_ref, kd_ref, v_ref, iw_ref, sv_ref, den_ref,
                        floor_ref, osig_ref, nout_ref, mout_ref):
    x = x_ref[...]
    nb = x.shape[0]
    proj = _dot_nt(x, wt_ref[0:CUT_G, :])
    u = proj[:, 0:D_POOL]
    gates = _dot_nt(wt_ref[CUT_G:CUT_G + 2 * NH, :], x) + bg_ref[...]
    ig_r = gates[0:NH]
    inter_r = jax.nn.log_sigmoid(gates[NH:2 * NH]) + m_ref[...]
    m_t_r = jnp.maximum(inter_r, ig_r)
    dw_r = jnp.exp(ig_r - m_t_r)
    iw_r = jnp.exp(inter_r - m_t_r)
    floor_r = jnp.exp(-m_t_r)
    mout_ref[...] = m_t_r

    def as_column(row):
        return jnp.broadcast_to(row, (DH, nb)).T

    for g, w in enumerate(POOL_WINDOWS):
        lo = g * POOL_GC
        u_g = u[:, lo:lo + POOL_GC]
        acc = u_g
        for r in range(POOL_PAD - w, POOL_BUF):
            acc = acc + sp_ref[r, :, lo:lo + POOL_GC]
        pooled = acc / float(min(PAST_LEN + 1, w)) - u_g
        ypool_ref[:, lo:lo + POOL_GC] = _dot(pooled, wp_ref[g]) * ps_ref[:, lo:lo + POOL_GC]
    pool_ref[0:POOL_BUF - 1] = sp_ref[1:POOL_BUF]
    pool_ref[POOL_BUF - 1] = u

    for h in range(NH):
        col = slice(h * DH, (h + 1) * DH)
        dw = as_column(dw_r[h:h + 1])
        iw = as_column(iw_r[h:h + 1])
        q = proj[:, D_POOL + h * DH:D_POOL + (h + 1) * DH]
        k = proj[:, CUT_K + h * DH:CUT_K + (h + 1) * DH] * K_SCALE
        v = proj[:, CUT_V + h * DH:CUT_V + (h + 1) * DH]
        o = proj[:, CUT_V + D_MLSTM + h * DH:CUT_V + D_MLSTM + (h + 1) * DH]
        n_old = n_ref[:, col]
        s = jnp.sum(q * k, axis=-1, keepdims=True) * dw
        q_ref[:, col] = q
        kd_ref[:, col] = dw * k
        v_ref[:, col] = v
        iw_ref[:, col] = iw
        sv_ref[:, col] = s * v
        den_ref[:, col] = s + iw * jnp.sum(q * n_old, axis=-1, keepdims=True)
        floor_ref[:, col] = as_column(floor_r[h:h + 1])
        osig_ref[:, col] = jax.nn.sigmoid(o)
        nout_ref[:, col] = iw * n_old + dw * k


def _sample_proj(x, w_in_t, bg_col, pool_rows, n2d, m_rows, w_pool, pool_scale):
    nb = x.shape[0]
    wide = jax.ShapeDtypeStruct((nb, D_MLSTM), F32)
    return pl.pallas_call(
        _sample_proj_kernel,
        name="sample_proj",
        out_shape=([wide, jax.ShapeDtypeStruct(pool_rows.shape, F32)] + [wide] * 9
                   + [jax.ShapeDtypeStruct(m_rows.shape, F32)]),
        compiler_params=pltpu.CompilerParams(vmem_limit_bytes=V7X_VMEM_LIMIT),
    )(x, w_in_t, bg_col, pool_rows, n2d, m_rows, w_pool, pool_scale)


def _sample_state_kernel(c_ref, q_ref, kd_ref, v_ref, iw_ref, cout_ref, inter_ref):
    nb = SAMPLE_BLOCK
    pad = jnp.zeros((DH - nb, DH), F32)
    for h in range(NH):
        col = slice(h * DH, (h + 1) * DH)
        q_t = jnp.concatenate([q_ref[:, col], pad], axis=0).T
        kd_t = jnp.concatenate([kd_ref[:, col], pad], axis=0).T
        for j in range(nb):
            c_old = c_ref[0, j, h]
            inter_ref[j:j + 1, col] = jnp.sum(q_t[:, j:j + 1] * c_old, axis=0, keepdims=True)
            cout_ref[0, j, h] = iw_ref[j:j + 1, col] * c_old + kd_t[:, j:j + 1] * v_ref[j:j + 1, col]


def _sample_state(state_c, q, kd, v, iw):
    nb = q.shape[0]
    blk = SAMPLE_BLOCK
    assert nb % blk == 0
    row_spec = pl.BlockSpec((blk, D_MLSTM), lambda i: (i, 0))
    c_spec = pl.BlockSpec((1, blk, NH, DH, DH), lambda i: (0, i, 0, 0, 0))
    return pl.pallas_call(
        _sample_state_kernel,
        name="sample_state",
        grid=(nb // blk,),
        in_specs=[c_spec, row_spec, row_spec, row_spec, row_spec],
        out_specs=[c_spec, row_spec],
        out_shape=[jax.ShapeDtypeStruct(state_c.shape, F32),
                   jax.ShapeDtypeStruct((nb, D_MLSTM), F32)],
        compiler_params=pltpu.CompilerParams(
            dimension_semantics=("arbitrary",),
            vmem_limit_bytes=V7X_VMEM_LIMIT),
    )(state_c, q, kd, v, iw)


def _sample_out_kernel(x_ref, ypool_ref, sv_ref, iw_ref, inter_ref, den_ref, floor_ref, osig_ref,
                       wo_ref, g1_ref, b1_ref, x1_ref):
    hh = (sv_ref[...] + iw_ref[...] * inter_ref[...]) / jnp.maximum(jnp.abs(den_ref[...]), floor_ref[...])
    mixin = jnp.concatenate([ypool_ref[...], osig_ref[...] * hh], axis=-1)
    x1_ref[...] = _layer_norm(ALPHA * x_ref[...] + _dot(mixin, wo_ref[...]), g1_ref[...], b1_ref[...])


def _sample_out(x, ypool, sv, iw, inter, den, floor, osig, w_out, ln_g, ln_b):
    return pl.pallas_call(
        _sample_out_kernel,
        name="sample_out",
        out_shape=jax.ShapeDtypeStruct(x.shape, F32),
        compiler_params=pltpu.CompilerParams(vmem_limit_bytes=V7X_VMEM_LIMIT),
    )(x, ypool, sv, iw, inter, den, floor, osig, w_out, ln_g, ln_b)


def kernel(x_prompt, x_sample, state_pool, state_C, state_n, state_m, w_in, b_gate, w_pool, pool_scale,
           w_out, ln1_g, ln1_b, w_ff1, b_ff1, w_ff2, b_ff2, ln2_g, ln2_b):
    assert w_in.shape[0] == DEPTH == 1
    bp, t_len, _ = x_prompt.shape
    bs = x_sample.shape[0]
    assert x_sample.shape[1] == 1

    w_t = jnp.transpose(w_in[0])
    bg_col = b_gate[0].reshape(2 * NH, 1)
    wp = w_pool[0]
    ps = pool_scale[0].reshape(1, D_POOL)
    wo = w_out[0]
    g1, b1 = ln1_g[0].reshape(1, D_MODEL), ln1_b[0].reshape(1, D_MODEL)
    g2, b2 = ln2_g[0].reshape(1, D_MODEL), ln2_b[0].reshape(1, D_MODEL)
    wf1, wf2 = w_ff1[0], w_ff2[0]
    bf1, bf2 = b_ff1[0].reshape(1, D_FF), b_ff2[0].reshape(1, D_MODEL)

    x1_p, pool_p, c_p, n_p, m_rep = _prompt_mixer(x_prompt, w_t, bg_col, wp, ps, wo, g1, b1)
    m_p = m_rep[:, :, 0].reshape(DEPTH, bp, NH)

    xs = x_sample.reshape(bs, D_MODEL)
    (ypool, pool_s, q_s, kd_s, v_s, iw_s, sv_s, den_s, floor_s, osig_s, n_s, m_s_rows) = _sample_proj(
        xs, w_t, bg_col, jnp.transpose(state_pool[0], (1, 0, 2)),
        state_n[0].reshape(bs, D_MLSTM), jnp.transpose(state_m[0]), wp, ps)
    c_s, inter_s = _sample_state(state_C, q_s, kd_s, v_s, iw_s)
    x1_s = _sample_out(xs, ypool, sv_s, iw_s, inter_s, den_s, floor_s, osig_s, wo, g1, b1)

    y_p, y_s = _ffn(x1_p.reshape(bp * t_len, D_MODEL), x1_s, wf1, bf1, wf2, bf2, g2, b2)

    return (y_p.reshape(bp, t_len, D_MODEL), y_s.reshape(bs, 1, D_MODEL),
            pool_p, c_p, n_p, m_p,
            jnp.transpose(pool_s, (1, 0, 2))[None], c_s,
            n_s.reshape(DEPTH, bs, NH, DH), jnp.transpose(m_s_rows)[None])
```

```python
import functools

import jax
import jax.numpy as jnp
from jax import lax
from jax.experimental import pallas as pl
from jax.experimental.pallas import tpu as pltpu

F32 = jnp.float32
BF16 = jnp.bfloat16

D_MODEL = 1024
D_POOL = D_MODEL // 2
D_MLSTM = D_MODEL - D_POOL
POOL_WINDOWS = (2, 4, 8, 16)
POOL_GC = D_POOL // len(POOL_WINDOWS)
POOL_BUF = max(POOL_WINDOWS) - 1
POOL_PAD = POOL_BUF + 1
NH = 4
DH = D_MLSTM // NH
D_FF = 4 * D_MODEL
DEPTH = 1
PAST_LEN = 16384
ALPHA = (2.0 * DEPTH) ** 0.25
LN_EPS = 1e-5
K_SCALE = DH ** -0.5

MLSTM_CHUNK = 128
MIX_TILE = 512
FFN_TILE = 512
FF_CHUNK = 1024
SAMPLE_BLOCK = 8
PROJ_COLS = 256
LANES = 128
assert POOL_GC == LANES and DH == LANES
CUT_K = D_POOL + D_MLSTM
CUT_V = CUT_K + D_MLSTM
CUT_G = D_POOL + 4 * D_MLSTM
LB_Q = D_POOL // LANES
LB_K = CUT_K // LANES
LB_V = CUT_V // LANES
LB_O = LB_V + NH
LB_G = CUT_G // LANES
N_LB = LB_G + 1
V7X_VMEM_LIMIT = 56 * 1024 * 1024


def _layer_norm(y, g, b):
    mu = jnp.mean(y, axis=-1, keepdims=True)
    yc = y - mu
    var = jnp.mean(yc * yc, axis=-1, keepdims=True)
    return yc * lax.rsqrt(var + LN_EPS) * g + b


def _dot(a, b):
    return jnp.dot(a, b, preferred_element_type=F32)


def _dot_nt(a, b):
    return lax.dot_general(a, b, (((1,), (1,)), ((), ())), preferred_element_type=F32)


def _twice(row):
    return jnp.concatenate([row, row], axis=-1)


def _mixer_kernel(xn_ref, xc_ref, wt_ref, bg_ref, wp_ref, ps_ref, wo_ref, g1_ref, b1_ref,
                  x1_ref, pool_ref, c_ref, n_ref, m_ref,
                  win_ref, wob_ref, xb_ref, pm_a, pm_b, ext_ref, mix_ref, caug_ref, mst_ref,
                  sloc_ref, vaug_ref, kv_ref, qiw_ref, floor_ref, *, tiles_per_seq):
    tq = MIX_TILE
    L = MLSTM_CHUNK
    step = pl.program_id(0)
    t_idx = lax.rem(step + tiles_per_seq - 1, tiles_per_seq)
    last_t = tiles_per_seq - 1
    parity = lax.rem(step, 2)

    @pl.when(step == 0)
    def _no_tile_yet():
        pm_b[...] = jnp.zeros(pm_b.shape, F32)
        for j in range(CUT_G // PROJ_COLS):
            win_ref[:, j * PROJ_COLS:(j + 1) * PROJ_COLS] = (
                wt_ref[j * PROJ_COLS:(j + 1) * PROJ_COLS, :].T.astype(BF16))
        gate_rows = jnp.concatenate([wt_ref[CUT_G:CUT_G + 2 * NH, :],
                                     jnp.zeros((LANES - 2 * NH, D_MODEL), F32)], axis=0)
        win_ref[:, CUT_G:CUT_G + LANES] = gate_rows.T.astype(BF16)
        wob_ref[...] = wo_ref[...].astype(BF16)

    @pl.when((t_idx == 0) | (step == 0))
    def _init():
        ext_ref[:, 0:POOL_PAD, :] = jnp.zeros((len(POOL_WINDOWS), POOL_PAD, POOL_GC), F32)
        caug_ref[...] = jnp.zeros(caug_ref.shape, F32)
        mst_ref[...] = jnp.zeros(mst_ref.shape, F32)

    tt = lax.broadcasted_iota(jnp.int32, (L, L), 0)
    ss = lax.broadcasted_iota(jnp.int32, (L, L), 1)
    causal = ss <= tt
    diag = ss == tt

    def run(pm_next, pm_cur):
        xb_ref[...] = xn_ref[0].astype(BF16)

        def piece(c0, c1):
            res = _dot(xb_ref[...], win_ref[:, c0:c1])
            for i in range((c1 - c0) // LANES):
                pm_next[c0 // LANES + i] = res[:, i * LANES:(i + 1) * LANES]

        n_cols = N_LB * LANES
        pieces = [functools.partial(piece, c0, min(c0 + PROJ_COLS, n_cols))
                  for c0 in range(0, n_cols, PROJ_COLS)]
        n_chunks = tq // L
        n_slots = len(POOL_WINDOWS) + 2 * n_chunks * NH
        slot_of = [(k * n_slots) // len(pieces) for k in range(len(pieces))]
        slot = [0]

        def next_slot():
            for k, piece in enumerate(pieces):
                if slot_of[k] == slot[0]:
                    piece()
            slot[0] += 1

        pos = t_idx * tq + lax.broadcasted_iota(jnp.int32, (tq, POOL_GC), 0)
        for g, w in enumerate(POOL_WINDOWS):
            lo = g * POOL_GC
            u_g = pm_cur[g]
            ext_ref[g, POOL_PAD:POOL_PAD + tq, :] = u_g
            s = ext_ref[g]
            k = 1
            while k < w:
                s = s + pltpu.roll(s, k, axis=0)
                k *= 2
            cnt = jnp.minimum(pos + 1, w).astype(F32)
            pooled = s[POOL_PAD:, :] / cnt - u_g
            mix_ref[:, lo:lo + POOL_GC] = (_dot(pooled, wp_ref[g]) * ps_ref[:, lo:lo + POOL_GC]).astype(BF16)
            ext_ref[g, 0:POOL_PAD, :] = ext_ref[g, tq:tq + POOL_PAD, :]
            next_slot()

        lane = lax.broadcasted_iota(jnp.int32, (NH, L), 1)
        m_prev = mst_ref[...]
        chunk_rows = []
        for c in range(n_chunks):
            rows = slice(c * L, (c + 1) * L)
            gates = pm_cur[LB_G, rows, :].T[0:2 * NH, :] + bg_ref[...]
            lf = jax.nn.log_sigmoid(gates[NH:2 * NH])
            b = lf
            k = 1
            while k < L:
                b = b + jnp.where(lane >= k, pltpu.roll(b, k, axis=1), 0.0)
                k *= 2
            a = gates[0:NH] - b
            cmax = a
            k = 1
            while k < L:
                cmax = jnp.maximum(cmax, jnp.where(lane >= k, pltpu.roll(cmax, k, axis=1), -jnp.inf))
                k *= 2
            amax = jnp.max(a, axis=-1, keepdims=True)
            big_m = jnp.maximum(cmax, m_prev)
            mm = jnp.maximum(m_prev, amax)
            chunk_rows.append(dict(
                a=a, big_m=big_m,
                iw=jnp.exp(m_prev - big_m),
                floor=jnp.exp(-(b + big_m)),
                w_loc=jnp.exp(a - amax),
                g_state=jnp.exp(m_prev - mm),
                f_state=jnp.exp(amax - mm)))
            m_prev = jnp.sum(lf, axis=-1, keepdims=True) + mm
        mst_ref[...] = m_prev

        def as_column(row):
            return jnp.broadcast_to(row, (L, L)).T

        ones_blk = jnp.ones((L, DH), BF16)
        for c in range(n_chunks):
            rows = slice(c * L, (c + 1) * L)
            cr = chunk_rows[c]
            for h in range(NH):
                i = c * NH + h
                q = pm_cur[LB_Q + h, rows, :]
                qb = q.astype(BF16)
                kt = pm_cur[LB_K + h, rows, :].T * K_SCALE
                v_aug = jnp.concatenate([pm_cur[LB_V + h, rows, :].astype(BF16), ones_blk], axis=1)
                p = jnp.exp(jnp.where(causal, cr["a"][h:h + 1], -jnp.inf) - as_column(cr["big_m"][h:h + 1]))
                sloc_ref[i] = (_dot(qb, kt.astype(BF16)) * p).astype(BF16)
                vaug_ref[i] = v_aug
                kv_ref[i] = _dot((kt * cr["w_loc"][h:h + 1]).astype(BF16), v_aug)
                qiw_ref[i] = (q * as_column(cr["iw"][h:h + 1])).astype(BF16)
                floor_ref[i] = as_column(cr["floor"][h:h + 1])
                next_slot()

        for c in range(n_chunks):
            rows = slice(c * L, (c + 1) * L)
            cr = chunk_rows[c]
            for h in range(NH):
                i = c * NH + h
                caug = caug_ref[h]
                comb = _dot(sloc_ref[i], vaug_ref[i]) + _dot(qiw_ref[i], caug.astype(BF16))
                hh = comb[:, :DH] / jnp.maximum(jnp.abs(comb[:, DH:]), floor_ref[i])
                o = pm_cur[LB_O + h, rows, :]
                mix_ref[rows, D_POOL + h * DH:D_POOL + (h + 1) * DH] = (
                    jax.nn.sigmoid(o) * hh).astype(BF16)
                caug_ref[h] = (_twice(cr["g_state"][h:h + 1]) * caug
                               + _twice(cr["f_state"][h:h + 1]) * kv_ref[i])
                next_slot()

        mix = _dot(mix_ref[...], wob_ref[...])
        x1_ref[0] = _layer_norm(ALPHA * xc_ref[0] + mix, g1_ref[...], b1_ref[...])

    @pl.when(parity == 0)
    def _even():
        run(pm_a, pm_b)

    @pl.when(parity == 1)
    def _odd():
        run(pm_b, pm_a)

    @pl.when((t_idx == last_t) & (step > 0))
    def _final_state():
        for g in range(len(POOL_WINDOWS)):
            pool_ref[0, 0, :, g * POOL_GC:(g + 1) * POOL_GC] = ext_ref[g, tq + 1:tq + POOL_PAD, :]
        for h in range(NH):
            caug = caug_ref[h]
            c_ref[0, 0, h] = caug[:, :DH]
            n_ref[0, 0, h:h + 1, :] = jnp.sum(jnp.where(diag, caug[:, DH:], 0.0),
                                              axis=0, keepdims=True)
        m_ref[0] = mst_ref[...]


def _prompt_mixer(x, w_in_t, bg_col, w_pool, pool_scale, w_out, ln_g, ln_b):
    bsz, t_len, _ = x.shape
    tq = MIX_TILE
    assert t_len % tq == 0 and tq % MLSTM_CHUNK == 0 and tq >= POOL_PAD
    nt = t_len // tq
    n_tiles = bsz * nt
    n_items = (tq // MLSTM_CHUNK) * NH
    const2 = lambda i: (0, 0)
    nxt = lambda i: jnp.minimum(i, n_tiles - 1)
    cur = lambda i: jnp.maximum(i - 1, 0)
    return pl.pallas_call(
        functools.partial(_mixer_kernel, tiles_per_seq=nt),
        name="prompt_mixer",
        grid=(n_tiles + 1,),
        in_specs=[
            pl.BlockSpec((1, tq, D_MODEL), lambda i: (nxt(i) // nt, nxt(i) % nt, 0)),
            pl.BlockSpec((1, tq, D_MODEL), lambda i: (cur(i) // nt, cur(i) % nt, 0)),
            pl.BlockSpec(w_in_t.shape, const2, pipeline_mode=pl.Buffered(1)),
            pl.BlockSpec(bg_col.shape, const2),
            pl.BlockSpec(w_pool.shape, lambda i: (0, 0, 0)),
            pl.BlockSpec(pool_scale.shape, const2),
            pl.BlockSpec(w_out.shape, const2, pipeline_mode=pl.Buffered(1)),
            pl.BlockSpec(ln_g.shape, const2),
            pl.BlockSpec(ln_b.shape, const2),
        ],
        out_specs=[
            pl.BlockSpec((1, tq, D_MODEL), lambda i: (cur(i) // nt, cur(i) % nt, 0)),
            pl.BlockSpec((1, 1, POOL_BUF, D_POOL), lambda i: (0, cur(i) // nt, 0, 0)),
            pl.BlockSpec((1, 1, NH, DH, DH), lambda i: (0, cur(i) // nt, 0, 0, 0)),
            pl.BlockSpec((1, 1, NH, DH), lambda i: (0, cur(i) // nt, 0, 0)),
            pl.BlockSpec((1, NH, MLSTM_CHUNK), lambda i: (cur(i) // nt, 0, 0)),
        ],
        out_shape=[
            jax.ShapeDtypeStruct((bsz, t_len, D_MODEL), F32),
            jax.ShapeDtypeStruct((DEPTH, bsz, POOL_BUF, D_POOL), F32),
            jax.ShapeDtypeStruct((DEPTH, bsz, NH, DH, DH), F32),
            jax.ShapeDtypeStruct((DEPTH, bsz, NH, DH), F32),
            jax.ShapeDtypeStruct((bsz, NH, MLSTM_CHUNK), F32),
        ],
        scratch_shapes=[
            pltpu.VMEM((D_MODEL, N_LB * LANES), BF16),
            pltpu.VMEM((D_MODEL, D_MODEL), BF16),
            pltpu.VMEM((tq, D_MODEL), BF16),
            pltpu.VMEM((N_LB, tq, LANES), F32),
            pltpu.VMEM((N_LB, tq, LANES), F32),
            pltpu.VMEM((len(POOL_WINDOWS), POOL_PAD + tq, POOL_GC), F32),
            pltpu.VMEM((tq, D_MODEL), BF16),
            pltpu.VMEM((NH, DH, 2 * DH), F32),
            pltpu.VMEM((NH, MLSTM_CHUNK), F32),
            pltpu.VMEM((n_items, MLSTM_CHUNK, MLSTM_CHUNK), BF16),
            pltpu.VMEM((n_items, MLSTM_CHUNK, 2 * DH), BF16),
            pltpu.VMEM((n_items, DH, 2 * DH), F32),
            pltpu.VMEM((n_items, MLSTM_CHUNK, DH), BF16),
            pltpu.VMEM((n_items, MLSTM_CHUNK, DH), F32),
        ],
        compiler_params=pltpu.CompilerParams(
            dimension_semantics=("arbitrary",),
            vmem_limit_bytes=V7X_VMEM_LIMIT),
    )(x, x, w_in_t, bg_col, w_pool, pool_scale, w_out, ln_g, ln_b)


def _ffn_kernel(x1p_ref, x1s_ref, w1_ref, b1_ref, w2_ref, b2_ref, g_ref, be_ref, yp_ref, ys_ref,
                *, n_prompt_tiles):
    def rows(x1_ref, o_ref):
        acc = None
        for c in range(D_FF // FF_CHUNK):
            cols = slice(c * FF_CHUNK, (c + 1) * FF_CHUNK)
            hid = jnp.maximum(_dot(x1_ref[...], w1_ref[:, cols]) + b1_ref[:, cols], 0.0)
            part = _dot(hid * hid, w2_ref[cols, :])
            acc = part if acc is None else acc + part
        y = ALPHA * x1_ref[...] + (acc + b2_ref[...])
        o_ref[...] = _layer_norm(y, g_ref[...], be_ref[...])

    step = pl.program_id(0)

    @pl.when(step < n_prompt_tiles)
    def _prompt():
        rows(x1p_ref, yp_ref)

    @pl.when(step == n_prompt_tiles)
    def _sample():
        rows(x1s_ref, ys_ref)


def _ffn(x1p, x1s, w1, b1, w2, b2, ln_g, ln_b):
    tile = FFN_TILE
    n_tok = x1p.shape[0]
    assert n_tok % tile == 0
    n_tiles = n_tok // tile
    const2 = lambda i: (0, 0)
    ptile = lambda i: (jnp.minimum(i, n_tiles - 1), 0)
    return pl.pallas_call(
        functools.partial(_ffn_kernel, n_prompt_tiles=n_tiles),
        name="ffn_ln2",
        grid=(n_tiles + 1,),
        in_specs=[
            pl.BlockSpec((tile, D_MODEL), ptile),
            pl.BlockSpec(x1s.shape, const2),
            pl.BlockSpec(w1.shape, const2, pipeline_mode=pl.Buffered(1)),
            pl.BlockSpec(b1.shape, const2),
            pl.BlockSpec(w2.shape, const2, pipeline_mode=pl.Buffered(1)),
            pl.BlockSpec(b2.shape, const2),
            pl.BlockSpec(ln_g.shape, const2),
            pl.BlockSpec(ln_b.shape, const2),
        ],
        out_specs=[pl.BlockSpec((tile, D_MODEL), ptile), pl.BlockSpec(x1s.shape, const2)],
        out_shape=[jax.ShapeDtypeStruct((n_tok, D_MODEL), F32), jax.ShapeDtypeStruct(x1s.shape, F32)],
        compiler_params=pltpu.CompilerParams(
            dimension_semantics=("arbitrary",),
            vmem_limit_bytes=V7X_VMEM_LIMIT),
    )(x1p, x1s, w1, b1, w2, b2, ln_g, ln_b)


def _sample_proj_kernel(x_ref, wt_ref, bg_ref, sp_ref, n_ref, m_ref, wp_ref, ps_ref,
                        ypool_ref, pool_ref, q_ref, kd_ref, v_ref, iw_ref, sv_ref, den_ref,
                        floor_ref, osig_ref, nout_ref, mout_ref):
    x = x_ref[...]
    nb = x.shape[0]
    proj = _dot_nt(x, wt_ref[0:CUT_G, :])
    u = proj[:, 0:D_POOL]
    gates = _dot_nt(wt_ref[CUT_G:CUT_G + 2 * NH, :], x) + bg_ref[...]
    ig_r = gates[0:NH]
    inter_r = jax.nn.log_sigmoid(gates[NH:2 * NH]) + m_ref[...]
    m_t_r = jnp.maximum(inter_r, ig_r)
    dw_r = jnp.exp(ig_r - m_t_r)
    iw_r = jnp.exp(inter_r - m_t_r)
    floor_r = jnp.exp(-m_t_r)
    mout_ref[...] = m_t_r

    def as_column(row):
        return jnp.broadcast_to(row, (DH, nb)).T

    for g, w in enumerate(POOL_WINDOWS):
        lo = g * POOL_GC
        u_g = u[:, lo:lo + POOL_GC]
        acc = u_g
        for r in range(POOL_PAD - w, POOL_BUF):
            acc = acc + sp_ref[r, :, lo:lo + POOL_GC]
        pooled = acc / float(min(PAST_LEN + 1, w)) - u_g
        ypool_ref[:, lo:lo + POOL_GC] = _dot(pooled, wp_ref[g]) * ps_ref[:, lo:lo + POOL_GC]
    pool_ref[0:POOL_BUF - 1] = sp_ref[1:POOL_BUF]
    pool_ref[POOL_BUF - 1] = u

    for h in range(NH):
        col = slice(h * DH, (h + 1) * DH)
        dw = as_column(dw_r[h:h + 1])
        iw = as_column(iw_r[h:h + 1])
        q = proj[:, D_POOL + h * DH:D_POOL + (h + 1) * DH]
        k = proj[:, CUT_K + h * DH:CUT_K + (h + 1) * DH] * K_SCALE
        v = proj[:, CUT_V + h * DH:CUT_V + (h + 1) * DH]
        o = proj[:, CUT_V + D_MLSTM + h * DH:CUT_V + D_MLSTM + (h + 1) * DH]
        n_old = n_ref[:, col]
        s = jnp.sum(q * k, axis=-1, keepdims=True) * dw
        q_ref[:, col] = q
        kd_ref[:, col] = dw * k
        v_ref[:, col] = v
        iw_ref[:, col] = iw
        sv_ref[:, col] = s * v
        den_ref[:, col] = s + iw * jnp.sum(q * n_old, axis=-1, keepdims=True)
        floor_ref[:, col] = as_column(floor_r[h:h + 1])
        osig_ref[:, col] = jax.nn.sigmoid(o)
        nout_ref[:, col] = iw * n_old + dw * k


def _sample_proj(x, w_in_t, bg_col, pool_rows, n2d, m_rows, w_pool, pool_scale):
    nb = x.shape[0]
    wide = jax.ShapeDtypeStruct((nb, D_MLSTM), F32)
    return pl.pallas_call(
        _sample_proj_kernel,
        name="sample_proj",
        out_shape=([wide, jax.ShapeDtypeStruct(pool_rows.shape, F32)] + [wide] * 9
                   + [jax.ShapeDtypeStruct(m_rows.shape, F32)]),
        compiler_params=pltpu.CompilerParams(vmem_limit_bytes=V7X_VMEM_LIMIT),
    )(x, w_in_t, bg_col, pool_rows, n2d, m_rows, w_pool, pool_scale)


def _sample_state_kernel(c_ref, q_ref, kd_ref, v_ref, iw_ref, cout_ref, inter_ref):
    nb = SAMPLE_BLOCK
    pad = jnp.zeros((DH - nb, DH), F32)
    for h in range(NH):
        col = slice(h * DH, (h + 1) * DH)
        q_t = jnp.concatenate([q_ref[:, col], pad], axis=0).T
        kd_t = jnp.concatenate([kd_ref[:, col], pad], axis=0).T
        for j in range(nb):
            c_old = c_ref[0, j, h]
            inter_ref[j:j + 1, col] = jnp.sum(q_t[:, j:j + 1] * c_old, axis=0, keepdims=True)
            cout_ref[0, j, h] = iw_ref[j:j + 1, col] * c_old + kd_t[:, j:j + 1] * v_ref[j:j + 1, col]


def _sample_state(state_c, q, kd, v, iw):
    nb = q.shape[0]
    blk = SAMPLE_BLOCK
    assert nb % blk == 0
    row_spec = pl.BlockSpec((blk, D_MLSTM), lambda i: (i, 0))
    c_spec = pl.BlockSpec((1, blk, NH, DH, DH), lambda i: (0, i, 0, 0, 0))
    return pl.pallas_call(
        _sample_state_kernel,
        name="sample_state",
        grid=(nb // blk,),
        in_specs=[c_spec, row_spec, row_spec, row_spec, row_spec],
        out_specs=[c_spec, row_spec],
        out_shape=[jax.ShapeDtypeStruct(state_c.shape, F32),
                   jax.ShapeDtypeStruct((nb, D_MLSTM), F32)],
        compiler_params=pltpu.CompilerParams(
            dimension_semantics=("arbitrary",),
            vmem_limit_bytes=V7X_VMEM_LIMIT),
    )(state_c, q, kd, v, iw)


def _sample_out_kernel(x_ref, ypool_ref, sv_ref, iw_ref, inter_ref, den_ref, floor_ref, osig_ref,
                       wo_ref, g1_ref, b1_ref, x1_ref):
    hh = (sv_ref[...] + iw_ref[...] * inter_ref[...]) / jnp.maximum(jnp.abs(den_ref[...]), floor_ref[...])
    mixin = jnp.concatenate([ypool_ref[...], osig_ref[...] * hh], axis=-1)
    x1_ref[...] = _layer_norm(ALPHA * x_ref[...] + _dot(mixin, wo_ref[...]), g1_ref[...], b1_ref[...])


def _sample_out(x, ypool, sv, iw, inter, den, floor, osig, w_out, ln_g, ln_b):
    return pl.pallas_call(
        _sample_out_kernel,
        name="sample_out",
        out_shape=jax.ShapeDtypeStruct(x.shape, F32),
        compiler_params=pltpu.CompilerParams(vmem_limit_bytes=V7X_VMEM_LIMIT),
    )(x, ypool, sv, iw, inter, den, floor, osig, w_out, ln_g, ln_b)


def kernel(x_prompt, x_sample, state_pool, state_C, state_n, state_m, w_in, b_gate, w_pool, pool_scale,
           w_out, ln1_g, ln1_b, w_ff1, b_ff1, w_ff2, b_ff2, ln2_g, ln2_b):
    assert w_in.shape[0] == DEPTH == 1
    bp, t_len, _ = x_prompt.shape
    bs = x_sample.shape[0]
    assert x_sample.shape[1] == 1

    w_t = jnp.transpose(w_in[0])
    bg_col = b_gate[0].reshape(2 * NH, 1)
    wp = w_pool[0]
    ps = pool_scale[0].reshape(1, D_POOL)
    wo = w_out[0]
    g1, b1 = ln1_g[0].reshape(1, D_MODEL), ln1_b[0].reshape(1, D_MODEL)
    g2, b2 = ln2_g[0].reshape(1, D_MODEL), ln2_b[0].reshape(1, D_MODEL)
    wf1, wf2 = w_ff1[0], w_ff2[0]
    bf1, bf2 = b_ff1[0].reshape(1, D_FF), b_ff2[0].reshape(1, D_MODEL)

    x1_p, pool_p, c_p, n_p, m_rep = _prompt_mixer(x_prompt, w_t, bg_col, wp, ps, wo, g1, b1)
    m_p = m_rep[:, :, 0].reshape(DEPTH, bp, NH)

    xs = x_sample.reshape(bs, D_MODEL)
    (ypool, pool_s, q_s, kd_s, v_s, iw_s, sv_s, den_s, floor_s, osig_s, n_s, m_s_rows) = _sample_proj(
        xs, w_t, bg_col, jnp.transpose(state_pool[0], (1, 0, 2)),
        state_n[0].reshape(bs, D_MLSTM), jnp.transpose(state_m[0]), wp, ps)
    c_s, inter_s = _sample_state(state_C, q_s, kd_s, v_s, iw_s)
    x1_s = _sample_out(xs, ypool, sv_s, iw_s, inter_s, den_s, floor_s, osig_s, wo, g1, b1)

    y_p, y_s = _ffn(x1_p.reshape(bp * t_len, D_MODEL), x1_s, wf1, bf1, wf2, bf2, g2, b2)

    return (y_p.reshape(bp, t_len, D_MODEL), y_s.reshape(bs, 1, D_MODEL),
            pool_p, c_p, n_p, m_p,
            jnp.transpose(pool_s, (1, 0, 2))[None], c_s,
            n_s.reshape(DEPTH, bs, NH, DH), jnp.transpose(m_s_rows)[None])
```

```python
import functools

import jax
import jax.numpy as jnp
from jax import lax
from jax.experimental import pallas as pl
from jax.experimental.pallas import tpu as pltpu

F32 = jnp.float32
BF16 = jnp.bfloat16

D_MODEL = 1024
D_POOL = D_MODEL // 2
D_MLSTM = D_MODEL - D_POOL
POOL_WINDOWS = (2, 4, 8, 16)
POOL_GC = D_POOL // len(POOL_WINDOWS)
POOL_BUF = max(POOL_WINDOWS) - 1
POOL_PAD = POOL_BUF + 1
NH = 4
DH = D_MLSTM // NH
D_FF = 4 * D_MODEL
DEPTH = 1
PAST_LEN = 16384
ALPHA = (2.0 * DEPTH) ** 0.25
LN_EPS = 1e-5
K_SCALE = DH ** -0.5

MLSTM_CHUNK = 128
MIX_TILE = 512
FFN_TILE = 1024
FF_CHUNK = 512
SAMPLE_BLOCK = 8
PROJ_COLS = 256
LANES = 128
assert POOL_GC == LANES and DH == LANES
CUT_K = D_POOL + D_MLSTM
CUT_V = CUT_K + D_MLSTM
CUT_G = D_POOL + 4 * D_MLSTM
LB_Q = D_POOL // LANES
LB_K = CUT_K // LANES
LB_V = CUT_V // LANES
LB_O = LB_V + NH
LB_G = CUT_G // LANES
N_LB = LB_G + 1
V7X_VMEM_LIMIT = 62 * 1024 * 1024


def _layer_norm(y, g, b):
    mu = jnp.mean(y, axis=-1, keepdims=True)
    yc = y - mu
    var = jnp.mean(yc * yc, axis=-1, keepdims=True)
    return yc * lax.rsqrt(var + LN_EPS) * g + b


def _dot(a, b):
    return jnp.dot(a, b, preferred_element_type=F32)


def _dot_nt(a, b):
    return lax.dot_general(a, b, (((1,), (1,)), ((), ())), preferred_element_type=F32)


def _twice(row):
    return jnp.concatenate([row, row], axis=-1)


def _mixer_kernel(xn_ref, xc_ref, wt_ref, bg_ref, wp_ref, ps_ref, wo_ref, g1_ref, b1_ref,
                  x1_ref, pool_ref, c_ref, n_ref, m_ref,
                  win_ref, wob_ref, xb_ref, pm_a, pm_b, ext_ref, mix_ref, caug_ref, mst_ref,
                  sloc_ref, vaug_ref, kv_ref, qiw_ref, floor_ref, *, tiles_per_seq):
    tq = MIX_TILE
    L = MLSTM_CHUNK
    step = pl.program_id(0)
    t_idx = lax.rem(step + tiles_per_seq - 1, tiles_per_seq)
    last_t = tiles_per_seq - 1
    parity = lax.rem(step, 2)

    @pl.when(step == 0)
    def _no_tile_yet():
        pm_b[...] = jnp.zeros(pm_b.shape, F32)
        for j in range(CUT_G // PROJ_COLS):
            win_ref[:, j * PROJ_COLS:(j + 1) * PROJ_COLS] = (
                wt_ref[j * PROJ_COLS:(j + 1) * PROJ_COLS, :].T.astype(BF16))
        gate_rows = jnp.concatenate([wt_ref[CUT_G:CUT_G + 2 * NH, :],
                                     jnp.zeros((LANES - 2 * NH, D_MODEL), F32)], axis=0)
        win_ref[:, CUT_G:CUT_G + LANES] = gate_rows.T.astype(BF16)
        wob_ref[...] = wo_ref[...].astype(BF16)

    @pl.when((t_idx == 0) | (step == 0))
    def _init():
        ext_ref[:, 0:POOL_PAD, :] = jnp.zeros((len(POOL_WINDOWS), POOL_PAD, POOL_GC), F32)
        caug_ref[...] = jnp.zeros(caug_ref.shape, F32)
        mst_ref[...] = jnp.zeros(mst_ref.shape, F32)

    tt = lax.broadcasted_iota(jnp.int32, (L, L), 0)
    ss = lax.broadcasted_iota(jnp.int32, (L, L), 1)
    causal = ss <= tt
    diag = ss == tt

    def run(pm_next, pm_cur):
        xb_ref[...] = xn_ref[0].astype(BF16)

        def piece(c0, c1):
            res = _dot(xb_ref[...], win_ref[:, c0:c1])
            for i in range((c1 - c0) // LANES):
                pm_next[c0 // LANES + i] = res[:, i * LANES:(i + 1) * LANES]

        n_cols = N_LB * LANES
        pieces = [functools.partial(piece, c0, min(c0 + PROJ_COLS, n_cols))
                  for c0 in range(0, n_cols, PROJ_COLS)]
        n_chunks = tq // L
        n_slots = len(POOL_WINDOWS) + 2 * n_chunks * NH
        slot_of = [(k * n_slots) // len(pieces) for k in range(len(pieces))]
        slot = [0]

        def next_slot():
            for k, piece in enumerate(pieces):
                if slot_of[k] == slot[0]:
                    piece()
            slot[0] += 1

        pos = t_idx * tq + lax.broadcasted_iota(jnp.int32, (tq, POOL_GC), 0)
        for g, w in enumerate(POOL_WINDOWS):
            lo = g * POOL_GC
            u_g = pm_cur[g]
            ext_ref[g, POOL_PAD:POOL_PAD + tq, :] = u_g
            s = ext_ref[g]
            k = 1
            while k < w:
                s = s + pltpu.roll(s, k, axis=0)
                k *= 2
            cnt = jnp.minimum(pos + 1, w).astype(F32)
            pooled = s[POOL_PAD:, :] / cnt - u_g
            mix_ref[:, lo:lo + POOL_GC] = (_dot(pooled, wp_ref[g]) * ps_ref[:, lo:lo + POOL_GC]).astype(BF16)
            ext_ref[g, 0:POOL_PAD, :] = ext_ref[g, tq:tq + POOL_PAD, :]
            next_slot()

        lane = lax.broadcasted_iota(jnp.int32, (NH, L), 1)
        m_prev = mst_ref[...]
        chunk_rows = []
        for c in range(n_chunks):
            rows = slice(c * L, (c + 1) * L)
            gates = pm_cur[LB_G, rows, :].T[0:2 * NH, :] + bg_ref[...]
            lf = jax.nn.log_sigmoid(gates[NH:2 * NH])
            b = lf
            k = 1
            while k < L:
                b = b + jnp.where(lane >= k, pltpu.roll(b, k, axis=1), 0.0)
                k *= 2
            a = gates[0:NH] - b
            cmax = a
            k = 1
            while k < L:
                cmax = jnp.maximum(cmax, jnp.where(lane >= k, pltpu.roll(cmax, k, axis=1), -jnp.inf))
                k *= 2
            amax = jnp.max(a, axis=-1, keepdims=True)
            big_m = jnp.maximum(cmax, m_prev)
            mm = jnp.maximum(m_prev, amax)
            chunk_rows.append(dict(
                a=a, big_m=big_m,
                iw=jnp.exp(m_prev - big_m),
                floor=jnp.exp(-(b + big_m)),
                w_loc=jnp.exp(a - amax),
                g_state=jnp.exp(m_prev - mm),
                f_state=jnp.exp(amax - mm)))
            m_prev = jnp.sum(lf, axis=-1, keepdims=True) + mm
        mst_ref[...] = m_prev

        def as_column(row):
            return jnp.broadcast_to(row, (L, L)).T

        ones_blk = jnp.ones((L, DH), BF16)
        for c in range(n_chunks):
            rows = slice(c * L, (c + 1) * L)
            cr = chunk_rows[c]
            for h in range(NH):
                i = c * NH + h
                q = pm_cur[LB_Q + h, rows, :]
                qb = q.astype(BF16)
                kt = pm_cur[LB_K + h, rows, :].T * K_SCALE
                v_aug = jnp.concatenate([pm_cur[LB_V + h, rows, :].astype(BF16), ones_blk], axis=1)
                p = jnp.exp(jnp.where(causal, cr["a"][h:h + 1], -jnp.inf) - as_column(cr["big_m"][h:h + 1]))
                sloc_ref[i] = (_dot(qb, kt.astype(BF16)) * p).astype(BF16)
                vaug_ref[i] = v_aug
                kv_ref[i] = _dot((kt * cr["w_loc"][h:h + 1]).astype(BF16), v_aug)
                qiw_ref[i] = (q * as_column(cr["iw"][h:h + 1])).astype(BF16)
                floor_ref[i] = as_column(cr["floor"][h:h + 1])
                next_slot()

        for c in range(n_chunks):
            rows = slice(c * L, (c + 1) * L)
            cr = chunk_rows[c]
            for h in range(NH):
                i = c * NH + h
                caug = caug_ref[h]
                comb = _dot(sloc_ref[i], vaug_ref[i]) + _dot(qiw_ref[i], caug.astype(BF16))
                hh = comb[:, :DH] / jnp.maximum(jnp.abs(comb[:, DH:]), floor_ref[i])
                o = pm_cur[LB_O + h, rows, :]
                mix_ref[rows, D_POOL + h * DH:D_POOL + (h + 1) * DH] = (
                    jax.nn.sigmoid(o) * hh).astype(BF16)
                caug_ref[h] = (_twice(cr["g_state"][h:h + 1]) * caug
                               + _twice(cr["f_state"][h:h + 1]) * kv_ref[i])
                next_slot()

        mix = _dot(mix_ref[...], wob_ref[...])
        x1_ref[0] = _layer_norm(ALPHA * xc_ref[0] + mix, g1_ref[...], b1_ref[...])

    @pl.when(parity == 0)
    def _even():
        run(pm_a, pm_b)

    @pl.when(parity == 1)
    def _odd():
        run(pm_b, pm_a)

    @pl.when((t_idx == last_t) & (step > 0))
    def _final_state():
        for g in range(len(POOL_WINDOWS)):
            pool_ref[0, 0, :, g * POOL_GC:(g + 1) * POOL_GC] = ext_ref[g, tq + 1:tq + POOL_PAD, :]
        for h in range(NH):
            caug = caug_ref[h]
            c_ref[0, 0, h] = caug[:, :DH]
            n_ref[0, 0, h:h + 1, :] = jnp.sum(jnp.where(diag, caug[:, DH:], 0.0),
                                              axis=0, keepdims=True)
        m_ref[0] = mst_ref[...]


def _prompt_mixer(x, w_in_t, bg_col, w_pool, pool_scale, w_out, ln_g, ln_b):
    bsz, t_len, _ = x.shape
    tq = MIX_TILE
    assert t_len % tq == 0 and tq % MLSTM_CHUNK == 0 and tq >= POOL_PAD
    nt = t_len // tq
    n_tiles = bsz * nt
    n_items = (tq // MLSTM_CHUNK) * NH
    const2 = lambda i: (0, 0)
    nxt = lambda i: jnp.minimum(i, n_tiles - 1)
    cur = lambda i: jnp.maximum(i - 1, 0)
    return pl.pallas_call(
        functools.partial(_mixer_kernel, tiles_per_seq=nt),
        name="prompt_mixer",
        grid=(n_tiles + 1,),
        in_specs=[
            pl.BlockSpec((1, tq, D_MODEL), lambda i: (nxt(i) // nt, nxt(i) % nt, 0)),
            pl.BlockSpec((1, tq, D_MODEL), lambda i: (cur(i) // nt, cur(i) % nt, 0)),
            pl.BlockSpec(w_in_t.shape, const2, pipeline_mode=pl.Buffered(1)),
            pl.BlockSpec(bg_col.shape, const2),
            pl.BlockSpec(w_pool.shape, lambda i: (0, 0, 0)),
            pl.BlockSpec(pool_scale.shape, const2),
            pl.BlockSpec(w_out.shape, const2, pipeline_mode=pl.Buffered(1)),
            pl.BlockSpec(ln_g.shape, const2),
            pl.BlockSpec(ln_b.shape, const2),
        ],
        out_specs=[
            pl.BlockSpec((1, tq, D_MODEL), lambda i: (cur(i) // nt, cur(i) % nt, 0)),
            pl.BlockSpec((1, 1, POOL_BUF, D_POOL), lambda i: (0, cur(i) // nt, 0, 0)),
            pl.BlockSpec((1, 1, NH, DH, DH), lambda i: (0, cur(i) // nt, 0, 0, 0)),
            pl.BlockSpec((1, 1, NH, DH), lambda i: (0, cur(i) // nt, 0, 0)),
            pl.BlockSpec((1, NH, MLSTM_CHUNK), lambda i: (cur(i) // nt, 0, 0)),
        ],
        out_shape=[
            jax.ShapeDtypeStruct((bsz, t_len, D_MODEL), F32),
            jax.ShapeDtypeStruct((DEPTH, bsz, POOL_BUF, D_POOL), F32),
            jax.ShapeDtypeStruct((DEPTH, bsz, NH, DH, DH), F32),
            jax.ShapeDtypeStruct((DEPTH, bsz, NH, DH), F32),
            jax.ShapeDtypeStruct((bsz, NH, MLSTM_CHUNK), F32),
        ],
        scratch_shapes=[
            pltpu.VMEM((D_MODEL, N_LB * LANES), BF16),
            pltpu.VMEM((D_MODEL, D_MODEL), BF16),
            pltpu.VMEM((tq, D_MODEL), BF16),
            pltpu.VMEM((N_LB, tq, LANES), F32),
            pltpu.VMEM((N_LB, tq, LANES), F32),
            pltpu.VMEM((len(POOL_WINDOWS), POOL_PAD + tq, POOL_GC), F32),
            pltpu.VMEM((tq, D_MODEL), BF16),
            pltpu.VMEM((NH, DH, 2 * DH), F32),
            pltpu.VMEM((NH, MLSTM_CHUNK), F32),
            pltpu.VMEM((n_items, MLSTM_CHUNK, MLSTM_CHUNK), BF16),
            pltpu.VMEM((n_items, MLSTM_CHUNK, 2 * DH), BF16),
            pltpu.VMEM((n_items, DH, 2 * DH), F32),
            pltpu.VMEM((n_items, MLSTM_CHUNK, DH), BF16),
            pltpu.VMEM((n_items, MLSTM_CHUNK, DH), F32),
        ],
        compiler_params=pltpu.CompilerParams(
            dimension_semantics=("arbitrary",),
            vmem_limit_bytes=V7X_VMEM_LIMIT),
    )(x, x, w_in_t, bg_col, w_pool, pool_scale, w_out, ln_g, ln_b)


def _ffn_kernel(x1p_ref, x1s_ref, w1_ref, b1_ref, w2_ref, b2_ref, g_ref, be_ref, yp_ref, ys_ref,
                *, n_prompt_tiles):
    def rows(x1_ref, o_ref):
        acc = None
        for c in range(D_FF // FF_CHUNK):
            cols = slice(c * FF_CHUNK, (c + 1) * FF_CHUNK)
            hid = jnp.maximum(_dot(x1_ref[...], w1_ref[:, cols]) + b1_ref[:, cols], 0.0)
            part = _dot(hid * hid, w2_ref[cols, :])
            acc = part if acc is None else acc + part
        y = ALPHA * x1_ref[...] + (acc + b2_ref[...])
        o_ref[...] = _layer_norm(y, g_ref[...], be_ref[...])

    step = pl.program_id(0)

    @pl.when(step < n_prompt_tiles)
    def _prompt():
        rows(x1p_ref, yp_ref)

    @pl.when(step == n_prompt_tiles)
    def _sample():
        rows(x1s_ref, ys_ref)


def _ffn(x1p, x1s, w1, b1, w2, b2, ln_g, ln_b):
    tile = FFN_TILE
    n_tok = x1p.shape[0]
    assert n_tok % tile == 0
    n_tiles = n_tok // tile
    const2 = lambda i: (0, 0)
    ptile = lambda i: (jnp.minimum(i, n_tiles - 1), 0)
    return pl.pallas_call(
        functools.partial(_ffn_kernel, n_prompt_tiles=n_tiles),
        name="ffn_ln2",
        grid=(n_tiles + 1,),
        in_specs=[
            pl.BlockSpec((tile, D_MODEL), ptile),
            pl.BlockSpec(x1s.shape, const2),
            pl.BlockSpec(w1.shape, const2, pipeline_mode=pl.Buffered(1)),
            pl.BlockSpec(b1.shape, const2),
            pl.BlockSpec(w2.shape, const2, pipeline_mode=pl.Buffered(1)),
            pl.BlockSpec(b2.shape, const2),
            pl.BlockSpec(ln_g.shape, const2),
            pl.BlockSpec(ln_b.shape, const2),
        ],
        out_specs=[pl.BlockSpec((tile, D_MODEL), ptile), pl.BlockSpec(x1s.shape, const2)],
        out_shape=[jax.ShapeDtypeStruct((n_tok, D_MODEL), F32), jax.ShapeDtypeStruct(x1s.shape, F32)],
        compiler_params=pltpu.CompilerParams(
            dimension_semantics=("arbitrary",),
            vmem_limit_bytes=V7X_VMEM_LIMIT),
    )(x1p, x1s, w1, b1, w2, b2, ln_g, ln_b)


def _sample_proj_kernel(x_ref, wt_ref, bg_ref, sp_ref, n_ref, m_ref, wp_ref, ps_ref,
                        ypool_ref, pool_ref, q_ref, kd_ref, v_ref, iw_ref, sv_ref, den_ref,
                        floor_ref, osig_ref, nout_ref, mout_ref):
    x = x_ref[...]
    nb = x.shape[0]
    proj = _dot_nt(x, wt_ref[0:CUT_G, :])
    u = proj[:, 0:D_POOL]
    gates = _dot_nt(wt_ref[CUT_G:CUT_G + 2 * NH, :], x) + bg_ref[...]
    ig_r = gates[0:NH]
    inter_r = jax.nn.log_sigmoid(gates[NH:2 * NH]) + m_ref[...]
    m_t_r = jnp.maximum(inter_r, ig_r)
    dw_r = jnp.exp(ig_r - m_t_r)
    iw_r = jnp.exp(inter_r - m_t_r)
    floor_r = jnp.exp(-m_t_r)
    mout_ref[...] = m_t_r

    def as_column(row):
        return jnp.broadcast_to(row, (DH, nb)).T

    for g, w in enumerate(POOL_WINDOWS):
        lo = g * POOL_GC
        u_g = u[:, lo:lo + POOL_GC]
        acc = u_g
        for r in range(POOL_PAD - w, POOL_BUF):
            acc = acc + sp_ref[r, :, lo:lo + POOL_GC]
        pooled = acc / float(min(PAST_LEN + 1, w)) - u_g
        ypool_ref[:, lo:lo + POOL_GC] = _dot(pooled, wp_ref[g]) * ps_ref[:, lo:lo + POOL_GC]
    pool_ref[0:POOL_BUF - 1] = sp_ref[1:POOL_BUF]
    pool_ref[POOL_BUF - 1] = u

    for h in range(NH):
        col = slice(h * DH, (h + 1) * DH)
        dw = as_column(dw_r[h:h + 1])
        iw = as_column(iw_r[h:h + 1])
        q = proj[:, D_POOL + h * DH:D_POOL + (h + 1) * DH]
        k = proj[:, CUT_K + h * DH:CUT_K + (h + 1) * DH] * K_SCALE
        v = proj[:, CUT_V + h * DH:CUT_V + (h + 1) * DH]
        o = proj[:, CUT_V + D_MLSTM + h * DH:CUT_V + D_MLSTM + (h + 1) * DH]
        n_old = n_ref[:, col]
        s = jnp.sum(q * k, axis=-1, keepdims=True) * dw
        q_ref[:, col] = q
        kd_ref[:, col] = dw * k
        v_ref[:, col] = v
        iw_ref[:, col] = iw
        sv_ref[:, col] = s * v
        den_ref[:, col] = s + iw * jnp.sum(q * n_old, axis=-1, keepdims=True)
        floor_ref[:, col] = as_column(floor_r[h:h + 1])
        osig_ref[:, col] = jax.nn.sigmoid(o)
        nout_ref[:, col] = iw * n_old + dw * k


def _sample_proj(x, w_in_t, bg_col, pool_rows, n2d, m_rows, w_pool, pool_scale):
    nb = x.shape[0]
    wide = jax.ShapeDtypeStruct((nb, D_MLSTM), F32)
    return pl.pallas_call(
        _sample_proj_kernel,
        name="sample_proj",
        out_shape=([wide, jax.ShapeDtypeStruct(pool_rows.shape, F32)] + [wide] * 9
                   + [jax.ShapeDtypeStruct(m_rows.shape, F32)]),
        compiler_params=pltpu.CompilerParams(vmem_limit_bytes=V7X_VMEM_LIMIT),
    )(x, w_in_t, bg_col, pool_rows, n2d, m_rows, w_pool, pool_scale)


def _sample_state_kernel(c_ref, q_ref, kd_ref, v_ref, iw_ref, cout_ref, inter_ref):
    nb = SAMPLE_BLOCK
    pad = jnp.zeros((DH - nb, DH), F32)
    for h in range(NH):
        col = slice(h * DH, (h + 1) * DH)
        q_t = jnp.concatenate([q_ref[:, col], pad], axis=0).T
        kd_t = jnp.concatenate([kd_ref[:, col], pad], axis=0).T
        for j in range(nb):
            c_old = c_ref[0, j, h]
            inter_ref[j:j + 1, col] = jnp.sum(q_t[:, j:j + 1] * c_old, axis=0, keepdims=True)
            cout_ref[0, j, h] = iw_ref[j:j + 1, col] * c_old + kd_t[:, j:j + 1] * v_ref[j:j + 1, col]


def _sample_state(state_c, q, kd, v, iw):
    nb = q.shape[0]
    blk = SAMPLE_BLOCK
    assert nb % blk == 0
    row_spec = pl.BlockSpec((blk, D_MLSTM), lambda i: (i, 0))
    c_spec = pl.BlockSpec((1, blk, NH, DH, DH), lambda i: (0, i, 0, 0, 0))
    return pl.pallas_call(
        _sample_state_kernel,
        name="sample_state",
        grid=(nb // blk,),
        in_specs=[c_spec, row_spec, row_spec, row_spec, row_spec],
        out_specs=[c_spec, row_spec],
        out_shape=[jax.ShapeDtypeStruct(state_c.shape, F32),
                   jax.ShapeDtypeStruct((nb, D_MLSTM), F32)],
        compiler_params=pltpu.CompilerParams(
            dimension_semantics=("arbitrary",),
            vmem_limit_bytes=V7X_VMEM_LIMIT),
    )(state_c, q, kd, v, iw)


def _sample_out_kernel(x_ref, ypool_ref, sv_ref, iw_ref, inter_ref, den_ref, floor_ref, osig_ref,
                       wo_ref, g1_ref, b1_ref, x1_ref):
    hh = (sv_ref[...] + iw_ref[...] * inter_ref[...]) / jnp.maximum(jnp.abs(den_ref[...]), floor_ref[...])
    mixin = jnp.concatenate([ypool_ref[...], osig_ref[...] * hh], axis=-1)
    x1_ref[...] = _layer_norm(ALPHA * x_ref[...] + _dot(mixin, wo_ref[...]), g1_ref[...], b1_ref[...])


def _sample_out(x, ypool, sv, iw, inter, den, floor, osig, w_out, ln_g, ln_b):
    return pl.pallas_call(
        _sample_out_kernel,
        name="sample_out",
        out_shape=jax.ShapeDtypeStruct(x.shape, F32),
        compiler_params=pltpu.CompilerParams(vmem_limit_bytes=V7X_VMEM_LIMIT),
    )(x, ypool, sv, iw, inter, den, floor, osig, w_out, ln_g, ln_b)


def kernel(x_prompt, x_sample, state_pool, state_C, state_n, state_m, w_in, b_gate, w_pool, pool_scale,
           w_out, ln1_g, ln1_b, w_ff1, b_ff1, w_ff2, b_ff2, ln2_g, ln2_b):
    assert w_in.shape[0] == DEPTH == 1
    bp, t_len, _ = x_prompt.shape
    bs = x_sample.shape[0]
    assert x_sample.shape[1] == 1

    w_t = jnp.transpose(w_in[0])
    bg_col = b_gate[0].reshape(2 * NH, 1)
    wp = w_pool[0]
    ps = pool_scale[0].reshape(1, D_POOL)
    wo = w_out[0]
    g1, b1 = ln1_g[0].reshape(1, D_MODEL), ln1_b[0].reshape(1, D_MODEL)
    g2, b2 = ln2_g[0].reshape(1, D_MODEL), ln2_b[0].reshape(1, D_MODEL)
    wf1, wf2 = w_ff1[0], w_ff2[0]
    bf1, bf2 = b_ff1[0].reshape(1, D_FF), b_ff2[0].reshape(1, D_MODEL)

    x1_p, pool_p, c_p, n_p, m_rep = _prompt_mixer(x_prompt, w_t, bg_col, wp, ps, wo, g1, b1)
    m_p = m_rep[:, :, 0].reshape(DEPTH, bp, NH)

    xs = x_sample.reshape(bs, D_MODEL)
    (ypool, pool_s, q_s, kd_s, v_s, iw_s, sv_s, den_s, floor_s, osig_s, n_s, m_s_rows) = _sample_proj(
        xs, w_t, bg_col, jnp.transpose(state_pool[0], (1, 0, 2)),
        state_n[0].reshape(bs, D_MLSTM), jnp.transpose(state_m[0]), wp, ps)
    c_s, inter_s = _sample_state(state_C, q_s, kd_s, v_s, iw_s)
    x1_s = _sample_out(xs, ypool, sv_s, iw_s, inter_s, den_s, floor_s, osig_s, wo, g1, b1)

    y_p, y_s = _ffn(x1_p.reshape(bp * t_len, D_MODEL), x1_s, wf1, bf1, wf2, bf2, g2, b2)

    return (y_p.reshape(bp, t_len, D_MODEL), y_s.reshape(bs, 1, D_MODEL),
            pool_p, c_p, n_p, m_p,
            jnp.transpose(pool_s, (1, 0, 2))[None], c_s,
            n_s.reshape(DEPTH, bs, NH, DH), jnp.transpose(m_s_rows)[None])
```

```python
import functools

import jax
import jax.numpy as jnp
from jax import lax
from jax.experimental import pallas as pl
from jax.experimental.pallas import tpu as pltpu

F32 = jnp.float32
BF16 = jnp.bfloat16

D_MODEL = 1024
D_POOL = D_MODEL // 2
D_MLSTM = D_MODEL - D_POOL
POOL_WINDOWS = (2, 4, 8, 16)
POOL_GC = D_POOL // len(POOL_WINDOWS)
POOL_BUF = max(POOL_WINDOWS) - 1
POOL_PAD = POOL_BUF + 1
NH = 4
DH = D_MLSTM // NH
D_FF = 4 * D_MODEL
DEPTH = 1
PAST_LEN = 16384
ALPHA = (2.0 * DEPTH) ** 0.25
LN_EPS = 1e-5
K_SCALE = DH ** -0.5

MLSTM_CHUNK = 128
MIX_TILE = 512
FFN_TILE = 512
FF_CHUNK = 1024
SAMPLE_BLOCK = 16
PROJ_COLS = 256
LANES = 128
assert POOL_GC == LANES and DH == LANES
CUT_K = D_POOL + D_MLSTM
CUT_V = CUT_K + D_MLSTM
CUT_G = D_POOL + 4 * D_MLSTM
LB_Q = D_POOL // LANES
LB_K = CUT_K // LANES
LB_V = CUT_V // LANES
LB_O = LB_V + NH
LB_G = CUT_G // LANES
N_LB = LB_G + 1
V7X_VMEM_LIMIT = 56 * 1024 * 1024


def _layer_norm(y, g, b):
    mu = jnp.mean(y, axis=-1, keepdims=True)
    yc = y - mu
    var = jnp.mean(yc * yc, axis=-1, keepdims=True)
    return yc * lax.rsqrt(var + LN_EPS) * g + b


def _dot(a, b):
    return jnp.dot(a, b, preferred_element_type=F32)


def _dot_nt(a, b):
    return lax.dot_general(a, b, (((1,), (1,)), ((), ())), preferred_element_type=F32)


def _twice(row):
    return jnp.concatenate([row, row], axis=-1)


def _mixer_kernel(xn_ref, xc_ref, wt_ref, bg_ref, wp_ref, ps_ref, wo_ref, g1_ref, b1_ref,
                  x1_ref, pool_ref, c_ref, n_ref, m_ref,
                  win_ref, wob_ref, xb_ref, pm_a, pm_b, ext_ref, mix_ref, caug_ref, mst_ref,
                  lhs_ref, vaug_ref, kv_ref, floor_ref, *, tiles_per_seq):
    tq = MIX_TILE
    L = MLSTM_CHUNK
    step = pl.program_id(0)
    t_idx = lax.rem(step + tiles_per_seq - 1, tiles_per_seq)
    last_t = tiles_per_seq - 1
    parity = lax.rem(step, 2)

    @pl.when(step == 0)
    def _no_tile_yet():
        pm_b[...] = jnp.zeros(pm_b.shape, F32)
        for j in range(CUT_G // PROJ_COLS):
            win_ref[:, j * PROJ_COLS:(j + 1) * PROJ_COLS] = (
                wt_ref[j * PROJ_COLS:(j + 1) * PROJ_COLS, :].T.astype(BF16))
        gate_rows = jnp.concatenate([wt_ref[CUT_G:CUT_G + 2 * NH, :],
                                     jnp.zeros((LANES - 2 * NH, D_MODEL), F32)], axis=0)
        win_ref[:, CUT_G:CUT_G + LANES] = gate_rows.T.astype(BF16)
        wob_ref[...] = wo_ref[...].astype(BF16)

    @pl.when((t_idx == 0) | (step == 0))
    def _init():
        ext_ref[:, 0:POOL_PAD, :] = jnp.zeros((len(POOL_WINDOWS), POOL_PAD, POOL_GC), F32)
        caug_ref[...] = jnp.zeros(caug_ref.shape, F32)
        mst_ref[...] = jnp.zeros(mst_ref.shape, F32)

    tt = lax.broadcasted_iota(jnp.int32, (L, L), 0)
    ss = lax.broadcasted_iota(jnp.int32, (L, L), 1)
    causal = ss <= tt
    diag = ss == tt

    def run(pm_next, pm_cur):
        xb_ref[...] = xn_ref[0].astype(BF16)

        def piece(c0, c1):
            res = _dot(xb_ref[...], win_ref[:, c0:c1])
            for i in range((c1 - c0) // LANES):
                pm_next[c0 // LANES + i] = res[:, i * LANES:(i + 1) * LANES]

        n_cols = N_LB * LANES
        pieces = [functools.partial(piece, c0, min(c0 + PROJ_COLS, n_cols))
                  for c0 in range(0, n_cols, PROJ_COLS)]
        n_chunks = tq // L
        n_slots = len(POOL_WINDOWS) + 2 * n_chunks * NH
        slot_of = [(k * n_slots) // len(pieces) for k in range(len(pieces))]
        slot = [0]

        def next_slot():
            for k, piece in enumerate(pieces):
                if slot_of[k] == slot[0]:
                    piece()
            slot[0] += 1

        pos = t_idx * tq + lax.broadcasted_iota(jnp.int32, (tq, POOL_GC), 0)
        for g, w in enumerate(POOL_WINDOWS):
            lo = g * POOL_GC
            u_g = pm_cur[g]
            ext_ref[g, POOL_PAD:POOL_PAD + tq, :] = u_g
            s = ext_ref[g]
            k = 1
            while k < w:
                s = s + pltpu.roll(s, k, axis=0)
                k *= 2
            cnt = jnp.minimum(pos + 1, w).astype(F32)
            pooled = s[POOL_PAD:, :] / cnt - u_g
            mix_ref[:, lo:lo + POOL_GC] = (_dot(pooled, wp_ref[g]) * ps_ref[:, lo:lo + POOL_GC]).astype(BF16)
            ext_ref[g, 0:POOL_PAD, :] = ext_ref[g, tq:tq + POOL_PAD, :]
            next_slot()

        lane = lax.broadcasted_iota(jnp.int32, (NH, L), 1)
        m_prev = mst_ref[...]
        chunk_rows = []
        for c in range(n_chunks):
            rows = slice(c * L, (c + 1) * L)
            gates = pm_cur[LB_G, rows, :].T[0:2 * NH, :] + bg_ref[...]
            lf = jax.nn.log_sigmoid(gates[NH:2 * NH])
            b = lf
            k = 1
            while k < L:
                b = b + jnp.where(lane >= k, pltpu.roll(b, k, axis=1), 0.0)
                k *= 2
            a = gates[0:NH] - b
            cmax = a
            k = 1
            while k < L:
                cmax = jnp.maximum(cmax, jnp.where(lane >= k, pltpu.roll(cmax, k, axis=1), -jnp.inf))
                k *= 2
            amax = jnp.max(a, axis=-1, keepdims=True)
            big_m = jnp.maximum(cmax, m_prev)
            mm = jnp.maximum(m_prev, amax)
            chunk_rows.append(dict(
                a=a, big_m=big_m,
                iw=jnp.exp(m_prev - big_m),
                floor=jnp.exp(-(b + big_m)),
                w_loc=jnp.exp(a - amax),
                g_state=jnp.exp(m_prev - mm),
                f_state=jnp.exp(amax - mm)))
            m_prev = jnp.sum(lf, axis=-1, keepdims=True) + mm
        mst_ref[...] = m_prev

        def as_column(row):
            return jnp.broadcast_to(row, (L, L)).T

        ones_blk = jnp.ones((L, DH), BF16)
        for c in range(n_chunks):
            rows = slice(c * L, (c + 1) * L)
            cr = chunk_rows[c]
            for h in range(NH):
                i = c * NH + h
                q = pm_cur[LB_Q + h, rows, :]
                qb = q.astype(BF16)
                kt = pm_cur[LB_K + h, rows, :].T * K_SCALE
                v_aug = jnp.concatenate([pm_cur[LB_V + h, rows, :].astype(BF16), ones_blk], axis=1)
                p = jnp.exp(jnp.where(causal, cr["a"][h:h + 1], -jnp.inf) - as_column(cr["big_m"][h:h + 1]))
                lhs_ref[i, :, 0:L] = (_dot(qb, kt.astype(BF16)) * p).astype(BF16)
                vaug_ref[i] = v_aug
                kv_ref[i] = _dot((kt * cr["w_loc"][h:h + 1]).astype(BF16), v_aug)
                lhs_ref[i, :, L:L + DH] = (q * as_column(cr["iw"][h:h + 1])).astype(BF16)
                floor_ref[i] = as_column(cr["floor"][h:h + 1])
                next_slot()

        for c in range(n_chunks):
            rows = slice(c * L, (c + 1) * L)
            cr = chunk_rows[c]
            for h in range(NH):
                i = c * NH + h
                caug = caug_ref[h]
                comb = _dot(lhs_ref[i], jnp.concatenate([vaug_ref[i], caug.astype(BF16)], axis=0))
                hh = comb[:, :DH] / jnp.maximum(jnp.abs(comb[:, DH:]), floor_ref[i])
                o = pm_cur[LB_O + h, rows, :]
                mix_ref[rows, D_POOL + h * DH:D_POOL + (h + 1) * DH] = (
                    jax.nn.sigmoid(o) * hh).astype(BF16)
                caug_ref[h] = (_twice(cr["g_state"][h:h + 1]) * caug
                               + _twice(cr["f_state"][h:h + 1]) * kv_ref[i])
                next_slot()

        mix = _dot(mix_ref[...], wob_ref[...])
        x1_ref[0] = _layer_norm(ALPHA * xc_ref[0] + mix, g1_ref[...], b1_ref[...])

    @pl.when(parity == 0)
    def _even():
        run(pm_a, pm_b)

    @pl.when(parity == 1)
    def _odd():
        run(pm_b, pm_a)

    @pl.when((t_idx == last_t) & (step > 0))
    def _final_state():
        for g in range(len(POOL_WINDOWS)):
            pool_ref[0, 0, :, g * POOL_GC:(g + 1) * POOL_GC] = ext_ref[g, tq + 1:tq + POOL_PAD, :]
        for h in range(NH):
            caug = caug_ref[h]
            c_ref[0, 0, h] = caug[:, :DH]
            n_ref[0, 0, h:h + 1, :] = jnp.sum(jnp.where(diag, caug[:, DH:], 0.0),
                                              axis=0, keepdims=True)
        m_ref[0] = mst_ref[...]


def _prompt_mixer(x, w_in_t, bg_col, w_pool, pool_scale, w_out, ln_g, ln_b):
    bsz, t_len, _ = x.shape
    tq = MIX_TILE
    assert t_len % tq == 0 and tq % MLSTM_CHUNK == 0 and tq >= POOL_PAD
    nt = t_len // tq
    n_tiles = bsz * nt
    n_items = (tq // MLSTM_CHUNK) * NH
    const2 = lambda i: (0, 0)
    nxt = lambda i: jnp.minimum(i, n_tiles - 1)
    cur = lambda i: jnp.maximum(i - 1, 0)
    return pl.pallas_call(
        functools.partial(_mixer_kernel, tiles_per_seq=nt),
        name="prompt_mixer",
        grid=(n_tiles + 1,),
        in_specs=[
            pl.BlockSpec((1, tq, D_MODEL), lambda i: (nxt(i) // nt, nxt(i) % nt, 0)),
            pl.BlockSpec((1, tq, D_MODEL), lambda i: (cur(i) // nt, cur(i) % nt, 0)),
            pl.BlockSpec(w_in_t.shape, const2, pipeline_mode=pl.Buffered(1)),
            pl.BlockSpec(bg_col.shape, const2),
            pl.BlockSpec(w_pool.shape, lambda i: (0, 0, 0)),
            pl.BlockSpec(pool_scale.shape, const2),
            pl.BlockSpec(w_out.shape, const2, pipeline_mode=pl.Buffered(1)),
            pl.BlockSpec(ln_g.shape, const2),
            pl.BlockSpec(ln_b.shape, const2),
        ],
        out_specs=[
            pl.BlockSpec((1, tq, D_MODEL), lambda i: (cur(i) // nt, cur(i) % nt, 0)),
            pl.BlockSpec((1, 1, POOL_BUF, D_POOL), lambda i: (0, cur(i) // nt, 0, 0)),
            pl.BlockSpec((1, 1, NH, DH, DH), lambda i: (0, cur(i) // nt, 0, 0, 0)),
            pl.BlockSpec((1, 1, NH, DH), lambda i: (0, cur(i) // nt, 0, 0)),
            pl.BlockSpec((1, NH, MLSTM_CHUNK), lambda i: (cur(i) // nt, 0, 0)),
        ],
        out_shape=[
            jax.ShapeDtypeStruct((bsz, t_len, D_MODEL), F32),
            jax.ShapeDtypeStruct((DEPTH, bsz, POOL_BUF, D_POOL), F32),
            jax.ShapeDtypeStruct((DEPTH, bsz, NH, DH, DH), F32),
            jax.ShapeDtypeStruct((DEPTH, bsz, NH, DH), F32),
            jax.ShapeDtypeStruct((bsz, NH, MLSTM_CHUNK), F32),
        ],
        scratch_shapes=[
            pltpu.VMEM((D_MODEL, N_LB * LANES), BF16),
            pltpu.VMEM((D_MODEL, D_MODEL), BF16),
            pltpu.VMEM((tq, D_MODEL), BF16),
            pltpu.VMEM((N_LB, tq, LANES), F32),
            pltpu.VMEM((N_LB, tq, LANES), F32),
            pltpu.VMEM((len(POOL_WINDOWS), POOL_PAD + tq, POOL_GC), F32),
            pltpu.VMEM((tq, D_MODEL), BF16),
            pltpu.VMEM((NH, DH, 2 * DH), F32),
            pltpu.VMEM((NH, MLSTM_CHUNK), F32),
            pltpu.VMEM((n_items, MLSTM_CHUNK, MLSTM_CHUNK + DH), BF16),
            pltpu.VMEM((n_items, MLSTM_CHUNK, 2 * DH), BF16),
            pltpu.VMEM((n_items, DH, 2 * DH), F32),
            pltpu.VMEM((n_items, MLSTM_CHUNK, DH), F32),
        ],
        compiler_params=pltpu.CompilerParams(
            dimension_semantics=("arbitrary",),
            vmem_limit_bytes=V7X_VMEM_LIMIT),
    )(x, x, w_in_t, bg_col, w_pool, pool_scale, w_out, ln_g, ln_b)


def _ffn_kernel(x1p_ref, x1s_ref, w1_ref, b1_ref, w2_ref, b2_ref, g_ref, be_ref, yp_ref, ys_ref,
                *, n_prompt_tiles):
    def rows(x1_ref, o_ref):
        acc = None
        for c in range(D_FF // FF_CHUNK):
            cols = slice(c * FF_CHUNK, (c + 1) * FF_CHUNK)
            hid = jnp.maximum(_dot(x1_ref[...], w1_ref[:, cols]) + b1_ref[:, cols], 0.0)
            part = _dot(hid * hid, w2_ref[cols, :])
            acc = part if acc is None else acc + part
        y = ALPHA * x1_ref[...] + (acc + b2_ref[...])
        o_ref[...] = _layer_norm(y, g_ref[...], be_ref[...])

    step = pl.program_id(0)

    @pl.when(step < n_prompt_tiles)
    def _prompt():
        rows(x1p_ref, yp_ref)

    @pl.when(step == n_prompt_tiles)
    def _sample():
        rows(x1s_ref, ys_ref)


def _ffn(x1p, x1s, w1, b1, w2, b2, ln_g, ln_b):
    tile = FFN_TILE
    n_tok = x1p.shape[0]
    assert n_tok % tile == 0
    n_tiles = n_tok // tile
    const2 = lambda i: (0, 0)
    ptile = lambda i: (jnp.minimum(i, n_tiles - 1), 0)
    return pl.pallas_call(
        functools.partial(_ffn_kernel, n_prompt_tiles=n_tiles),
        name="ffn_ln2",
        grid=(n_tiles + 1,),
        in_specs=[
            pl.BlockSpec((tile, D_MODEL), ptile),
            pl.BlockSpec(x1s.shape, const2),
            pl.BlockSpec(w1.shape, const2, pipeline_mode=pl.Buffered(1)),
            pl.BlockSpec(b1.shape, const2),
            pl.BlockSpec(w2.shape, const2, pipeline_mode=pl.Buffered(1)),
            pl.BlockSpec(b2.shape, const2),
            pl.BlockSpec(ln_g.shape, const2),
            pl.BlockSpec(ln_b.shape, const2),
        ],
        out_specs=[pl.BlockSpec((tile, D_MODEL), ptile), pl.BlockSpec(x1s.shape, const2)],
        out_shape=[jax.ShapeDtypeStruct((n_tok, D_MODEL), F32), jax.ShapeDtypeStruct(x1s.shape, F32)],
        compiler_params=pltpu.CompilerParams(
            dimension_semantics=("arbitrary",),
            vmem_limit_bytes=V7X_VMEM_LIMIT),
    )(x1p, x1s, w1, b1, w2, b2, ln_g, ln_b)


def _sample_proj_kernel(x_ref, wt_ref, bg_ref, sp_ref, n_ref, m_ref, wp_ref, ps_ref,
                        ypool_ref, pool_ref, q_ref, kd_ref, v_ref, iw_ref, sv_ref, den_ref,
                        floor_ref, osig_ref, nout_ref, mout_ref):
    x = x_ref[...]
    nb = x.shape[0]
    proj = _dot_nt(x, wt_ref[0:CUT_G, :])
    u = proj[:, 0:D_POOL]
    gates = _dot_nt(wt_ref[CUT_G:CUT_G + 2 * NH, :], x) + bg_ref[...]
    ig_r = gates[0:NH]
    inter_r = jax.nn.log_sigmoid(gates[NH:2 * NH]) + m_ref[...]
    m_t_r = jnp.maximum(inter_r, ig_r)
    dw_r = jnp.exp(ig_r - m_t_r)
    iw_r = jnp.exp(inter_r - m_t_r)
    floor_r = jnp.exp(-m_t_r)
    mout_ref[...] = m_t_r

    def as_column(row):
        return jnp.broadcast_to(row, (DH, nb)).T

    for g, w in enumerate(POOL_WINDOWS):
        lo = g * POOL_GC
        u_g = u[:, lo:lo + POOL_GC]
        acc = u_g
        for r in range(POOL_PAD - w, POOL_BUF):
            acc = acc + sp_ref[r, :, lo:lo + POOL_GC]
        pooled = acc / float(min(PAST_LEN + 1, w)) - u_g
        ypool_ref[:, lo:lo + POOL_GC] = _dot(pooled, wp_ref[g]) * ps_ref[:, lo:lo + POOL_GC]
    pool_ref[0:POOL_BUF - 1] = sp_ref[1:POOL_BUF]
    pool_ref[POOL_BUF - 1] = u

    for h in range(NH):
        col = slice(h * DH, (h + 1) * DH)
        dw = as_column(dw_r[h:h + 1])
        iw = as_column(iw_r[h:h + 1])
        q = proj[:, D_POOL + h * DH:D_POOL + (h + 1) * DH]
        k = proj[:, CUT_K + h * DH:CUT_K + (h + 1) * DH] * K_SCALE
        v = proj[:, CUT_V + h * DH:CUT_V + (h + 1) * DH]
        o = proj[:, CUT_V + D_MLSTM + h * DH:CUT_V + D_MLSTM + (h + 1) * DH]
        n_old = n_ref[:, col]
        s = jnp.sum(q * k, axis=-1, keepdims=True) * dw
        q_ref[:, col] = q
        kd_ref[:, col] = dw * k
        v_ref[:, col] = v
        iw_ref[:, col] = iw
        sv_ref[:, col] = s * v
        den_ref[:, col] = s + iw * jnp.sum(q * n_old, axis=-1, keepdims=True)
        floor_ref[:, col] = as_column(floor_r[h:h + 1])
        osig_ref[:, col] = jax.nn.sigmoid(o)
        nout_ref[:, col] = iw * n_old + dw * k


def _sample_proj(x, w_in_t, bg_col, pool_rows, n2d, m_rows, w_pool, pool_scale):
    nb = x.shape[0]
    wide = jax.ShapeDtypeStruct((nb, D_MLSTM), F32)
    return pl.pallas_call(
        _sample_proj_kernel,
        name="sample_proj",
        out_shape=([wide, jax.ShapeDtypeStruct(pool_rows.shape, F32)] + [wide] * 9
                   + [jax.ShapeDtypeStruct(m_rows.shape, F32)]),
        compiler_params=pltpu.CompilerParams(vmem_limit_bytes=V7X_VMEM_LIMIT),
    )(x, w_in_t, bg_col, pool_rows, n2d, m_rows, w_pool, pool_scale)


def _sample_state_kernel(c_ref, q_ref, kd_ref, v_ref, iw_ref, cout_ref, inter_ref):
    nb = SAMPLE_BLOCK
    pad = jnp.zeros((DH - nb, DH), F32)
    for h in range(NH):
        col = slice(h * DH, (h + 1) * DH)
        q_t = jnp.concatenate([q_ref[:, col], pad], axis=0).T
        kd_t = jnp.concatenate([kd_ref[:, col], pad], axis=0).T
        for j in range(nb):
            c_old = c_ref[0, j, h]
            inter_ref[j:j + 1, col] = jnp.sum(q_t[:, j:j + 1] * c_old, axis=0, keepdims=True)
            cout_ref[0, j, h] = iw_ref[j:j + 1, col] * c_old + kd_t[:, j:j + 1] * v_ref[j:j + 1, col]


def _sample_state(state_c, q, kd, v, iw):
    nb = q.shape[0]
    blk = SAMPLE_BLOCK
    assert nb % blk == 0
    row_spec = pl.BlockSpec((blk, D_MLSTM), lambda i: (i, 0))
    c_spec = pl.BlockSpec((1, blk, NH, DH, DH), lambda i: (0, i, 0, 0, 0))
    return pl.pallas_call(
        _sample_state_kernel,
        name="sample_state",
        grid=(nb // blk,),
        in_specs=[c_spec, row_spec, row_spec, row_spec, row_spec],
        out_specs=[c_spec, row_spec],
        out_shape=[jax.ShapeDtypeStruct(state_c.shape, F32),
                   jax.ShapeDtypeStruct((nb, D_MLSTM), F32)],
        compiler_params=pltpu.CompilerParams(
            dimension_semantics=("arbitrary",),
            vmem_limit_bytes=V7X_VMEM_LIMIT),
    )(state_c, q, kd, v, iw)


def _sample_out_kernel(x_ref, ypool_ref, sv_ref, iw_ref, inter_ref, den_ref, floor_ref, osig_ref,
                       wo_ref, g1_ref, b1_ref, x1_ref):
    hh = (sv_ref[...] + iw_ref[...] * inter_ref[...]) / jnp.maximum(jnp.abs(den_ref[...]), floor_ref[...])
    mixin = jnp.concatenate([ypool_ref[...], osig_ref[...] * hh], axis=-1)
    x1_ref[...] = _layer_norm(ALPHA * x_ref[...] + _dot(mixin, wo_ref[...]), g1_ref[...], b1_ref[...])


def _sample_out(x, ypool, sv, iw, inter, den, floor, osig, w_out, ln_g, ln_b):
    return pl.pallas_call(
        _sample_out_kernel,
        name="sample_out",
        out_shape=jax.ShapeDtypeStruct(x.shape, F32),
        compiler_params=pltpu.CompilerParams(vmem_limit_bytes=V7X_VMEM_LIMIT),
    )(x, ypool, sv, iw, inter, den, floor, osig, w_out, ln_g, ln_b)


def kernel(x_prompt, x_sample, state_pool, state_C, state_n, state_m, w_in, b_gate, w_pool, pool_scale,
           w_out, ln1_g, ln1_b, w_ff1, b_ff1, w_ff2, b_ff2, ln2_g, ln2_b):
    assert w_in.shape[0] == DEPTH == 1
    bp, t_len, _ = x_prompt.shape
    bs = x_sample.shape[0]
    assert x_sample.shape[1] == 1

    w_t = jnp.transpose(w_in[0])
    bg_col = b_gate[0].reshape(2 * NH, 1)
    wp = w_pool[0]
    ps = pool_scale[0].reshape(1, D_POOL)
    wo = w_out[0]
    g1, b1 = ln1_g[0].reshape(1, D_MODEL), ln1_b[0].reshape(1, D_MODEL)
    g2, b2 = ln2_g[0].reshape(1, D_MODEL), ln2_b[0].reshape(1, D_MODEL)
    wf1, wf2 = w_ff1[0], w_ff2[0]
    bf1, bf2 = b_ff1[0].reshape(1, D_FF), b_ff2[0].reshape(1, D_MODEL)

    x1_p, pool_p, c_p, n_p, m_rep = _prompt_mixer(x_prompt, w_t, bg_col, wp, ps, wo, g1, b1)
    m_p = m_rep[:, :, 0].reshape(DEPTH, bp, NH)

    xs = x_sample.reshape(bs, D_MODEL)
    (ypool, pool_s, q_s, kd_s, v_s, iw_s, sv_s, den_s, floor_s, osig_s, n_s, m_s_rows) = _sample_proj(
        xs, w_t, bg_col, jnp.transpose(state_pool[0], (1, 0, 2)),
        state_n[0].reshape(bs, D_MLSTM), jnp.transpose(state_m[0]), wp, ps)
    c_s, inter_s = _sample_state(state_C, q_s, kd_s, v_s, iw_s)
    x1_s = _sample_out(xs, ypool, sv_s, iw_s, inter_s, den_s, floor_s, osig_s, wo, g1, b1)

    y_p, y_s = _ffn(x1_p.reshape(bp * t_len, D_MODEL), x1_s, wf1, bf1, wf2, bf2, g2, b2)

    return (y_p.reshape(bp, t_len, D_MODEL), y_s.reshape(bs, 1, D_MODEL),
            pool_p, c_p, n_p, m_p,
            jnp.transpose(pool_s, (1, 0, 2))[None], c_s,
            n_s.reshape(DEPTH, bs, NH, DH), jnp.transpose(m_s_rows)[None])
```

```python
import functools

import jax
import jax.numpy as jnp
from jax import lax
from jax.experimental import pallas as pl
from jax.experimental.pallas import tpu as pltpu

F32 = jnp.float32
BF16 = jnp.bfloat16

D_MODEL = 1024
D_POOL = D_MODEL // 2
D_MLSTM = D_MODEL - D_POOL
POOL_WINDOWS = (2, 4, 8, 16)
POOL_GC = D_POOL // len(POOL_WINDOWS)
POOL_BUF = max(POOL_WINDOWS) - 1
POOL_PAD = POOL_BUF + 1
NH = 4
DH = D_MLSTM // NH
D_FF = 4 * D_MODEL
DEPTH = 1
PAST_LEN = 16384
ALPHA = (2.0 * DEPTH) ** 0.25
LN_EPS = 1e-5
K_SCALE = DH ** -0.5

MLSTM_CHUNK = 128
MIX_TILE = 512
FFN_TILE = 512
FF_CHUNK = 1024
SAMPLE_BLOCK = 16
PROJ_COLS = 256
LANES = 128
assert POOL_GC == LANES and DH == LANES
CUT_K = D_POOL + D_MLSTM
CUT_V = CUT_K + D_MLSTM
CUT_G = D_POOL + 4 * D_MLSTM
LB_Q = D_POOL // LANES
LB_K = CUT_K // LANES
LB_V = CUT_V // LANES
LB_O = LB_V + NH
LB_G = CUT_G // LANES
N_LB = LB_G + 1
V7X_VMEM_LIMIT = 56 * 1024 * 1024


def _layer_norm(y, g, b):
    mu = jnp.mean(y, axis=-1, keepdims=True)
    yc = y - mu
    var = jnp.mean(yc * yc, axis=-1, keepdims=True)
    return yc * lax.rsqrt(var + LN_EPS) * g + b


def _dot(a, b):
    return jnp.dot(a, b, preferred_element_type=F32)


def _dot_nt(a, b):
    return lax.dot_general(a, b, (((1,), (1,)), ((), ())), preferred_element_type=F32)


def _twice(row):
    return jnp.concatenate([row, row], axis=-1)


def _mixer_kernel(xn_ref, xc_ref, wt_ref, bg_ref, wp_ref, ps_ref, wo_ref, g1_ref, b1_ref,
                  x1_ref, pool_ref, c_ref, n_ref, m_ref,
                  win_ref, wob_ref, xb_ref, pm_a, pm_b, ext_ref, mix_ref, caug_ref, mst_ref,
                  lhs_ref, vaug_ref, kv_ref, floor_ref, *, tiles_per_seq):
    tq = MIX_TILE
    L = MLSTM_CHUNK
    step = pl.program_id(0)
    t_idx = lax.rem(step + tiles_per_seq - 1, tiles_per_seq)
    last_t = tiles_per_seq - 1
    parity = lax.rem(step, 2)

    @pl.when(step == 0)
    def _no_tile_yet():
        pm_b[...] = jnp.zeros(pm_b.shape, F32)
        for j in range(CUT_G // PROJ_COLS):
            win_ref[:, j * PROJ_COLS:(j + 1) * PROJ_COLS] = (
                wt_ref[j * PROJ_COLS:(j + 1) * PROJ_COLS, :].T.astype(BF16))
        gate_rows = jnp.concatenate([wt_ref[CUT_G:CUT_G + 2 * NH, :],
                                     jnp.zeros((LANES - 2 * NH, D_MODEL), F32)], axis=0)
        win_ref[:, CUT_G:CUT_G + LANES] = gate_rows.T.astype(BF16)
        wob_ref[...] = wo_ref[...].astype(BF16)

    @pl.when((t_idx == 0) | (step == 0))
    def _init():
        ext_ref[:, 0:POOL_PAD, :] = jnp.zeros((len(POOL_WINDOWS), POOL_PAD, POOL_GC), F32)
        caug_ref[...] = jnp.zeros(caug_ref.shape, F32)
        mst_ref[...] = jnp.zeros(mst_ref.shape, F32)

    tt = lax.broadcasted_iota(jnp.int32, (L, L), 0)
    ss = lax.broadcasted_iota(jnp.int32, (L, L), 1)
    causal = ss <= tt
    diag = ss == tt

    def run(pm_next, pm_cur):
        xb_ref[...] = xn_ref[0].astype(BF16)

        def piece(c0, c1):
            res = _dot(xb_ref[...], win_ref[:, c0:c1])
            for i in range((c1 - c0) // LANES):
                pm_next[c0 // LANES + i] = res[:, i * LANES:(i + 1) * LANES]

        n_cols = N_LB * LANES
        pieces = [functools.partial(piece, c0, min(c0 + PROJ_COLS, n_cols))
                  for c0 in range(0, n_cols, PROJ_COLS)]
        n_chunks = tq // L
        n_slots = len(POOL_WINDOWS) + 2 * n_chunks * NH
        slot_of = [(k * n_slots) // len(pieces) for k in range(len(pieces))]
        slot = [0]

        def next_slot():
            for k, piece in enumerate(pieces):
                if slot_of[k] == slot[0]:
                    piece()
            slot[0] += 1

        pos = t_idx * tq + lax.broadcasted_iota(jnp.int32, (tq, POOL_GC), 0)
        for g, w in enumerate(POOL_WINDOWS):
            lo = g * POOL_GC
            u_g = pm_cur[g]
            ext_ref[g, POOL_PAD:POOL_PAD + tq, :] = u_g
            s = ext_ref[g]
            k = 1
            while k < w:
                s = s + pltpu.roll(s, k, axis=0)
                k *= 2
            cnt = jnp.minimum(pos + 1, w).astype(F32)
            pooled = s[POOL_PAD:, :] / cnt - u_g
            mix_ref[:, lo:lo + POOL_GC] = (_dot(pooled, wp_ref[g]) * ps_ref[:, lo:lo + POOL_GC]).astype(BF16)
            ext_ref[g, 0:POOL_PAD, :] = ext_ref[g, tq:tq + POOL_PAD, :]
            next_slot()

        lane = lax.broadcasted_iota(jnp.int32, (NH, L), 1)
        m_prev = mst_ref[...]
        chunk_rows = []
        for c in range(n_chunks):
            rows = slice(c * L, (c + 1) * L)
            gates = pm_cur[LB_G, rows, :].T[0:2 * NH, :] + bg_ref[...]
            lf = jax.nn.log_sigmoid(gates[NH:2 * NH])
            b = lf
            k = 1
            while k < L:
                b = b + jnp.where(lane >= k, pltpu.roll(b, k, axis=1), 0.0)
                k *= 2
            a = gates[0:NH] - b
            cmax = a
            k = 1
            while k < L:
                cmax = jnp.maximum(cmax, jnp.where(lane >= k, pltpu.roll(cmax, k, axis=1), -jnp.inf))
                k *= 2
            amax = jnp.max(a, axis=-1, keepdims=True)
            big_m = jnp.maximum(cmax, m_prev)
            mm = jnp.maximum(m_prev, amax)
            chunk_rows.append(dict(
                a=a, big_m=big_m,
                iw=jnp.exp(m_prev - big_m),
                floor=jnp.exp(-(b + big_m)),
                w_loc=jnp.exp(a - amax),
                g_state=jnp.exp(m_prev - mm),
                f_state=jnp.exp(amax - mm)))
            m_prev = jnp.sum(lf, axis=-1, keepdims=True) + mm
        mst_ref[...] = m_prev

        def as_column(row):
            return jnp.broadcast_to(row, (L, L)).T

        ones_blk = jnp.ones((L, DH), BF16)
        for c in range(n_chunks):
            rows = slice(c * L, (c + 1) * L)
            cr = chunk_rows[c]
            for h in range(NH):
                i = c * NH + h
                q = pm_cur[LB_Q + h, rows, :]
                qb = q.astype(BF16)
                kt = pm_cur[LB_K + h, rows, :].T * K_SCALE
                v_aug = jnp.concatenate([pm_cur[LB_V + h, rows, :].astype(BF16), ones_blk], axis=1)
                p = jnp.exp(jnp.where(causal, cr["a"][h:h + 1], -jnp.inf) - as_column(cr["big_m"][h:h + 1]))
                lhs_ref[i, :, 0:L] = (_dot(qb, kt.astype(BF16)) * p).astype(BF16)
                vaug_ref[i] = v_aug
                kv_ref[i] = _dot((kt * cr["w_loc"][h:h + 1]).astype(BF16), v_aug)
                lhs_ref[i, :, L:L + DH] = (q * as_column(cr["iw"][h:h + 1])).astype(BF16)
                floor_ref[i] = as_column(cr["floor"][h:h + 1])
                next_slot()

        for c in range(n_chunks):
            rows = slice(c * L, (c + 1) * L)
            cr = chunk_rows[c]
            for h in range(NH):
                i = c * NH + h
                caug = caug_ref[h]
                comb = _dot(lhs_ref[i], jnp.concatenate([vaug_ref[i], caug.astype(BF16)], axis=0))
                hh = comb[:, :DH] / jnp.maximum(jnp.abs(comb[:, DH:]), floor_ref[i])
                o = pm_cur[LB_O + h, rows, :]
                mix_ref[rows, D_POOL + h * DH:D_POOL + (h + 1) * DH] = (
                    jax.nn.sigmoid(o) * hh).astype(BF16)
                caug_ref[h] = (_twice(cr["g_state"][h:h + 1]) * caug
                               + _twice(cr["f_state"][h:h + 1]) * kv_ref[i])
                next_slot()

        mix = _dot(mix_ref[...], wob_ref[...])
        x1_ref[0] = _layer_norm(ALPHA * xc_ref[0] + mix, g1_ref[...], b1_ref[...])

    @pl.when(parity == 0)
    def _even():
        run(pm_a, pm_b)

    @pl.when(parity == 1)
    def _odd():
        run(pm_b, pm_a)

    @pl.when((t_idx == last_t) & (step > 0))
    def _final_state():
        for g in range(len(POOL_WINDOWS)):
            pool_ref[0, 0, :, g * POOL_GC:(g + 1) * POOL_GC] = ext_ref[g, tq + 1:tq + POOL_PAD, :]
        for h in range(NH):
            caug = caug_ref[h]
            c_ref[0, 0, h] = caug[:, :DH]
            n_ref[0, 0, h:h + 1, :] = jnp.sum(jnp.where(diag, caug[:, DH:], 0.0),
                                              axis=0, keepdims=True)
        m_ref[0] = mst_ref[...]


def _prompt_mixer(x, w_in_t, bg_col, w_pool, pool_scale, w_out, ln_g, ln_b):
    bsz, t_len, _ = x.shape
    tq = MIX_TILE
    assert t_len % tq == 0 and tq % MLSTM_CHUNK == 0 and tq >= POOL_PAD
    nt = t_len // tq
    n_tiles = bsz * nt
    n_items = (tq // MLSTM_CHUNK) * NH
    const2 = lambda i: (0, 0)
    nxt = lambda i: jnp.minimum(i, n_tiles - 1)
    cur = lambda i: jnp.maximum(i - 1, 0)
    return pl.pallas_call(
        functools.partial(_mixer_kernel, tiles_per_seq=nt),
        name="prompt_mixer",
        grid=(n_tiles + 1,),
        in_specs=[
            pl.BlockSpec((1, tq, D_MODEL), lambda i: (nxt(i) // nt, nxt(i) % nt, 0)),
            pl.BlockSpec((1, tq, D_MODEL), lambda i: (cur(i) // nt, cur(i) % nt, 0)),
            pl.BlockSpec(w_in_t.shape, const2, pipeline_mode=pl.Buffered(1)),
            pl.BlockSpec(bg_col.shape, const2),
            pl.BlockSpec(w_pool.shape, lambda i: (0, 0, 0)),
            pl.BlockSpec(pool_scale.shape, const2),
            pl.BlockSpec(w_out.shape, const2, pipeline_mode=pl.Buffered(1)),
            pl.BlockSpec(ln_g.shape, const2),
            pl.BlockSpec(ln_b.shape, const2),
        ],
        out_specs=[
            pl.BlockSpec((1, tq, D_MODEL), lambda i: (cur(i) // nt, cur(i) % nt, 0)),
            pl.BlockSpec((1, 1, POOL_BUF, D_POOL), lambda i: (0, cur(i) // nt, 0, 0)),
            pl.BlockSpec((1, 1, NH, DH, DH), lambda i: (0, cur(i) // nt, 0, 0, 0)),
            pl.BlockSpec((1, 1, NH, DH), lambda i: (0, cur(i) // nt, 0, 0)),
            pl.BlockSpec((1, NH, MLSTM_CHUNK), lambda i: (cur(i) // nt, 0, 0)),
        ],
        out_shape=[
            jax.ShapeDtypeStruct((bsz, t_len, D_MODEL), F32),
            jax.ShapeDtypeStruct((DEPTH, bsz, POOL_BUF, D_POOL), F32),
            jax.ShapeDtypeStruct((DEPTH, bsz, NH, DH, DH), F32),
            jax.ShapeDtypeStruct((DEPTH, bsz, NH, DH), F32),
            jax.ShapeDtypeStruct((bsz, NH, MLSTM_CHUNK), F32),
        ],
        scratch_shapes=[
            pltpu.VMEM((D_MODEL, N_LB * LANES), BF16),
            pltpu.VMEM((D_MODEL, D_MODEL), BF16),
            pltpu.VMEM((tq, D_MODEL), BF16),
            pltpu.VMEM((N_LB, tq, LANES), F32),
            pltpu.VMEM((N_LB, tq, LANES), F32),
            pltpu.VMEM((len(POOL_WINDOWS), POOL_PAD + tq, POOL_GC), F32),
            pltpu.VMEM((tq, D_MODEL), BF16),
            pltpu.VMEM((NH, DH, 2 * DH), F32),
            pltpu.VMEM((NH, MLSTM_CHUNK), F32),
            pltpu.VMEM((n_items, MLSTM_CHUNK, MLSTM_CHUNK + DH), BF16),
            pltpu.VMEM((n_items, MLSTM_CHUNK, 2 * DH), BF16),
            pltpu.VMEM((n_items, DH, 2 * DH), F32),
            pltpu.VMEM((n_items, MLSTM_CHUNK, DH), F32),
        ],
        compiler_params=pltpu.CompilerParams(
            dimension_semantics=("arbitrary",),
            vmem_limit_bytes=V7X_VMEM_LIMIT),
    )(x, x, w_in_t, bg_col, w_pool, pool_scale, w_out, ln_g, ln_b)


def _ffn_kernel(x1p_ref, x1s_ref, w1_ref, b1_ref, w2_ref, b2_ref, g_ref, be_ref, yp_ref, ys_ref,
                pre_ref, *, n_prompt_tiles):
    def residual_plus_mlp(x1_ref):
        acc = None
        for c in range(D_FF // FF_CHUNK):
            cols = slice(c * FF_CHUNK, (c + 1) * FF_CHUNK)
            hid = jnp.maximum(_dot(x1_ref[...], w1_ref[:, cols]) + b1_ref[:, cols], 0.0)
            part = _dot(hid * hid, w2_ref[cols, :])
            acc = part if acc is None else acc + part
        return ALPHA * x1_ref[...] + (acc + b2_ref[...])

    step = pl.program_id(0)

    @pl.when(step == 0)
    def _no_tile_yet():
        pre_ref[...] = jnp.zeros(pre_ref.shape, F32)

    @pl.when(step < n_prompt_tiles)
    def _prompt():
        yp_ref[...] = _layer_norm(pre_ref[...], g_ref[...], be_ref[...])
        pre_ref[...] = residual_plus_mlp(x1p_ref)

    @pl.when(step == n_prompt_tiles)
    def _sample():
        yp_ref[...] = _layer_norm(pre_ref[...], g_ref[...], be_ref[...])
        ys_ref[...] = _layer_norm(residual_plus_mlp(x1s_ref), g_ref[...], be_ref[...])


def _ffn(x1p, x1s, w1, b1, w2, b2, ln_g, ln_b):
    tile = FFN_TILE
    n_tok = x1p.shape[0]
    assert n_tok % tile == 0
    n_tiles = n_tok // tile
    const2 = lambda i: (0, 0)
    ptile = lambda i: (jnp.minimum(i, n_tiles - 1), 0)
    return pl.pallas_call(
        functools.partial(_ffn_kernel, n_prompt_tiles=n_tiles),
        name="ffn_ln2",
        grid=(n_tiles + 1,),
        in_specs=[
            pl.BlockSpec((tile, D_MODEL), ptile),
            pl.BlockSpec(x1s.shape, const2),
            pl.BlockSpec(w1.shape, const2, pipeline_mode=pl.Buffered(1)),
            pl.BlockSpec(b1.shape, const2),
            pl.BlockSpec(w2.shape, const2, pipeline_mode=pl.Buffered(1)),
            pl.BlockSpec(b2.shape, const2),
            pl.BlockSpec(ln_g.shape, const2),
            pl.BlockSpec(ln_b.shape, const2),
        ],
        out_specs=[pl.BlockSpec((tile, D_MODEL), lambda i: (jnp.maximum(i - 1, 0), 0)),
                   pl.BlockSpec(x1s.shape, const2)],
        out_shape=[jax.ShapeDtypeStruct((n_tok, D_MODEL), F32), jax.ShapeDtypeStruct(x1s.shape, F32)],
        scratch_shapes=[pltpu.VMEM((tile, D_MODEL), F32)],
        compiler_params=pltpu.CompilerParams(
            dimension_semantics=("arbitrary",),
            vmem_limit_bytes=V7X_VMEM_LIMIT),
    )(x1p, x1s, w1, b1, w2, b2, ln_g, ln_b)


def _sample_proj_kernel(x_ref, wt_ref, bg_ref, sp_ref, n_ref, m_ref, wp_ref, ps_ref,
                        ypool_ref, pool_ref, q_ref, kd_ref, v_ref, iw_ref, sv_ref, den_ref,
                        floor_ref, osig_ref, nout_ref, mout_ref):
    x = x_ref[...]
    nb = x.shape[0]
    proj = _dot_nt(x, wt_ref[0:CUT_G, :])
    u = proj[:, 0:D_POOL]
    gates = _dot_nt(wt_ref[CUT_G:CUT_G + 2 * NH, :], x) + bg_ref[...]
    ig_r = gates[0:NH]
    inter_r = jax.nn.log_sigmoid(gates[NH:2 * NH]) + m_ref[...]
    m_t_r = jnp.maximum(inter_r, ig_r)
    dw_r = jnp.exp(ig_r - m_t_r)
    iw_r = jnp.exp(inter_r - m_t_r)
    floor_r = jnp.exp(-m_t_r)
    mout_ref[...] = m_t_r

    def as_column(row):
        return jnp.broadcast_to(row, (DH, nb)).T

    for g, w in enumerate(POOL_WINDOWS):
        lo = g * POOL_GC
        u_g = u[:, lo:lo + POOL_GC]
        acc = u_g
        for r in range(POOL_PAD - w, POOL_BUF):
            acc = acc + sp_ref[r, :, lo:lo + POOL_GC]
        pooled = acc / float(min(PAST_LEN + 1, w)) - u_g
        ypool_ref[:, lo:lo + POOL_GC] = _dot(pooled, wp_ref[g]) * ps_ref[:, lo:lo + POOL_GC]
    pool_ref[0:POOL_BUF - 1] = sp_ref[1:POOL_BUF]
    pool_ref[POOL_BUF - 1] = u

    for h in range(NH):
        col = slice(h * DH, (h + 1) * DH)
        dw = as_column(dw_r[h:h + 1])
        iw = as_column(iw_r[h:h + 1])
        q = proj[:, D_POOL + h * DH:D_POOL + (h + 1) * DH]
        k = proj[:, CUT_K + h * DH:CUT_K + (h + 1) * DH] * K_SCALE
        v = proj[:, CUT_V + h * DH:CUT_V + (h + 1) * DH]
        o = proj[:, CUT_V + D_MLSTM + h * DH:CUT_V + D_MLSTM + (h + 1) * DH]
        n_old = n_ref[:, col]
        s = jnp.sum(q * k, axis=-1, keepdims=True) * dw
        q_ref[:, col] = q
        kd_ref[:, col] = dw * k
        v_ref[:, col] = v
        iw_ref[:, col] = iw
        sv_ref[:, col] = s * v
        den_ref[:, col] = s + iw * jnp.sum(q * n_old, axis=-1, keepdims=True)
        floor_ref[:, col] = as_column(floor_r[h:h + 1])
        osig_ref[:, col] = jax.nn.sigmoid(o)
        nout_ref[:, col] = iw * n_old + dw * k


def _sample_proj(x, w_in_t, bg_col, pool_rows, n2d, m_rows, w_pool, pool_scale):
    nb = x.shape[0]
    wide = jax.ShapeDtypeStruct((nb, D_MLSTM), F32)
    return pl.pallas_call(
        _sample_proj_kernel,
        name="sample_proj",
        out_shape=([wide, jax.ShapeDtypeStruct(pool_rows.shape, F32)] + [wide] * 9
                   + [jax.ShapeDtypeStruct(m_rows.shape, F32)]),
        compiler_params=pltpu.CompilerParams(vmem_limit_bytes=V7X_VMEM_LIMIT),
    )(x, w_in_t, bg_col, pool_rows, n2d, m_rows, w_pool, pool_scale)


def _sample_state_kernel(c_ref, q_ref, kd_ref, v_ref, iw_ref, cout_ref, inter_ref):
    nb = SAMPLE_BLOCK
    pad = jnp.zeros((DH - nb, DH), F32)
    for h in range(NH):
        col = slice(h * DH, (h + 1) * DH)
        q_t = jnp.concatenate([q_ref[:, col], pad], axis=0).T
        kd_t = jnp.concatenate([kd_ref[:, col], pad], axis=0).T
        for j in range(nb):
            c_old = c_ref[0, j, h]
            inter_ref[j:j + 1, col] = jnp.sum(q_t[:, j:j + 1] * c_old, axis=0, keepdims=True)
            cout_ref[0, j, h] = iw_ref[j:j + 1, col] * c_old + kd_t[:, j:j + 1] * v_ref[j:j + 1, col]


def _sample_state(state_c, q, kd, v, iw):
    nb = q.shape[0]
    blk = SAMPLE_BLOCK
    assert nb % blk == 0
    row_spec = pl.BlockSpec((blk, D_MLSTM), lambda i: (i, 0))
    c_spec = pl.BlockSpec((1, blk, NH, DH, DH), lambda i: (0, i, 0, 0, 0))
    return pl.pallas_call(
        _sample_state_kernel,
        name="sample_state",
        grid=(nb // blk,),
        in_specs=[c_spec, row_spec, row_spec, row_spec, row_spec],
        out_specs=[c_spec, row_spec],
        out_shape=[jax.ShapeDtypeStruct(state_c.shape, F32),
                   jax.ShapeDtypeStruct((nb, D_MLSTM), F32)],
        compiler_params=pltpu.CompilerParams(
            dimension_semantics=("arbitrary",),
            vmem_limit_bytes=V7X_VMEM_LIMIT),
    )(state_c, q, kd, v, iw)


def _sample_out_kernel(x_ref, ypool_ref, sv_ref, iw_ref, inter_ref, den_ref, floor_ref, osig_ref,
                       wo_ref, g1_ref, b1_ref, x1_ref):
    hh = (sv_ref[...] + iw_ref[...] * inter_ref[...]) / jnp.maximum(jnp.abs(den_ref[...]), floor_ref[...])
    mixin = jnp.concatenate([ypool_ref[...], osig_ref[...] * hh], axis=-1)
    x1_ref[...] = _layer_norm(ALPHA * x_ref[...] + _dot(mixin, wo_ref[...]), g1_ref[...], b1_ref[...])


def _sample_out(x, ypool, sv, iw, inter, den, floor, osig, w_out, ln_g, ln_b):
    return pl.pallas_call(
        _sample_out_kernel,
        name="sample_out",
        out_shape=jax.ShapeDtypeStruct(x.shape, F32),
        compiler_params=pltpu.CompilerParams(vmem_limit_bytes=V7X_VMEM_LIMIT),
    )(x, ypool, sv, iw, inter, den, floor, osig, w_out, ln_g, ln_b)


def kernel(x_prompt, x_sample, state_pool, state_C, state_n, state_m, w_in, b_gate, w_pool, pool_scale,
           w_out, ln1_g, ln1_b, w_ff1, b_ff1, w_ff2, b_ff2, ln2_g, ln2_b):
    assert w_in.shape[0] == DEPTH == 1
    bp, t_len, _ = x_prompt.shape
    bs = x_sample.shape[0]
    assert x_sample.shape[1] == 1

    w_t = jnp.transpose(w_in[0])
    bg_col = b_gate[0].reshape(2 * NH, 1)
    wp = w_pool[0]
    ps = pool_scale[0].reshape(1, D_POOL)
    wo = w_out[0]
    g1, b1 = ln1_g[0].reshape(1, D_MODEL), ln1_b[0].reshape(1, D_MODEL)
    g2, b2 = ln2_g[0].reshape(1, D_MODEL), ln2_b[0].reshape(1, D_MODEL)
    wf1, wf2 = w_ff1[0], w_ff2[0]
    bf1, bf2 = b_ff1[0].reshape(1, D_FF), b_ff2[0].reshape(1, D_MODEL)

    x1_p, pool_p, c_p, n_p, m_rep = _prompt_mixer(x_prompt, w_t, bg_col, wp, ps, wo, g1, b1)
    m_p = m_rep[:, :, 0].reshape(DEPTH, bp, NH)

    xs = x_sample.reshape(bs, D_MODEL)
    (ypool, pool_s, q_s, kd_s, v_s, iw_s, sv_s, den_s, floor_s, osig_s, n_s, m_s_rows) = _sample_proj(
        xs, w_t, bg_col, jnp.transpose(state_pool[0], (1, 0, 2)),
        state_n[0].reshape(bs, D_MLSTM), jnp.transpose(state_m[0]), wp, ps)
    c_s, inter_s = _sample_state(state_C, q_s, kd_s, v_s, iw_s)
    x1_s = _sample_out(xs, ypool, sv_s, iw_s, inter_s, den_s, floor_s, osig_s, wo, g1, b1)

    y_p, y_s = _ffn(x1_p.reshape(bp * t_len, D_MODEL), x1_s, wf1, bf1, wf2, bf2, g2, b2)

    return (y_p.reshape(bp, t_len, D_MODEL), y_s.reshape(bs, 1, D_MODEL),
            pool_p, c_p, n_p, m_p,
            jnp.transpose(pool_s, (1, 0, 2))[None], c_s,
            n_s.reshape(DEPTH, bs, NH, DH), jnp.transpose(m_s_rows)[None])
```

```python
import functools

import jax
import jax.numpy as jnp
from jax import lax
from jax.experimental import pallas as pl
from jax.experimental.pallas import tpu as pltpu

F32 = jnp.float32
BF16 = jnp.bfloat16

D_MODEL = 1024
D_POOL = D_MODEL // 2
D_MLSTM = D_MODEL - D_POOL
POOL_WINDOWS = (2, 4, 8, 16)
POOL_GC = D_POOL // len(POOL_WINDOWS)
POOL_BUF = max(POOL_WINDOWS) - 1
POOL_PAD = POOL_BUF + 1
NH = 4
DH = D_MLSTM // NH
D_FF = 4 * D_MODEL
DEPTH = 1
PAST_LEN = 16384
ALPHA = (2.0 * DEPTH) ** 0.25
LN_EPS = 1e-5
K_SCALE = DH ** -0.5

MLSTM_CHUNK = 128
MIX_TILE = 512
FFN_TILE = 512
FF_CHUNK = 1024
SAMPLE_BLOCK = 16
PROJ_COLS = 256
LANES = 128
assert POOL_GC == LANES and DH == LANES
CUT_K = D_POOL + D_MLSTM
CUT_V = CUT_K + D_MLSTM
CUT_G = D_POOL + 4 * D_MLSTM
LB_Q = D_POOL // LANES
LB_K = CUT_K // LANES
LB_V = CUT_V // LANES
LB_O = LB_V + NH
LB_G = CUT_G // LANES
N_LB = LB_G + 1
V7X_VMEM_LIMIT = 56 * 1024 * 1024


def _layer_norm(y, g, b):
    mu = jnp.mean(y, axis=-1, keepdims=True)
    yc = y - mu
    var = jnp.mean(yc * yc, axis=-1, keepdims=True)
    return yc * lax.rsqrt(var + LN_EPS) * g + b


def _dot(a, b):
    return jnp.dot(a, b, preferred_element_type=F32)


def _dot_nt(a, b):
    return lax.dot_general(a, b, (((1,), (1,)), ((), ())), preferred_element_type=F32)


def _twice(row):
    return jnp.concatenate([row, row], axis=-1)


def _mixer_kernel(xn_ref, xc_ref, wt_ref, bg_ref, wp_ref, ps_ref, wo_ref, g1_ref, b1_ref,
                  x1_ref, pool_ref, c_ref, n_ref, m_ref,
                  win_ref, wob_ref, xb_ref, pm_a, pm_b, ext_ref, mix_ref, caug_ref, mst_ref,
                  lhs_ref, vaug_ref, kv_ref, floor_ref, *, tiles_per_seq):
    tq = MIX_TILE
    L = MLSTM_CHUNK
    step = pl.program_id(0)
    t_idx = lax.rem(step + tiles_per_seq - 1, tiles_per_seq)
    last_t = tiles_per_seq - 1
    parity = lax.rem(step, 2)

    def project(pm_next):
        xb_ref[...] = xn_ref[0].astype(BF16)

        def piece(c0, c1):
            res = _dot(xb_ref[...], win_ref[:, c0:c1])
            for i in range((c1 - c0) // LANES):
                pm_next[c0 // LANES + i] = res[:, i * LANES:(i + 1) * LANES]

        n_cols = N_LB * LANES
        return [functools.partial(piece, c0, min(c0 + PROJ_COLS, n_cols))
                for c0 in range(0, n_cols, PROJ_COLS)]

    @pl.when(step == 0)
    def _first_step():
        for j in range(CUT_G // PROJ_COLS):
            win_ref[:, j * PROJ_COLS:(j + 1) * PROJ_COLS] = (
                wt_ref[j * PROJ_COLS:(j + 1) * PROJ_COLS, :].T.astype(BF16))
        gate_rows = jnp.concatenate([wt_ref[CUT_G:CUT_G + 2 * NH, :],
                                     jnp.zeros((LANES - 2 * NH, D_MODEL), F32)], axis=0)
        win_ref[:, CUT_G:CUT_G + LANES] = gate_rows.T.astype(BF16)
        wob_ref[...] = wo_ref[...].astype(BF16)
        for piece in project(pm_a):
            piece()

    @pl.when(t_idx == 0)
    def _init():
        ext_ref[:, 0:POOL_PAD, :] = jnp.zeros((len(POOL_WINDOWS), POOL_PAD, POOL_GC), F32)
        caug_ref[...] = jnp.zeros(caug_ref.shape, F32)
        mst_ref[...] = jnp.zeros(mst_ref.shape, F32)

    tt = lax.broadcasted_iota(jnp.int32, (L, L), 0)
    ss = lax.broadcasted_iota(jnp.int32, (L, L), 1)
    causal = ss <= tt
    diag = ss == tt

    def run(pm_next, pm_cur):
        pieces = project(pm_next)
        n_chunks = tq // L
        n_slots = len(POOL_WINDOWS) + 2 * n_chunks * NH
        slot_of = [(k * n_slots) // len(pieces) for k in range(len(pieces))]
        slot = [0]

        def next_slot():
            for k, piece in enumerate(pieces):
                if slot_of[k] == slot[0]:
                    piece()
            slot[0] += 1

        pos = t_idx * tq + lax.broadcasted_iota(jnp.int32, (tq, POOL_GC), 0)
        for g, w in enumerate(POOL_WINDOWS):
            lo = g * POOL_GC
            u_g = pm_cur[g]
            ext_ref[g, POOL_PAD:POOL_PAD + tq, :] = u_g
            s = ext_ref[g]
            k = 1
            while k < w:
                s = s + pltpu.roll(s, k, axis=0)
                k *= 2
            cnt = jnp.minimum(pos + 1, w).astype(F32)
            pooled = s[POOL_PAD:, :] / cnt - u_g
            mix_ref[:, lo:lo + POOL_GC] = (_dot(pooled, wp_ref[g]) * ps_ref[:, lo:lo + POOL_GC]).astype(BF16)
            ext_ref[g, 0:POOL_PAD, :] = ext_ref[g, tq:tq + POOL_PAD, :]
            next_slot()

        lane = lax.broadcasted_iota(jnp.int32, (NH, L), 1)
        m_prev = mst_ref[...]
        chunk_rows = []
        for c in range(n_chunks):
            rows = slice(c * L, (c + 1) * L)
            gates = pm_cur[LB_G, rows, :].T[0:2 * NH, :] + bg_ref[...]
            lf = jax.nn.log_sigmoid(gates[NH:2 * NH])
            b = lf
            k = 1
            while k < L:
                b = b + jnp.where(lane >= k, pltpu.roll(b, k, axis=1), 0.0)
                k *= 2
            a = gates[0:NH] - b
            cmax = a
            k = 1
            while k < L:
                cmax = jnp.maximum(cmax, jnp.where(lane >= k, pltpu.roll(cmax, k, axis=1), -jnp.inf))
                k *= 2
            amax = jnp.max(a, axis=-1, keepdims=True)
            big_m = jnp.maximum(cmax, m_prev)
            mm = jnp.maximum(m_prev, amax)
            chunk_rows.append(dict(
                a=a, big_m=big_m,
                iw=jnp.exp(m_prev - big_m),
                floor=jnp.exp(-(b + big_m)),
                w_loc=jnp.exp(a - amax),
                g_state=jnp.exp(m_prev - mm),
                f_state=jnp.exp(amax - mm)))
            m_prev = jnp.sum(lf, axis=-1, keepdims=True) + mm
        mst_ref[...] = m_prev

        def as_column(row):
            return jnp.broadcast_to(row, (L, L)).T

        ones_blk = jnp.ones((L, DH), BF16)
        for c in range(n_chunks):
            rows = slice(c * L, (c + 1) * L)
            cr = chunk_rows[c]
            for h in range(NH):
                i = c * NH + h
                q = pm_cur[LB_Q + h, rows, :]
                qb = q.astype(BF16)
                kt = pm_cur[LB_K + h, rows, :].T * K_SCALE
                v_aug = jnp.concatenate([pm_cur[LB_V + h, rows, :].astype(BF16), ones_blk], axis=1)
                p = jnp.exp(jnp.where(causal, cr["a"][h:h + 1], -jnp.inf) - as_column(cr["big_m"][h:h + 1]))
                lhs_ref[i, :, 0:L] = (_dot(qb, kt.astype(BF16)) * p).astype(BF16)
                vaug_ref[i] = v_aug
                kv_ref[i] = _dot((kt * cr["w_loc"][h:h + 1]).astype(BF16), v_aug)
                lhs_ref[i, :, L:L + DH] = (q * as_column(cr["iw"][h:h + 1])).astype(BF16)
                floor_ref[i] = as_column(cr["floor"][h:h + 1])
                next_slot()

        for c in range(n_chunks):
            rows = slice(c * L, (c + 1) * L)
            cr = chunk_rows[c]
            for h in range(NH):
                i = c * NH + h
                caug = caug_ref[h]
                comb = _dot(lhs_ref[i], jnp.concatenate([vaug_ref[i], caug.astype(BF16)], axis=0))
                hh = comb[:, :DH] / jnp.maximum(jnp.abs(comb[:, DH:]), floor_ref[i])
                o = pm_cur[LB_O + h, rows, :]
                mix_ref[rows, D_POOL + h * DH:D_POOL + (h + 1) * DH] = (
                    jax.nn.sigmoid(o) * hh).astype(BF16)
                caug_ref[h] = (_twice(cr["g_state"][h:h + 1]) * caug
                               + _twice(cr["f_state"][h:h + 1]) * kv_ref[i])
                next_slot()

        mix = _dot(mix_ref[...], wob_ref[...])
        x1_ref[0] = _layer_norm(ALPHA * xc_ref[0] + mix, g1_ref[...], b1_ref[...])

    @pl.when((parity == 0) & (step > 0))
    def _even():
        run(pm_a, pm_b)

    @pl.when(parity == 1)
    def _odd():
        run(pm_b, pm_a)

    @pl.when((t_idx == last_t) & (step > 0))
    def _final_state():
        for g in range(len(POOL_WINDOWS)):
            pool_ref[0, 0, :, g * POOL_GC:(g + 1) * POOL_GC] = ext_ref[g, tq + 1:tq + POOL_PAD, :]
        for h in range(NH):
            caug = caug_ref[h]
            c_ref[0, 0, h] = caug[:, :DH]
            n_ref[0, 0, h:h + 1, :] = jnp.sum(jnp.where(diag, caug[:, DH:], 0.0),
                                              axis=0, keepdims=True)
        m_ref[0] = mst_ref[...]


def _prompt_mixer(x, w_in_t, bg_col, w_pool, pool_scale, w_out, ln_g, ln_b):
    bsz, t_len, _ = x.shape
    tq = MIX_TILE
    assert t_len % tq == 0 and tq % MLSTM_CHUNK == 0 and tq >= POOL_PAD
    nt = t_len // tq
    n_tiles = bsz * nt
    n_items = (tq // MLSTM_CHUNK) * NH
    const2 = lambda i: (0, 0)
    nxt = lambda i: jnp.minimum(i, n_tiles - 1)
    cur = lambda i: jnp.maximum(i - 1, 0)
    return pl.pallas_call(
        functools.partial(_mixer_kernel, tiles_per_seq=nt),
        name="prompt_mixer",
        grid=(n_tiles + 1,),
        in_specs=[
            pl.BlockSpec((1, tq, D_MODEL), lambda i: (nxt(i) // nt, nxt(i) % nt, 0)),
            pl.BlockSpec((1, tq, D_MODEL), lambda i: (cur(i) // nt, cur(i) % nt, 0)),
            pl.BlockSpec(w_in_t.shape, const2, pipeline_mode=pl.Buffered(1)),
            pl.BlockSpec(bg_col.shape, const2),
            pl.BlockSpec(w_pool.shape, lambda i: (0, 0, 0)),
            pl.BlockSpec(pool_scale.shape, const2),
            pl.BlockSpec(w_out.shape, const2, pipeline_mode=pl.Buffered(1)),
            pl.BlockSpec(ln_g.shape, const2),
            pl.BlockSpec(ln_b.shape, const2),
        ],
        out_specs=[
            pl.BlockSpec((1, tq, D_MODEL), lambda i: (cur(i) // nt, cur(i) % nt, 0)),
            pl.BlockSpec((1, 1, POOL_BUF, D_POOL), lambda i: (0, cur(i) // nt, 0, 0)),
            pl.BlockSpec((1, 1, NH, DH, DH), lambda i: (0, cur(i) // nt, 0, 0, 0)),
            pl.BlockSpec((1, 1, NH, DH), lambda i: (0, cur(i) // nt, 0, 0)),
            pl.BlockSpec((1, NH, MLSTM_CHUNK), lambda i: (cur(i) // nt, 0, 0)),
        ],
        out_shape=[
            jax.ShapeDtypeStruct((bsz, t_len, D_MODEL), F32),
            jax.ShapeDtypeStruct((DEPTH, bsz, POOL_BUF, D_POOL), F32),
            jax.ShapeDtypeStruct((DEPTH, bsz, NH, DH, DH), F32),
            jax.ShapeDtypeStruct((DEPTH, bsz, NH, DH), F32),
            jax.ShapeDtypeStruct((bsz, NH, MLSTM_CHUNK), F32),
        ],
        scratch_shapes=[
            pltpu.VMEM((D_MODEL, N_LB * LANES), BF16),
            pltpu.VMEM((D_MODEL, D_MODEL), BF16),
            pltpu.VMEM((tq, D_MODEL), BF16),
            pltpu.VMEM((N_LB, tq, LANES), F32),
            pltpu.VMEM((N_LB, tq, LANES), F32),
            pltpu.VMEM((len(POOL_WINDOWS), POOL_PAD + tq, POOL_GC), F32),
            pltpu.VMEM((tq, D_MODEL), BF16),
            pltpu.VMEM((NH, DH, 2 * DH), F32),
            pltpu.VMEM((NH, MLSTM_CHUNK), F32),
            pltpu.VMEM((n_items, MLSTM_CHUNK, MLSTM_CHUNK + DH), BF16),
            pltpu.VMEM((n_items, MLSTM_CHUNK, 2 * DH), BF16),
            pltpu.VMEM((n_items, DH, 2 * DH), F32),
            pltpu.VMEM((n_items, MLSTM_CHUNK, DH), F32),
        ],
        compiler_params=pltpu.CompilerParams(
            dimension_semantics=("arbitrary",),
            vmem_limit_bytes=V7X_VMEM_LIMIT),
    )(x, x, w_in_t, bg_col, w_pool, pool_scale, w_out, ln_g, ln_b)


def _ffn_kernel(x1p_ref, x1s_ref, w1_ref, b1_ref, w2_ref, b2_ref, g_ref, be_ref, yp_ref, ys_ref,
                pre_ref, *, n_prompt_tiles):
    def residual_plus_mlp(x1_ref):
        acc = None
        for c in range(D_FF // FF_CHUNK):
            cols = slice(c * FF_CHUNK, (c + 1) * FF_CHUNK)
            hid = jnp.maximum(_dot(x1_ref[...], w1_ref[:, cols]) + b1_ref[:, cols], 0.0)
            part = _dot(hid * hid, w2_ref[cols, :])
            acc = part if acc is None else acc + part
        return ALPHA * x1_ref[...] + (acc + b2_ref[...])

    step = pl.program_id(0)

    @pl.when(step == 0)
    def _no_tile_yet():
        pre_ref[...] = jnp.zeros(pre_ref.shape, F32)

    @pl.when(step < n_prompt_tiles)
    def _prompt():
        yp_ref[...] = _layer_norm(pre_ref[...], g_ref[...], be_ref[...])
        pre_ref[...] = residual_plus_mlp(x1p_ref)

    @pl.when(step == n_prompt_tiles)
    def _sample():
        yp_ref[...] = _layer_norm(pre_ref[...], g_ref[...], be_ref[...])
        ys_ref[:, 0, :] = _layer_norm(residual_plus_mlp(x1s_ref), g_ref[...], be_ref[...])


def _ffn(x1p, x1s, w1, b1, w2, b2, ln_g, ln_b):
    tile = FFN_TILE
    n_tok = x1p.shape[0]
    assert n_tok % tile == 0
    n_tiles = n_tok // tile
    const2 = lambda i: (0, 0)
    ptile = lambda i: (jnp.minimum(i, n_tiles - 1), 0)
    return pl.pallas_call(
        functools.partial(_ffn_kernel, n_prompt_tiles=n_tiles),
        name="ffn_ln2",
        grid=(n_tiles + 1,),
        in_specs=[
            pl.BlockSpec((tile, D_MODEL), ptile),
            pl.BlockSpec(x1s.shape, const2),
            pl.BlockSpec(w1.shape, const2, pipeline_mode=pl.Buffered(1)),
            pl.BlockSpec(b1.shape, const2),
            pl.BlockSpec(w2.shape, const2, pipeline_mode=pl.Buffered(1)),
            pl.BlockSpec(b2.shape, const2),
            pl.BlockSpec(ln_g.shape, const2),
            pl.BlockSpec(ln_b.shape, const2),
        ],
        out_specs=[pl.BlockSpec((tile, D_MODEL), lambda i: (jnp.maximum(i - 1, 0), 0)),
                   pl.BlockSpec((x1s.shape[0], 1, D_MODEL), lambda i: (0, 0, 0))],
        out_shape=[jax.ShapeDtypeStruct((n_tok, D_MODEL), F32),
                   jax.ShapeDtypeStruct((x1s.shape[0], 1, D_MODEL), F32)],
        scratch_shapes=[pltpu.VMEM((tile, D_MODEL), F32)],
        compiler_params=pltpu.CompilerParams(
            dimension_semantics=("arbitrary",),
            vmem_limit_bytes=V7X_VMEM_LIMIT),
    )(x1p, x1s, w1, b1, w2, b2, ln_g, ln_b)


def _sample_proj_kernel(x_ref, wt_ref, bg_ref, sp_ref, n_ref, m_ref, wp_ref, ps_ref,
                        ypool_ref, pool_ref, q_ref, kd_ref, v_ref, iw_ref, sv_ref, den_ref,
                        floor_ref, osig_ref, nout_ref, mout_ref):
    x = x_ref[:, 0, :]
    nb = x.shape[0]
    proj = _dot_nt(x, wt_ref[0:CUT_G, :])
    u = proj[:, 0:D_POOL]
    gates = _dot_nt(wt_ref[CUT_G:CUT_G + 2 * NH, :], x) + bg_ref[...]
    ig_r = gates[0:NH]
    inter_r = jax.nn.log_sigmoid(gates[NH:2 * NH]) + m_ref[...]
    m_t_r = jnp.maximum(inter_r, ig_r)
    dw_r = jnp.exp(ig_r - m_t_r)
    iw_r = jnp.exp(inter_r - m_t_r)
    floor_r = jnp.exp(-m_t_r)
    mout_ref[...] = m_t_r

    def as_column(row):
        return jnp.broadcast_to(row, (DH, nb)).T

    for g, w in enumerate(POOL_WINDOWS):
        lo = g * POOL_GC
        u_g = u[:, lo:lo + POOL_GC]
        acc = u_g
        for r in range(POOL_PAD - w, POOL_BUF):
            acc = acc + sp_ref[r, :, lo:lo + POOL_GC]
        pooled = acc / float(min(PAST_LEN + 1, w)) - u_g
        ypool_ref[:, lo:lo + POOL_GC] = _dot(pooled, wp_ref[g]) * ps_ref[:, lo:lo + POOL_GC]
    pool_ref[0:POOL_BUF - 1] = sp_ref[1:POOL_BUF]
    pool_ref[POOL_BUF - 1] = u

    for h in range(NH):
        col = slice(h * DH, (h + 1) * DH)
        dw = as_column(dw_r[h:h + 1])
        iw = as_column(iw_r[h:h + 1])
        q = proj[:, D_POOL + h * DH:D_POOL + (h + 1) * DH]
        k = proj[:, CUT_K + h * DH:CUT_K + (h + 1) * DH] * K_SCALE
        v = proj[:, CUT_V + h * DH:CUT_V + (h + 1) * DH]
        o = proj[:, CUT_V + D_MLSTM + h * DH:CUT_V + D_MLSTM + (h + 1) * DH]
        n_old = n_ref[:, col]
        s = jnp.sum(q * k, axis=-1, keepdims=True) * dw
        q_ref[:, col] = q
        kd_ref[:, col] = dw * k
        v_ref[:, col] = v
        iw_ref[:, col] = iw
        sv_ref[:, col] = s * v
        den_ref[:, col] = s + iw * jnp.sum(q * n_old, axis=-1, keepdims=True)
        floor_ref[:, col] = as_column(floor_r[h:h + 1])
        osig_ref[:, col] = jax.nn.sigmoid(o)
        nout_ref[:, col] = iw * n_old + dw * k


def _sample_proj(x, w_in_t, bg_col, pool_rows, n2d, m_rows, w_pool, pool_scale):
    nb = x.shape[0]
    wide = jax.ShapeDtypeStruct((nb, D_MLSTM), F32)
    return pl.pallas_call(
        _sample_proj_kernel,
        name="sample_proj",
        out_shape=([wide, jax.ShapeDtypeStruct(pool_rows.shape, F32)] + [wide] * 9
                   + [jax.ShapeDtypeStruct(m_rows.shape, F32)]),
        compiler_params=pltpu.CompilerParams(vmem_limit_bytes=V7X_VMEM_LIMIT),
    )(x, w_in_t, bg_col, pool_rows, n2d, m_rows, w_pool, pool_scale)


def _sample_state_kernel(c_ref, q_ref, kd_ref, v_ref, iw_ref, cout_ref, inter_ref):
    nb = SAMPLE_BLOCK
    pad = jnp.zeros((DH - nb, DH), F32)
    for h in range(NH):
        col = slice(h * DH, (h + 1) * DH)
        q_t = jnp.concatenate([q_ref[:, col], pad], axis=0).T
        kd_t = jnp.concatenate([kd_ref[:, col], pad], axis=0).T
        for j in range(nb):
            c_old = c_ref[0, j, h]
            inter_ref[j:j + 1, col] = jnp.sum(q_t[:, j:j + 1] * c_old, axis=0, keepdims=True)
            cout_ref[0, j, h] = iw_ref[j:j + 1, col] * c_old + kd_t[:, j:j + 1] * v_ref[j:j + 1, col]


def _sample_state(state_c, q, kd, v, iw):
    nb = q.shape[0]
    blk = SAMPLE_BLOCK
    assert nb % blk == 0
    row_spec = pl.BlockSpec((blk, D_MLSTM), lambda i: (i, 0))
    c_spec = pl.BlockSpec((1, blk, NH, DH, DH), lambda i: (0, i, 0, 0, 0))
    return pl.pallas_call(
        _sample_state_kernel,
        name="sample_state",
        grid=(nb // blk,),
        in_specs=[c_spec, row_spec, row_spec, row_spec, row_spec],
        out_specs=[c_spec, row_spec],
        out_shape=[jax.ShapeDtypeStruct(state_c.shape, F32),
                   jax.ShapeDtypeStruct((nb, D_MLSTM), F32)],
        compiler_params=pltpu.CompilerParams(
            dimension_semantics=("arbitrary",),
            vmem_limit_bytes=V7X_VMEM_LIMIT),
    )(state_c, q, kd, v, iw)


def _sample_out_kernel(x_ref, ypool_ref, sv_ref, iw_ref, inter_ref, den_ref, floor_ref, osig_ref,
                       wo_ref, g1_ref, b1_ref, x1_ref):
    hh = (sv_ref[...] + iw_ref[...] * inter_ref[...]) / jnp.maximum(jnp.abs(den_ref[...]), floor_ref[...])
    mixin = jnp.concatenate([ypool_ref[...], osig_ref[...] * hh], axis=-1)
    x1_ref[...] = _layer_norm(ALPHA * x_ref[:, 0, :] + _dot(mixin, wo_ref[...]), g1_ref[...], b1_ref[...])


def _sample_out(x, ypool, sv, iw, inter, den, floor, osig, w_out, ln_g, ln_b):
    return pl.pallas_call(
        _sample_out_kernel,
        name="sample_out",
        out_shape=jax.ShapeDtypeStruct((x.shape[0], D_MODEL), F32),
        compiler_params=pltpu.CompilerParams(vmem_limit_bytes=V7X_VMEM_LIMIT),
    )(x, ypool, sv, iw, inter, den, floor, osig, w_out, ln_g, ln_b)


def kernel(x_prompt, x_sample, state_pool, state_C, state_n, state_m, w_in, b_gate, w_pool, pool_scale,
           w_out, ln1_g, ln1_b, w_ff1, b_ff1, w_ff2, b_ff2, ln2_g, ln2_b):
    assert w_in.shape[0] == DEPTH == 1
    bp, t_len, _ = x_prompt.shape
    bs = x_sample.shape[0]
    assert x_sample.shape[1] == 1

    w_t = jnp.transpose(w_in[0])
    bg_col = b_gate[0].reshape(2 * NH, 1)
    wp = w_pool[0]
    ps = pool_scale[0].reshape(1, D_POOL)
    wo = w_out[0]
    g1, b1 = ln1_g[0].reshape(1, D_MODEL), ln1_b[0].reshape(1, D_MODEL)
    g2, b2 = ln2_g[0].reshape(1, D_MODEL), ln2_b[0].reshape(1, D_MODEL)
    wf1, wf2 = w_ff1[0], w_ff2[0]
    bf1, bf2 = b_ff1[0].reshape(1, D_FF), b_ff2[0].reshape(1, D_MODEL)

    x1_p, pool_p, c_p, n_p, m_rep = _prompt_mixer(x_prompt, w_t, bg_col, wp, ps, wo, g1, b1)
    m_p = m_rep[:, :, 0].reshape(DEPTH, bp, NH)

    (ypool, pool_s, q_s, kd_s, v_s, iw_s, sv_s, den_s, floor_s, osig_s, n_s, m_s_rows) = _sample_proj(
        x_sample, w_t, bg_col, jnp.transpose(state_pool[0], (1, 0, 2)),
        state_n[0].reshape(bs, D_MLSTM), jnp.transpose(state_m[0]), wp, ps)
    c_s, inter_s = _sample_state(state_C, q_s, kd_s, v_s, iw_s)
    x1_s = _sample_out(x_sample, ypool, sv_s, iw_s, inter_s, den_s, floor_s, osig_s, wo, g1, b1)

    y_p, y_s = _ffn(x1_p.reshape(bp * t_len, D_MODEL), x1_s, wf1, bf1, wf2, bf2, g2, b2)

    return (y_p.reshape(bp, t_len, D_MODEL), y_s,
            pool_p, c_p, n_p, m_p,
            jnp.transpose(pool_s, (1, 0, 2))[None], c_s,
            n_s.reshape(DEPTH, bs, NH, DH), jnp.transpose(m_s_rows)[None])
```

```python
import functools

import jax
import jax.numpy as jnp
from jax import lax
from jax.experimental import pallas as pl
from jax.experimental.pallas import tpu as pltpu

F32 = jnp.float32
BF16 = jnp.bfloat16

D_MODEL = 1024
D_POOL = D_MODEL // 2
D_MLSTM = D_MODEL - D_POOL
POOL_WINDOWS = (2, 4, 8, 16)
POOL_GC = D_POOL // len(POOL_WINDOWS)
POOL_BUF = max(POOL_WINDOWS) - 1
POOL_PAD = POOL_BUF + 1
NH = 4
DH = D_MLSTM // NH
D_FF = 4 * D_MODEL
DEPTH = 1
PAST_LEN = 16384
ALPHA = (2.0 * DEPTH) ** 0.25
LN_EPS = 1e-5
K_SCALE = DH ** -0.5

MLSTM_CHUNK = 128
MIX_TILE = 512
FFN_TILE = 512
FF_CHUNK = 1024
SAMPLE_BLOCK = 16
PROJ_COLS = 256
LANES = 128
assert POOL_GC == LANES and DH == LANES
CUT_K = D_POOL + D_MLSTM
CUT_V = CUT_K + D_MLSTM
CUT_G = D_POOL + 4 * D_MLSTM
LB_Q = D_POOL // LANES
LB_K = CUT_K // LANES
LB_V = CUT_V // LANES
LB_O = LB_V + NH
LB_G = CUT_G // LANES
N_LB = LB_G + 1
V7X_VMEM_LIMIT = 56 * 1024 * 1024


def _layer_norm(y, g, b):
    mu = jnp.mean(y, axis=-1, keepdims=True)
    yc = y - mu
    var = jnp.mean(yc * yc, axis=-1, keepdims=True)
    return yc * lax.rsqrt(var + LN_EPS) * g + b


def _dot(a, b):
    return jnp.dot(a, b, preferred_element_type=F32)


def _dot_nt(a, b):
    return lax.dot_general(a, b, (((1,), (1,)), ((), ())), preferred_element_type=F32)


def _twice(row):
    return jnp.concatenate([row, row], axis=-1)


def _gate_bias_rows(bg_ref, width):
    row = lax.broadcasted_iota(jnp.int32, (2 * NH, width), 0)
    bias = jnp.zeros((2 * NH, width), F32)
    for g in range(2 * NH):
        bias = jnp.where(row == g, bg_ref[0, g], bias)
    return bias


def _mixer_kernel(xn_ref, xc_ref, wt_ref, bg_ref, wp_ref, ps_ref, wo_ref, g1_ref, b1_ref,
                  x1_ref, pool_ref, c_ref, n_ref, m_ref,
                  win_ref, wob_ref, wpd_ref, xb_ref, pm_a, pm_b, ext_ref, mix_ref, caug_ref, mst_ref,
                  lhs_ref, vaug_ref, kv_ref, floor_ref, *, tiles_per_seq):
    tq = MIX_TILE
    L = MLSTM_CHUNK
    step = pl.program_id(0)
    t_idx = lax.rem(step + tiles_per_seq - 1, tiles_per_seq)
    last_t = tiles_per_seq - 1
    parity = lax.rem(step, 2)

    def project(pm_next):
        xb_ref[...] = xn_ref[0].astype(BF16)

        def piece(c0, c1):
            res = _dot(xb_ref[...], win_ref[:, c0:c1])
            for i in range((c1 - c0) // LANES):
                pm_next[c0 // LANES + i] = res[:, i * LANES:(i + 1) * LANES]

        n_cols = N_LB * LANES
        return [functools.partial(piece, c0, min(c0 + PROJ_COLS, n_cols))
                for c0 in range(0, n_cols, PROJ_COLS)]

    @pl.when(step == 0)
    def _first_step():
        for j in range(CUT_G // PROJ_COLS):
            win_ref[:, j * PROJ_COLS:(j + 1) * PROJ_COLS] = (
                wt_ref[j * PROJ_COLS:(j + 1) * PROJ_COLS, :].T.astype(BF16))
        gate_rows = jnp.concatenate([wt_ref[CUT_G:CUT_G + 2 * NH, :],
                                     jnp.zeros((LANES - 2 * NH, D_MODEL), F32)], axis=0)
        win_ref[:, CUT_G:CUT_G + LANES] = gate_rows.T.astype(BF16)
        wob_ref[...] = wo_ref[...].astype(BF16)
        zero_blk = jnp.zeros((POOL_GC, POOL_GC), F32)
        for p in range(len(POOL_WINDOWS) // 2):
            wpd_ref[p] = jnp.concatenate(
                [jnp.concatenate([wp_ref[2 * p], zero_blk], axis=1),
                 jnp.concatenate([zero_blk, wp_ref[2 * p + 1]], axis=1)], axis=0).astype(BF16)
        for piece in project(pm_a):
            piece()

    @pl.when(t_idx == 0)
    def _init():
        ext_ref[:, 0:POOL_PAD, :] = jnp.zeros((len(POOL_WINDOWS), POOL_PAD, POOL_GC), F32)
        caug_ref[...] = jnp.zeros(caug_ref.shape, F32)
        mst_ref[...] = jnp.zeros(mst_ref.shape, F32)

    tt = lax.broadcasted_iota(jnp.int32, (L, L), 0)
    ss = lax.broadcasted_iota(jnp.int32, (L, L), 1)
    causal = ss <= tt
    diag = ss == tt

    def run(pm_next, pm_cur):
        pieces = project(pm_next)
        n_chunks = tq // L
        n_slots = len(POOL_WINDOWS) + 2 * n_chunks * NH
        slot_of = [(k * n_slots) // len(pieces) for k in range(len(pieces))]
        slot = [0]

        def next_slot():
            for k, piece in enumerate(pieces):
                if slot_of[k] == slot[0]:
                    piece()
            slot[0] += 1

        pos = t_idx * tq + lax.broadcasted_iota(jnp.int32, (tq, POOL_GC), 0)
        pooled = []
        for g, w in enumerate(POOL_WINDOWS):
            lo = g * POOL_GC
            u_g = pm_cur[g]
            ext_ref[g, POOL_PAD:POOL_PAD + tq, :] = u_g
            s = ext_ref[g]
            k = 1
            while k < w:
                s = s + pltpu.roll(s, k, axis=0)
                k *= 2
            cnt = jnp.minimum(pos + 1, w).astype(F32)
            pooled.append((s[POOL_PAD:, :] / cnt - u_g).astype(BF16))
            ext_ref[g, 0:POOL_PAD, :] = ext_ref[g, tq:tq + POOL_PAD, :]
            if g % 2 == 1:
                pair = jnp.concatenate(pooled[g - 1:g + 1], axis=1)
                mix_ref[:, lo - POOL_GC:lo + POOL_GC] = (
                    _dot(pair, wpd_ref[g // 2]) * ps_ref[:, lo - POOL_GC:lo + POOL_GC]).astype(BF16)
            next_slot()

        lane = lax.broadcasted_iota(jnp.int32, (NH, L), 1)
        gate_bias = _gate_bias_rows(bg_ref, L)
        m_prev = mst_ref[...]
        chunk_rows = []
        for c in range(n_chunks):
            rows = slice(c * L, (c + 1) * L)
            gates = pm_cur[LB_G, rows, :].T[0:2 * NH, :] + gate_bias
            lf = jax.nn.log_sigmoid(gates[NH:2 * NH])
            b = lf
            k = 1
            while k < L:
                b = b + jnp.where(lane >= k, pltpu.roll(b, k, axis=1), 0.0)
                k *= 2
            a = gates[0:NH] - b
            cmax = a
            k = 1
            while k < L:
                cmax = jnp.maximum(cmax, jnp.where(lane >= k, pltpu.roll(cmax, k, axis=1), -jnp.inf))
                k *= 2
            amax = jnp.max(a, axis=-1, keepdims=True)
            big_m = jnp.maximum(cmax, m_prev)
            mm = jnp.maximum(m_prev, amax)
            chunk_rows.append(dict(
                a=a, big_m=big_m,
                iw=jnp.exp(m_prev - big_m),
                floor=jnp.exp(-(b + big_m)),
                w_loc=jnp.exp(a - amax),
                g_state=jnp.exp(m_prev - mm),
                f_state=jnp.exp(amax - mm)))
            m_prev = jnp.sum(lf, axis=-1, keepdims=True) + mm
        mst_ref[...] = m_prev

        def as_column(row):
            return jnp.broadcast_to(row, (L, L)).T

        ones_blk = jnp.ones((L, DH), BF16)
        for c in range(n_chunks):
            rows = slice(c * L, (c + 1) * L)
            cr = chunk_rows[c]
            for h in range(NH):
                i = c * NH + h
                q = pm_cur[LB_Q + h, rows, :]
                qb = q.astype(BF16)
                kt = pm_cur[LB_K + h, rows, :].T * K_SCALE
                v_aug = jnp.concatenate([pm_cur[LB_V + h, rows, :].astype(BF16), ones_blk], axis=1)
                p = jnp.exp(jnp.where(causal, cr["a"][h:h + 1], -jnp.inf) - as_column(cr["big_m"][h:h + 1]))
                lhs_ref[i, :, 0:L] = (_dot(qb, kt.astype(BF16)) * p).astype(BF16)
                vaug_ref[i] = v_aug
                kv_ref[i] = _dot((kt * cr["w_loc"][h:h + 1]).astype(BF16), v_aug)
                lhs_ref[i, :, L:L + DH] = (q * as_column(cr["iw"][h:h + 1])).astype(BF16)
                floor_ref[i] = as_column(cr["floor"][h:h + 1])
                next_slot()

        for c in range(n_chunks):
            rows = slice(c * L, (c + 1) * L)
            cr = chunk_rows[c]
            for h in range(NH):
                i = c * NH + h
                caug = caug_ref[h]
                comb = _dot(lhs_ref[i], jnp.concatenate([vaug_ref[i], caug.astype(BF16)], axis=0))
                hh = comb[:, :DH] / jnp.maximum(jnp.abs(comb[:, DH:]), floor_ref[i])
                o = pm_cur[LB_O + h, rows, :]
                mix_ref[rows, D_POOL + h * DH:D_POOL + (h + 1) * DH] = (
                    jax.nn.sigmoid(o) * hh).astype(BF16)
                caug_ref[h] = (_twice(cr["g_state"][h:h + 1]) * caug
                               + _twice(cr["f_state"][h:h + 1]) * kv_ref[i])
                next_slot()

        mix = _dot(mix_ref[...], wob_ref[...])
        x1_ref[0] = _layer_norm(ALPHA * xc_ref[0] + mix, g1_ref[...], b1_ref[...])

    @pl.when((parity == 0) & (step > 0))
    def _even():
        run(pm_a, pm_b)

    @pl.when(parity == 1)
    def _odd():
        run(pm_b, pm_a)

    @pl.when((t_idx == last_t) & (step > 0))
    def _final_state():
        for g in range(len(POOL_WINDOWS)):
            pool_ref[0, 0, :, g * POOL_GC:(g + 1) * POOL_GC] = ext_ref[g, tq + 1:tq + POOL_PAD, :]
        for h in range(NH):
            caug = caug_ref[h]
            c_ref[0, 0, h] = caug[:, :DH]
            n_ref[0, 0, h:h + 1, :] = jnp.sum(jnp.where(diag, caug[:, DH:], 0.0),
                                              axis=0, keepdims=True)
        m_ref[0] = mst_ref[...]


def _prompt_mixer(x, w_in_t, b_gate2d, w_pool, pool_scale, w_out, ln_g, ln_b):
    bsz, t_len, _ = x.shape
    tq = MIX_TILE
    assert t_len % tq == 0 and tq % MLSTM_CHUNK == 0 and tq >= POOL_PAD
    nt = t_len // tq
    n_tiles = bsz * nt
    n_items = (tq // MLSTM_CHUNK) * NH
    const2 = lambda i: (0, 0)
    nxt = lambda i: jnp.minimum(i, n_tiles - 1)
    cur = lambda i: jnp.maximum(i - 1, 0)
    return pl.pallas_call(
        functools.partial(_mixer_kernel, tiles_per_seq=nt),
        name="prompt_mixer",
        grid=(n_tiles + 1,),
        in_specs=[
            pl.BlockSpec((1, tq, D_MODEL), lambda i: (nxt(i) // nt, nxt(i) % nt, 0)),
            pl.BlockSpec((1, tq, D_MODEL), lambda i: (cur(i) // nt, cur(i) % nt, 0)),
            pl.BlockSpec(w_in_t.shape, const2, pipeline_mode=pl.Buffered(1)),
            pl.BlockSpec(memory_space=pltpu.SMEM),
            pl.BlockSpec(w_pool.shape, lambda i: (0, 0, 0)),
            pl.BlockSpec(pool_scale.shape, const2),
            pl.BlockSpec(w_out.shape, const2, pipeline_mode=pl.Buffered(1)),
            pl.BlockSpec(ln_g.shape, const2),
            pl.BlockSpec(ln_b.shape, const2),
        ],
        out_specs=[
            pl.BlockSpec((1, tq, D_MODEL), lambda i: (cur(i) // nt, cur(i) % nt, 0)),
            pl.BlockSpec((1, 1, POOL_BUF, D_POOL), lambda i: (0, cur(i) // nt, 0, 0)),
            pl.BlockSpec((1, 1, NH, DH, DH), lambda i: (0, cur(i) // nt, 0, 0, 0)),
            pl.BlockSpec((1, 1, NH, DH), lambda i: (0, cur(i) // nt, 0, 0)),
            pl.BlockSpec((1, NH, MLSTM_CHUNK), lambda i: (cur(i) // nt, 0, 0)),
        ],
        out_shape=[
            jax.ShapeDtypeStruct((bsz, t_len, D_MODEL), F32),
            jax.ShapeDtypeStruct((DEPTH, bsz, POOL_BUF, D_POOL), F32),
            jax.ShapeDtypeStruct((DEPTH, bsz, NH, DH, DH), F32),
            jax.ShapeDtypeStruct((DEPTH, bsz, NH, DH), F32),
            jax.ShapeDtypeStruct((bsz, NH, MLSTM_CHUNK), F32),
        ],
        scratch_shapes=[
            pltpu.VMEM((D_MODEL, N_LB * LANES), BF16),
            pltpu.VMEM((D_MODEL, D_MODEL), BF16),
            pltpu.VMEM((len(POOL_WINDOWS) // 2, 2 * POOL_GC, 2 * POOL_GC), BF16),
            pltpu.VMEM((tq, D_MODEL), BF16),
            pltpu.VMEM((N_LB, tq, LANES), F32),
            pltpu.VMEM((N_LB, tq, LANES), F32),
            pltpu.VMEM((len(POOL_WINDOWS), POOL_PAD + tq, POOL_GC), F32),
            pltpu.VMEM((tq, D_MODEL), BF16),
            pltpu.VMEM((NH, DH, 2 * DH), F32),
            pltpu.VMEM((NH, MLSTM_CHUNK), F32),
            pltpu.VMEM((n_items, MLSTM_CHUNK, MLSTM_CHUNK + DH), BF16),
            pltpu.VMEM((n_items, MLSTM_CHUNK, 2 * DH), BF16),
            pltpu.VMEM((n_items, DH, 2 * DH), F32),
            pltpu.VMEM((n_items, MLSTM_CHUNK, DH), F32),
        ],
        compiler_params=pltpu.CompilerParams(
            dimension_semantics=("arbitrary",),
            vmem_limit_bytes=V7X_VMEM_LIMIT),
    )(x, x, w_in_t, b_gate2d, w_pool, pool_scale, w_out, ln_g, ln_b)


def _ffn_kernel(x1p_ref, x1s_ref, w1_ref, b1_ref, w2_ref, b2_ref, g_ref, be_ref, yp_ref, ys_ref,
                pre_ref, *, n_prompt_tiles):
    def residual_plus_mlp(x1_ref):
        acc = None
        for c in range(D_FF // FF_CHUNK):
            cols = slice(c * FF_CHUNK, (c + 1) * FF_CHUNK)
            hid = jnp.maximum(_dot(x1_ref[...], w1_ref[:, cols]) + b1_ref[:, cols], 0.0)
            part = _dot(hid * hid, w2_ref[cols, :])
            acc = part if acc is None else acc + part
        return ALPHA * x1_ref[...] + (acc + b2_ref[...])

    step = pl.program_id(0)

    @pl.when(step == 0)
    def _no_tile_yet():
        pre_ref[...] = jnp.zeros(pre_ref.shape, F32)

    @pl.when(step < n_prompt_tiles)
    def _prompt():
        yp_ref[...] = _layer_norm(pre_ref[...], g_ref[...], be_ref[...])
        pre_ref[...] = residual_plus_mlp(x1p_ref)

    @pl.when(step == n_prompt_tiles)
    def _sample():
        yp_ref[...] = _layer_norm(pre_ref[...], g_ref[...], be_ref[...])
        ys_ref[:, 0, :] = _layer_norm(residual_plus_mlp(x1s_ref), g_ref[...], be_ref[...])


def _ffn(x1p, x1s, w1, b1, w2, b2, ln_g, ln_b):
    tile = FFN_TILE
    n_tok = x1p.shape[0]
    assert n_tok % tile == 0
    n_tiles = n_tok // tile
    const2 = lambda i: (0, 0)
    ptile = lambda i: (jnp.minimum(i, n_tiles - 1), 0)
    return pl.pallas_call(
        functools.partial(_ffn_kernel, n_prompt_tiles=n_tiles),
        name="ffn_ln2",
        grid=(n_tiles + 1,),
        in_specs=[
            pl.BlockSpec((tile, D_MODEL), ptile),
            pl.BlockSpec(x1s.shape, const2),
            pl.BlockSpec(w1.shape, const2, pipeline_mode=pl.Buffered(1)),
            pl.BlockSpec(b1.shape, const2),
            pl.BlockSpec(w2.shape, const2, pipeline_mode=pl.Buffered(1)),
            pl.BlockSpec(b2.shape, const2),
            pl.BlockSpec(ln_g.shape, const2),
            pl.BlockSpec(ln_b.shape, const2),
        ],
        out_specs=[pl.BlockSpec((tile, D_MODEL), lambda i: (jnp.maximum(i - 1, 0), 0)),
                   pl.BlockSpec((x1s.shape[0], 1, D_MODEL), lambda i: (0, 0, 0))],
        out_shape=[jax.ShapeDtypeStruct((n_tok, D_MODEL), F32),
                   jax.ShapeDtypeStruct((x1s.shape[0], 1, D_MODEL), F32)],
        scratch_shapes=[pltpu.VMEM((tile, D_MODEL), F32)],
        compiler_params=pltpu.CompilerParams(
            dimension_semantics=("arbitrary",),
            vmem_limit_bytes=V7X_VMEM_LIMIT),
    )(x1p, x1s, w1, b1, w2, b2, ln_g, ln_b)


def _sample_proj_kernel(x_ref, wt_ref, bg_ref, sp_ref, n_ref, m_ref, wp_ref, ps_ref,
                        ypool_ref, pool_ref, q_ref, kd_ref, v_ref, iw_ref, sv_ref, den_ref,
                        floor_ref, osig_ref, nout_ref, mout_ref):
    x = x_ref[:, 0, :]
    nb = x.shape[0]
    proj = _dot_nt(x, wt_ref[0:CUT_G, :])
    u = proj[:, 0:D_POOL]
    gates = _dot_nt(wt_ref[CUT_G:CUT_G + 2 * NH, :], x) + _gate_bias_rows(bg_ref, nb)
    ig_r = gates[0:NH]
    inter_r = jax.nn.log_sigmoid(gates[NH:2 * NH]) + m_ref[...]
    m_t_r = jnp.maximum(inter_r, ig_r)
    dw_r = jnp.exp(ig_r - m_t_r)
    iw_r = jnp.exp(inter_r - m_t_r)
    floor_r = jnp.exp(-m_t_r)
    mout_ref[...] = m_t_r

    def as_column(row):
        return jnp.broadcast_to(row, (DH, nb)).T

    for g, w in enumerate(POOL_WINDOWS):
        lo = g * POOL_GC
        u_g = u[:, lo:lo + POOL_GC]
        acc = u_g
        for r in range(POOL_PAD - w, POOL_BUF):
            acc = acc + sp_ref[r, :, lo:lo + POOL_GC]
        pooled = acc / float(min(PAST_LEN + 1, w)) - u_g
        ypool_ref[:, lo:lo + POOL_GC] = _dot(pooled, wp_ref[g]) * ps_ref[:, lo:lo + POOL_GC]
    pool_ref[0:POOL_BUF - 1] = sp_ref[1:POOL_BUF]
    pool_ref[POOL_BUF - 1] = u

    for h in range(NH):
        col = slice(h * DH, (h + 1) * DH)
        dw = as_column(dw_r[h:h + 1])
        iw = as_column(iw_r[h:h + 1])
        q = proj[:, D_POOL + h * DH:D_POOL + (h + 1) * DH]
        k = proj[:, CUT_K + h * DH:CUT_K + (h + 1) * DH] * K_SCALE
        v = proj[:, CUT_V + h * DH:CUT_V + (h + 1) * DH]
        o = proj[:, CUT_V + D_MLSTM + h * DH:CUT_V + D_MLSTM + (h + 1) * DH]
        n_old = n_ref[:, h, :]
        s = jnp.sum(q * k, axis=-1, keepdims=True) * dw
        q_ref[:, col] = q
        kd_ref[:, col] = dw * k
        v_ref[:, col] = v
        iw_ref[:, col] = iw
        sv_ref[:, col] = s * v
        den_ref[:, col] = s + iw * jnp.sum(q * n_old, axis=-1, keepdims=True)
        floor_ref[:, col] = as_column(floor_r[h:h + 1])
        osig_ref[:, col] = jax.nn.sigmoid(o)
        nout_ref[:, h, :] = iw * n_old + dw * k


def _sample_proj(x, w_in_t, b_gate2d, pool_rows, n_state, m_rows, w_pool, pool_scale):
    nb = x.shape[0]
    wide = jax.ShapeDtypeStruct((nb, D_MLSTM), F32)
    vmem = pl.BlockSpec(memory_space=pltpu.VMEM)
    return pl.pallas_call(
        _sample_proj_kernel,
        name="sample_proj",
        in_specs=[vmem, vmem, pl.BlockSpec(memory_space=pltpu.SMEM), vmem, vmem, vmem, vmem, vmem],
        out_shape=([wide, jax.ShapeDtypeStruct(pool_rows.shape, F32)] + [wide] * 8
                   + [jax.ShapeDtypeStruct(n_state.shape, F32), jax.ShapeDtypeStruct(m_rows.shape, F32)]),
        compiler_params=pltpu.CompilerParams(vmem_limit_bytes=V7X_VMEM_LIMIT),
    )(x, w_in_t, b_gate2d, pool_rows, n_state, m_rows, w_pool, pool_scale)


def _sample_state_kernel(c_ref, q_ref, kd_ref, v_ref, iw_ref, cout_ref, inter_ref):
    nb = SAMPLE_BLOCK
    pad = jnp.zeros((DH - nb, DH), F32)
    for h in range(NH):
        col = slice(h * DH, (h + 1) * DH)
        q_t = jnp.concatenate([q_ref[:, col], pad], axis=0).T
        kd_t = jnp.concatenate([kd_ref[:, col], pad], axis=0).T
        for j in range(nb):
            c_old = c_ref[0, j, h]
            inter_ref[j:j + 1, col] = jnp.sum(q_t[:, j:j + 1] * c_old, axis=0, keepdims=True)
            cout_ref[0, j, h] = iw_ref[j:j + 1, col] * c_old + kd_t[:, j:j + 1] * v_ref[j:j + 1, col]


def _sample_state(state_c, q, kd, v, iw):
    nb = q.shape[0]
    blk = SAMPLE_BLOCK
    assert nb % blk == 0
    row_spec = pl.BlockSpec((blk, D_MLSTM), lambda i: (i, 0))
    c_spec = pl.BlockSpec((1, blk, NH, DH, DH), lambda i: (0, i, 0, 0, 0))
    return pl.pallas_call(
        _sample_state_kernel,
        name="sample_state",
        grid=(nb // blk,),
        in_specs=[c_spec, row_spec, row_spec, row_spec, row_spec],
        out_specs=[c_spec, row_spec],
        out_shape=[jax.ShapeDtypeStruct(state_c.shape, F32),
                   jax.ShapeDtypeStruct((nb, D_MLSTM), F32)],
        compiler_params=pltpu.CompilerParams(
            dimension_semantics=("arbitrary",),
            vmem_limit_bytes=V7X_VMEM_LIMIT),
    )(state_c, q, kd, v, iw)


def _sample_out_kernel(x_ref, ypool_ref, sv_ref, iw_ref, inter_ref, den_ref, floor_ref, osig_ref,
                       wo_ref, g1_ref, b1_ref, x1_ref):
    hh = (sv_ref[...] + iw_ref[...] * inter_ref[...]) / jnp.maximum(jnp.abs(den_ref[...]), floor_ref[...])
    mixin = jnp.concatenate([ypool_ref[...], osig_ref[...] * hh], axis=-1)
    x1_ref[...] = _layer_norm(ALPHA * x_ref[:, 0, :] + _dot(mixin, wo_ref[...]), g1_ref[...], b1_ref[...])


def _sample_out(x, ypool, sv, iw, inter, den, floor, osig, w_out, ln_g, ln_b):
    return pl.pallas_call(
        _sample_out_kernel,
        name="sample_out",
        out_shape=jax.ShapeDtypeStruct((x.shape[0], D_MODEL), F32),
        compiler_params=pltpu.CompilerParams(vmem_limit_bytes=V7X_VMEM_LIMIT),
    )(x, ypool, sv, iw, inter, den, floor, osig, w_out, ln_g, ln_b)


def kernel(x_prompt, x_sample, state_pool, state_C, state_n, state_m, w_in, b_gate, w_pool, pool_scale,
           w_out, ln1_g, ln1_b, w_ff1, b_ff1, w_ff2, b_ff2, ln2_g, ln2_b):
    assert w_in.shape[0] == DEPTH == 1
    bp, t_len, _ = x_prompt.shape
    bs = x_sample.shape[0]
    assert x_sample.shape[1] == 1

    w_t = jnp.transpose(w_in[0])
    wp = w_pool[0]
    ps = pool_scale[0].reshape(1, D_POOL)
    wo = w_out[0]
    g1, b1 = ln1_g[0].reshape(1, D_MODEL), ln1_b[0].reshape(1, D_MODEL)
    g2, b2 = ln2_g[0].reshape(1, D_MODEL), ln2_b[0].reshape(1, D_MODEL)
    wf1, wf2 = w_ff1[0], w_ff2[0]
    bf1, bf2 = b_ff1[0].reshape(1, D_FF), b_ff2[0].reshape(1, D_MODEL)

    x1_p, pool_p, c_p, n_p, m_rep = _prompt_mixer(x_prompt, w_t, b_gate, wp, ps, wo, g1, b1)
    m_p = m_rep[:, :, 0].reshape(DEPTH, bp, NH)

    (ypool, pool_s, q_s, kd_s, v_s, iw_s, sv_s, den_s, floor_s, osig_s, n_s, m_s_rows) = _sample_proj(
        x_sample, w_t, b_gate, jnp.transpose(state_pool[0], (1, 0, 2)),
        state_n[0], jnp.transpose(state_m[0]), wp, ps)
    c_s, inter_s = _sample_state(state_C, q_s, kd_s, v_s, iw_s)
    x1_s = _sample_out(x_sample, ypool, sv_s, iw_s, inter_s, den_s, floor_s, osig_s, wo, g1, b1)

    y_p, y_s = _ffn(x1_p.reshape(bp * t_len, D_MODEL), x1_s, wf1, bf1, wf2, bf2, g2, b2)

    return (y_p.reshape(bp, t_len, D_MODEL), y_s,
            pool_p, c_p, n_p, m_p,
            jnp.transpose(pool_s, (1, 0, 2))[None], c_s,
            n_s[None], jnp.transpose(m_s_rows)[None])
```

```python
import functools

import jax
import jax.numpy as jnp
from jax import lax
from jax.experimental import pallas as pl
from jax.experimental.pallas import tpu as pltpu

F32 = jnp.float32
BF16 = jnp.bfloat16

D_MODEL = 1024
D_POOL = D_MODEL // 2
D_MLSTM = D_MODEL - D_POOL
POOL_WINDOWS = (2, 4, 8, 16)
POOL_GC = D_POOL // len(POOL_WINDOWS)
POOL_BUF = max(POOL_WINDOWS) - 1
POOL_PAD = POOL_BUF + 1
NH = 4
DH = D_MLSTM // NH
D_FF = 4 * D_MODEL
DEPTH = 1
PAST_LEN = 16384
ALPHA = (2.0 * DEPTH) ** 0.25
LN_EPS = 1e-5
K_SCALE = DH ** -0.5

MLSTM_CHUNK = 128
MIX_TILE = 512
FFN_TILE = 512
FF_CHUNK = 2048
SAMPLE_BLOCK = 16
PROJ_COLS = 256
LANES = 128
assert POOL_GC == LANES and DH == LANES
CUT_K = D_POOL + D_MLSTM
CUT_V = CUT_K + D_MLSTM
CUT_G = D_POOL + 4 * D_MLSTM
LB_Q = D_POOL // LANES
LB_K = CUT_K // LANES
LB_V = CUT_V // LANES
LB_O = LB_V + NH
LB_G = CUT_G // LANES
N_LB = LB_G + 1
V7X_VMEM_LIMIT = 56 * 1024 * 1024


def _layer_norm(y, g, b):
    mu = jnp.mean(y, axis=-1, keepdims=True)
    yc = y - mu
    var = jnp.mean(yc * yc, axis=-1, keepdims=True)
    return yc * lax.rsqrt(var + LN_EPS) * g + b


def _dot(a, b):
    return jnp.dot(a, b, preferred_element_type=F32)


def _dot_nt(a, b):
    return lax.dot_general(a, b, (((1,), (1,)), ((), ())), preferred_element_type=F32)


def _twice(row):
    return jnp.concatenate([row, row], axis=-1)


def _gate_bias_rows(bg_ref, width):
    row = lax.broadcasted_iota(jnp.int32, (2 * NH, width), 0)
    bias = jnp.zeros((2 * NH, width), F32)
    for g in range(2 * NH):
        bias = jnp.where(row == g, bg_ref[0, g], bias)
    return bias


def _mixer_kernel(xn_ref, xc_ref, wt_ref, bg_ref, wp_ref, ps_ref, wo_ref, g1_ref, b1_ref,
                  x1_ref, pool_ref, c_ref, n_ref, m_ref,
                  win_ref, wob_ref, xb_ref, pm_a, pm_b, ext_ref, mix_ref, caug_ref, mst_ref,
                  lhs_ref, vaug_ref, kv_ref, floor_ref, *, tiles_per_seq):
    tq = MIX_TILE
    L = MLSTM_CHUNK
    step = pl.program_id(0)
    t_idx = lax.rem(step + tiles_per_seq - 1, tiles_per_seq)
    last_t = tiles_per_seq - 1
    parity = lax.rem(step, 2)

    def project(pm_next):
        xb_ref[...] = xn_ref[0].astype(BF16)

        def piece(c0, c1):
            res = _dot(xb_ref[...], win_ref[:, c0:c1])
            for i in range((c1 - c0) // LANES):
                pm_next[c0 // LANES + i] = res[:, i * LANES:(i + 1) * LANES]

        n_cols = N_LB * LANES
        return [functools.partial(piece, c0, min(c0 + PROJ_COLS, n_cols))
                for c0 in range(0, n_cols, PROJ_COLS)]

    @pl.when(step == 0)
    def _first_step():
        for j in range(CUT_G // PROJ_COLS):
            win_ref[:, j * PROJ_COLS:(j + 1) * PROJ_COLS] = (
                wt_ref[j * PROJ_COLS:(j + 1) * PROJ_COLS, :].T.astype(BF16))
        gate_rows = jnp.concatenate([wt_ref[CUT_G:CUT_G + 2 * NH, :],
                                     jnp.zeros((LANES - 2 * NH, D_MODEL), F32)], axis=0)
        win_ref[:, CUT_G:CUT_G + LANES] = gate_rows.T.astype(BF16)
        wob_ref[...] = wo_ref[...].astype(BF16)
        for piece in project(pm_a):
            piece()

    @pl.when(t_idx == 0)
    def _init():
        ext_ref[:, 0:POOL_PAD, :] = jnp.zeros((len(POOL_WINDOWS), POOL_PAD, POOL_GC), F32)
        caug_ref[...] = jnp.zeros(caug_ref.shape, F32)
        mst_ref[...] = jnp.zeros(mst_ref.shape, F32)

    tt = lax.broadcasted_iota(jnp.int32, (L, L), 0)
    ss = lax.broadcasted_iota(jnp.int32, (L, L), 1)
    causal = ss <= tt
    diag = ss == tt

    def run(pm_next, pm_cur):
        pieces = project(pm_next)
        n_chunks = tq // L
        n_slots = len(POOL_WINDOWS) + 2 * n_chunks * NH
        slot_of = [(k * n_slots) // len(pieces) for k in range(len(pieces))]
        slot = [0]

        def next_slot():
            for k, piece in enumerate(pieces):
                if slot_of[k] == slot[0]:
                    piece()
            slot[0] += 1

        pos = t_idx * tq + lax.broadcasted_iota(jnp.int32, (tq, POOL_GC), 0)
        for g, w in enumerate(POOL_WINDOWS):
            lo = g * POOL_GC
            u_g = pm_cur[g]
            ext_ref[g, POOL_PAD:POOL_PAD + tq, :] = u_g
            s = ext_ref[g]
            k = 1
            while k < w:
                s = s + pltpu.roll(s, k, axis=0)
                k *= 2
            cnt = jnp.minimum(pos + 1, w).astype(F32)
            pooled = s[POOL_PAD:, :] / cnt - u_g
            mix_ref[:, lo:lo + POOL_GC] = (_dot(pooled, wp_ref[g]) * ps_ref[:, lo:lo + POOL_GC]).astype(BF16)
            ext_ref[g, 0:POOL_PAD, :] = ext_ref[g, tq:tq + POOL_PAD, :]
            next_slot()

        lane = lax.broadcasted_iota(jnp.int32, (NH, L), 1)
        gate_bias = _gate_bias_rows(bg_ref, L)
        m_prev = mst_ref[...]
        chunk_rows = []
        for c in range(n_chunks):
            rows = slice(c * L, (c + 1) * L)
            gates = pm_cur[LB_G, rows, :].T[0:2 * NH, :] + gate_bias
            lf = jax.nn.log_sigmoid(gates[NH:2 * NH])
            b = lf
            k = 1
            while k < L:
                b = b + jnp.where(lane >= k, pltpu.roll(b, k, axis=1), 0.0)
                k *= 2
            a = gates[0:NH] - b
            cmax = a
            k = 1
            while k < L:
                cmax = jnp.maximum(cmax, jnp.where(lane >= k, pltpu.roll(cmax, k, axis=1), -jnp.inf))
                k *= 2
            amax = jnp.max(a, axis=-1, keepdims=True)
            big_m = jnp.maximum(cmax, m_prev)
            mm = jnp.maximum(m_prev, amax)
            chunk_rows.append(dict(
                a=a, big_m=big_m,
                iw=jnp.exp(m_prev - big_m),
                floor=jnp.exp(-(b + big_m)),
                w_loc=jnp.exp(a - amax),
                g_state=jnp.exp(m_prev - mm),
                f_state=jnp.exp(amax - mm)))
            m_prev = jnp.sum(lf, axis=-1, keepdims=True) + mm
        mst_ref[...] = m_prev

        def as_column(row):
            return jnp.broadcast_to(row, (L, L)).T

        ones_blk = jnp.ones((L, DH), BF16)
        for c in range(n_chunks):
            rows = slice(c * L, (c + 1) * L)
            cr = chunk_rows[c]
            for h in range(NH):
                i = c * NH + h
                q = pm_cur[LB_Q + h, rows, :]
                qb = q.astype(BF16)
                kt = pm_cur[LB_K + h, rows, :].T * K_SCALE
                v_aug = jnp.concatenate([pm_cur[LB_V + h, rows, :].astype(BF16), ones_blk], axis=1)
                p = jnp.exp(jnp.where(causal, cr["a"][h:h + 1], -jnp.inf) - as_column(cr["big_m"][h:h + 1]))
                lhs_ref[i, :, 0:L] = (_dot(qb, kt.astype(BF16)) * p).astype(BF16)
                vaug_ref[i] = v_aug
                kv_ref[i] = _dot((kt * cr["w_loc"][h:h + 1]).astype(BF16), v_aug)
                lhs_ref[i, :, L:L + DH] = (q * as_column(cr["iw"][h:h + 1])).astype(BF16)
                floor_ref[i] = as_column(cr["floor"][h:h + 1])
                next_slot()

        for c in range(n_chunks):
            rows = slice(c * L, (c + 1) * L)
            cr = chunk_rows[c]
            for h in range(NH):
                i = c * NH + h
                caug = caug_ref[h]
                comb = _dot(lhs_ref[i], jnp.concatenate([vaug_ref[i], caug.astype(BF16)], axis=0))
                hh = comb[:, :DH] / jnp.maximum(jnp.abs(comb[:, DH:]), floor_ref[i])
                o = pm_cur[LB_O + h, rows, :]
                mix_ref[rows, D_POOL + h * DH:D_POOL + (h + 1) * DH] = (
                    jax.nn.sigmoid(o) * hh).astype(BF16)
                caug_ref[h] = (_twice(cr["g_state"][h:h + 1]) * caug
                               + _twice(cr["f_state"][h:h + 1]) * kv_ref[i])
                next_slot()

        mix = _dot(mix_ref[...], wob_ref[...])
        x1_ref[0] = _layer_norm(ALPHA * xc_ref[0] + mix, g1_ref[...], b1_ref[...])

    @pl.when((parity == 0) & (step > 0))
    def _even():
        run(pm_a, pm_b)

    @pl.when(parity == 1)
    def _odd():
        run(pm_b, pm_a)

    @pl.when((t_idx == last_t) & (step > 0))
    def _final_state():
        for g in range(len(POOL_WINDOWS)):
            pool_ref[0, 0, :, g * POOL_GC:(g + 1) * POOL_GC] = ext_ref[g, tq + 1:tq + POOL_PAD, :]
        for h in range(NH):
            caug = caug_ref[h]
            c_ref[0, 0, h] = caug[:, :DH]
            n_ref[0, 0, h:h + 1, :] = jnp.sum(jnp.where(diag, caug[:, DH:], 0.0),
                                              axis=0, keepdims=True)
        m_ref[0] = mst_ref[...]


def _prompt_mixer(x, w_in_t, b_gate2d, w_pool, pool_scale, w_out, ln_g, ln_b):
    bsz, t_len, _ = x.shape
    tq = MIX_TILE
    assert t_len % tq == 0 and tq % MLSTM_CHUNK == 0 and tq >= POOL_PAD
    nt = t_len // tq
    n_tiles = bsz * nt
    n_items = (tq // MLSTM_CHUNK) * NH
    const2 = lambda i: (0, 0)
    nxt = lambda i: jnp.minimum(i, n_tiles - 1)
    cur = lambda i: jnp.maximum(i - 1, 0)
    return pl.pallas_call(
        functools.partial(_mixer_kernel, tiles_per_seq=nt),
        name="prompt_mixer",
        grid=(n_tiles + 1,),
        in_specs=[
            pl.BlockSpec((1, tq, D_MODEL), lambda i: (nxt(i) // nt, nxt(i) % nt, 0)),
            pl.BlockSpec((1, tq, D_MODEL), lambda i: (cur(i) // nt, cur(i) % nt, 0)),
            pl.BlockSpec(w_in_t.shape, const2, pipeline_mode=pl.Buffered(1)),
            pl.BlockSpec(memory_space=pltpu.SMEM),
            pl.BlockSpec(w_pool.shape, lambda i: (0, 0, 0)),
            pl.BlockSpec(pool_scale.shape, const2),
            pl.BlockSpec(w_out.shape, const2, pipeline_mode=pl.Buffered(1)),
            pl.BlockSpec(ln_g.shape, const2),
            pl.BlockSpec(ln_b.shape, const2),
        ],
        out_specs=[
            pl.BlockSpec((1, tq, D_MODEL), lambda i: (cur(i) // nt, cur(i) % nt, 0)),
            pl.BlockSpec((1, 1, POOL_BUF, D_POOL), lambda i: (0, cur(i) // nt, 0, 0)),
            pl.BlockSpec((1, 1, NH, DH, DH), lambda i: (0, cur(i) // nt, 0, 0, 0)),
            pl.BlockSpec((1, 1, NH, DH), lambda i: (0, cur(i) // nt, 0, 0)),
            pl.BlockSpec((1, NH, MLSTM_CHUNK), lambda i: (cur(i) // nt, 0, 0)),
        ],
        out_shape=[
            jax.ShapeDtypeStruct((bsz, t_len, D_MODEL), F32),
            jax.ShapeDtypeStruct((DEPTH, bsz, POOL_BUF, D_POOL), F32),
            jax.ShapeDtypeStruct((DEPTH, bsz, NH, DH, DH), F32),
            jax.ShapeDtypeStruct((DEPTH, bsz, NH, DH), F32),
            jax.ShapeDtypeStruct((bsz, NH, MLSTM_CHUNK), F32),
        ],
        scratch_shapes=[
            pltpu.VMEM((D_MODEL, N_LB * LANES), BF16),
            pltpu.VMEM((D_MODEL, D_MODEL), BF16),
            pltpu.VMEM((tq, D_MODEL), BF16),
            pltpu.VMEM((N_LB, tq, LANES), F32),
            pltpu.VMEM((N_LB, tq, LANES), F32),
            pltpu.VMEM((len(POOL_WINDOWS), POOL_PAD + tq, POOL_GC), F32),
            pltpu.VMEM((tq, D_MODEL), BF16),
            pltpu.VMEM((NH, DH, 2 * DH), F32),
            pltpu.VMEM((NH, MLSTM_CHUNK), F32),
            pltpu.VMEM((n_items, MLSTM_CHUNK, MLSTM_CHUNK + DH), BF16),
            pltpu.VMEM((n_items, MLSTM_CHUNK, 2 * DH), BF16),
            pltpu.VMEM((n_items, DH, 2 * DH), F32),
            pltpu.VMEM((n_items, MLSTM_CHUNK, DH), F32),
        ],
        compiler_params=pltpu.CompilerParams(
            dimension_semantics=("arbitrary",),
            vmem_limit_bytes=V7X_VMEM_LIMIT),
    )(x, x, w_in_t, b_gate2d, w_pool, pool_scale, w_out, ln_g, ln_b)


def _ffn_kernel(x1p_ref, x1s_ref, w1_ref, b1_ref, w2_ref, b2_ref, g_ref, be_ref, yp_ref, ys_ref,
                pre_ref, *, n_prompt_tiles):
    def residual_plus_mlp(x1_ref):
        acc = None
        for c in range(D_FF // FF_CHUNK):
            cols = slice(c * FF_CHUNK, (c + 1) * FF_CHUNK)
            hid = jnp.maximum(_dot(x1_ref[...], w1_ref[:, cols]) + b1_ref[:, cols], 0.0)
            part = _dot(hid * hid, w2_ref[cols, :])
            acc = part if acc is None else acc + part
        return ALPHA * x1_ref[...] + (acc + b2_ref[...])

    step = pl.program_id(0)

    @pl.when(step == 0)
    def _no_tile_yet():
        pre_ref[...] = jnp.zeros(pre_ref.shape, F32)

    @pl.when(step < n_prompt_tiles)
    def _prompt():
        yp_ref[...] = _layer_norm(pre_ref[...], g_ref[...], be_ref[...])
        pre_ref[...] = residual_plus_mlp(x1p_ref)

    @pl.when(step == n_prompt_tiles)
    def _sample():
        yp_ref[...] = _layer_norm(pre_ref[...], g_ref[...], be_ref[...])
        ys_ref[:, 0, :] = _layer_norm(residual_plus_mlp(x1s_ref), g_ref[...], be_ref[...])


def _ffn(x1p, x1s, w1, b1, w2, b2, ln_g, ln_b):
    tile = FFN_TILE
    n_tok = x1p.shape[0]
    assert n_tok % tile == 0
    n_tiles = n_tok // tile
    const2 = lambda i: (0, 0)
    ptile = lambda i: (jnp.minimum(i, n_tiles - 1), 0)
    return pl.pallas_call(
        functools.partial(_ffn_kernel, n_prompt_tiles=n_tiles),
        name="ffn_ln2",
        grid=(n_tiles + 1,),
        in_specs=[
            pl.BlockSpec((tile, D_MODEL), ptile),
            pl.BlockSpec(x1s.shape, const2),
            pl.BlockSpec(w1.shape, const2, pipeline_mode=pl.Buffered(1)),
            pl.BlockSpec(b1.shape, const2),
            pl.BlockSpec(w2.shape, const2, pipeline_mode=pl.Buffered(1)),
            pl.BlockSpec(b2.shape, const2),
            pl.BlockSpec(ln_g.shape, const2),
            pl.BlockSpec(ln_b.shape, const2),
        ],
        out_specs=[pl.BlockSpec((tile, D_MODEL), lambda i: (jnp.maximum(i - 1, 0), 0)),
                   pl.BlockSpec((x1s.shape[0], 1, D_MODEL), lambda i: (0, 0, 0))],
        out_shape=[jax.ShapeDtypeStruct((n_tok, D_MODEL), F32),
                   jax.ShapeDtypeStruct((x1s.shape[0], 1, D_MODEL), F32)],
        scratch_shapes=[pltpu.VMEM((tile, D_MODEL), F32)],
        compiler_params=pltpu.CompilerParams(
            dimension_semantics=("arbitrary",),
            vmem_limit_bytes=V7X_VMEM_LIMIT),
    )(x1p, x1s, w1, b1, w2, b2, ln_g, ln_b)


def _sample_proj_kernel(x_ref, wt_ref, bg_ref, sp_ref, n_ref, m_ref, wp_ref, ps_ref,
                        ypool_ref, pool_ref, q_ref, kd_ref, v_ref, iw_ref, sv_ref, den_ref,
                        floor_ref, osig_ref, nout_ref, mout_ref):
    x = x_ref[:, 0, :]
    nb = x.shape[0]
    proj = _dot_nt(x, wt_ref[0:CUT_G, :])
    u = proj[:, 0:D_POOL]
    gates = _dot_nt(wt_ref[CUT_G:CUT_G + 2 * NH, :], x) + _gate_bias_rows(bg_ref, nb)
    ig_r = gates[0:NH]
    inter_r = jax.nn.log_sigmoid(gates[NH:2 * NH]) + m_ref[...]
    m_t_r = jnp.maximum(inter_r, ig_r)
    dw_r = jnp.exp(ig_r - m_t_r)
    iw_r = jnp.exp(inter_r - m_t_r)
    floor_r = jnp.exp(-m_t_r)
    mout_ref[...] = m_t_r

    def as_column(row):
        return jnp.broadcast_to(row, (DH, nb)).T

    for g, w in enumerate(POOL_WINDOWS):
        lo = g * POOL_GC
        u_g = u[:, lo:lo + POOL_GC]
        acc = u_g
        for r in range(POOL_PAD - w, POOL_BUF):
            acc = acc + sp_ref[r, :, lo:lo + POOL_GC]
        pooled = acc / float(min(PAST_LEN + 1, w)) - u_g
        ypool_ref[:, lo:lo + POOL_GC] = _dot(pooled, wp_ref[g]) * ps_ref[:, lo:lo + POOL_GC]
    pool_ref[0:POOL_BUF - 1] = sp_ref[1:POOL_BUF]
    pool_ref[POOL_BUF - 1] = u

    for h in range(NH):
        col = slice(h * DH, (h + 1) * DH)
        dw = as_column(dw_r[h:h + 1])
        iw = as_column(iw_r[h:h + 1])
        q = proj[:, D_POOL + h * DH:D_POOL + (h + 1) * DH]
        k = proj[:, CUT_K + h * DH:CUT_K + (h + 1) * DH] * K_SCALE
        v = proj[:, CUT_V + h * DH:CUT_V + (h + 1) * DH]
        o = proj[:, CUT_V + D_MLSTM + h * DH:CUT_V + D_MLSTM + (h + 1) * DH]
        n_old = n_ref[:, h, :]
        s = jnp.sum(q * k, axis=-1, keepdims=True) * dw
        q_ref[:, col] = q
        kd_ref[:, col] = dw * k
        v_ref[:, col] = v
        iw_ref[:, col] = iw
        sv_ref[:, col] = s * v
        den_ref[:, col] = s + iw * jnp.sum(q * n_old, axis=-1, keepdims=True)
        floor_ref[:, col] = as_column(floor_r[h:h + 1])
        osig_ref[:, col] = jax.nn.sigmoid(o)
        nout_ref[:, h, :] = iw * n_old + dw * k


def _sample_proj(x, w_in_t, b_gate2d, pool_rows, n_state, m_rows, w_pool, pool_scale):
    nb = x.shape[0]
    wide = jax.ShapeDtypeStruct((nb, D_MLSTM), F32)
    vmem = pl.BlockSpec(memory_space=pltpu.VMEM)
    return pl.pallas_call(
        _sample_proj_kernel,
        name="sample_proj",
        in_specs=[vmem, vmem, pl.BlockSpec(memory_space=pltpu.SMEM), vmem, vmem, vmem, vmem, vmem],
        out_shape=([wide, jax.ShapeDtypeStruct(pool_rows.shape, F32)] + [wide] * 8
                   + [jax.ShapeDtypeStruct(n_state.shape, F32), jax.ShapeDtypeStruct(m_rows.shape, F32)]),
        compiler_params=pltpu.CompilerParams(vmem_limit_bytes=V7X_VMEM_LIMIT),
    )(x, w_in_t, b_gate2d, pool_rows, n_state, m_rows, w_pool, pool_scale)


def _sample_state_kernel(c_ref, q_ref, kd_ref, v_ref, iw_ref, cout_ref, inter_ref):
    nb = SAMPLE_BLOCK
    pad = jnp.zeros((DH - nb, DH), F32)
    for h in range(NH):
        col = slice(h * DH, (h + 1) * DH)
        q_t = jnp.concatenate([q_ref[:, col], pad], axis=0).T
        kd_t = jnp.concatenate([kd_ref[:, col], pad], axis=0).T
        for j in range(nb):
            c_old = c_ref[0, j, h]
            inter_ref[j:j + 1, col] = jnp.sum(q_t[:, j:j + 1] * c_old, axis=0, keepdims=True)
            cout_ref[0, j, h] = iw_ref[j:j + 1, col] * c_old + kd_t[:, j:j + 1] * v_ref[j:j + 1, col]


def _sample_state(state_c, q, kd, v, iw):
    nb = q.shape[0]
    blk = SAMPLE_BLOCK
    assert nb % blk == 0
    row_spec = pl.BlockSpec((blk, D_MLSTM), lambda i: (i, 0))
    c_spec = pl.BlockSpec((1, blk, NH, DH, DH), lambda i: (0, i, 0, 0, 0))
    return pl.pallas_call(
        _sample_state_kernel,
        name="sample_state",
        grid=(nb // blk,),
        in_specs=[c_spec, row_spec, row_spec, row_spec, row_spec],
        out_specs=[c_spec, row_spec],
        out_shape=[jax.ShapeDtypeStruct(state_c.shape, F32),
                   jax.ShapeDtypeStruct((nb, D_MLSTM), F32)],
        compiler_params=pltpu.CompilerParams(
            dimension_semantics=("arbitrary",),
            vmem_limit_bytes=V7X_VMEM_LIMIT),
    )(state_c, q, kd, v, iw)


def _sample_out_kernel(x_ref, ypool_ref, sv_ref, iw_ref, inter_ref, den_ref, floor_ref, osig_ref,
                       wo_ref, g1_ref, b1_ref, x1_ref):
    hh = (sv_ref[...] + iw_ref[...] * inter_ref[...]) / jnp.maximum(jnp.abs(den_ref[...]), floor_ref[...])
    mixin = jnp.concatenate([ypool_ref[...], osig_ref[...] * hh], axis=-1)
    x1_ref[...] = _layer_norm(ALPHA * x_ref[:, 0, :] + _dot(mixin, wo_ref[...]), g1_ref[...], b1_ref[...])


def _sample_out(x, ypool, sv, iw, inter, den, floor, osig, w_out, ln_g, ln_b):
    return pl.pallas_call(
        _sample_out_kernel,
        name="sample_out",
        out_shape=jax.ShapeDtypeStruct((x.shape[0], D_MODEL), F32),
        compiler_params=pltpu.CompilerParams(vmem_limit_bytes=V7X_VMEM_LIMIT),
    )(x, ypool, sv, iw, inter, den, floor, osig, w_out, ln_g, ln_b)


def kernel(x_prompt, x_sample, state_pool, state_C, state_n, state_m, w_in, b_gate, w_pool, pool_scale,
           w_out, ln1_g, ln1_b, w_ff1, b_ff1, w_ff2, b_ff2, ln2_g, ln2_b):
    assert w_in.shape[0] == DEPTH == 1
    bp, t_len, _ = x_prompt.shape
    bs = x_sample.shape[0]
    assert x_sample.shape[1] == 1

    w_t = jnp.transpose(w_in[0])
    wp = w_pool[0]
    ps = pool_scale[0].reshape(1, D_POOL)
    wo = w_out[0]
    g1, b1 = ln1_g[0].reshape(1, D_MODEL), ln1_b[0].reshape(1, D_MODEL)
    g2, b2 = ln2_g[0].reshape(1, D_MODEL), ln2_b[0].reshape(1, D_MODEL)
    wf1, wf2 = w_ff1[0], w_ff2[0]
    bf1, bf2 = b_ff1[0].reshape(1, D_FF), b_ff2[0].reshape(1, D_MODEL)

    x1_p, pool_p, c_p, n_p, m_rep = _prompt_mixer(x_prompt, w_t, b_gate, wp, ps, wo, g1, b1)
    m_p = m_rep[:, :, 0].reshape(DEPTH, bp, NH)

    (ypool, pool_s, q_s, kd_s, v_s, iw_s, sv_s, den_s, floor_s, osig_s, n_s, m_s_rows) = _sample_proj(
        x_sample, w_t, b_gate, jnp.transpose(state_pool[0], (1, 0, 2)),
        state_n[0], jnp.transpose(state_m[0]), wp, ps)
    c_s, inter_s = _sample_state(state_C, q_s, kd_s, v_s, iw_s)
    x1_s = _sample_out(x_sample, ypool, sv_s, iw_s, inter_s, den_s, floor_s, osig_s, wo, g1, b1)

    y_p, y_s = _ffn(x1_p.reshape(bp * t_len, D_MODEL), x1_s, wf1, bf1, wf2, bf2, g2, b2)

    return (y_p.reshape(bp, t_len, D_MODEL), y_s,
            pool_p, c_p, n_p, m_p,
            jnp.transpose(pool_s, (1, 0, 2))[None], c_s,
            n_s[None], jnp.transpose(m_s_rows)[None])
```

```python
import functools

import jax
import jax.numpy as jnp
from jax import lax
from jax.experimental import pallas as pl
from jax.experimental.pallas import tpu as pltpu

F32 = jnp.float32
BF16 = jnp.bfloat16

D_MODEL = 1024
D_POOL = D_MODEL // 2
D_MLSTM = D_MODEL - D_POOL
POOL_WINDOWS = (2, 4, 8, 16)
POOL_GC = D_POOL // len(POOL_WINDOWS)
POOL_BUF = max(POOL_WINDOWS) - 1
POOL_PAD = POOL_BUF + 1
NH = 4
DH = D_MLSTM // NH
D_FF = 4 * D_MODEL
DEPTH = 1
PAST_LEN = 16384
ALPHA = (2.0 * DEPTH) ** 0.25
LN_EPS = 1e-5
K_SCALE = DH ** -0.5

MLSTM_CHUNK = 128
MIX_TILE = 512
FFN_TILE = 512
FF_CHUNK = 2048
SAMPLE_BLOCK = 16
PROJ_COLS = 256
LANES = 128
assert POOL_GC == LANES and DH == LANES
CUT_K = D_POOL + D_MLSTM
CUT_V = CUT_K + D_MLSTM
CUT_G = D_POOL + 4 * D_MLSTM
LB_Q = D_POOL // LANES
LB_K = CUT_K // LANES
LB_V = CUT_V // LANES
LB_O = LB_V + NH
LB_G = CUT_G // LANES
N_LB = LB_G + 1
V7X_VMEM_LIMIT = 56 * 1024 * 1024


def _layer_norm(y, g, b):
    mu = jnp.mean(y, axis=-1, keepdims=True)
    yc = y - mu
    var = jnp.mean(yc * yc, axis=-1, keepdims=True)
    return yc * lax.rsqrt(var + LN_EPS) * g + b


def _dot(a, b):
    return jnp.dot(a, b, preferred_element_type=F32)


def _dot_nt(a, b):
    return lax.dot_general(a, b, (((1,), (1,)), ((), ())), preferred_element_type=F32)


def _twice(row):
    return jnp.concatenate([row, row], axis=-1)


def _gate_bias_rows(bg_ref, width):
    row = lax.broadcasted_iota(jnp.int32, (2 * NH, width), 0)
    bias = jnp.zeros((2 * NH, width), F32)
    for g in range(2 * NH):
        bias = jnp.where(row == g, bg_ref[0, g], bias)
    return bias


def _mixer_kernel(xn_ref, xc_ref, wt_ref, bg_ref, wp_ref, ps_ref, wo_ref, g1_ref, b1_ref,
                  x1_ref, pool_ref, c_ref, n_ref, m_ref,
                  win_ref, wob_ref, xb_ref, pm_a, pm_b, ext_ref, mix_ref, caug_ref, mst_ref,
                  lhs_ref, vaug_ref, ktw_ref, gbias_ref, *, tiles_per_seq):
    tq = MIX_TILE
    L = MLSTM_CHUNK
    step = pl.program_id(0)
    t_idx = lax.rem(step + tiles_per_seq - 1, tiles_per_seq)
    last_t = tiles_per_seq - 1
    parity = lax.rem(step, 2)

    def project(pm_next):
        xb_ref[...] = xn_ref[0].astype(BF16)

        def piece(c0, c1):
            res = _dot(xb_ref[...], win_ref[:, c0:c1])
            for i in range((c1 - c0) // LANES):
                pm_next[c0 // LANES + i] = res[:, i * LANES:(i + 1) * LANES]

        n_cols = N_LB * LANES
        return [functools.partial(piece, c0, min(c0 + PROJ_COLS, n_cols))
                for c0 in range(0, n_cols, PROJ_COLS)]

    @pl.when(step == 0)
    def _first_step():
        for j in range(CUT_G // PROJ_COLS):
            win_ref[:, j * PROJ_COLS:(j + 1) * PROJ_COLS] = (
                wt_ref[j * PROJ_COLS:(j + 1) * PROJ_COLS, :].T.astype(BF16))
        gate_rows = jnp.concatenate([wt_ref[CUT_G:CUT_G + 2 * NH, :],
                                     jnp.zeros((LANES - 2 * NH, D_MODEL), F32)], axis=0)
        win_ref[:, CUT_G:CUT_G + LANES] = gate_rows.T.astype(BF16)
        wob_ref[...] = wo_ref[...].astype(BF16)
        gbias_ref[...] = _gate_bias_rows(bg_ref, L)
        for piece in project(pm_a):
            piece()

    @pl.when(t_idx == 0)
    def _init():
        ext_ref[:, 0:POOL_PAD, :] = jnp.zeros((len(POOL_WINDOWS), POOL_PAD, POOL_GC), F32)
        caug_ref[...] = jnp.zeros(caug_ref.shape, F32)
        mst_ref[...] = jnp.zeros(mst_ref.shape, F32)

    tt = lax.broadcasted_iota(jnp.int32, (L, L), 0)
    ss = lax.broadcasted_iota(jnp.int32, (L, L), 1)
    causal = ss <= tt
    diag = ss == tt

    def run(pm_next, pm_cur):
        pieces = project(pm_next)
        n_chunks = tq // L
        n_slots = len(POOL_WINDOWS) + 2 * n_chunks * NH
        slot_of = [(k * n_slots) // len(pieces) for k in range(len(pieces))]
        slot = [0]

        def next_slot():
            for k, piece in enumerate(pieces):
                if slot_of[k] == slot[0]:
                    piece()
            slot[0] += 1

        pos = t_idx * tq + lax.broadcasted_iota(jnp.int32, (tq, POOL_GC), 0)
        for g, w in enumerate(POOL_WINDOWS):
            lo = g * POOL_GC
            u_g = pm_cur[g]
            ext_ref[g, POOL_PAD:POOL_PAD + tq, :] = u_g
            s = ext_ref[g]
            k = 1
            while k < w:
                s = s + pltpu.roll(s, k, axis=0)
                k *= 2
            cnt = jnp.minimum(pos + 1, w).astype(F32)
            pooled = s[POOL_PAD:, :] / cnt - u_g
            mix_ref[:, lo:lo + POOL_GC] = (_dot(pooled, wp_ref[g]) * ps_ref[:, lo:lo + POOL_GC]).astype(BF16)
            ext_ref[g, 0:POOL_PAD, :] = ext_ref[g, tq:tq + POOL_PAD, :]
            next_slot()

        lane = lax.broadcasted_iota(jnp.int32, (NH, L), 1)
        gate_bias = gbias_ref[...]
        m_prev = mst_ref[...]
        chunk_rows = []
        for c in range(n_chunks):
            rows = slice(c * L, (c + 1) * L)
            gates = pm_cur[LB_G, rows, :].T[0:2 * NH, :] + gate_bias
            lf = jax.nn.log_sigmoid(gates[NH:2 * NH])
            b = lf
            k = 1
            while k < L:
                b = b + jnp.where(lane >= k, pltpu.roll(b, k, axis=1), 0.0)
                k *= 2
            a = gates[0:NH] - b
            cmax = a
            k = 1
            while k < L:
                cmax = jnp.maximum(cmax, jnp.where(lane >= k, pltpu.roll(cmax, k, axis=1), -jnp.inf))
                k *= 2
            amax = jnp.max(a, axis=-1, keepdims=True)
            big_m = jnp.maximum(cmax, m_prev)
            mm = jnp.maximum(m_prev, amax)
            chunk_rows.append(dict(
                a=a, big_m=big_m,
                iw=jnp.exp(m_prev - big_m),
                floor=jnp.exp(-(b + big_m)),
                w_loc=jnp.exp(a - amax),
                g_state=jnp.exp(m_prev - mm),
                f_state=jnp.exp(amax - mm)))
            m_prev = jnp.sum(lf, axis=-1, keepdims=True) + mm
        mst_ref[...] = m_prev

        def as_column(row):
            return jnp.broadcast_to(row, (L, L)).T

        ones_blk = jnp.ones((L, DH), BF16)
        for c in range(n_chunks):
            rows = slice(c * L, (c + 1) * L)
            cr = chunk_rows[c]
            for h in range(NH):
                i = c * NH + h
                q = pm_cur[LB_Q + h, rows, :]
                qb = q.astype(BF16)
                kt = pm_cur[LB_K + h, rows, :].T * K_SCALE
                v_aug = jnp.concatenate([pm_cur[LB_V + h, rows, :].astype(BF16), ones_blk], axis=1)
                p = jnp.exp(jnp.where(causal, cr["a"][h:h + 1], -jnp.inf) - as_column(cr["big_m"][h:h + 1]))
                lhs_ref[i, :, 0:L] = (_dot(qb, kt.astype(BF16)) * p).astype(BF16)
                vaug_ref[i] = v_aug
                ktw_ref[i] = (kt * cr["w_loc"][h:h + 1]).astype(BF16)
                lhs_ref[i, :, L:L + DH] = (q * as_column(cr["iw"][h:h + 1])).astype(BF16)
                next_slot()

        for c in range(n_chunks):
            rows = slice(c * L, (c + 1) * L)
            cr = chunk_rows[c]
            for h in range(NH):
                i = c * NH + h
                caug = caug_ref[h]
                comb = _dot(lhs_ref[i], jnp.concatenate([vaug_ref[i], caug.astype(BF16)], axis=0))
                hh = comb[:, :DH] / jnp.maximum(jnp.abs(comb[:, DH:]), as_column(cr["floor"][h:h + 1]))
                o = pm_cur[LB_O + h, rows, :]
                mix_ref[rows, D_POOL + h * DH:D_POOL + (h + 1) * DH] = (
                    jax.nn.sigmoid(o) * hh).astype(BF16)
                caug_ref[h] = (_twice(cr["g_state"][h:h + 1]) * caug
                               + _twice(cr["f_state"][h:h + 1]) * _dot(ktw_ref[i], vaug_ref[i]))
                next_slot()

        mix = _dot(mix_ref[...], wob_ref[...])
        x1_ref[0] = _layer_norm(ALPHA * xc_ref[0] + mix, g1_ref[...], b1_ref[...])

    @pl.when((parity == 0) & (step > 0))
    def _even():
        run(pm_a, pm_b)

    @pl.when(parity == 1)
    def _odd():
        run(pm_b, pm_a)

    @pl.when((t_idx == last_t) & (step > 0))
    def _final_state():
        for g in range(len(POOL_WINDOWS)):
            pool_ref[0, 0, :, g * POOL_GC:(g + 1) * POOL_GC] = ext_ref[g, tq + 1:tq + POOL_PAD, :]
        for h in range(NH):
            caug = caug_ref[h]
            c_ref[0, 0, h] = caug[:, :DH]
            n_ref[0, 0, h:h + 1, :] = jnp.sum(jnp.where(diag, caug[:, DH:], 0.0),
                                              axis=0, keepdims=True)
        m_ref[0] = mst_ref[...]


def _prompt_mixer(x, w_in_t, b_gate2d, w_pool, pool_scale, w_out, ln_g, ln_b):
    bsz, t_len, _ = x.shape
    tq = MIX_TILE
    assert t_len % tq == 0 and tq % MLSTM_CHUNK == 0 and tq >= POOL_PAD
    nt = t_len // tq
    n_tiles = bsz * nt
    n_items = (tq // MLSTM_CHUNK) * NH
    const2 = lambda i: (0, 0)
    nxt = lambda i: jnp.minimum(i, n_tiles - 1)
    cur = lambda i: jnp.maximum(i - 1, 0)
    return pl.pallas_call(
        functools.partial(_mixer_kernel, tiles_per_seq=nt),
        name="prompt_mixer",
        grid=(n_tiles + 1,),
        in_specs=[
            pl.BlockSpec((1, tq, D_MODEL), lambda i: (nxt(i) // nt, nxt(i) % nt, 0)),
            pl.BlockSpec((1, tq, D_MODEL), lambda i: (cur(i) // nt, cur(i) % nt, 0)),
            pl.BlockSpec(w_in_t.shape, const2, pipeline_mode=pl.Buffered(1)),
            pl.BlockSpec(memory_space=pltpu.SMEM),
            pl.BlockSpec(w_pool.shape, lambda i: (0, 0, 0)),
            pl.BlockSpec(pool_scale.shape, const2),
            pl.BlockSpec(w_out.shape, const2, pipeline_mode=pl.Buffered(1)),
            pl.BlockSpec(ln_g.shape, const2),
            pl.BlockSpec(ln_b.shape, const2),
        ],
        out_specs=[
            pl.BlockSpec((1, tq, D_MODEL), lambda i: (cur(i) // nt, cur(i) % nt, 0)),
            pl.BlockSpec((1, 1, POOL_BUF, D_POOL), lambda i: (0, cur(i) // nt, 0, 0)),
            pl.BlockSpec((1, 1, NH, DH, DH), lambda i: (0, cur(i) // nt, 0, 0, 0)),
            pl.BlockSpec((1, 1, NH, DH), lambda i: (0, cur(i) // nt, 0, 0)),
            pl.BlockSpec((1, NH, MLSTM_CHUNK), lambda i: (cur(i) // nt, 0, 0)),
        ],
        out_shape=[
            jax.ShapeDtypeStruct((bsz, t_len, D_MODEL), F32),
            jax.ShapeDtypeStruct((DEPTH, bsz, POOL_BUF, D_POOL), F32),
            jax.ShapeDtypeStruct((DEPTH, bsz, NH, DH, DH), F32),
            jax.ShapeDtypeStruct((DEPTH, bsz, NH, DH), F32),
            jax.ShapeDtypeStruct((bsz, NH, MLSTM_CHUNK), F32),
        ],
        scratch_shapes=[
            pltpu.VMEM((D_MODEL, N_LB * LANES), BF16),
            pltpu.VMEM((D_MODEL, D_MODEL), BF16),
            pltpu.VMEM((tq, D_MODEL), BF16),
            pltpu.VMEM((N_LB, tq, LANES), F32),
            pltpu.VMEM((N_LB, tq, LANES), F32),
            pltpu.VMEM((len(POOL_WINDOWS), POOL_PAD + tq, POOL_GC), F32),
            pltpu.VMEM((tq, D_MODEL), BF16),
            pltpu.VMEM((NH, DH, 2 * DH), F32),
            pltpu.VMEM((NH, MLSTM_CHUNK), F32),
            pltpu.VMEM((n_items, MLSTM_CHUNK, MLSTM_CHUNK + DH), BF16),
            pltpu.VMEM((n_items, MLSTM_CHUNK, 2 * DH), BF16),
            pltpu.VMEM((n_items, DH, MLSTM_CHUNK), BF16),
            pltpu.VMEM((2 * NH, MLSTM_CHUNK), F32),
        ],
        compiler_params=pltpu.CompilerParams(
            dimension_semantics=("arbitrary",),
            vmem_limit_bytes=V7X_VMEM_LIMIT),
    )(x, x, w_in_t, b_gate2d, w_pool, pool_scale, w_out, ln_g, ln_b)


def _ffn_kernel(x1p_ref, x1s_ref, w1_ref, b1_ref, w2_ref, b2_ref, g_ref, be_ref, yp_ref, ys_ref,
                pre_ref, *, n_prompt_tiles):
    def residual_plus_mlp(x1_ref):
        acc = None
        for c in range(D_FF // FF_CHUNK):
            cols = slice(c * FF_CHUNK, (c + 1) * FF_CHUNK)
            hid = jnp.maximum(_dot(x1_ref[...], w1_ref[:, cols]) + b1_ref[:, cols], 0.0)
            part = _dot(hid * hid, w2_ref[cols, :])
            acc = part if acc is None else acc + part
        return ALPHA * x1_ref[...] + (acc + b2_ref[...])

    step = pl.program_id(0)

    @pl.when(step == 0)
    def _no_tile_yet():
        pre_ref[...] = jnp.zeros(pre_ref.shape, F32)

    @pl.when(step < n_prompt_tiles)
    def _prompt():
        yp_ref[...] = _layer_norm(pre_ref[...], g_ref[...], be_ref[...])
        pre_ref[...] = residual_plus_mlp(x1p_ref)

    @pl.when(step == n_prompt_tiles)
    def _sample():
        yp_ref[...] = _layer_norm(pre_ref[...], g_ref[...], be_ref[...])
        ys_ref[:, 0, :] = _layer_norm(residual_plus_mlp(x1s_ref), g_ref[...], be_ref[...])


def _ffn(x1p, x1s, w1, b1, w2, b2, ln_g, ln_b):
    tile = FFN_TILE
    n_tok = x1p.shape[0]
    assert n_tok % tile == 0
    n_tiles = n_tok // tile
    const2 = lambda i: (0, 0)
    ptile = lambda i: (jnp.minimum(i, n_tiles - 1), 0)
    return pl.pallas_call(
        functools.partial(_ffn_kernel, n_prompt_tiles=n_tiles),
        name="ffn_ln2",
        grid=(n_tiles + 1,),
        in_specs=[
            pl.BlockSpec((tile, D_MODEL), ptile),
            pl.BlockSpec(x1s.shape, const2),
            pl.BlockSpec(w1.shape, const2, pipeline_mode=pl.Buffered(1)),
            pl.BlockSpec(b1.shape, const2),
            pl.BlockSpec(w2.shape, const2, pipeline_mode=pl.Buffered(1)),
            pl.BlockSpec(b2.shape, const2),
            pl.BlockSpec(ln_g.shape, const2),
            pl.BlockSpec(ln_b.shape, const2),
        ],
        out_specs=[pl.BlockSpec((tile, D_MODEL), lambda i: (jnp.maximum(i - 1, 0), 0)),
                   pl.BlockSpec((x1s.shape[0], 1, D_MODEL), lambda i: (0, 0, 0))],
        out_shape=[jax.ShapeDtypeStruct((n_tok, D_MODEL), F32),
                   jax.ShapeDtypeStruct((x1s.shape[0], 1, D_MODEL), F32)],
        scratch_shapes=[pltpu.VMEM((tile, D_MODEL), F32)],
        compiler_params=pltpu.CompilerParams(
            dimension_semantics=("arbitrary",),
            vmem_limit_bytes=V7X_VMEM_LIMIT),
    )(x1p, x1s, w1, b1, w2, b2, ln_g, ln_b)


def _sample_proj_kernel(x_ref, wt_ref, bg_ref, sp_ref, n_ref, m_ref, wp_ref, ps_ref,
                        ypool_ref, pool_ref, q_ref, kd_ref, v_ref, iw_ref, sv_ref, den_ref,
                        floor_ref, osig_ref, nout_ref, mout_ref):
    x = x_ref[:, 0, :]
    nb = x.shape[0]
    proj = _dot_nt(x, wt_ref[0:CUT_G, :])
    u = proj[:, 0:D_POOL]
    gates = _dot_nt(wt_ref[CUT_G:CUT_G + 2 * NH, :], x) + _gate_bias_rows(bg_ref, nb)
    ig_r = gates[0:NH]
    inter_r = jax.nn.log_sigmoid(gates[NH:2 * NH]) + m_ref[...]
    m_t_r = jnp.maximum(inter_r, ig_r)
    dw_r = jnp.exp(ig_r - m_t_r)
    iw_r = jnp.exp(inter_r - m_t_r)
    floor_r = jnp.exp(-m_t_r)
    mout_ref[...] = m_t_r

    def as_column(row):
        return jnp.broadcast_to(row, (DH, nb)).T

    for g, w in enumerate(POOL_WINDOWS):
        lo = g * POOL_GC
        u_g = u[:, lo:lo + POOL_GC]
        acc = u_g
        for r in range(POOL_PAD - w, POOL_BUF):
            acc = acc + sp_ref[r, :, lo:lo + POOL_GC]
        pooled = acc / float(min(PAST_LEN + 1, w)) - u_g
        ypool_ref[:, lo:lo + POOL_GC] = _dot(pooled, wp_ref[g]) * ps_ref[:, lo:lo + POOL_GC]
    pool_ref[0:POOL_BUF - 1] = sp_ref[1:POOL_BUF]
    pool_ref[POOL_BUF - 1] = u

    for h in range(NH):
        col = slice(h * DH, (h + 1) * DH)
        dw = as_column(dw_r[h:h + 1])
        iw = as_column(iw_r[h:h + 1])
        q = proj[:, D_POOL + h * DH:D_POOL + (h + 1) * DH]
        k = proj[:, CUT_K + h * DH:CUT_K + (h + 1) * DH] * K_SCALE
        v = proj[:, CUT_V + h * DH:CUT_V + (h + 1) * DH]
        o = proj[:, CUT_V + D_MLSTM + h * DH:CUT_V + D_MLSTM + (h + 1) * DH]
        n_old = n_ref[:, h, :]
        s = jnp.sum(q * k, axis=-1, keepdims=True) * dw
        q_ref[:, col] = q
        kd_ref[:, col] = dw * k
        v_ref[:, col] = v
        iw_ref[:, col] = iw
        sv_ref[:, col] = s * v
        den_ref[:, col] = s + iw * jnp.sum(q * n_old, axis=-1, keepdims=True)
        floor_ref[:, col] = as_column(floor_r[h:h + 1])
        osig_ref[:, col] = jax.nn.sigmoid(o)
        nout_ref[:, h, :] = iw * n_old + dw * k


def _sample_proj(x, w_in_t, b_gate2d, pool_rows, n_state, m_rows, w_pool, pool_scale):
    nb = x.shape[0]
    wide = jax.ShapeDtypeStruct((nb, D_MLSTM), F32)
    vmem = pl.BlockSpec(memory_space=pltpu.VMEM)
    return pl.pallas_call(
        _sample_proj_kernel,
        name="sample_proj",
        in_specs=[vmem, vmem, pl.BlockSpec(memory_space=pltpu.SMEM), vmem, vmem, vmem, vmem, vmem],
        out_shape=([wide, jax.ShapeDtypeStruct(pool_rows.shape, F32)] + [wide] * 8
                   + [jax.ShapeDtypeStruct(n_state.shape, F32), jax.ShapeDtypeStruct(m_rows.shape, F32)]),
        compiler_params=pltpu.CompilerParams(vmem_limit_bytes=V7X_VMEM_LIMIT),
    )(x, w_in_t, b_gate2d, pool_rows, n_state, m_rows, w_pool, pool_scale)


def _sample_state_kernel(c_ref, q_ref, kd_ref, v_ref, iw_ref, cout_ref, inter_ref):
    nb = SAMPLE_BLOCK
    pad = jnp.zeros((DH - nb, DH), F32)
    for h in range(NH):
        col = slice(h * DH, (h + 1) * DH)
        q_t = jnp.concatenate([q_ref[:, col], pad], axis=0).T
        kd_t = jnp.concatenate([kd_ref[:, col], pad], axis=0).T
        for j in range(nb):
            c_old = c_ref[0, j, h]
            inter_ref[j:j + 1, col] = jnp.sum(q_t[:, j:j + 1] * c_old, axis=0, keepdims=True)
            cout_ref[0, j, h] = iw_ref[j:j + 1, col] * c_old + kd_t[:, j:j + 1] * v_ref[j:j + 1, col]


def _sample_state(state_c, q, kd, v, iw):
    nb = q.shape[0]
    blk = SAMPLE_BLOCK
    assert nb % blk == 0
    row_spec = pl.BlockSpec((blk, D_MLSTM), lambda i: (i, 0))
    c_spec = pl.BlockSpec((1, blk, NH, DH, DH), lambda i: (0, i, 0, 0, 0))
    return pl.pallas_call(
        _sample_state_kernel,
        name="sample_state",
        grid=(nb // blk,),
        in_specs=[c_spec, row_spec, row_spec, row_spec, row_spec],
        out_specs=[c_spec, row_spec],
        out_shape=[jax.ShapeDtypeStruct(state_c.shape, F32),
                   jax.ShapeDtypeStruct((nb, D_MLSTM), F32)],
        compiler_params=pltpu.CompilerParams(
            dimension_semantics=("arbitrary",),
            vmem_limit_bytes=V7X_VMEM_LIMIT),
    )(state_c, q, kd, v, iw)


def _sample_out_kernel(x_ref, ypool_ref, sv_ref, iw_ref, inter_ref, den_ref, floor_ref, osig_ref,
                       wo_ref, g1_ref, b1_ref, x1_ref):
    hh = (sv_ref[...] + iw_ref[...] * inter_ref[...]) / jnp.maximum(jnp.abs(den_ref[...]), floor_ref[...])
    mixin = jnp.concatenate([ypool_ref[...], osig_ref[...] * hh], axis=-1)
    x1_ref[...] = _layer_norm(ALPHA * x_ref[:, 0, :] + _dot(mixin, wo_ref[...]), g1_ref[...], b1_ref[...])


def _sample_out(x, ypool, sv, iw, inter, den, floor, osig, w_out, ln_g, ln_b):
    return pl.pallas_call(
        _sample_out_kernel,
        name="sample_out",
        out_shape=jax.ShapeDtypeStruct((x.shape[0], D_MODEL), F32),
        compiler_params=pltpu.CompilerParams(vmem_limit_bytes=V7X_VMEM_LIMIT),
    )(x, ypool, sv, iw, inter, den, floor, osig, w_out, ln_g, ln_b)


def kernel(x_prompt, x_sample, state_pool, state_C, state_n, state_m, w_in, b_gate, w_pool, pool_scale,
           w_out, ln1_g, ln1_b, w_ff1, b_ff1, w_ff2, b_ff2, ln2_g, ln2_b):
    assert w_in.shape[0] == DEPTH == 1
    bp, t_len, _ = x_prompt.shape
    bs = x_sample.shape[0]
    assert x_sample.shape[1] == 1

    w_t = jnp.transpose(w_in[0])
    wp = w_pool[0]
    ps = pool_scale[0].reshape(1, D_POOL)
    wo = w_out[0]
    g1, b1 = ln1_g[0].reshape(1, D_MODEL), ln1_b[0].reshape(1, D_MODEL)
    g2, b2 = ln2_g[0].reshape(1, D_MODEL), ln2_b[0].reshape(1, D_MODEL)
    wf1, wf2 = w_ff1[0], w_ff2[0]
    bf1, bf2 = b_ff1[0].reshape(1, D_FF), b_ff2[0].reshape(1, D_MODEL)

    x1_p, pool_p, c_p, n_p, m_rep = _prompt_mixer(x_prompt, w_t, b_gate, wp, ps, wo, g1, b1)
    m_p = m_rep[:, :, 0].reshape(DEPTH, bp, NH)

    (ypool, pool_s, q_s, kd_s, v_s, iw_s, sv_s, den_s, floor_s, osig_s, n_s, m_s_rows) = _sample_proj(
        x_sample, w_t, b_gate, jnp.transpose(state_pool[0], (1, 0, 2)),
        state_n[0], jnp.transpose(state_m[0]), wp, ps)
    c_s, inter_s = _sample_state(state_C, q_s, kd_s, v_s, iw_s)
    x1_s = _sample_out(x_sample, ypool, sv_s, iw_s, inter_s, den_s, floor_s, osig_s, wo, g1, b1)

    y_p, y_s = _ffn(x1_p.reshape(bp * t_len, D_MODEL), x1_s, wf1, bf1, wf2, bf2, g2, b2)

    return (y_p.reshape(bp, t_len, D_MODEL), y_s,
            pool_p, c_p, n_p, m_p,
            jnp.transpose(pool_s, (1, 0, 2))[None], c_s,
            n_s[None], jnp.transpose(m_s_rows)[None])
```

```python
import functools

import jax
import jax.numpy as jnp
from jax import lax
from jax.experimental import pallas as pl
from jax.experimental.pallas import tpu as pltpu

F32 = jnp.float32
BF16 = jnp.bfloat16

D_MODEL = 1024
D_POOL = D_MODEL // 2
D_MLSTM = D_MODEL - D_POOL
POOL_WINDOWS = (2, 4, 8, 16)
POOL_GC = D_POOL // len(POOL_WINDOWS)
POOL_BUF = max(POOL_WINDOWS) - 1
POOL_PAD = POOL_BUF + 1
NH = 4
DH = D_MLSTM // NH
D_FF = 4 * D_MODEL
DEPTH = 1
PAST_LEN = 16384
ALPHA = (2.0 * DEPTH) ** 0.25
LN_EPS = 1e-5
K_SCALE = DH ** -0.5

MLSTM_CHUNK = 128
MIX_TILE = 512
FFN_TILE = 512
FF_CHUNK = 2048
WEIGHT_BLOCKS = 8
SAMPLE_BLOCK = 16
PROJ_COLS = 256
LANES = 128
assert POOL_GC == LANES and DH == LANES
CUT_K = D_POOL + D_MLSTM
CUT_V = CUT_K + D_MLSTM
CUT_G = D_POOL + 4 * D_MLSTM
LB_Q = D_POOL // LANES
LB_K = CUT_K // LANES
LB_V = CUT_V // LANES
LB_O = LB_V + NH
LB_G = CUT_G // LANES
N_LB = LB_G + 1
V7X_VMEM_LIMIT = 56 * 1024 * 1024


def _layer_norm(y, g, b):
    mu = jnp.mean(y, axis=-1, keepdims=True)
    yc = y - mu
    var = jnp.mean(yc * yc, axis=-1, keepdims=True)
    return yc * lax.rsqrt(var + LN_EPS) * g + b


def _dot(a, b):
    return jnp.dot(a, b, preferred_element_type=F32)


def _dot_nt(a, b):
    return lax.dot_general(a, b, (((1,), (1,)), ((), ())), preferred_element_type=F32)


def _twice(row):
    return jnp.concatenate([row, row], axis=-1)


def _gate_bias_rows(bg_ref, width):
    row = lax.broadcasted_iota(jnp.int32, (2 * NH, width), 0)
    bias = jnp.zeros((2 * NH, width), F32)
    for g in range(2 * NH):
        bias = jnp.where(row == g, bg_ref[0, g], bias)
    return bias


def _mixer_kernel(xn_ref, xc_ref, wt_ref, bg_ref, wp_ref, ps_ref, wo_ref, g1_ref, b1_ref,
                  x1_ref, pool_ref, c_ref, n_ref, m_ref,
                  win_ref, wob_ref, xb_ref, pm_a, pm_b, ext_ref, mix_ref, caug_ref, mst_ref,
                  lhs_ref, vaug_ref, ktw_ref, gbias_ref, *, tiles_per_seq):
    tq = MIX_TILE
    L = MLSTM_CHUNK
    step = pl.program_id(0)
    t_idx = lax.rem(step + tiles_per_seq - 1, tiles_per_seq)
    last_t = tiles_per_seq - 1
    parity = lax.rem(step, 2)

    def project(pm_next):
        xb_ref[...] = xn_ref[0].astype(BF16)

        def piece(c0, c1):
            res = _dot(xb_ref[...], win_ref[:, c0:c1])
            for i in range((c1 - c0) // LANES):
                pm_next[c0 // LANES + i] = res[:, i * LANES:(i + 1) * LANES]

        n_cols = N_LB * LANES
        return [functools.partial(piece, c0, min(c0 + PROJ_COLS, n_cols))
                for c0 in range(0, n_cols, PROJ_COLS)]

    @pl.when(step == 0)
    def _first_step():
        for j in range(CUT_G // PROJ_COLS):
            win_ref[:, j * PROJ_COLS:(j + 1) * PROJ_COLS] = (
                wt_ref[j * PROJ_COLS:(j + 1) * PROJ_COLS, :].T.astype(BF16))
        gate_rows = jnp.concatenate([wt_ref[CUT_G:CUT_G + 2 * NH, :],
                                     jnp.zeros((LANES - 2 * NH, D_MODEL), F32)], axis=0)
        win_ref[:, CUT_G:CUT_G + LANES] = gate_rows.T.astype(BF16)
        wob_ref[...] = wo_ref[...].astype(BF16)
        gbias_ref[...] = _gate_bias_rows(bg_ref, L)
        for piece in project(pm_a):
            piece()

    @pl.when(t_idx == 0)
    def _init():
        ext_ref[:, 0:POOL_PAD, :] = jnp.zeros((len(POOL_WINDOWS), POOL_PAD, POOL_GC), F32)
        caug_ref[...] = jnp.zeros(caug_ref.shape, F32)
        mst_ref[...] = jnp.zeros(mst_ref.shape, F32)

    tt = lax.broadcasted_iota(jnp.int32, (L, L), 0)
    ss = lax.broadcasted_iota(jnp.int32, (L, L), 1)
    causal = ss <= tt
    diag = ss == tt

    def run(pm_next, pm_cur):
        pieces = project(pm_next)
        n_chunks = tq // L
        n_slots = len(POOL_WINDOWS) + 2 * n_chunks * NH
        slot_of = [(k * n_slots) // len(pieces) for k in range(len(pieces))]
        slot = [0]

        def next_slot():
            for k, piece in enumerate(pieces):
                if slot_of[k] == slot[0]:
                    piece()
            slot[0] += 1

        pos = t_idx * tq + lax.broadcasted_iota(jnp.int32, (tq, POOL_GC), 0)
        for g, w in enumerate(POOL_WINDOWS):
            lo = g * POOL_GC
            u_g = pm_cur[g]
            ext_ref[g, POOL_PAD:POOL_PAD + tq, :] = u_g
            s = ext_ref[g]
            k = 1
            while k < w:
                s = s + pltpu.roll(s, k, axis=0)
                k *= 2
            cnt = jnp.minimum(pos + 1, w).astype(F32)
            pooled = s[POOL_PAD:, :] / cnt - u_g
            mix_ref[:, lo:lo + POOL_GC] = (_dot(pooled, wp_ref[g]) * ps_ref[:, lo:lo + POOL_GC]).astype(BF16)
            ext_ref[g, 0:POOL_PAD, :] = ext_ref[g, tq:tq + POOL_PAD, :]
            next_slot()

        lane = lax.broadcasted_iota(jnp.int32, (NH, L), 1)
        gate_bias = gbias_ref[...]
        m_prev = mst_ref[...]
        chunk_rows = []
        for c in range(n_chunks):
            rows = slice(c * L, (c + 1) * L)
            gates = pm_cur[LB_G, rows, :].T[0:2 * NH, :] + gate_bias
            lf = jax.nn.log_sigmoid(gates[NH:2 * NH])
            b = lf
            k = 1
            while k < L:
                b = b + jnp.where(lane >= k, pltpu.roll(b, k, axis=1), 0.0)
                k *= 2
            a = gates[0:NH] - b
            cmax = a
            k = 1
            while k < L:
                cmax = jnp.maximum(cmax, jnp.where(lane >= k, pltpu.roll(cmax, k, axis=1), -jnp.inf))
                k *= 2
            amax = jnp.max(a, axis=-1, keepdims=True)
            big_m = jnp.maximum(cmax, m_prev)
            mm = jnp.maximum(m_prev, amax)
            chunk_rows.append(dict(
                a=a, big_m=big_m,
                iw=jnp.exp(m_prev - big_m),
                floor=jnp.exp(-(b + big_m)),
                w_loc=jnp.exp(a - amax),
                g_state=jnp.exp(m_prev - mm),
                f_state=jnp.exp(amax - mm)))
            m_prev = jnp.sum(lf, axis=-1, keepdims=True) + mm
        mst_ref[...] = m_prev

        def as_column(row):
            return jnp.broadcast_to(row, (L, L)).T

        ones_blk = jnp.ones((L, DH), BF16)
        for c in range(n_chunks):
            rows = slice(c * L, (c + 1) * L)
            cr = chunk_rows[c]
            for h in range(NH):
                i = c * NH + h
                q = pm_cur[LB_Q + h, rows, :]
                qb = q.astype(BF16)
                kt = pm_cur[LB_K + h, rows, :].T * K_SCALE
                v_aug = jnp.concatenate([pm_cur[LB_V + h, rows, :].astype(BF16), ones_blk], axis=1)
                p = jnp.exp(jnp.where(causal, cr["a"][h:h + 1], -jnp.inf) - as_column(cr["big_m"][h:h + 1]))
                lhs_ref[i, :, 0:L] = (_dot(qb, kt.astype(BF16)) * p).astype(BF16)
                vaug_ref[i] = v_aug
                ktw_ref[i] = (kt * cr["w_loc"][h:h + 1]).astype(BF16)
                lhs_ref[i, :, L:L + DH] = (q * as_column(cr["iw"][h:h + 1])).astype(BF16)
                next_slot()

        for c in range(n_chunks):
            rows = slice(c * L, (c + 1) * L)
            cr = chunk_rows[c]
            for h in range(NH):
                i = c * NH + h
                caug = caug_ref[h]
                comb = _dot(lhs_ref[i], jnp.concatenate([vaug_ref[i], caug.astype(BF16)], axis=0))
                hh = comb[:, :DH] / jnp.maximum(jnp.abs(comb[:, DH:]), as_column(cr["floor"][h:h + 1]))
                o = pm_cur[LB_O + h, rows, :]
                mix_ref[rows, D_POOL + h * DH:D_POOL + (h + 1) * DH] = (
                    jax.nn.sigmoid(o) * hh).astype(BF16)
                caug_ref[h] = (_twice(cr["g_state"][h:h + 1]) * caug
                               + _twice(cr["f_state"][h:h + 1]) * _dot(ktw_ref[i], vaug_ref[i]))
                next_slot()

        mix = _dot(mix_ref[...], wob_ref[...])
        x1_ref[0] = _layer_norm(ALPHA * xc_ref[0] + mix, g1_ref[...], b1_ref[...])

    @pl.when((parity == 0) & (step > 0))
    def _even():
        run(pm_a, pm_b)

    @pl.when(parity == 1)
    def _odd():
        run(pm_b, pm_a)

    @pl.when((t_idx == last_t) & (step > 0))
    def _final_state():
        for g in range(len(POOL_WINDOWS)):
            pool_ref[0, 0, :, g * POOL_GC:(g + 1) * POOL_GC] = ext_ref[g, tq + 1:tq + POOL_PAD, :]
        for h in range(NH):
            caug = caug_ref[h]
            c_ref[0, 0, h] = caug[:, :DH]
            n_ref[0, 0, h:h + 1, :] = jnp.sum(jnp.where(diag, caug[:, DH:], 0.0),
                                              axis=0, keepdims=True)
        m_ref[0] = mst_ref[...]


def _prompt_mixer(x, w_in_t, b_gate2d, w_pool, pool_scale, w_out, ln_g, ln_b):
    bsz, t_len, _ = x.shape
    tq = MIX_TILE
    assert t_len % tq == 0 and tq % MLSTM_CHUNK == 0 and tq >= POOL_PAD
    nt = t_len // tq
    n_tiles = bsz * nt
    n_items = (tq // MLSTM_CHUNK) * NH
    const2 = lambda i: (0, 0)
    nxt = lambda i: jnp.minimum(i, n_tiles - 1)
    cur = lambda i: jnp.maximum(i - 1, 0)
    return pl.pallas_call(
        functools.partial(_mixer_kernel, tiles_per_seq=nt),
        name="prompt_mixer",
        grid=(n_tiles + 1,),
        in_specs=[
            pl.BlockSpec((1, tq, D_MODEL), lambda i: (nxt(i) // nt, nxt(i) % nt, 0)),
            pl.BlockSpec((1, tq, D_MODEL), lambda i: (cur(i) // nt, cur(i) % nt, 0)),
            pl.BlockSpec(w_in_t.shape, const2, pipeline_mode=pl.Buffered(1)),
            pl.BlockSpec(memory_space=pltpu.SMEM),
            pl.BlockSpec(w_pool.shape, lambda i: (0, 0, 0)),
            pl.BlockSpec(pool_scale.shape, const2),
            pl.BlockSpec(w_out.shape, const2, pipeline_mode=pl.Buffered(1)),
            pl.BlockSpec(ln_g.shape, const2),
            pl.BlockSpec(ln_b.shape, const2),
        ],
        out_specs=[
            pl.BlockSpec((1, tq, D_MODEL), lambda i: (cur(i) // nt, cur(i) % nt, 0)),
            pl.BlockSpec((1, 1, POOL_BUF, D_POOL), lambda i: (0, cur(i) // nt, 0, 0)),
            pl.BlockSpec((1, 1, NH, DH, DH), lambda i: (0, cur(i) // nt, 0, 0, 0)),
            pl.BlockSpec((1, 1, NH, DH), lambda i: (0, cur(i) // nt, 0, 0)),
            pl.BlockSpec((1, NH, MLSTM_CHUNK), lambda i: (cur(i) // nt, 0, 0)),
        ],
        out_shape=[
            jax.ShapeDtypeStruct((bsz, t_len, D_MODEL), F32),
            jax.ShapeDtypeStruct((DEPTH, bsz, POOL_BUF, D_POOL), F32),
            jax.ShapeDtypeStruct((DEPTH, bsz, NH, DH, DH), F32),
            jax.ShapeDtypeStruct((DEPTH, bsz, NH, DH), F32),
            jax.ShapeDtypeStruct((bsz, NH, MLSTM_CHUNK), F32),
        ],
        scratch_shapes=[
            pltpu.VMEM((D_MODEL, N_LB * LANES), BF16),
            pltpu.VMEM((D_MODEL, D_MODEL), BF16),
            pltpu.VMEM((tq, D_MODEL), BF16),
            pltpu.VMEM((N_LB, tq, LANES), F32),
            pltpu.VMEM((N_LB, tq, LANES), F32),
            pltpu.VMEM((len(POOL_WINDOWS), POOL_PAD + tq, POOL_GC), F32),
            pltpu.VMEM((tq, D_MODEL), BF16),
            pltpu.VMEM((NH, DH, 2 * DH), F32),
            pltpu.VMEM((NH, MLSTM_CHUNK), F32),
            pltpu.VMEM((n_items, MLSTM_CHUNK, MLSTM_CHUNK + DH), BF16),
            pltpu.VMEM((n_items, MLSTM_CHUNK, 2 * DH), BF16),
            pltpu.VMEM((n_items, DH, MLSTM_CHUNK), BF16),
            pltpu.VMEM((2 * NH, MLSTM_CHUNK), F32),
        ],
        compiler_params=pltpu.CompilerParams(
            dimension_semantics=("arbitrary",),
            vmem_limit_bytes=V7X_VMEM_LIMIT),
    )(x, x, w_in_t, b_gate2d, w_pool, pool_scale, w_out, ln_g, ln_b)


def _ffn_kernel(x1p_ref, x1s_ref, w1_hbm, b1_ref, w2_hbm, b2_ref, g_ref, be_ref, yp_ref, ys_ref,
                pre_ref, w1_ref, w2_ref, stage1_ref, stage2_ref, sem1, sem2, *, n_prompt_tiles):
    def residual_plus_mlp(x1_ref):
        xb = x1_ref[...].astype(BF16)
        acc = None
        for c in range(D_FF // FF_CHUNK):
            cols = slice(c * FF_CHUNK, (c + 1) * FF_CHUNK)
            hid = jnp.maximum(_dot(xb, w1_ref[:, cols]) + b1_ref[:, cols], 0.0)
            part = _dot((hid * hid).astype(BF16), w2_ref[cols, :])
            acc = part if acc is None else acc + part
        return ALPHA * x1_ref[...] + (acc + b2_ref[...])

    step = pl.program_id(0)

    def staged_copy(src_hbm, stage_ref, sem, j):
        rows = stage_ref.shape[1]
        return pltpu.make_async_copy(src_hbm.at[pl.ds(j * rows, rows), :], stage_ref.at[j % 2],
                                     sem.at[j % 2])

    def load_bf16(src_hbm, dst_ref, stage_ref, sem):
        rows = stage_ref.shape[1]
        n_blocks = dst_ref.shape[0] // rows
        for j in range(n_blocks):
            staged_copy(src_hbm, stage_ref, sem, j).wait()
            dst_ref[j * rows:(j + 1) * rows, :] = stage_ref[j % 2].astype(BF16)
            if j + 2 < n_blocks:
                staged_copy(src_hbm, stage_ref, sem, j + 2).start()

    @pl.when(step == 0)
    def _first_step():
        pre_ref[...] = jnp.zeros(pre_ref.shape, F32)
        for j in range(2):
            staged_copy(w1_hbm, stage1_ref, sem1, j).start()
            staged_copy(w2_hbm, stage2_ref, sem2, j).start()
        load_bf16(w1_hbm, w1_ref, stage1_ref, sem1)
        load_bf16(w2_hbm, w2_ref, stage2_ref, sem2)

    @pl.when(step < n_prompt_tiles)
    def _prompt():
        yp_ref[...] = _layer_norm(pre_ref[...], g_ref[...], be_ref[...])
        pre_ref[...] = residual_plus_mlp(x1p_ref)

    @pl.when(step == n_prompt_tiles)
    def _sample():
        yp_ref[...] = _layer_norm(pre_ref[...], g_ref[...], be_ref[...])
        ys_ref[:, 0, :] = _layer_norm(residual_plus_mlp(x1s_ref), g_ref[...], be_ref[...])


def _ffn(x1p, x1s, w1, b1, w2, b2, ln_g, ln_b):
    tile = FFN_TILE
    n_tok = x1p.shape[0]
    assert n_tok % tile == 0
    n_tiles = n_tok // tile
    const2 = lambda i: (0, 0)
    ptile = lambda i: (jnp.minimum(i, n_tiles - 1), 0)
    return pl.pallas_call(
        functools.partial(_ffn_kernel, n_prompt_tiles=n_tiles),
        name="ffn_ln2",
        grid=(n_tiles + 1,),
        in_specs=[
            pl.BlockSpec((tile, D_MODEL), ptile),
            pl.BlockSpec(x1s.shape, const2),
            pl.BlockSpec(memory_space=pl.ANY),
            pl.BlockSpec(b1.shape, const2),
            pl.BlockSpec(memory_space=pl.ANY),
            pl.BlockSpec(b2.shape, const2),
            pl.BlockSpec(ln_g.shape, const2),
            pl.BlockSpec(ln_b.shape, const2),
        ],
        out_specs=[pl.BlockSpec((tile, D_MODEL), lambda i: (jnp.maximum(i - 1, 0), 0)),
                   pl.BlockSpec((x1s.shape[0], 1, D_MODEL), lambda i: (0, 0, 0))],
        out_shape=[jax.ShapeDtypeStruct((n_tok, D_MODEL), F32),
                   jax.ShapeDtypeStruct((x1s.shape[0], 1, D_MODEL), F32)],
        scratch_shapes=[
            pltpu.VMEM((tile, D_MODEL), F32),
            pltpu.VMEM(w1.shape, BF16),
            pltpu.VMEM(w2.shape, BF16),
            pltpu.VMEM((2, w1.shape[0] // WEIGHT_BLOCKS, w1.shape[1]), F32),
            pltpu.VMEM((2, w2.shape[0] // WEIGHT_BLOCKS, w2.shape[1]), F32),
            pltpu.SemaphoreType.DMA((2,)),
            pltpu.SemaphoreType.DMA((2,)),
        ],
        compiler_params=pltpu.CompilerParams(
            dimension_semantics=("arbitrary",),
            vmem_limit_bytes=V7X_VMEM_LIMIT),
    )(x1p, x1s, w1, b1, w2, b2, ln_g, ln_b)


def _sample_proj_kernel(x_ref, wt_ref, bg_ref, sp_ref, n_ref, m_ref, wp_ref, ps_ref,
                        ypool_ref, pool_ref, q_ref, kd_ref, v_ref, iw_ref, sv_ref, den_ref,
                        floor_ref, osig_ref, nout_ref, mout_ref):
    x = x_ref[:, 0, :]
    nb = x.shape[0]
    proj = _dot_nt(x, wt_ref[0:CUT_G, :])
    u = proj[:, 0:D_POOL]
    gates = _dot_nt(wt_ref[CUT_G:CUT_G + 2 * NH, :], x) + _gate_bias_rows(bg_ref, nb)
    ig_r = gates[0:NH]
    inter_r = jax.nn.log_sigmoid(gates[NH:2 * NH]) + m_ref[...]
    m_t_r = jnp.maximum(inter_r, ig_r)
    dw_r = jnp.exp(ig_r - m_t_r)
    iw_r = jnp.exp(inter_r - m_t_r)
    floor_r = jnp.exp(-m_t_r)
    mout_ref[...] = m_t_r

    def as_column(row):
        return jnp.broadcast_to(row, (DH, nb)).T

    for g, w in enumerate(POOL_WINDOWS):
        lo = g * POOL_GC
        u_g = u[:, lo:lo + POOL_GC]
        acc = u_g
        for r in range(POOL_PAD - w, POOL_BUF):
            acc = acc + sp_ref[r, :, lo:lo + POOL_GC]
        pooled = acc / float(min(PAST_LEN + 1, w)) - u_g
        ypool_ref[:, lo:lo + POOL_GC] = _dot(pooled, wp_ref[g]) * ps_ref[:, lo:lo + POOL_GC]
    pool_ref[0:POOL_BUF - 1] = sp_ref[1:POOL_BUF]
    pool_ref[POOL_BUF - 1] = u

    for h in range(NH):
        col = slice(h * DH, (h + 1) * DH)
        dw = as_column(dw_r[h:h + 1])
        iw = as_column(iw_r[h:h + 1])
        q = proj[:, D_POOL + h * DH:D_POOL + (h + 1) * DH]
        k = proj[:, CUT_K + h * DH:CUT_K + (h + 1) * DH] * K_SCALE
        v = proj[:, CUT_V + h * DH:CUT_V + (h + 1) * DH]
        o = proj[:, CUT_V + D_MLSTM + h * DH:CUT_V + D_MLSTM + (h + 1) * DH]
        n_old = n_ref[:, h, :]
        s = jnp.sum(q * k, axis=-1, keepdims=True) * dw
        q_ref[:, col] = q
        kd_ref[:, col] = dw * k
        v_ref[:, col] = v
        iw_ref[:, col] = iw
        sv_ref[:, col] = s * v
        den_ref[:, col] = s + iw * jnp.sum(q * n_old, axis=-1, keepdims=True)
        floor_ref[:, col] = as_column(floor_r[h:h + 1])
        osig_ref[:, col] = jax.nn.sigmoid(o)
        nout_ref[:, h, :] = iw * n_old + dw * k


def _sample_proj(x, w_in_t, b_gate2d, pool_rows, n_state, m_rows, w_pool, pool_scale):
    nb = x.shape[0]
    wide = jax.ShapeDtypeStruct((nb, D_MLSTM), F32)
    vmem = pl.BlockSpec(memory_space=pltpu.VMEM)
    return pl.pallas_call(
        _sample_proj_kernel,
        name="sample_proj",
        in_specs=[vmem, vmem, pl.BlockSpec(memory_space=pltpu.SMEM), vmem, vmem, vmem, vmem, vmem],
        out_shape=([wide, jax.ShapeDtypeStruct(pool_rows.shape, F32)] + [wide] * 8
                   + [jax.ShapeDtypeStruct(n_state.shape, F32), jax.ShapeDtypeStruct(m_rows.shape, F32)]),
        compiler_params=pltpu.CompilerParams(vmem_limit_bytes=V7X_VMEM_LIMIT),
    )(x, w_in_t, b_gate2d, pool_rows, n_state, m_rows, w_pool, pool_scale)


def _sample_state_kernel(c_ref, q_ref, kd_ref, v_ref, iw_ref, cout_ref, inter_ref):
    nb = SAMPLE_BLOCK
    pad = jnp.zeros((DH - nb, DH), F32)
    for h in range(NH):
        col = slice(h * DH, (h + 1) * DH)
        q_t = jnp.concatenate([q_ref[:, col], pad], axis=0).T
        kd_t = jnp.concatenate([kd_ref[:, col], pad], axis=0).T
        for j in range(nb):
            c_old = c_ref[0, j, h]
            inter_ref[j:j + 1, col] = jnp.sum(q_t[:, j:j + 1] * c_old, axis=0, keepdims=True)
            cout_ref[0, j, h] = iw_ref[j:j + 1, col] * c_old + kd_t[:, j:j + 1] * v_ref[j:j + 1, col]


def _sample_state(state_c, q, kd, v, iw):
    nb = q.shape[0]
    blk = SAMPLE_BLOCK
    assert nb % blk == 0
    row_spec = pl.BlockSpec((blk, D_MLSTM), lambda i: (i, 0))
    c_spec = pl.BlockSpec((1, blk, NH, DH, DH), lambda i: (0, i, 0, 0, 0))
    return pl.pallas_call(
        _sample_state_kernel,
        name="sample_state",
        grid=(nb // blk,),
        in_specs=[c_spec, row_spec, row_spec, row_spec, row_spec],
        out_specs=[c_spec, row_spec],
        out_shape=[jax.ShapeDtypeStruct(state_c.shape, F32),
                   jax.ShapeDtypeStruct((nb, D_MLSTM), F32)],
        compiler_params=pltpu.CompilerParams(
            dimension_semantics=("arbitrary",),
            vmem_limit_bytes=V7X_VMEM_LIMIT),
    )(state_c, q, kd, v, iw)


def _sample_out_kernel(x_ref, ypool_ref, sv_ref, iw_ref, inter_ref, den_ref, floor_ref, osig_ref,
                       wo_ref, g1_ref, b1_ref, x1_ref):
    hh = (sv_ref[...] + iw_ref[...] * inter_ref[...]) / jnp.maximum(jnp.abs(den_ref[...]), floor_ref[...])
    mixin = jnp.concatenate([ypool_ref[...], osig_ref[...] * hh], axis=-1)
    x1_ref[...] = _layer_norm(ALPHA * x_ref[:, 0, :] + _dot(mixin, wo_ref[...]), g1_ref[...], b1_ref[...])


def _sample_out(x, ypool, sv, iw, inter, den, floor, osig, w_out, ln_g, ln_b):
    return pl.pallas_call(
        _sample_out_kernel,
        name="sample_out",
        out_shape=jax.ShapeDtypeStruct((x.shape[0], D_MODEL), F32),
        compiler_params=pltpu.CompilerParams(vmem_limit_bytes=V7X_VMEM_LIMIT),
    )(x, ypool, sv, iw, inter, den, floor, osig, w_out, ln_g, ln_b)


def kernel(x_prompt, x_sample, state_pool, state_C, state_n, state_m, w_in, b_gate, w_pool, pool_scale,
           w_out, ln1_g, ln1_b, w_ff1, b_ff1, w_ff2, b_ff2, ln2_g, ln2_b):
    assert w_in.shape[0] == DEPTH == 1
    bp, t_len, _ = x_prompt.shape
    bs = x_sample.shape[0]
    assert x_sample.shape[1] == 1

    w_t = jnp.transpose(w_in[0])
    wp = w_pool[0]
    ps = pool_scale[0].reshape(1, D_POOL)
    wo = w_out[0]
    g1, b1 = ln1_g[0].reshape(1, D_MODEL), ln1_b[0].reshape(1, D_MODEL)
    g2, b2 = ln2_g[0].reshape(1, D_MODEL), ln2_b[0].reshape(1, D_MODEL)
    wf1, wf2 = w_ff1[0], w_ff2[0]
    bf1, bf2 = b_ff1[0].reshape(1, D_FF), b_ff2[0].reshape(1, D_MODEL)

    x1_p, pool_p, c_p, n_p, m_rep = _prompt_mixer(x_prompt, w_t, b_gate, wp, ps, wo, g1, b1)
    m_p = m_rep[:, :, 0].reshape(DEPTH, bp, NH)

    (ypool, pool_s, q_s, kd_s, v_s, iw_s, sv_s, den_s, floor_s, osig_s, n_s, m_s_rows) = _sample_proj(
        x_sample, w_t, b_gate, jnp.transpose(state_pool[0], (1, 0, 2)),
        state_n[0], jnp.transpose(state_m[0]), wp, ps)
    c_s, inter_s = _sample_state(state_C, q_s, kd_s, v_s, iw_s)
    x1_s = _sample_out(x_sample, ypool, sv_s, iw_s, inter_s, den_s, floor_s, osig_s, wo, g1, b1)

    y_p, y_s = _ffn(x1_p.reshape(bp * t_len, D_MODEL), x1_s, wf1, bf1, wf2, bf2, g2, b2)

    return (y_p.reshape(bp, t_len, D_MODEL), y_s,
            pool_p, c_p, n_p, m_p,
            jnp.transpose(pool_s, (1, 0, 2))[None], c_s,
            n_s[None], jnp.transpose(m_s_rows)[None])
```

```python
import functools

import jax
import jax.numpy as jnp
from jax import lax
from jax.experimental import pallas as pl
from jax.experimental.pallas import tpu as pltpu

F32 = jnp.float32
BF16 = jnp.bfloat16

D_MODEL = 1024
D_POOL = D_MODEL // 2
D_MLSTM = D_MODEL - D_POOL
POOL_WINDOWS = (2, 4, 8, 16)
POOL_GC = D_POOL // len(POOL_WINDOWS)
POOL_BUF = max(POOL_WINDOWS) - 1
POOL_PAD = POOL_BUF + 1
NH = 4
DH = D_MLSTM // NH
D_FF = 4 * D_MODEL
DEPTH = 1
PAST_LEN = 16384
ALPHA = (2.0 * DEPTH) ** 0.25
LN_EPS = 1e-5
K_SCALE = DH ** -0.5

MLSTM_CHUNK = 128
MIX_TILE = 512
FFN_TILE = 512
FF_CHUNK = 2048
SAMPLE_BLOCK = 16
PROJ_COLS = 256
LANES = 128
assert POOL_GC == LANES and DH == LANES
CUT_K = D_POOL + D_MLSTM
CUT_V = CUT_K + D_MLSTM
CUT_G = D_POOL + 4 * D_MLSTM
LB_Q = D_POOL // LANES
LB_K = CUT_K // LANES
LB_V = CUT_V // LANES
LB_O = LB_V + NH
LB_G = CUT_G // LANES
N_LB = LB_G + 1
V7X_VMEM_LIMIT = 56 * 1024 * 1024


def _layer_norm(y, g, b):
    mu = jnp.mean(y, axis=-1, keepdims=True)
    yc = y - mu
    var = jnp.mean(yc * yc, axis=-1, keepdims=True)
    return yc * lax.rsqrt(var + LN_EPS) * g + b


def _dot(a, b):
    return jnp.dot(a, b, preferred_element_type=F32)


def _dot_nt(a, b):
    return lax.dot_general(a, b, (((1,), (1,)), ((), ())), preferred_element_type=F32)


def _twice(row):
    return jnp.concatenate([row, row], axis=-1)


def _gate_bias_rows(bg_ref, width):
    row = lax.broadcasted_iota(jnp.int32, (2 * NH, width), 0)
    bias = jnp.zeros((2 * NH, width), F32)
    for g in range(2 * NH):
        bias = jnp.where(row == g, bg_ref[0, g], bias)
    return bias


def _mixer_kernel(xn_ref, xc_ref, wt_ref, bg_ref, wp_ref, ps_ref, wo_ref, g1_ref, b1_ref,
                  x1_ref, pool_ref, c_ref, n_ref, m_ref,
                  win_ref, wob_ref, xb_ref, pm_a, pm_b, ext_ref, mix_ref, caug_ref, mst_ref,
                  lhs_ref, vaug_ref, ktw_ref, gbias_ref, *, tiles_per_seq, n_tiles):
    tq = MIX_TILE
    L = MLSTM_CHUNK
    step = pl.program_id(0)
    t_idx = lax.rem(step + tiles_per_seq - 1, tiles_per_seq)
    last_t = tiles_per_seq - 1
    parity = lax.rem(step, 2)

    def project(pm_next):
        xb_ref[...] = xn_ref[0].astype(BF16)

        def piece(c0, c1):
            res = _dot(xb_ref[...], win_ref[:, c0:c1])
            for i in range((c1 - c0) // LANES):
                pm_next[c0 // LANES + i] = res[:, i * LANES:(i + 1) * LANES]

        n_cols = N_LB * LANES
        return [functools.partial(piece, c0, min(c0 + PROJ_COLS, n_cols))
                for c0 in range(0, n_cols, PROJ_COLS)]

    @pl.when(step == 0)
    def _first_step():
        for j in range(CUT_G // PROJ_COLS):
            win_ref[:, j * PROJ_COLS:(j + 1) * PROJ_COLS] = (
                wt_ref[j * PROJ_COLS:(j + 1) * PROJ_COLS, :].T.astype(BF16))
        gate_rows = jnp.concatenate([wt_ref[CUT_G:CUT_G + 2 * NH, :],
                                     jnp.zeros((LANES - 2 * NH, D_MODEL), F32)], axis=0)
        win_ref[:, CUT_G:CUT_G + LANES] = gate_rows.T.astype(BF16)
        wob_ref[...] = wo_ref[...].astype(BF16)
        gbias_ref[...] = _gate_bias_rows(bg_ref, L)
        for piece in project(pm_a):
            piece()

    @pl.when(t_idx == 0)
    def _init():
        ext_ref[:, 0:POOL_PAD, :] = jnp.zeros((len(POOL_WINDOWS), POOL_PAD, POOL_GC), F32)
        caug_ref[...] = jnp.zeros(caug_ref.shape, F32)
        mst_ref[...] = jnp.zeros(mst_ref.shape, F32)

    tt = lax.broadcasted_iota(jnp.int32, (L, L), 0)
    ss = lax.broadcasted_iota(jnp.int32, (L, L), 1)
    causal = ss <= tt
    diag = ss == tt

    def run(pm_next, pm_cur):
        pieces = project(pm_next) if pm_next is not None else []
        n_chunks = tq // L
        n_slots = len(POOL_WINDOWS) + 2 * n_chunks * NH
        slot_of = [(k * n_slots) // len(pieces) for k in range(len(pieces))]
        slot = [0]

        def next_slot():
            for k, piece in enumerate(pieces):
                if slot_of[k] == slot[0]:
                    piece()
            slot[0] += 1

        pos = t_idx * tq + lax.broadcasted_iota(jnp.int32, (tq, POOL_GC), 0)
        for g, w in enumerate(POOL_WINDOWS):
            lo = g * POOL_GC
            u_g = pm_cur[g]
            ext_ref[g, POOL_PAD:POOL_PAD + tq, :] = u_g
            s = ext_ref[g]
            k = 1
            while k < w:
                s = s + pltpu.roll(s, k, axis=0)
                k *= 2
            cnt = jnp.minimum(pos + 1, w).astype(F32)
            pooled = s[POOL_PAD:, :] / cnt - u_g
            mix_ref[:, lo:lo + POOL_GC] = (_dot(pooled, wp_ref[g]) * ps_ref[:, lo:lo + POOL_GC]).astype(BF16)
            ext_ref[g, 0:POOL_PAD, :] = ext_ref[g, tq:tq + POOL_PAD, :]
            next_slot()

        lane = lax.broadcasted_iota(jnp.int32, (NH, L), 1)
        gate_bias = gbias_ref[...]
        m_prev = mst_ref[...]
        chunk_rows = []
        for c in range(n_chunks):
            rows = slice(c * L, (c + 1) * L)
            gates = pm_cur[LB_G, rows, :].T[0:2 * NH, :] + gate_bias
            lf = jax.nn.log_sigmoid(gates[NH:2 * NH])
            b = lf
            k = 1
            while k < L:
                b = b + jnp.where(lane >= k, pltpu.roll(b, k, axis=1), 0.0)
                k *= 2
            a = gates[0:NH] - b
            cmax = a
            k = 1
            while k < L:
                cmax = jnp.maximum(cmax, jnp.where(lane >= k, pltpu.roll(cmax, k, axis=1), -jnp.inf))
                k *= 2
            amax = jnp.max(a, axis=-1, keepdims=True)
            big_m = jnp.maximum(cmax, m_prev)
            mm = jnp.maximum(m_prev, amax)
            chunk_rows.append(dict(
                a=a, big_m=big_m,
                iw=jnp.exp(m_prev - big_m),
                floor=jnp.exp(-(b + big_m)),
                w_loc=jnp.exp(a - amax),
                g_state=jnp.exp(m_prev - mm),
                f_state=jnp.exp(amax - mm)))
            m_prev = jnp.sum(lf, axis=-1, keepdims=True) + mm
        mst_ref[...] = m_prev

        def as_column(row):
            return jnp.broadcast_to(row, (L, L)).T

        ones_blk = jnp.ones((L, DH), BF16)
        for c in range(n_chunks):
            rows = slice(c * L, (c + 1) * L)
            cr = chunk_rows[c]
            for h in range(NH):
                i = c * NH + h
                q = pm_cur[LB_Q + h, rows, :]
                qb = q.astype(BF16)
                kt = pm_cur[LB_K + h, rows, :].T * K_SCALE
                v_aug = jnp.concatenate([pm_cur[LB_V + h, rows, :].astype(BF16), ones_blk], axis=1)
                p = jnp.exp(jnp.where(causal, cr["a"][h:h + 1], -jnp.inf) - as_column(cr["big_m"][h:h + 1]))
                lhs_ref[i, :, 0:L] = (_dot(qb, kt.astype(BF16)) * p).astype(BF16)
                vaug_ref[i] = v_aug
                ktw_ref[i] = (kt * cr["w_loc"][h:h + 1]).astype(BF16)
                lhs_ref[i, :, L:L + DH] = (q * as_column(cr["iw"][h:h + 1])).astype(BF16)
                next_slot()

        for c in range(n_chunks):
            rows = slice(c * L, (c + 1) * L)
            cr = chunk_rows[c]
            for h in range(NH):
                i = c * NH + h
                caug = caug_ref[h]
                comb = _dot(lhs_ref[i], jnp.concatenate([vaug_ref[i], caug.astype(BF16)], axis=0))
                hh = comb[:, :DH] / jnp.maximum(jnp.abs(comb[:, DH:]), as_column(cr["floor"][h:h + 1]))
                o = pm_cur[LB_O + h, rows, :]
                mix_ref[rows, D_POOL + h * DH:D_POOL + (h + 1) * DH] = (
                    jax.nn.sigmoid(o) * hh).astype(BF16)
                caug_ref[h] = (_twice(cr["g_state"][h:h + 1]) * caug
                               + _twice(cr["f_state"][h:h + 1]) * _dot(ktw_ref[i], vaug_ref[i]))
                next_slot()

        mix = _dot(mix_ref[...], wob_ref[...])
        x1_ref[0] = _layer_norm(ALPHA * xc_ref[0] + mix, g1_ref[...], b1_ref[...])

    last_step = n_tiles

    @pl.when((parity == 0) & (step > 0) & (step < last_step))
    def _even():
        run(pm_a, pm_b)

    @pl.when((parity == 1) & (step < last_step))
    def _odd():
        run(pm_b, pm_a)

    @pl.when(step == last_step)
    def _last():
        run(None, pm_b if n_tiles % 2 == 0 else pm_a)

    @pl.when((t_idx == last_t) & (step > 0))
    def _final_state():
        for g in range(len(POOL_WINDOWS)):
            pool_ref[0, 0, :, g * POOL_GC:(g + 1) * POOL_GC] = ext_ref[g, tq + 1:tq + POOL_PAD, :]
        for h in range(NH):
            caug = caug_ref[h]
            c_ref[0, 0, h] = caug[:, :DH]
            n_ref[0, 0, h:h + 1, :] = jnp.sum(jnp.where(diag, caug[:, DH:], 0.0),
                                              axis=0, keepdims=True)
        m_ref[0] = mst_ref[...]


def _prompt_mixer(x, w_in_t, b_gate2d, w_pool, pool_scale, w_out, ln_g, ln_b):
    bsz, t_len, _ = x.shape
    tq = MIX_TILE
    assert t_len % tq == 0 and tq % MLSTM_CHUNK == 0 and tq >= POOL_PAD
    nt = t_len // tq
    n_tiles = bsz * nt
    n_items = (tq // MLSTM_CHUNK) * NH
    const2 = lambda i: (0, 0)
    nxt = lambda i: jnp.minimum(i, n_tiles - 1)
    cur = lambda i: jnp.maximum(i - 1, 0)
    return pl.pallas_call(
        functools.partial(_mixer_kernel, tiles_per_seq=nt, n_tiles=n_tiles),
        name="prompt_mixer",
        grid=(n_tiles + 1,),
        in_specs=[
            pl.BlockSpec((1, tq, D_MODEL), lambda i: (nxt(i) // nt, nxt(i) % nt, 0)),
            pl.BlockSpec((1, tq, D_MODEL), lambda i: (cur(i) // nt, cur(i) % nt, 0)),
            pl.BlockSpec(w_in_t.shape, const2, pipeline_mode=pl.Buffered(1)),
            pl.BlockSpec(memory_space=pltpu.SMEM),
            pl.BlockSpec(w_pool.shape, lambda i: (0, 0, 0)),
            pl.BlockSpec(pool_scale.shape, const2),
            pl.BlockSpec(w_out.shape, const2, pipeline_mode=pl.Buffered(1)),
            pl.BlockSpec(ln_g.shape, const2),
            pl.BlockSpec(ln_b.shape, const2),
        ],
        out_specs=[
            pl.BlockSpec((1, tq, D_MODEL), lambda i: (cur(i) // nt, cur(i) % nt, 0)),
            pl.BlockSpec((1, 1, POOL_BUF, D_POOL), lambda i: (0, cur(i) // nt, 0, 0)),
            pl.BlockSpec((1, 1, NH, DH, DH), lambda i: (0, cur(i) // nt, 0, 0, 0)),
            pl.BlockSpec((1, 1, NH, DH), lambda i: (0, cur(i) // nt, 0, 0)),
            pl.BlockSpec((1, NH, MLSTM_CHUNK), lambda i: (cur(i) // nt, 0, 0)),
        ],
        out_shape=[
            jax.ShapeDtypeStruct((bsz, t_len, D_MODEL), F32),
            jax.ShapeDtypeStruct((DEPTH, bsz, POOL_BUF, D_POOL), F32),
            jax.ShapeDtypeStruct((DEPTH, bsz, NH, DH, DH), F32),
            jax.ShapeDtypeStruct((DEPTH, bsz, NH, DH), F32),
            jax.ShapeDtypeStruct((bsz, NH, MLSTM_CHUNK), F32),
        ],
        scratch_shapes=[
            pltpu.VMEM((D_MODEL, N_LB * LANES), BF16),
            pltpu.VMEM((D_MODEL, D_MODEL), BF16),
            pltpu.VMEM((tq, D_MODEL), BF16),
            pltpu.VMEM((N_LB, tq, LANES), F32),
            pltpu.VMEM((N_LB, tq, LANES), F32),
            pltpu.VMEM((len(POOL_WINDOWS), POOL_PAD + tq, POOL_GC), F32),
            pltpu.VMEM((tq, D_MODEL), BF16),
            pltpu.VMEM((NH, DH, 2 * DH), F32),
            pltpu.VMEM((NH, MLSTM_CHUNK), F32),
            pltpu.VMEM((n_items, MLSTM_CHUNK, MLSTM_CHUNK + DH), BF16),
            pltpu.VMEM((n_items, MLSTM_CHUNK, 2 * DH), BF16),
            pltpu.VMEM((n_items, DH, MLSTM_CHUNK), BF16),
            pltpu.VMEM((2 * NH, MLSTM_CHUNK), F32),
        ],
        compiler_params=pltpu.CompilerParams(
            dimension_semantics=("arbitrary",),
            vmem_limit_bytes=V7X_VMEM_LIMIT),
    )(x, x, w_in_t, b_gate2d, w_pool, pool_scale, w_out, ln_g, ln_b)


def _ffn_kernel(x1p_ref, x1s_ref, w1_ref, b1_ref, w2_ref, b2_ref, g_ref, be_ref, yp_ref, ys_ref,
                pre_ref, *, n_prompt_tiles):
    def residual_plus_mlp(x1_ref):
        acc = None
        for c in range(D_FF // FF_CHUNK):
            cols = slice(c * FF_CHUNK, (c + 1) * FF_CHUNK)
            hid = jnp.maximum(_dot(x1_ref[...], w1_ref[:, cols]) + b1_ref[:, cols], 0.0)
            part = _dot(hid * hid, w2_ref[cols, :])
            acc = part if acc is None else acc + part
        return ALPHA * x1_ref[...] + (acc + b2_ref[...])

    step = pl.program_id(0)

    @pl.when(step == 0)
    def _no_tile_yet():
        pre_ref[...] = jnp.zeros(pre_ref.shape, F32)

    @pl.when(step < n_prompt_tiles)
    def _prompt():
        yp_ref[...] = _layer_norm(pre_ref[...], g_ref[...], be_ref[...])
        pre_ref[...] = residual_plus_mlp(x1p_ref)

    @pl.when(step == n_prompt_tiles)
    def _sample():
        yp_ref[...] = _layer_norm(pre_ref[...], g_ref[...], be_ref[...])
        ys_ref[:, 0, :] = _layer_norm(residual_plus_mlp(x1s_ref), g_ref[...], be_ref[...])


def _ffn(x1p, x1s, w1, b1, w2, b2, ln_g, ln_b):
    tile = FFN_TILE
    n_tok = x1p.shape[0]
    assert n_tok % tile == 0
    n_tiles = n_tok // tile
    const2 = lambda i: (0, 0)
    ptile = lambda i: (jnp.minimum(i, n_tiles - 1), 0)
    return pl.pallas_call(
        functools.partial(_ffn_kernel, n_prompt_tiles=n_tiles),
        name="ffn_ln2",
        grid=(n_tiles + 1,),
        in_specs=[
            pl.BlockSpec((tile, D_MODEL), ptile),
            pl.BlockSpec(x1s.shape, const2),
            pl.BlockSpec(w1.shape, const2, pipeline_mode=pl.Buffered(1)),
            pl.BlockSpec(b1.shape, const2),
            pl.BlockSpec(w2.shape, const2, pipeline_mode=pl.Buffered(1)),
            pl.BlockSpec(b2.shape, const2),
            pl.BlockSpec(ln_g.shape, const2),
            pl.BlockSpec(ln_b.shape, const2),
        ],
        out_specs=[pl.BlockSpec((tile, D_MODEL), lambda i: (jnp.maximum(i - 1, 0), 0)),
                   pl.BlockSpec((x1s.shape[0], 1, D_MODEL), lambda i: (0, 0, 0))],
        out_shape=[jax.ShapeDtypeStruct((n_tok, D_MODEL), F32),
                   jax.ShapeDtypeStruct((x1s.shape[0], 1, D_MODEL), F32)],
        scratch_shapes=[pltpu.VMEM((tile, D_MODEL), F32)],
        compiler_params=pltpu.CompilerParams(
            dimension_semantics=("arbitrary",),
            vmem_limit_bytes=V7X_VMEM_LIMIT),
    )(x1p, x1s, w1, b1, w2, b2, ln_g, ln_b)


def _sample_proj_kernel(x_ref, wt_ref, bg_ref, sp_ref, n_ref, m_ref, wp_ref, ps_ref,
                        ypool_ref, pool_ref, q_ref, kd_ref, v_ref, iw_ref, sv_ref, den_ref,
                        floor_ref, osig_ref, nout_ref, mout_ref):
    x = x_ref[:, 0, :]
    nb = x.shape[0]
    proj = _dot_nt(x, wt_ref[0:CUT_G, :])
    u = proj[:, 0:D_POOL]
    gates = _dot_nt(wt_ref[CUT_G:CUT_G + 2 * NH, :], x) + _gate_bias_rows(bg_ref, nb)
    ig_r = gates[0:NH]
    inter_r = jax.nn.log_sigmoid(gates[NH:2 * NH]) + m_ref[...]
    m_t_r = jnp.maximum(inter_r, ig_r)
    dw_r = jnp.exp(ig_r - m_t_r)
    iw_r = jnp.exp(inter_r - m_t_r)
    floor_r = jnp.exp(-m_t_r)
    mout_ref[...] = m_t_r

    def as_column(row):
        return jnp.broadcast_to(row, (DH, nb)).T

    for g, w in enumerate(POOL_WINDOWS):
        lo = g * POOL_GC
        u_g = u[:, lo:lo + POOL_GC]
        acc = u_g
        for r in range(POOL_PAD - w, POOL_BUF):
            acc = acc + sp_ref[r, :, lo:lo + POOL_GC]
        pooled = acc / float(min(PAST_LEN + 1, w)) - u_g
        ypool_ref[:, lo:lo + POOL_GC] = _dot(pooled, wp_ref[g]) * ps_ref[:, lo:lo + POOL_GC]
    pool_ref[0:POOL_BUF - 1] = sp_ref[1:POOL_BUF]
    pool_ref[POOL_BUF - 1] = u

    for h in range(NH):
        col = slice(h * DH, (h + 1) * DH)
        dw = as_column(dw_r[h:h + 1])
        iw = as_column(iw_r[h:h + 1])
        q = proj[:, D_POOL + h * DH:D_POOL + (h + 1) * DH]
        k = proj[:, CUT_K + h * DH:CUT_K + (h + 1) * DH] * K_SCALE
        v = proj[:, CUT_V + h * DH:CUT_V + (h + 1) * DH]
        o = proj[:, CUT_V + D_MLSTM + h * DH:CUT_V + D_MLSTM + (h + 1) * DH]
        n_old = n_ref[:, h, :]
        s = jnp.sum(q * k, axis=-1, keepdims=True) * dw
        q_ref[:, col] = q
        kd_ref[:, col] = dw * k
        v_ref[:, col] = v
        iw_ref[:, col] = iw
        sv_ref[:, col] = s * v
        den_ref[:, col] = s + iw * jnp.sum(q * n_old, axis=-1, keepdims=True)
        floor_ref[:, col] = as_column(floor_r[h:h + 1])
        osig_ref[:, col] = jax.nn.sigmoid(o)
        nout_ref[:, h, :] = iw * n_old + dw * k


def _sample_proj(x, w_in_t, b_gate2d, pool_rows, n_state, m_rows, w_pool, pool_scale):
    nb = x.shape[0]
    wide = jax.ShapeDtypeStruct((nb, D_MLSTM), F32)
    vmem = pl.BlockSpec(memory_space=pltpu.VMEM)
    return pl.pallas_call(
        _sample_proj_kernel,
        name="sample_proj",
        in_specs=[vmem, vmem, pl.BlockSpec(memory_space=pltpu.SMEM), vmem, vmem, vmem, vmem, vmem],
        out_shape=([wide, jax.ShapeDtypeStruct(pool_rows.shape, F32)] + [wide] * 8
                   + [jax.ShapeDtypeStruct(n_state.shape, F32), jax.ShapeDtypeStruct(m_rows.shape, F32)]),
        compiler_params=pltpu.CompilerParams(vmem_limit_bytes=V7X_VMEM_LIMIT),
    )(x, w_in_t, b_gate2d, pool_rows, n_state, m_rows, w_pool, pool_scale)


def _sample_state_kernel(c_ref, q_ref, kd_ref, v_ref, iw_ref, cout_ref, inter_ref):
    nb = SAMPLE_BLOCK
    pad = jnp.zeros((DH - nb, DH), F32)
    for h in range(NH):
        col = slice(h * DH, (h + 1) * DH)
        q_t = jnp.concatenate([q_ref[:, col], pad], axis=0).T
        kd_t = jnp.concatenate([kd_ref[:, col], pad], axis=0).T
        for j in range(nb):
            c_old = c_ref[0, j, h]
            inter_ref[j:j + 1, col] = jnp.sum(q_t[:, j:j + 1] * c_old, axis=0, keepdims=True)
            cout_ref[0, j, h] = iw_ref[j:j + 1, col] * c_old + kd_t[:, j:j + 1] * v_ref[j:j + 1, col]


def _sample_state(state_c, q, kd, v, iw):
    nb = q.shape[0]
    blk = SAMPLE_BLOCK
    assert nb % blk == 0
    row_spec = pl.BlockSpec((blk, D_MLSTM), lambda i: (i, 0))
    c_spec = pl.BlockSpec((1, blk, NH, DH, DH), lambda i: (0, i, 0, 0, 0))
    return pl.pallas_call(
        _sample_state_kernel,
        name="sample_state",
        grid=(nb // blk,),
        in_specs=[c_spec, row_spec, row_spec, row_spec, row_spec],
        out_specs=[c_spec, row_spec],
        out_shape=[jax.ShapeDtypeStruct(state_c.shape, F32),
                   jax.ShapeDtypeStruct((nb, D_MLSTM), F32)],
        compiler_params=pltpu.CompilerParams(
            dimension_semantics=("arbitrary",),
            vmem_limit_bytes=V7X_VMEM_LIMIT),
    )(state_c, q, kd, v, iw)


def _sample_out_kernel(x_ref, ypool_ref, sv_ref, iw_ref, inter_ref, den_ref, floor_ref, osig_ref,
                       wo_ref, g1_ref, b1_ref, x1_ref):
    hh = (sv_ref[...] + iw_ref[...] * inter_ref[...]) / jnp.maximum(jnp.abs(den_ref[...]), floor_ref[...])
    mixin = jnp.concatenate([ypool_ref[...], osig_ref[...] * hh], axis=-1)
    x1_ref[...] = _layer_norm(ALPHA * x_ref[:, 0, :] + _dot(mixin, wo_ref[...]), g1_ref[...], b1_ref[...])


def _sample_out(x, ypool, sv, iw, inter, den, floor, osig, w_out, ln_g, ln_b):
    return pl.pallas_call(
        _sample_out_kernel,
        name="sample_out",
        out_shape=jax.ShapeDtypeStruct((x.shape[0], D_MODEL), F32),
        compiler_params=pltpu.CompilerParams(vmem_limit_bytes=V7X_VMEM_LIMIT),
    )(x, ypool, sv, iw, inter, den, floor, osig, w_out, ln_g, ln_b)


def kernel(x_prompt, x_sample, state_pool, state_C, state_n, state_m, w_in, b_gate, w_pool, pool_scale,
           w_out, ln1_g, ln1_b, w_ff1, b_ff1, w_ff2, b_ff2, ln2_g, ln2_b):
    assert w_in.shape[0] == DEPTH == 1
    bp, t_len, _ = x_prompt.shape
    bs = x_sample.shape[0]
    assert x_sample.shape[1] == 1

    w_t = jnp.transpose(w_in[0])
    wp = w_pool[0]
    ps = pool_scale[0].reshape(1, D_POOL)
    wo = w_out[0]
    g1, b1 = ln1_g[0].reshape(1, D_MODEL), ln1_b[0].reshape(1, D_MODEL)
    g2, b2 = ln2_g[0].reshape(1, D_MODEL), ln2_b[0].reshape(1, D_MODEL)
    wf1, wf2 = w_ff1[0], w_ff2[0]
    bf1, bf2 = b_ff1[0].reshape(1, D_FF), b_ff2[0].reshape(1, D_MODEL)

    x1_p, pool_p, c_p, n_p, m_rep = _prompt_mixer(x_prompt, w_t, b_gate, wp, ps, wo, g1, b1)
    m_p = m_rep[:, :, 0].reshape(DEPTH, bp, NH)

    (ypool, pool_s, q_s, kd_s, v_s, iw_s, sv_s, den_s, floor_s, osig_s, n_s, m_s_rows) = _sample_proj(
        x_sample, w_t, b_gate, jnp.transpose(state_pool[0], (1, 0, 2)),
        state_n[0], jnp.transpose(state_m[0]), wp, ps)
    c_s, inter_s = _sample_state(state_C, q_s, kd_s, v_s, iw_s)
    x1_s = _sample_out(x_sample, ypool, sv_s, iw_s, inter_s, den_s, floor_s, osig_s, wo, g1, b1)

    y_p, y_s = _ffn(x1_p.reshape(bp * t_len, D_MODEL), x1_s, wf1, bf1, wf2, bf2, g2, b2)

    return (y_p.reshape(bp, t_len, D_MODEL), y_s,
            pool_p, c_p, n_p, m_p,
            jnp.transpose(pool_s, (1, 0, 2))[None], c_s,
            n_s[None], jnp.transpose(m_s_rows)[None])
```

```python
import functools

import jax
import jax.numpy as jnp
from jax import lax
from jax.experimental import pallas as pl
from jax.experimental.pallas import tpu as pltpu

F32 = jnp.float32
BF16 = jnp.bfloat16

D_MODEL = 1024
D_POOL = D_MODEL // 2
D_MLSTM = D_MODEL - D_POOL
POOL_WINDOWS = (2, 4, 8, 16)
POOL_GC = D_POOL // len(POOL_WINDOWS)
POOL_BUF = max(POOL_WINDOWS) - 1
POOL_PAD = POOL_BUF + 1
NH = 4
DH = D_MLSTM // NH
D_FF = 4 * D_MODEL
DEPTH = 1
PAST_LEN = 16384
ALPHA = (2.0 * DEPTH) ** 0.25
LN_EPS = 1e-5
K_SCALE = DH ** -0.5

MLSTM_CHUNK = 128
MIX_TILE = 512
FFN_TILE = 512
FF_CHUNK = 2048
SAMPLE_BLOCK = 16
PROJ_COLS = 256
LANES = 128
assert POOL_GC == LANES and DH == LANES
CUT_K = D_POOL + D_MLSTM
CUT_V = CUT_K + D_MLSTM
CUT_G = D_POOL + 4 * D_MLSTM
LB_Q = D_POOL // LANES
LB_K = CUT_K // LANES
LB_V = CUT_V // LANES
LB_O = LB_V + NH
LB_G = CUT_G // LANES
N_PIECES = CUT_G // PROJ_COLS + 1
N_LB = N_PIECES * (PROJ_COLS // LANES)
V7X_VMEM_LIMIT = 56 * 1024 * 1024


def _layer_norm(y, g, b):
    mu = jnp.mean(y, axis=-1, keepdims=True)
    yc = y - mu
    var = jnp.mean(yc * yc, axis=-1, keepdims=True)
    return yc * lax.rsqrt(var + LN_EPS) * g + b


def _dot(a, b):
    return jnp.dot(a, b, preferred_element_type=F32)


def _dot_nt(a, b):
    return lax.dot_general(a, b, (((1,), (1,)), ((), ())), preferred_element_type=F32)


def _twice(row):
    return jnp.concatenate([row, row], axis=-1)


def _gate_bias_rows(bg_ref, width):
    row = lax.broadcasted_iota(jnp.int32, (2 * NH, width), 0)
    bias = jnp.zeros((2 * NH, width), F32)
    for g in range(2 * NH):
        bias = jnp.where(row == g, bg_ref[0, g], bias)
    return bias


def _mixer_kernel(xn_ref, xc_ref, wt_ref, bg_ref, wp_ref, ps_ref, wo_ref, g1_ref, b1_ref,
                  x1_ref, pool_ref, c_ref, n_ref, m_ref,
                  win_ref, wob_ref, xb_ref, pm_a, pm_b, ext_ref, mix_ref, caug_ref, mst_ref,
                  lhs_ref, vaug_ref, ktw_ref, gbias_ref, *, tiles_per_seq):
    tq = MIX_TILE
    L = MLSTM_CHUNK
    step = pl.program_id(0)
    t_idx = lax.rem(step + tiles_per_seq - 1, tiles_per_seq)
    last_t = tiles_per_seq - 1
    parity = lax.rem(step, 2)

    lb_per_piece = PROJ_COLS // LANES

    def project_piece(pm_next, j):
        res = _dot(xb_ref[...], win_ref[j])
        for i in range(lb_per_piece):
            pm_next[j * lb_per_piece + i] = res[:, i * LANES:(i + 1) * LANES]

    def project(pm_next):
        xb_ref[...] = xn_ref[0].astype(BF16)
        return [functools.partial(project_piece, pm_next, j) for j in range(N_PIECES)]

    @pl.when(step == 0)
    def _first_step():
        @pl.loop(0, CUT_G // PROJ_COLS)
        def _untranspose(j):
            r0 = pl.multiple_of(j * PROJ_COLS, PROJ_COLS)
            win_ref[j] = wt_ref[pl.ds(r0, PROJ_COLS), :].T.astype(BF16)

        gate_rows = jnp.concatenate([wt_ref[CUT_G:CUT_G + 2 * NH, :],
                                     jnp.zeros((PROJ_COLS - 2 * NH, D_MODEL), F32)], axis=0)
        win_ref[N_PIECES - 1] = gate_rows.T.astype(BF16)
        wob_ref[...] = wo_ref[...].astype(BF16)
        gbias_ref[...] = _gate_bias_rows(bg_ref, L)
        xb_ref[...] = xn_ref[0].astype(BF16)

        @pl.loop(0, N_PIECES)
        def _project_first(j):
            project_piece(pm_a, j)

    @pl.when(t_idx == 0)
    def _init():
        ext_ref[:, 0:POOL_PAD, :] = jnp.zeros((len(POOL_WINDOWS), POOL_PAD, POOL_GC), F32)
        caug_ref[...] = jnp.zeros(caug_ref.shape, F32)
        mst_ref[...] = jnp.zeros(mst_ref.shape, F32)

    tt = lax.broadcasted_iota(jnp.int32, (L, L), 0)
    ss = lax.broadcasted_iota(jnp.int32, (L, L), 1)
    causal = ss <= tt
    diag = ss == tt

    def run(pm_next, pm_cur):
        pieces = project(pm_next)
        n_chunks = tq // L
        n_slots = len(POOL_WINDOWS) + 2 * n_chunks * NH
        slot_of = [(k * n_slots) // len(pieces) for k in range(len(pieces))]
        slot = [0]

        def next_slot():
            for k, piece in enumerate(pieces):
                if slot_of[k] == slot[0]:
                    piece()
            slot[0] += 1

        pos = t_idx * tq + lax.broadcasted_iota(jnp.int32, (tq, POOL_GC), 0)
        for g, w in enumerate(POOL_WINDOWS):
            lo = g * POOL_GC
            u_g = pm_cur[g]
            ext_ref[g, POOL_PAD:POOL_PAD + tq, :] = u_g
            s = ext_ref[g]
            k = 1
            while k < w:
                s = s + pltpu.roll(s, k, axis=0)
                k *= 2
            cnt = jnp.minimum(pos + 1, w).astype(F32)
            pooled = s[POOL_PAD:, :] / cnt - u_g
            mix_ref[:, lo:lo + POOL_GC] = (_dot(pooled, wp_ref[g]) * ps_ref[:, lo:lo + POOL_GC]).astype(BF16)
            ext_ref[g, 0:POOL_PAD, :] = ext_ref[g, tq:tq + POOL_PAD, :]
            next_slot()

        lane = lax.broadcasted_iota(jnp.int32, (NH, L), 1)
        gate_bias = gbias_ref[...]
        m_prev = mst_ref[...]
        chunk_rows = []
        for c in range(n_chunks):
            rows = slice(c * L, (c + 1) * L)
            gates = pm_cur[LB_G, rows, :].T[0:2 * NH, :] + gate_bias
            lf = jax.nn.log_sigmoid(gates[NH:2 * NH])
            b = lf
            k = 1
            while k < L:
                b = b + jnp.where(lane >= k, pltpu.roll(b, k, axis=1), 0.0)
                k *= 2
            a = gates[0:NH] - b
            cmax = a
            k = 1
            while k < L:
                cmax = jnp.maximum(cmax, jnp.where(lane >= k, pltpu.roll(cmax, k, axis=1), -jnp.inf))
                k *= 2
            amax = jnp.max(a, axis=-1, keepdims=True)
            big_m = jnp.maximum(cmax, m_prev)
            mm = jnp.maximum(m_prev, amax)
            chunk_rows.append(dict(
                a=a, big_m=big_m,
                iw=jnp.exp(m_prev - big_m),
                floor=jnp.exp(-(b + big_m)),
                w_loc=jnp.exp(a - amax),
                g_state=jnp.exp(m_prev - mm),
                f_state=jnp.exp(amax - mm)))
            m_prev = jnp.sum(lf, axis=-1, keepdims=True) + mm
        mst_ref[...] = m_prev

        def as_column(row):
            return jnp.broadcast_to(row, (L, L)).T

        ones_blk = jnp.ones((L, DH), BF16)
        for c in range(n_chunks):
            rows = slice(c * L, (c + 1) * L)
            cr = chunk_rows[c]
            for h in range(NH):
                i = c * NH + h
                q = pm_cur[LB_Q + h, rows, :]
                qb = q.astype(BF16)
                kt = pm_cur[LB_K + h, rows, :].T * K_SCALE
                v_aug = jnp.concatenate([pm_cur[LB_V + h, rows, :].astype(BF16), ones_blk], axis=1)
                p = jnp.exp(jnp.where(causal, cr["a"][h:h + 1], -jnp.inf) - as_column(cr["big_m"][h:h + 1]))
                lhs_ref[i, :, 0:L] = (_dot(qb, kt.astype(BF16)) * p).astype(BF16)
                vaug_ref[i] = v_aug
                ktw_ref[i] = (kt * cr["w_loc"][h:h + 1]).astype(BF16)
                lhs_ref[i, :, L:L + DH] = (q * as_column(cr["iw"][h:h + 1])).astype(BF16)
                next_slot()

        for c in range(n_chunks):
            rows = slice(c * L, (c + 1) * L)
            cr = chunk_rows[c]
            for h in range(NH):
                i = c * NH + h
                caug = caug_ref[h]
                comb = _dot(lhs_ref[i], jnp.concatenate([vaug_ref[i], caug.astype(BF16)], axis=0))
                hh = comb[:, :DH] / jnp.maximum(jnp.abs(comb[:, DH:]), as_column(cr["floor"][h:h + 1]))
                o = pm_cur[LB_O + h, rows, :]
                mix_ref[rows, D_POOL + h * DH:D_POOL + (h + 1) * DH] = (
                    jax.nn.sigmoid(o) * hh).astype(BF16)
                caug_ref[h] = (_twice(cr["g_state"][h:h + 1]) * caug
                               + _twice(cr["f_state"][h:h + 1]) * _dot(ktw_ref[i], vaug_ref[i]))
                next_slot()

        mix = _dot(mix_ref[...], wob_ref[...])
        x1_ref[0] = _layer_norm(ALPHA * xc_ref[0] + mix, g1_ref[...], b1_ref[...])

    @pl.when((parity == 0) & (step > 0))
    def _even():
        run(pm_a, pm_b)

    @pl.when(parity == 1)
    def _odd():
        run(pm_b, pm_a)

    @pl.when((t_idx == last_t) & (step > 0))
    def _final_state():
        for g in range(len(POOL_WINDOWS)):
            pool_ref[0, 0, :, g * POOL_GC:(g + 1) * POOL_GC] = ext_ref[g, tq + 1:tq + POOL_PAD, :]
        for h in range(NH):
            caug = caug_ref[h]
            c_ref[0, 0, h] = caug[:, :DH]
            n_ref[0, 0, h:h + 1, :] = jnp.sum(jnp.where(diag, caug[:, DH:], 0.0),
                                              axis=0, keepdims=True)
        m_ref[0] = mst_ref[...]


def _prompt_mixer(x, w_in_t, b_gate2d, w_pool, pool_scale, w_out, ln_g, ln_b):
    bsz, t_len, _ = x.shape
    tq = MIX_TILE
    assert t_len % tq == 0 and tq % MLSTM_CHUNK == 0 and tq >= POOL_PAD
    nt = t_len // tq
    n_tiles = bsz * nt
    n_items = (tq // MLSTM_CHUNK) * NH
    const2 = lambda i: (0, 0)
    nxt = lambda i: jnp.minimum(i, n_tiles - 1)
    cur = lambda i: jnp.maximum(i - 1, 0)
    return pl.pallas_call(
        functools.partial(_mixer_kernel, tiles_per_seq=nt),
        name="prompt_mixer",
        grid=(n_tiles + 1,),
        in_specs=[
            pl.BlockSpec((1, tq, D_MODEL), lambda i: (nxt(i) // nt, nxt(i) % nt, 0)),
            pl.BlockSpec((1, tq, D_MODEL), lambda i: (cur(i) // nt, cur(i) % nt, 0)),
            pl.BlockSpec(w_in_t.shape, const2, pipeline_mode=pl.Buffered(1)),
            pl.BlockSpec(memory_space=pltpu.SMEM),
            pl.BlockSpec(w_pool.shape, lambda i: (0, 0, 0)),
            pl.BlockSpec(pool_scale.shape, const2),
            pl.BlockSpec(w_out.shape, const2, pipeline_mode=pl.Buffered(1)),
            pl.BlockSpec(ln_g.shape, const2),
            pl.BlockSpec(ln_b.shape, const2),
        ],
        out_specs=[
            pl.BlockSpec((1, tq, D_MODEL), lambda i: (cur(i) // nt, cur(i) % nt, 0)),
            pl.BlockSpec((1, 1, POOL_BUF, D_POOL), lambda i: (0, cur(i) // nt, 0, 0)),
            pl.BlockSpec((1, 1, NH, DH, DH), lambda i: (0, cur(i) // nt, 0, 0, 0)),
            pl.BlockSpec((1, 1, NH, DH), lambda i: (0, cur(i) // nt, 0, 0)),
            pl.BlockSpec((1, NH, MLSTM_CHUNK), lambda i: (cur(i) // nt, 0, 0)),
        ],
        out_shape=[
            jax.ShapeDtypeStruct((bsz, t_len, D_MODEL), F32),
            jax.ShapeDtypeStruct((DEPTH, bsz, POOL_BUF, D_POOL), F32),
            jax.ShapeDtypeStruct((DEPTH, bsz, NH, DH, DH), F32),
            jax.ShapeDtypeStruct((DEPTH, bsz, NH, DH), F32),
            jax.ShapeDtypeStruct((bsz, NH, MLSTM_CHUNK), F32),
        ],
        scratch_shapes=[
            pltpu.VMEM((N_PIECES, D_MODEL, PROJ_COLS), BF16),
            pltpu.VMEM((D_MODEL, D_MODEL), BF16),
            pltpu.VMEM((tq, D_MODEL), BF16),
            pltpu.VMEM((N_LB, tq, LANES), F32),
            pltpu.VMEM((N_LB, tq, LANES), F32),
            pltpu.VMEM((len(POOL_WINDOWS), POOL_PAD + tq, POOL_GC), F32),
            pltpu.VMEM((tq, D_MODEL), BF16),
            pltpu.VMEM((NH, DH, 2 * DH), F32),
            pltpu.VMEM((NH, MLSTM_CHUNK), F32),
            pltpu.VMEM((n_items, MLSTM_CHUNK, MLSTM_CHUNK + DH), BF16),
            pltpu.VMEM((n_items, MLSTM_CHUNK, 2 * DH), BF16),
            pltpu.VMEM((n_items, DH, MLSTM_CHUNK), BF16),
            pltpu.VMEM((2 * NH, MLSTM_CHUNK), F32),
        ],
        compiler_params=pltpu.CompilerParams(
            dimension_semantics=("arbitrary",),
            vmem_limit_bytes=V7X_VMEM_LIMIT),
    )(x, x, w_in_t, b_gate2d, w_pool, pool_scale, w_out, ln_g, ln_b)


def _ffn_kernel(x1p_ref, x1s_ref, w1_ref, b1_ref, w2_ref, b2_ref, g_ref, be_ref, yp_ref, ys_ref,
                pre_ref, *, n_prompt_tiles):
    def residual_plus_mlp(x1_ref):
        acc = None
        for c in range(D_FF // FF_CHUNK):
            cols = slice(c * FF_CHUNK, (c + 1) * FF_CHUNK)
            hid = jnp.maximum(_dot(x1_ref[...], w1_ref[:, cols]) + b1_ref[:, cols], 0.0)
            part = _dot(hid * hid, w2_ref[cols, :])
            acc = part if acc is None else acc + part
        return ALPHA * x1_ref[...] + (acc + b2_ref[...])

    step = pl.program_id(0)

    @pl.when(step == 0)
    def _no_tile_yet():
        pre_ref[...] = jnp.zeros(pre_ref.shape, F32)

    @pl.when(step < n_prompt_tiles)
    def _prompt():
        yp_ref[...] = _layer_norm(pre_ref[...], g_ref[...], be_ref[...])
        pre_ref[...] = residual_plus_mlp(x1p_ref)

    @pl.when(step == n_prompt_tiles)
    def _sample():
        yp_ref[...] = _layer_norm(pre_ref[...], g_ref[...], be_ref[...])
        ys_ref[:, 0, :] = _layer_norm(residual_plus_mlp(x1s_ref), g_ref[...], be_ref[...])


def _ffn(x1p, x1s, w1, b1, w2, b2, ln_g, ln_b):
    tile = FFN_TILE
    n_tok = x1p.shape[0]
    assert n_tok % tile == 0
    n_tiles = n_tok // tile
    const2 = lambda i: (0, 0)
    ptile = lambda i: (jnp.minimum(i, n_tiles - 1), 0)
    return pl.pallas_call(
        functools.partial(_ffn_kernel, n_prompt_tiles=n_tiles),
        name="ffn_ln2",
        grid=(n_tiles + 1,),
        in_specs=[
            pl.BlockSpec((tile, D_MODEL), ptile),
            pl.BlockSpec(x1s.shape, const2),
            pl.BlockSpec(w1.shape, const2, pipeline_mode=pl.Buffered(1)),
            pl.BlockSpec(b1.shape, const2),
            pl.BlockSpec(w2.shape, const2, pipeline_mode=pl.Buffered(1)),
            pl.BlockSpec(b2.shape, const2),
            pl.BlockSpec(ln_g.shape, const2),
            pl.BlockSpec(ln_b.shape, const2),
        ],
        out_specs=[pl.BlockSpec((tile, D_MODEL), lambda i: (jnp.maximum(i - 1, 0), 0)),
                   pl.BlockSpec((x1s.shape[0], 1, D_MODEL), lambda i: (0, 0, 0))],
        out_shape=[jax.ShapeDtypeStruct((n_tok, D_MODEL), F32),
                   jax.ShapeDtypeStruct((x1s.shape[0], 1, D_MODEL), F32)],
        scratch_shapes=[pltpu.VMEM((tile, D_MODEL), F32)],
        compiler_params=pltpu.CompilerParams(
            dimension_semantics=("arbitrary",),
            vmem_limit_bytes=V7X_VMEM_LIMIT),
    )(x1p, x1s, w1, b1, w2, b2, ln_g, ln_b)


def _sample_proj_kernel(x_ref, wt_ref, bg_ref, sp_ref, n_ref, m_ref, wp_ref, ps_ref,
                        ypool_ref, pool_ref, q_ref, kd_ref, v_ref, iw_ref, sv_ref, den_ref,
                        floor_ref, osig_ref, nout_ref, mout_ref):
    x = x_ref[:, 0, :]
    nb = x.shape[0]
    proj = _dot_nt(x, wt_ref[0:CUT_G, :])
    u = proj[:, 0:D_POOL]
    gates = _dot_nt(wt_ref[CUT_G:CUT_G + 2 * NH, :], x) + _gate_bias_rows(bg_ref, nb)
    ig_r = gates[0:NH]
    inter_r = jax.nn.log_sigmoid(gates[NH:2 * NH]) + m_ref[...]
    m_t_r = jnp.maximum(inter_r, ig_r)
    dw_r = jnp.exp(ig_r - m_t_r)
    iw_r = jnp.exp(inter_r - m_t_r)
    floor_r = jnp.exp(-m_t_r)
    mout_ref[...] = m_t_r

    def as_column(row):
        return jnp.broadcast_to(row, (DH, nb)).T

    for g, w in enumerate(POOL_WINDOWS):
        lo = g * POOL_GC
        u_g = u[:, lo:lo + POOL_GC]
        acc = u_g
        for r in range(POOL_PAD - w, POOL_BUF):
            acc = acc + sp_ref[r, :, lo:lo + POOL_GC]
        pooled = acc / float(min(PAST_LEN + 1, w)) - u_g
        ypool_ref[:, lo:lo + POOL_GC] = _dot(pooled, wp_ref[g]) * ps_ref[:, lo:lo + POOL_GC]
    pool_ref[0:POOL_BUF - 1] = sp_ref[1:POOL_BUF]
    pool_ref[POOL_BUF - 1] = u

    for h in range(NH):
        col = slice(h * DH, (h + 1) * DH)
        dw = as_column(dw_r[h:h + 1])
        iw = as_column(iw_r[h:h + 1])
        q = proj[:, D_POOL + h * DH:D_POOL + (h + 1) * DH]
        k = proj[:, CUT_K + h * DH:CUT_K + (h + 1) * DH] * K_SCALE
        v = proj[:, CUT_V + h * DH:CUT_V + (h + 1) * DH]
        o = proj[:, CUT_V + D_MLSTM + h * DH:CUT_V + D_MLSTM + (h + 1) * DH]
        n_old = n_ref[:, h, :]
        s = jnp.sum(q * k, axis=-1, keepdims=True) * dw
        q_ref[:, col] = q
        kd_ref[:, col] = dw * k
        v_ref[:, col] = v
        iw_ref[:, col] = iw
        sv_ref[:, col] = s * v
        den_ref[:, col] = s + iw * jnp.sum(q * n_old, axis=-1, keepdims=True)
        floor_ref[:, col] = as_column(floor_r[h:h + 1])
        osig_ref[:, col] = jax.nn.sigmoid(o)
        nout_ref[:, h, :] = iw * n_old + dw * k


def _sample_proj(x, w_in_t, b_gate2d, pool_rows, n_state, m_rows, w_pool, pool_scale):
    nb = x.shape[0]
    wide = jax.ShapeDtypeStruct((nb, D_MLSTM), F32)
    vmem = pl.BlockSpec(memory_space=pltpu.VMEM)
    return pl.pallas_call(
        _sample_proj_kernel,
        name="sample_proj",
        in_specs=[vmem, vmem, pl.BlockSpec(memory_space=pltpu.SMEM), vmem, vmem, vmem, vmem, vmem],
        out_shape=([wide, jax.ShapeDtypeStruct(pool_rows.shape, F32)] + [wide] * 8
                   + [jax.ShapeDtypeStruct(n_state.shape, F32), jax.ShapeDtypeStruct(m_rows.shape, F32)]),
        compiler_params=pltpu.CompilerParams(vmem_limit_bytes=V7X_VMEM_LIMIT),
    )(x, w_in_t, b_gate2d, pool_rows, n_state, m_rows, w_pool, pool_scale)


def _sample_state_kernel(c_ref, q_ref, kd_ref, v_ref, iw_ref, cout_ref, inter_ref):
    nb = SAMPLE_BLOCK
    pad = jnp.zeros((DH - nb, DH), F32)
    for h in range(NH):
        col = slice(h * DH, (h + 1) * DH)
        q_t = jnp.concatenate([q_ref[:, col], pad], axis=0).T
        kd_t = jnp.concatenate([kd_ref[:, col], pad], axis=0).T
        for j in range(nb):
            c_old = c_ref[0, j, h]
            inter_ref[j:j + 1, col] = jnp.sum(q_t[:, j:j + 1] * c_old, axis=0, keepdims=True)
            cout_ref[0, j, h] = iw_ref[j:j + 1, col] * c_old + kd_t[:, j:j + 1] * v_ref[j:j + 1, col]


def _sample_state(state_c, q, kd, v, iw):
    nb = q.shape[0]
    blk = SAMPLE_BLOCK
    assert nb % blk == 0
    row_spec = pl.BlockSpec((blk, D_MLSTM), lambda i: (i, 0))
    c_spec = pl.BlockSpec((1, blk, NH, DH, DH), lambda i: (0, i, 0, 0, 0))
    return pl.pallas_call(
        _sample_state_kernel,
        name="sample_state",
        grid=(nb // blk,),
        in_specs=[c_spec, row_spec, row_spec, row_spec, row_spec],
        out_specs=[c_spec, row_spec],
        out_shape=[jax.ShapeDtypeStruct(state_c.shape, F32),
                   jax.ShapeDtypeStruct((nb, D_MLSTM), F32)],
        compiler_params=pltpu.CompilerParams(
            dimension_semantics=("arbitrary",),
            vmem_limit_bytes=V7X_VMEM_LIMIT),
    )(state_c, q, kd, v, iw)


def _sample_out_kernel(x_ref, ypool_ref, sv_ref, iw_ref, inter_ref, den_ref, floor_ref, osig_ref,
                       wo_ref, g1_ref, b1_ref, x1_ref):
    hh = (sv_ref[...] + iw_ref[...] * inter_ref[...]) / jnp.maximum(jnp.abs(den_ref[...]), floor_ref[...])
    mixin = jnp.concatenate([ypool_ref[...], osig_ref[...] * hh], axis=-1)
    x1_ref[...] = _layer_norm(ALPHA * x_ref[:, 0, :] + _dot(mixin, wo_ref[...]), g1_ref[...], b1_ref[...])


def _sample_out(x, ypool, sv, iw, inter, den, floor, osig, w_out, ln_g, ln_b):
    return pl.pallas_call(
        _sample_out_kernel,
        name="sample_out",
        out_shape=jax.ShapeDtypeStruct((x.shape[0], D_MODEL), F32),
        compiler_params=pltpu.CompilerParams(vmem_limit_bytes=V7X_VMEM_LIMIT),
    )(x, ypool, sv, iw, inter, den, floor, osig, w_out, ln_g, ln_b)


def kernel(x_prompt, x_sample, state_pool, state_C, state_n, state_m, w_in, b_gate, w_pool, pool_scale,
           w_out, ln1_g, ln1_b, w_ff1, b_ff1, w_ff2, b_ff2, ln2_g, ln2_b):
    assert w_in.shape[0] == DEPTH == 1
    bp, t_len, _ = x_prompt.shape
    bs = x_sample.shape[0]
    assert x_sample.shape[1] == 1

    w_t = jnp.transpose(w_in[0])
    wp = w_pool[0]
    ps = pool_scale[0].reshape(1, D_POOL)
    wo = w_out[0]
    g1, b1 = ln1_g[0].reshape(1, D_MODEL), ln1_b[0].reshape(1, D_MODEL)
    g2, b2 = ln2_g[0].reshape(1, D_MODEL), ln2_b[0].reshape(1, D_MODEL)
    wf1, wf2 = w_ff1[0], w_ff2[0]
    bf1, bf2 = b_ff1[0].reshape(1, D_FF), b_ff2[0].reshape(1, D_MODEL)

    x1_p, pool_p, c_p, n_p, m_rep = _prompt_mixer(x_prompt, w_t, b_gate, wp, ps, wo, g1, b1)
    m_p = m_rep[:, :, 0].reshape(DEPTH, bp, NH)

    (ypool, pool_s, q_s, kd_s, v_s, iw_s, sv_s, den_s, floor_s, osig_s, n_s, m_s_rows) = _sample_proj(
        x_sample, w_t, b_gate, jnp.transpose(state_pool[0], (1, 0, 2)),
        state_n[0], jnp.transpose(state_m[0]), wp, ps)
    c_s, inter_s = _sample_state(state_C, q_s, kd_s, v_s, iw_s)
    x1_s = _sample_out(x_sample, ypool, sv_s, iw_s, inter_s, den_s, floor_s, osig_s, wo, g1, b1)

    y_p, y_s = _ffn(x1_p.reshape(bp * t_len, D_MODEL), x1_s, wf1, bf1, wf2, bf2, g2, b2)

    return (y_p.reshape(bp, t_len, D_MODEL), y_s,
            pool_p, c_p, n_p, m_p,
            jnp.transpose(pool_s, (1, 0, 2))[None], c_s,
            n_s[None], jnp.transpose(m_s_rows)[None])
```

```python
import functools

import jax
import jax.numpy as jnp
from jax import lax
from jax.experimental import pallas as pl
from jax.experimental.pallas import tpu as pltpu

F32 = jnp.float32
BF16 = jnp.bfloat16

D_MODEL = 1024
D_POOL = D_MODEL // 2
D_MLSTM = D_MODEL - D_POOL
POOL_WINDOWS = (2, 4, 8, 16)
POOL_GC = D_POOL // len(POOL_WINDOWS)
POOL_BUF = max(POOL_WINDOWS) - 1
POOL_PAD = POOL_BUF + 1
NH = 4
DH = D_MLSTM // NH
D_FF = 4 * D_MODEL
DEPTH = 1
PAST_LEN = 16384
ALPHA = (2.0 * DEPTH) ** 0.25
LN_EPS = 1e-5
K_SCALE = DH ** -0.5

MLSTM_CHUNK = 128
MIX_TILE = 512
FFN_TILE = 512
FF_CHUNK = 2048
SAMPLE_BLOCK = 16
PROJ_COLS = 256
LANES = 128
assert POOL_GC == LANES and DH == LANES
CUT_K = D_POOL + D_MLSTM
CUT_V = CUT_K + D_MLSTM
CUT_G = D_POOL + 4 * D_MLSTM
LB_Q = D_POOL // LANES
LB_K = CUT_K // LANES
LB_V = CUT_V // LANES
LB_O = LB_V + NH
LB_G = CUT_G // LANES
N_LB = LB_G + 1
V7X_VMEM_LIMIT = 56 * 1024 * 1024


def _layer_norm(y, g, b):
    mu = jnp.mean(y, axis=-1, keepdims=True)
    yc = y - mu
    var = jnp.mean(yc * yc, axis=-1, keepdims=True)
    return yc * lax.rsqrt(var + LN_EPS) * g + b


def _dot(a, b):
    return jnp.dot(a, b, preferred_element_type=F32)


def _dot_nt(a, b):
    return lax.dot_general(a, b, (((1,), (1,)), ((), ())), preferred_element_type=F32)


def _twice(row):
    return jnp.concatenate([row, row], axis=-1)


def _gate_bias_rows(bg_ref, width):
    row = lax.broadcasted_iota(jnp.int32, (2 * NH, width), 0)
    bias = jnp.zeros((2 * NH, width), F32)
    for g in range(2 * NH):
        bias = jnp.where(row == g, bg_ref[0, g], bias)
    return bias


def _mixer_kernel(xn_ref, xc_ref, wt_ref, bg_ref, wp_ref, ps_ref, wo_ref, g1_ref, b1_ref,
                  x1_ref, pool_ref, c_ref, n_ref, m_ref,
                  win_ref, wob_ref, xb_ref, pm_a, pm_b, ext_ref, mix_ref, caug_ref, mst_ref,
                  lhs_ref, ktw_ref, gbias_ref, *, tiles_per_seq):
    tq = MIX_TILE
    L = MLSTM_CHUNK
    step = pl.program_id(0)
    t_idx = lax.rem(step + tiles_per_seq - 1, tiles_per_seq)
    last_t = tiles_per_seq - 1
    parity = lax.rem(step, 2)

    def project(pm_next):
        xb_ref[...] = xn_ref[0].astype(BF16)

        def piece(c0, c1):
            res = _dot(xb_ref[...], win_ref[:, c0:c1])
            for i in range((c1 - c0) // LANES):
                pm_next[c0 // LANES + i] = res[:, i * LANES:(i + 1) * LANES]

        n_cols = N_LB * LANES
        return [functools.partial(piece, c0, min(c0 + PROJ_COLS, n_cols))
                for c0 in range(0, n_cols, PROJ_COLS)]

    @pl.when(step == 0)
    def _first_step():
        for j in range(CUT_G // PROJ_COLS):
            win_ref[:, j * PROJ_COLS:(j + 1) * PROJ_COLS] = (
                wt_ref[j * PROJ_COLS:(j + 1) * PROJ_COLS, :].T.astype(BF16))
        gate_rows = jnp.concatenate([wt_ref[CUT_G:CUT_G + 2 * NH, :],
                                     jnp.zeros((LANES - 2 * NH, D_MODEL), F32)], axis=0)
        win_ref[:, CUT_G:CUT_G + LANES] = gate_rows.T.astype(BF16)
        wob_ref[...] = wo_ref[...].astype(BF16)
        gbias_ref[...] = _gate_bias_rows(bg_ref, L)
        for piece in project(pm_a):
            piece()

    @pl.when(t_idx == 0)
    def _init():
        ext_ref[:, 0:POOL_PAD, :] = jnp.zeros((len(POOL_WINDOWS), POOL_PAD, POOL_GC), F32)
        caug_ref[...] = jnp.zeros(caug_ref.shape, F32)
        mst_ref[...] = jnp.zeros(mst_ref.shape, F32)

    tt = lax.broadcasted_iota(jnp.int32, (L, L), 0)
    ss = lax.broadcasted_iota(jnp.int32, (L, L), 1)
    causal = ss <= tt
    diag = ss == tt

    def run(pm_next, pm_cur):
        pieces = project(pm_next)
        n_chunks = tq // L
        n_slots = len(POOL_WINDOWS) + 2 * n_chunks * NH
        slot_of = [(k * n_slots) // len(pieces) for k in range(len(pieces))]
        slot = [0]

        def next_slot():
            for k, piece in enumerate(pieces):
                if slot_of[k] == slot[0]:
                    piece()
            slot[0] += 1

        pos = t_idx * tq + lax.broadcasted_iota(jnp.int32, (tq, POOL_GC), 0)
        for g, w in enumerate(POOL_WINDOWS):
            lo = g * POOL_GC
            u_g = pm_cur[g]
            ext_ref[g, POOL_PAD:POOL_PAD + tq, :] = u_g
            s = ext_ref[g]
            k = 1
            while k < w:
                s = s + pltpu.roll(s, k, axis=0)
                k *= 2
            cnt = jnp.minimum(pos + 1, w).astype(F32)
            pooled = s[POOL_PAD:, :] / cnt - u_g
            mix_ref[:, lo:lo + POOL_GC] = (_dot(pooled, wp_ref[g]) * ps_ref[:, lo:lo + POOL_GC]).astype(BF16)
            ext_ref[g, 0:POOL_PAD, :] = ext_ref[g, tq:tq + POOL_PAD, :]
            next_slot()

        lane = lax.broadcasted_iota(jnp.int32, (NH, L), 1)
        gate_bias = gbias_ref[...]
        m_prev = mst_ref[...]
        chunk_rows = []
        for c in range(n_chunks):
            rows = slice(c * L, (c + 1) * L)
            gates = pm_cur[LB_G, rows, :].T[0:2 * NH, :] + gate_bias
            lf = jax.nn.log_sigmoid(gates[NH:2 * NH])
            b = lf
            k = 1
            while k < L:
                b = b + jnp.where(lane >= k, pltpu.roll(b, k, axis=1), 0.0)
                k *= 2
            a = gates[0:NH] - b
            cmax = a
            k = 1
            while k < L:
                cmax = jnp.maximum(cmax, jnp.where(lane >= k, pltpu.roll(cmax, k, axis=1), -jnp.inf))
                k *= 2
            amax = jnp.max(a, axis=-1, keepdims=True)
            big_m = jnp.maximum(cmax, m_prev)
            mm = jnp.maximum(m_prev, amax)
            chunk_rows.append(dict(
                a=a, big_m=big_m,
                iw=jnp.exp(m_prev - big_m),
                floor=jnp.exp(-(b + big_m)),
                w_loc=jnp.exp(a - amax),
                g_state=jnp.exp(m_prev - mm),
                f_state=jnp.exp(amax - mm)))
            m_prev = jnp.sum(lf, axis=-1, keepdims=True) + mm
        mst_ref[...] = m_prev

        def as_column(row):
            return jnp.broadcast_to(row, (L, L)).T

        for c in range(n_chunks):
            rows = slice(c * L, (c + 1) * L)
            cr = chunk_rows[c]
            for h in range(NH):
                i = c * NH + h
                q = pm_cur[LB_Q + h, rows, :]
                qb = q.astype(BF16)
                kt = pm_cur[LB_K + h, rows, :].T * K_SCALE
                p = jnp.exp(jnp.where(causal, cr["a"][h:h + 1], -jnp.inf) - as_column(cr["big_m"][h:h + 1]))
                lhs_ref[i, :, 0:L] = (_dot(qb, kt.astype(BF16)) * p).astype(BF16)
                ktw_ref[i] = (kt * cr["w_loc"][h:h + 1]).astype(BF16)
                lhs_ref[i, :, L:L + DH] = (q * as_column(cr["iw"][h:h + 1])).astype(BF16)
                next_slot()

        ones_blk = jnp.ones((L, DH), BF16)
        for c in range(n_chunks):
            rows = slice(c * L, (c + 1) * L)
            cr = chunk_rows[c]
            for h in range(NH):
                i = c * NH + h
                caug = caug_ref[h]
                v_aug = jnp.concatenate([pm_cur[LB_V + h, rows, :].astype(BF16), ones_blk], axis=1)
                comb = _dot(lhs_ref[i], jnp.concatenate([v_aug, caug.astype(BF16)], axis=0))
                hh = comb[:, :DH] / jnp.maximum(jnp.abs(comb[:, DH:]), as_column(cr["floor"][h:h + 1]))
                o = pm_cur[LB_O + h, rows, :]
                mix_ref[rows, D_POOL + h * DH:D_POOL + (h + 1) * DH] = (
                    jax.nn.sigmoid(o) * hh).astype(BF16)
                caug_ref[h] = (_twice(cr["g_state"][h:h + 1]) * caug
                               + _twice(cr["f_state"][h:h + 1]) * _dot(ktw_ref[i], v_aug))
                next_slot()

        mix = _dot(mix_ref[...], wob_ref[...])
        x1_ref[0] = _layer_norm(ALPHA * xc_ref[0] + mix, g1_ref[...], b1_ref[...])

    @pl.when((parity == 0) & (step > 0))
    def _even():
        run(pm_a, pm_b)

    @pl.when(parity == 1)
    def _odd():
        run(pm_b, pm_a)

    @pl.when((t_idx == last_t) & (step > 0))
    def _final_state():
        for g in range(len(POOL_WINDOWS)):
            pool_ref[0, 0, :, g * POOL_GC:(g + 1) * POOL_GC] = ext_ref[g, tq + 1:tq + POOL_PAD, :]
        for h in range(NH):
            caug = caug_ref[h]
            c_ref[0, 0, h] = caug[:, :DH]
            n_ref[0, 0, h:h + 1, :] = jnp.sum(jnp.where(diag, caug[:, DH:], 0.0),
                                              axis=0, keepdims=True)
        m_ref[0] = mst_ref[...]


def _prompt_mixer(x, w_in_t, b_gate2d, w_pool, pool_scale, w_out, ln_g, ln_b):
    bsz, t_len, _ = x.shape
    tq = MIX_TILE
    assert t_len % tq == 0 and tq % MLSTM_CHUNK == 0 and tq >= POOL_PAD
    nt = t_len // tq
    n_tiles = bsz * nt
    n_items = (tq // MLSTM_CHUNK) * NH
    const2 = lambda i: (0, 0)
    nxt = lambda i: jnp.minimum(i, n_tiles - 1)
    cur = lambda i: jnp.maximum(i - 1, 0)
    return pl.pallas_call(
        functools.partial(_mixer_kernel, tiles_per_seq=nt),
        name="prompt_mixer",
        grid=(n_tiles + 1,),
        in_specs=[
            pl.BlockSpec((1, tq, D_MODEL), lambda i: (nxt(i) // nt, nxt(i) % nt, 0)),
            pl.BlockSpec((1, tq, D_MODEL), lambda i: (cur(i) // nt, cur(i) % nt, 0)),
            pl.BlockSpec(w_in_t.shape, const2, pipeline_mode=pl.Buffered(1)),
            pl.BlockSpec(memory_space=pltpu.SMEM),
            pl.BlockSpec(w_pool.shape, lambda i: (0, 0, 0)),
            pl.BlockSpec(pool_scale.shape, const2),
            pl.BlockSpec(w_out.shape, const2, pipeline_mode=pl.Buffered(1)),
            pl.BlockSpec(ln_g.shape, const2),
            pl.BlockSpec(ln_b.shape, const2),
        ],
        out_specs=[
            pl.BlockSpec((1, tq, D_MODEL), lambda i: (cur(i) // nt, cur(i) % nt, 0)),
            pl.BlockSpec((1, 1, POOL_BUF, D_POOL), lambda i: (0, cur(i) // nt, 0, 0)),
            pl.BlockSpec((1, 1, NH, DH, DH), lambda i: (0, cur(i) // nt, 0, 0, 0)),
            pl.BlockSpec((1, 1, NH, DH), lambda i: (0, cur(i) // nt, 0, 0)),
            pl.BlockSpec((1, NH, MLSTM_CHUNK), lambda i: (cur(i) // nt, 0, 0)),
        ],
        out_shape=[
            jax.ShapeDtypeStruct((bsz, t_len, D_MODEL), F32),
            jax.ShapeDtypeStruct((DEPTH, bsz, POOL_BUF, D_POOL), F32),
            jax.ShapeDtypeStruct((DEPTH, bsz, NH, DH, DH), F32),
            jax.ShapeDtypeStruct((DEPTH, bsz, NH, DH), F32),
            jax.ShapeDtypeStruct((bsz, NH, MLSTM_CHUNK), F32),
        ],
        scratch_shapes=[
            pltpu.VMEM((D_MODEL, N_LB * LANES), BF16),
            pltpu.VMEM((D_MODEL, D_MODEL), BF16),
            pltpu.VMEM((tq, D_MODEL), BF16),
            pltpu.VMEM((N_LB, tq, LANES), F32),
            pltpu.VMEM((N_LB, tq, LANES), F32),
            pltpu.VMEM((len(POOL_WINDOWS), POOL_PAD + tq, POOL_GC), F32),
            pltpu.VMEM((tq, D_MODEL), BF16),
            pltpu.VMEM((NH, DH, 2 * DH), F32),
            pltpu.VMEM((NH, MLSTM_CHUNK), F32),
            pltpu.VMEM((n_items, MLSTM_CHUNK, MLSTM_CHUNK + DH), BF16),
            pltpu.VMEM((n_items, DH, MLSTM_CHUNK), BF16),
            pltpu.VMEM((2 * NH, MLSTM_CHUNK), F32),
        ],
        compiler_params=pltpu.CompilerParams(
            dimension_semantics=("arbitrary",),
            vmem_limit_bytes=V7X_VMEM_LIMIT),
    )(x, x, w_in_t, b_gate2d, w_pool, pool_scale, w_out, ln_g, ln_b)


def _ffn_kernel(x1p_ref, x1s_ref, w1_ref, b1_ref, w2_ref, b2_ref, g_ref, be_ref, yp_ref, ys_ref,
                pre_ref, *, n_prompt_tiles):
    def residual_plus_mlp(x1_ref):
        acc = None
        for c in range(D_FF // FF_CHUNK):
            cols = slice(c * FF_CHUNK, (c + 1) * FF_CHUNK)
            hid = jnp.maximum(_dot(x1_ref[...], w1_ref[:, cols]) + b1_ref[:, cols], 0.0)
            part = _dot(hid * hid, w2_ref[cols, :])
            acc = part if acc is None else acc + part
        return ALPHA * x1_ref[...] + (acc + b2_ref[...])

    step = pl.program_id(0)

    @pl.when(step == 0)
    def _no_tile_yet():
        pre_ref[...] = jnp.zeros(pre_ref.shape, F32)

    @pl.when(step < n_prompt_tiles)
    def _prompt():
        yp_ref[...] = _layer_norm(pre_ref[...], g_ref[...], be_ref[...])
        pre_ref[...] = residual_plus_mlp(x1p_ref)

    @pl.when(step == n_prompt_tiles)
    def _sample():
        yp_ref[...] = _layer_norm(pre_ref[...], g_ref[...], be_ref[...])
        ys_ref[:, 0, :] = _layer_norm(residual_plus_mlp(x1s_ref), g_ref[...], be_ref[...])


def _ffn(x1p, x1s, w1, b1, w2, b2, ln_g, ln_b):
    tile = FFN_TILE
    n_tok = x1p.shape[0]
    assert n_tok % tile == 0
    n_tiles = n_tok // tile
    const2 = lambda i: (0, 0)
    ptile = lambda i: (jnp.minimum(i, n_tiles - 1), 0)
    return pl.pallas_call(
        functools.partial(_ffn_kernel, n_prompt_tiles=n_tiles),
        name="ffn_ln2",
        grid=(n_tiles + 1,),
        in_specs=[
            pl.BlockSpec((tile, D_MODEL), ptile),
            pl.BlockSpec(x1s.shape, const2),
            pl.BlockSpec(w1.shape, const2, pipeline_mode=pl.Buffered(1)),
            pl.BlockSpec(b1.shape, const2),
            pl.BlockSpec(w2.shape, const2, pipeline_mode=pl.Buffered(1)),
            pl.BlockSpec(b2.shape, const2),
            pl.BlockSpec(ln_g.shape, const2),
            pl.BlockSpec(ln_b.shape, const2),
        ],
        out_specs=[pl.BlockSpec((tile, D_MODEL), lambda i: (jnp.maximum(i - 1, 0), 0)),
                   pl.BlockSpec((x1s.shape[0], 1, D_MODEL), lambda i: (0, 0, 0))],
        out_shape=[jax.ShapeDtypeStruct((n_tok, D_MODEL), F32),
                   jax.ShapeDtypeStruct((x1s.shape[0], 1, D_MODEL), F32)],
        scratch_shapes=[pltpu.VMEM((tile, D_MODEL), F32)],
        compiler_params=pltpu.CompilerParams(
            dimension_semantics=("arbitrary",),
            vmem_limit_bytes=V7X_VMEM_LIMIT),
    )(x1p, x1s, w1, b1, w2, b2, ln_g, ln_b)


def _sample_proj_kernel(x_ref, wt_ref, bg_ref, sp_ref, n_ref, m_ref, wp_ref, ps_ref,
                        ypool_ref, pool_ref, q_ref, kd_ref, v_ref, iw_ref, sv_ref, den_ref,
                        floor_ref, osig_ref, nout_ref, mout_ref):
    x = x_ref[:, 0, :]
    nb = x.shape[0]
    proj = _dot_nt(x, wt_ref[0:CUT_G, :])
    u = proj[:, 0:D_POOL]
    gates = _dot_nt(wt_ref[CUT_G:CUT_G + 2 * NH, :], x) + _gate_bias_rows(bg_ref, nb)
    ig_r = gates[0:NH]
    inter_r = jax.nn.log_sigmoid(gates[NH:2 * NH]) + m_ref[...]
    m_t_r = jnp.maximum(inter_r, ig_r)
    dw_r = jnp.exp(ig_r - m_t_r)
    iw_r = jnp.exp(inter_r - m_t_r)
    floor_r = jnp.exp(-m_t_r)
    mout_ref[...] = m_t_r

    def as_column(row):
        return jnp.broadcast_to(row, (DH, nb)).T

    for g, w in enumerate(POOL_WINDOWS):
        lo = g * POOL_GC
        u_g = u[:, lo:lo + POOL_GC]
        acc = u_g
        for r in range(POOL_PAD - w, POOL_BUF):
            acc = acc + sp_ref[r, :, lo:lo + POOL_GC]
        pooled = acc / float(min(PAST_LEN + 1, w)) - u_g
        ypool_ref[:, lo:lo + POOL_GC] = _dot(pooled, wp_ref[g]) * ps_ref[:, lo:lo + POOL_GC]
    pool_ref[0:POOL_BUF - 1] = sp_ref[1:POOL_BUF]
    pool_ref[POOL_BUF - 1] = u

    for h in range(NH):
        col = slice(h * DH, (h + 1) * DH)
        dw = as_column(dw_r[h:h + 1])
        iw = as_column(iw_r[h:h + 1])
        q = proj[:, D_POOL + h * DH:D_POOL + (h + 1) * DH]
        k = proj[:, CUT_K + h * DH:CUT_K + (h + 1) * DH] * K_SCALE
        v = proj[:, CUT_V + h * DH:CUT_V + (h + 1) * DH]
        o = proj[:, CUT_V + D_MLSTM + h * DH:CUT_V + D_MLSTM + (h + 1) * DH]
        n_old = n_ref[:, h, :]
        s = jnp.sum(q * k, axis=-1, keepdims=True) * dw
        q_ref[:, col] = q
        kd_ref[:, col] = dw * k
        v_ref[:, col] = v
        iw_ref[:, col] = iw
        sv_ref[:, col] = s * v
        den_ref[:, col] = s + iw * jnp.sum(q * n_old, axis=-1, keepdims=True)
        floor_ref[:, col] = as_column(floor_r[h:h + 1])
        osig_ref[:, col] = jax.nn.sigmoid(o)
        nout_ref[:, h, :] = iw * n_old + dw * k


def _sample_proj(x, w_in_t, b_gate2d, pool_rows, n_state, m_rows, w_pool, pool_scale):
    nb = x.shape[0]
    wide = jax.ShapeDtypeStruct((nb, D_MLSTM), F32)
    vmem = pl.BlockSpec(memory_space=pltpu.VMEM)
    return pl.pallas_call(
        _sample_proj_kernel,
        name="sample_proj",
        in_specs=[vmem, vmem, pl.BlockSpec(memory_space=pltpu.SMEM), vmem, vmem, vmem, vmem, vmem],
        out_shape=([wide, jax.ShapeDtypeStruct(pool_rows.shape, F32)] + [wide] * 8
                   + [jax.ShapeDtypeStruct(n_state.shape, F32), jax.ShapeDtypeStruct(m_rows.shape, F32)]),
        compiler_params=pltpu.CompilerParams(vmem_limit_bytes=V7X_VMEM_LIMIT),
    )(x, w_in_t, b_gate2d, pool_rows, n_state, m_rows, w_pool, pool_scale)


def _sample_state_kernel(c_ref, q_ref, kd_ref, v_ref, iw_ref, cout_ref, inter_ref):
    nb = SAMPLE_BLOCK
    pad = jnp.zeros((DH - nb, DH), F32)
    for h in range(NH):
        col = slice(h * DH, (h + 1) * DH)
        q_t = jnp.concatenate([q_ref[:, col], pad], axis=0).T
        kd_t = jnp.concatenate([kd_ref[:, col], pad], axis=0).T
        for j in range(nb):
            c_old = c_ref[0, j, h]
            inter_ref[j:j + 1, col] = jnp.sum(q_t[:, j:j + 1] * c_old, axis=0, keepdims=True)
            cout_ref[0, j, h] = iw_ref[j:j + 1, col] * c_old + kd_t[:, j:j + 1] * v_ref[j:j + 1, col]


def _sample_state(state_c, q, kd, v, iw):
    nb = q.shape[0]
    blk = SAMPLE_BLOCK
    assert nb % blk == 0
    row_spec = pl.BlockSpec((blk, D_MLSTM), lambda i: (i, 0))
    c_spec = pl.BlockSpec((1, blk, NH, DH, DH), lambda i: (0, i, 0, 0, 0))
    return pl.pallas_call(
        _sample_state_kernel,
        name="sample_state",
        grid=(nb // blk,),
        in_specs=[c_spec, row_spec, row_spec, row_spec, row_spec],
        out_specs=[c_spec, row_spec],
        out_shape=[jax.ShapeDtypeStruct(state_c.shape, F32),
                   jax.ShapeDtypeStruct((nb, D_MLSTM), F32)],
        compiler_params=pltpu.CompilerParams(
            dimension_semantics=("arbitrary",),
            vmem_limit_bytes=V7X_VMEM_LIMIT),
    )(state_c, q, kd, v, iw)


def _sample_out_kernel(x_ref, ypool_ref, sv_ref, iw_ref, inter_ref, den_ref, floor_ref, osig_ref,
                       wo_ref, g1_ref, b1_ref, x1_ref):
    hh = (sv_ref[...] + iw_ref[...] * inter_ref[...]) / jnp.maximum(jnp.abs(den_ref[...]), floor_ref[...])
    mixin = jnp.concatenate([ypool_ref[...], osig_ref[...] * hh], axis=-1)
    x1_ref[...] = _layer_norm(ALPHA * x_ref[:, 0, :] + _dot(mixin, wo_ref[...]), g1_ref[...], b1_ref[...])


def _sample_out(x, ypool, sv, iw, inter, den, floor, osig, w_out, ln_g, ln_b):
    return pl.pallas_call(
        _sample_out_kernel,
        name="sample_out",
        out_shape=jax.ShapeDtypeStruct((x.shape[0], D_MODEL), F32),
        compiler_params=pltpu.CompilerParams(vmem_limit_bytes=V7X_VMEM_LIMIT),
    )(x, ypool, sv, iw, inter, den, floor, osig, w_out, ln_g, ln_b)


def kernel(x_prompt, x_sample, state_pool, state_C, state_n, state_m, w_in, b_gate, w_pool, pool_scale,
           w_out, ln1_g, ln1_b, w_ff1, b_ff1, w_ff2, b_ff2, ln2_g, ln2_b):
    assert w_in.shape[0] == DEPTH == 1
    bp, t_len, _ = x_prompt.shape
    bs = x_sample.shape[0]
    assert x_sample.shape[1] == 1

    w_t = jnp.transpose(w_in[0])
    wp = w_pool[0]
    ps = pool_scale[0].reshape(1, D_POOL)
    wo = w_out[0]
    g1, b1 = ln1_g[0].reshape(1, D_MODEL), ln1_b[0].reshape(1, D_MODEL)
    g2, b2 = ln2_g[0].reshape(1, D_MODEL), ln2_b[0].reshape(1, D_MODEL)
    wf1, wf2 = w_ff1[0], w_ff2[0]
    bf1, bf2 = b_ff1[0].reshape(1, D_FF), b_ff2[0].reshape(1, D_MODEL)

    x1_p, pool_p, c_p, n_p, m_rep = _prompt_mixer(x_prompt, w_t, b_gate, wp, ps, wo, g1, b1)
    m_p = m_rep[:, :, 0].reshape(DEPTH, bp, NH)

    (ypool, pool_s, q_s, kd_s, v_s, iw_s, sv_s, den_s, floor_s, osig_s, n_s, m_s_rows) = _sample_proj(
        x_sample, w_t, b_gate, jnp.transpose(state_pool[0], (1, 0, 2)),
        state_n[0], jnp.transpose(state_m[0]), wp, ps)
    c_s, inter_s = _sample_state(state_C, q_s, kd_s, v_s, iw_s)
    x1_s = _sample_out(x_sample, ypool, sv_s, iw_s, inter_s, den_s, floor_s, osig_s, wo, g1, b1)

    y_p, y_s = _ffn(x1_p.reshape(bp * t_len, D_MODEL), x1_s, wf1, bf1, wf2, bf2, g2, b2)

    return (y_p.reshape(bp, t_len, D_MODEL), y_s,
            pool_p, c_p, n_p, m_p,
            jnp.transpose(pool_s, (1, 0, 2))[None], c_s,
            n_s[None], jnp.transpose(m_s_rows)[None])
```

```python
import functools

import jax
import jax.numpy as jnp
from jax import lax
from jax.experimental import pallas as pl
from jax.experimental.pallas import tpu as pltpu

F32 = jnp.float32
BF16 = jnp.bfloat16

D_MODEL = 1024
D_POOL = D_MODEL // 2
D_MLSTM = D_MODEL - D_POOL
POOL_WINDOWS = (2, 4, 8, 16)
POOL_GC = D_POOL // len(POOL_WINDOWS)
POOL_BUF = max(POOL_WINDOWS) - 1
POOL_PAD = POOL_BUF + 1
NH = 4
DH = D_MLSTM // NH
D_FF = 4 * D_MODEL
DEPTH = 1
PAST_LEN = 16384
ALPHA = (2.0 * DEPTH) ** 0.25
LN_EPS = 1e-5
K_SCALE = DH ** -0.5

MLSTM_CHUNK = 128
MIX_TILE = 512
FFN_TILE = 512
FF_CHUNK = 2048
SAMPLE_SEQS = 4
PROJ_COLS = 256
LANES = 128
assert POOL_GC == LANES and DH == LANES
CUT_K = D_POOL + D_MLSTM
CUT_V = CUT_K + D_MLSTM
CUT_G = D_POOL + 4 * D_MLSTM
LB_Q = D_POOL // LANES
LB_K = CUT_K // LANES
LB_V = CUT_V // LANES
LB_O = LB_V + NH
LB_G = CUT_G // LANES
N_LB = LB_G + 1
V7X_VMEM_LIMIT = 60 * 1024 * 1024


def _layer_norm(y, g, b):
    mu = jnp.mean(y, axis=-1, keepdims=True)
    yc = y - mu
    var = jnp.mean(yc * yc, axis=-1, keepdims=True)
    return yc * lax.rsqrt(var + LN_EPS) * g + b


def _dot(a, b):
    return jnp.dot(a, b, preferred_element_type=F32)


def _dot_nt(a, b):
    return lax.dot_general(a, b, (((1,), (1,)), ((), ())), preferred_element_type=F32)


def _twice(row):
    return jnp.concatenate([row, row], axis=-1)


def _gate_bias_rows(bg_ref, width):
    row = lax.broadcasted_iota(jnp.int32, (2 * NH, width), 0)
    bias = jnp.zeros((2 * NH, width), F32)
    for g in range(2 * NH):
        bias = jnp.where(row == g, bg_ref[0, g], bias)
    return bias


def _mixer_kernel(xn_ref, xc_ref, wt_ref, bg_ref, wp_ref, ps_ref, wo_ref, g1_ref, b1_ref,
                  sc_ref, sq_ref, skd_ref, sv_ref, siw_ref,
                  x1_ref, pool_ref, c_ref, n_ref, m_ref, scout_ref, sinter_ref,
                  win_ref, wob_ref, xb_ref, pm_a, pm_b, ext_ref, mix_ref, caug_ref, mst_ref,
                  lhs_ref, ktw_ref, gbias_ref, *, tiles_per_seq):
    tq = MIX_TILE
    L = MLSTM_CHUNK
    step = pl.program_id(0)
    t_idx = lax.rem(step + tiles_per_seq - 1, tiles_per_seq)
    last_t = tiles_per_seq - 1
    parity = lax.rem(step, 2)

    def project(pm_next):
        xb_ref[...] = xn_ref[0].astype(BF16)

        def piece(c0, c1):
            res = _dot(xb_ref[...], win_ref[:, c0:c1])
            for i in range((c1 - c0) // LANES):
                pm_next[c0 // LANES + i] = res[:, i * LANES:(i + 1) * LANES]

        n_cols = N_LB * LANES
        return [functools.partial(piece, c0, min(c0 + PROJ_COLS, n_cols))
                for c0 in range(0, n_cols, PROJ_COLS)]

    @pl.when(step == 0)
    def _first_step():
        for j in range(CUT_G // PROJ_COLS):
            win_ref[:, j * PROJ_COLS:(j + 1) * PROJ_COLS] = (
                wt_ref[j * PROJ_COLS:(j + 1) * PROJ_COLS, :].T.astype(BF16))
        gate_rows = jnp.concatenate([wt_ref[CUT_G:CUT_G + 2 * NH, :],
                                     jnp.zeros((LANES - 2 * NH, D_MODEL), F32)], axis=0)
        win_ref[:, CUT_G:CUT_G + LANES] = gate_rows.T.astype(BF16)
        wob_ref[...] = wo_ref[...].astype(BF16)
        gbias_ref[...] = _gate_bias_rows(bg_ref, L)
        for piece in project(pm_a):
            piece()

    @pl.when(t_idx == 0)
    def _init():
        ext_ref[:, 0:POOL_PAD, :] = jnp.zeros((len(POOL_WINDOWS), POOL_PAD, POOL_GC), F32)
        caug_ref[...] = jnp.zeros(caug_ref.shape, F32)
        mst_ref[...] = jnp.zeros(mst_ref.shape, F32)

    tt = lax.broadcasted_iota(jnp.int32, (L, L), 0)
    ss = lax.broadcasted_iota(jnp.int32, (L, L), 1)
    causal = ss <= tt
    diag = ss == tt

    def run(pm_next, pm_cur):
        pieces = project(pm_next)
        n_chunks = tq // L
        n_slots = len(POOL_WINDOWS) + 2 * n_chunks * NH
        slot_of = [(k * n_slots) // len(pieces) for k in range(len(pieces))]
        slot = [0]

        def next_slot():
            for k, piece in enumerate(pieces):
                if slot_of[k] == slot[0]:
                    piece()
            slot[0] += 1

        pos = t_idx * tq + lax.broadcasted_iota(jnp.int32, (tq, POOL_GC), 0)
        for g, w in enumerate(POOL_WINDOWS):
            lo = g * POOL_GC
            u_g = pm_cur[g]
            ext_ref[g, POOL_PAD:POOL_PAD + tq, :] = u_g
            s = ext_ref[g]
            k = 1
            while k < w:
                s = s + pltpu.roll(s, k, axis=0)
                k *= 2
            cnt = jnp.minimum(pos + 1, w).astype(F32)
            pooled = s[POOL_PAD:, :] / cnt - u_g
            mix_ref[:, lo:lo + POOL_GC] = (_dot(pooled, wp_ref[g]) * ps_ref[:, lo:lo + POOL_GC]).astype(BF16)
            ext_ref[g, 0:POOL_PAD, :] = ext_ref[g, tq:tq + POOL_PAD, :]
            next_slot()

        lane = lax.broadcasted_iota(jnp.int32, (NH, L), 1)
        gate_bias = gbias_ref[...]
        m_prev = mst_ref[...]
        chunk_rows = []
        for c in range(n_chunks):
            rows = slice(c * L, (c + 1) * L)
            gates = pm_cur[LB_G, rows, :].T[0:2 * NH, :] + gate_bias
            lf = jax.nn.log_sigmoid(gates[NH:2 * NH])
            b = lf
            k = 1
            while k < L:
                b = b + jnp.where(lane >= k, pltpu.roll(b, k, axis=1), 0.0)
                k *= 2
            a = gates[0:NH] - b
            cmax = a
            k = 1
            while k < L:
                cmax = jnp.maximum(cmax, jnp.where(lane >= k, pltpu.roll(cmax, k, axis=1), -jnp.inf))
                k *= 2
            amax = jnp.max(a, axis=-1, keepdims=True)
            big_m = jnp.maximum(cmax, m_prev)
            mm = jnp.maximum(m_prev, amax)
            chunk_rows.append(dict(
                a=a, big_m=big_m,
                iw=jnp.exp(m_prev - big_m),
                floor=jnp.exp(-(b + big_m)),
                w_loc=jnp.exp(a - amax),
                g_state=jnp.exp(m_prev - mm),
                f_state=jnp.exp(amax - mm)))
            m_prev = jnp.sum(lf, axis=-1, keepdims=True) + mm
        mst_ref[...] = m_prev

        def as_column(row):
            return jnp.broadcast_to(row, (L, L)).T

        pair = (step - 1) // 2
        upper_half = lax.rem(step - 1, 2) == 1
        pad = jnp.zeros((DH - SAMPLE_SEQS, DH), F32)

        def step_rows(ref):
            rows8 = ref[pl.ds(pl.multiple_of(pair * 2 * SAMPLE_SEQS, 2 * SAMPLE_SEQS), 2 * SAMPLE_SEQS), :]
            return jnp.where(upper_half, rows8[SAMPLE_SEQS:], rows8[:SAMPLE_SEQS])

        s_q, s_kd, s_v, s_iw = (step_rows(r) for r in (sq_ref, skd_ref, sv_ref, siw_ref))

        def sample_item(h, j):
            col = slice(h * DH, (h + 1) * DH)
            c_old = sc_ref[0, j, h]
            sinter_ref[0, j:j + 1, col] = jnp.sum(sample_q_t[h][:, j:j + 1] * c_old, axis=0, keepdims=True)
            scout_ref[0, j, h] = s_iw[j:j + 1, col] * c_old + sample_kd_t[h][:, j:j + 1] * s_v[j:j + 1, col]

        sample_q_t = [jnp.concatenate([s_q[:, h * DH:(h + 1) * DH], pad], axis=0).T for h in range(NH)]
        sample_kd_t = [jnp.concatenate([s_kd[:, h * DH:(h + 1) * DH], pad], axis=0).T for h in range(NH)]
        sample_items = [functools.partial(sample_item, h, j) for h in range(NH) for j in range(SAMPLE_SEQS)]

        for c in range(n_chunks):
            rows = slice(c * L, (c + 1) * L)
            cr = chunk_rows[c]
            for h in range(NH):
                i = c * NH + h
                q = pm_cur[LB_Q + h, rows, :]
                qb = q.astype(BF16)
                kt = pm_cur[LB_K + h, rows, :].T * K_SCALE
                p = jnp.exp(jnp.where(causal, cr["a"][h:h + 1], -jnp.inf) - as_column(cr["big_m"][h:h + 1]))
                lhs_ref[i, :, 0:L] = (_dot(qb, kt.astype(BF16)) * p).astype(BF16)
                ktw_ref[i] = (kt * cr["w_loc"][h:h + 1]).astype(BF16)
                lhs_ref[i, :, L:L + DH] = (q * as_column(cr["iw"][h:h + 1])).astype(BF16)
                if i < len(sample_items):
                    sample_items[i]()
                next_slot()

        ones_blk = jnp.ones((L, DH), BF16)
        for c in range(n_chunks):
            rows = slice(c * L, (c + 1) * L)
            cr = chunk_rows[c]
            for h in range(NH):
                i = c * NH + h
                caug = caug_ref[h]
                v_aug = jnp.concatenate([pm_cur[LB_V + h, rows, :].astype(BF16), ones_blk], axis=1)
                comb = _dot(lhs_ref[i], jnp.concatenate([v_aug, caug.astype(BF16)], axis=0))
                hh = comb[:, :DH] / jnp.maximum(jnp.abs(comb[:, DH:]), as_column(cr["floor"][h:h + 1]))
                o = pm_cur[LB_O + h, rows, :]
                mix_ref[rows, D_POOL + h * DH:D_POOL + (h + 1) * DH] = (
                    jax.nn.sigmoid(o) * hh).astype(BF16)
                caug_ref[h] = (_twice(cr["g_state"][h:h + 1]) * caug
                               + _twice(cr["f_state"][h:h + 1]) * _dot(ktw_ref[i], v_aug))
                next_slot()

        mix = _dot(mix_ref[...], wob_ref[...])
        x1_ref[0] = _layer_norm(ALPHA * xc_ref[0] + mix, g1_ref[...], b1_ref[...])

    @pl.when((parity == 0) & (step > 0))
    def _even():
        run(pm_a, pm_b)

    @pl.when(parity == 1)
    def _odd():
        run(pm_b, pm_a)

    @pl.when((t_idx == last_t) & (step > 0))
    def _final_state():
        for g in range(len(POOL_WINDOWS)):
            pool_ref[0, 0, :, g * POOL_GC:(g + 1) * POOL_GC] = ext_ref[g, tq + 1:tq + POOL_PAD, :]
        for h in range(NH):
            caug = caug_ref[h]
            c_ref[0, 0, h] = caug[:, :DH]
            n_ref[0, 0, h:h + 1, :] = jnp.sum(jnp.where(diag, caug[:, DH:], 0.0),
                                              axis=0, keepdims=True)
        m_ref[0] = mst_ref[...]


def _prompt_mixer(x, w_in_t, b_gate2d, w_pool, pool_scale, w_out, ln_g, ln_b, state_c, sq, skd, sv, siw):
    bsz, t_len, _ = x.shape
    tq = MIX_TILE
    assert t_len % tq == 0 and tq % MLSTM_CHUNK == 0 and tq >= POOL_PAD
    nt = t_len // tq
    n_tiles = bsz * nt
    n_seq = sq.shape[0]
    assert n_seq == n_tiles * SAMPLE_SEQS and SAMPLE_SEQS * NH <= (tq // MLSTM_CHUNK) * NH
    n_items = (tq // MLSTM_CHUNK) * NH
    const2 = lambda i: (0, 0)
    nxt = lambda i: jnp.minimum(i, n_tiles - 1)
    cur = lambda i: jnp.maximum(i - 1, 0)
    return pl.pallas_call(
        functools.partial(_mixer_kernel, tiles_per_seq=nt),
        name="prompt_mixer",
        grid=(n_tiles + 1,),
        in_specs=[
            pl.BlockSpec((1, tq, D_MODEL), lambda i: (nxt(i) // nt, nxt(i) % nt, 0)),
            pl.BlockSpec((1, tq, D_MODEL), lambda i: (cur(i) // nt, cur(i) % nt, 0)),
            pl.BlockSpec(w_in_t.shape, const2, pipeline_mode=pl.Buffered(1)),
            pl.BlockSpec(memory_space=pltpu.SMEM),
            pl.BlockSpec(w_pool.shape, lambda i: (0, 0, 0)),
            pl.BlockSpec(pool_scale.shape, const2),
            pl.BlockSpec(w_out.shape, const2, pipeline_mode=pl.Buffered(1)),
            pl.BlockSpec(ln_g.shape, const2),
            pl.BlockSpec(ln_b.shape, const2),
            pl.BlockSpec((1, SAMPLE_SEQS, NH, DH, DH), lambda i: (0, cur(i), 0, 0, 0)),
            pl.BlockSpec(sq.shape, const2, pipeline_mode=pl.Buffered(1)),
            pl.BlockSpec(skd.shape, const2, pipeline_mode=pl.Buffered(1)),
            pl.BlockSpec(sv.shape, const2, pipeline_mode=pl.Buffered(1)),
            pl.BlockSpec(siw.shape, const2, pipeline_mode=pl.Buffered(1)),
        ],
        out_specs=[
            pl.BlockSpec((1, tq, D_MODEL), lambda i: (cur(i) // nt, cur(i) % nt, 0)),
            pl.BlockSpec((1, 1, POOL_BUF, D_POOL), lambda i: (0, cur(i) // nt, 0, 0)),
            pl.BlockSpec((1, 1, NH, DH, DH), lambda i: (0, cur(i) // nt, 0, 0, 0)),
            pl.BlockSpec((1, 1, NH, DH), lambda i: (0, cur(i) // nt, 0, 0)),
            pl.BlockSpec((1, NH, MLSTM_CHUNK), lambda i: (cur(i) // nt, 0, 0)),
            pl.BlockSpec((1, SAMPLE_SEQS, NH, DH, DH), lambda i: (0, cur(i), 0, 0, 0)),
            pl.BlockSpec((1, SAMPLE_SEQS, D_MLSTM), lambda i: (cur(i), 0, 0)),
        ],
        out_shape=[
            jax.ShapeDtypeStruct((bsz, t_len, D_MODEL), F32),
            jax.ShapeDtypeStruct((DEPTH, bsz, POOL_BUF, D_POOL), F32),
            jax.ShapeDtypeStruct((DEPTH, bsz, NH, DH, DH), F32),
            jax.ShapeDtypeStruct((DEPTH, bsz, NH, DH), F32),
            jax.ShapeDtypeStruct((bsz, NH, MLSTM_CHUNK), F32),
            jax.ShapeDtypeStruct(state_c.shape, F32),
            jax.ShapeDtypeStruct((n_tiles, SAMPLE_SEQS, D_MLSTM), F32),
        ],
        scratch_shapes=[
            pltpu.VMEM((D_MODEL, N_LB * LANES), BF16),
            pltpu.VMEM((D_MODEL, D_MODEL), BF16),
            pltpu.VMEM((tq, D_MODEL), BF16),
            pltpu.VMEM((N_LB, tq, LANES), F32),
            pltpu.VMEM((N_LB, tq, LANES), F32),
            pltpu.VMEM((len(POOL_WINDOWS), POOL_PAD + tq, POOL_GC), F32),
            pltpu.VMEM((tq, D_MODEL), BF16),
            pltpu.VMEM((NH, DH, 2 * DH), F32),
            pltpu.VMEM((NH, MLSTM_CHUNK), F32),
            pltpu.VMEM((n_items, MLSTM_CHUNK, MLSTM_CHUNK + DH), BF16),
            pltpu.VMEM((n_items, DH, MLSTM_CHUNK), BF16),
            pltpu.VMEM((2 * NH, MLSTM_CHUNK), F32),
        ],
        compiler_params=pltpu.CompilerParams(
            dimension_semantics=("arbitrary",),
            vmem_limit_bytes=V7X_VMEM_LIMIT),
    )(x, x, w_in_t, b_gate2d, w_pool, pool_scale, w_out, ln_g, ln_b, state_c, sq, skd, sv, siw)


def _ffn_kernel(x1p_ref, x1s_ref, w1_ref, b1_ref, w2_ref, b2_ref, g_ref, be_ref, yp_ref, ys_ref,
                pre_ref, *, n_prompt_tiles):
    def residual_plus_mlp(x1_ref):
        acc = None
        for c in range(D_FF // FF_CHUNK):
            cols = slice(c * FF_CHUNK, (c + 1) * FF_CHUNK)
            hid = jnp.maximum(_dot(x1_ref[...], w1_ref[:, cols]) + b1_ref[:, cols], 0.0)
            part = _dot(hid * hid, w2_ref[cols, :])
            acc = part if acc is None else acc + part
        return ALPHA * x1_ref[...] + (acc + b2_ref[...])

    step = pl.program_id(0)

    @pl.when(step == 0)
    def _no_tile_yet():
        pre_ref[...] = jnp.zeros(pre_ref.shape, F32)

    @pl.when(step < n_prompt_tiles)
    def _prompt():
        yp_ref[...] = _layer_norm(pre_ref[...], g_ref[...], be_ref[...])
        pre_ref[...] = residual_plus_mlp(x1p_ref)

    @pl.when(step == n_prompt_tiles)
    def _sample():
        yp_ref[...] = _layer_norm(pre_ref[...], g_ref[...], be_ref[...])
        ys_ref[:, 0, :] = _layer_norm(residual_plus_mlp(x1s_ref), g_ref[...], be_ref[...])


def _ffn(x1p, x1s, w1, b1, w2, b2, ln_g, ln_b):
    tile = FFN_TILE
    n_tok = x1p.shape[0]
    assert n_tok % tile == 0
    n_tiles = n_tok // tile
    const2 = lambda i: (0, 0)
    ptile = lambda i: (jnp.minimum(i, n_tiles - 1), 0)
    return pl.pallas_call(
        functools.partial(_ffn_kernel, n_prompt_tiles=n_tiles),
        name="ffn_ln2",
        grid=(n_tiles + 1,),
        in_specs=[
            pl.BlockSpec((tile, D_MODEL), ptile),
            pl.BlockSpec(x1s.shape, const2),
            pl.BlockSpec(w1.shape, const2, pipeline_mode=pl.Buffered(1)),
            pl.BlockSpec(b1.shape, const2),
            pl.BlockSpec(w2.shape, const2, pipeline_mode=pl.Buffered(1)),
            pl.BlockSpec(b2.shape, const2),
            pl.BlockSpec(ln_g.shape, const2),
            pl.BlockSpec(ln_b.shape, const2),
        ],
        out_specs=[pl.BlockSpec((tile, D_MODEL), lambda i: (jnp.maximum(i - 1, 0), 0)),
                   pl.BlockSpec((x1s.shape[0], 1, D_MODEL), lambda i: (0, 0, 0))],
        out_shape=[jax.ShapeDtypeStruct((n_tok, D_MODEL), F32),
                   jax.ShapeDtypeStruct((x1s.shape[0], 1, D_MODEL), F32)],
        scratch_shapes=[pltpu.VMEM((tile, D_MODEL), F32)],
        compiler_params=pltpu.CompilerParams(
            dimension_semantics=("arbitrary",),
            vmem_limit_bytes=V7X_VMEM_LIMIT),
    )(x1p, x1s, w1, b1, w2, b2, ln_g, ln_b)


def _sample_proj_kernel(x_ref, wt_ref, bg_ref, sp_ref, n_ref, m_ref, wp_ref, ps_ref,
                        ypool_ref, pool_ref, q_ref, kd_ref, v_ref, iw_ref, sv_ref, den_ref,
                        floor_ref, osig_ref, nout_ref, mout_ref):
    x = x_ref[:, 0, :]
    nb = x.shape[0]
    proj = _dot_nt(x, wt_ref[0:CUT_G, :])
    u = proj[:, 0:D_POOL]
    gates = _dot_nt(wt_ref[CUT_G:CUT_G + 2 * NH, :], x) + _gate_bias_rows(bg_ref, nb)
    ig_r = gates[0:NH]
    inter_r = jax.nn.log_sigmoid(gates[NH:2 * NH]) + m_ref[...]
    m_t_r = jnp.maximum(inter_r, ig_r)
    dw_r = jnp.exp(ig_r - m_t_r)
    iw_r = jnp.exp(inter_r - m_t_r)
    floor_r = jnp.exp(-m_t_r)
    mout_ref[...] = m_t_r

    def as_column(row):
        return jnp.broadcast_to(row, (DH, nb)).T

    for g, w in enumerate(POOL_WINDOWS):
        lo = g * POOL_GC
        u_g = u[:, lo:lo + POOL_GC]
        acc = u_g
        for r in range(POOL_PAD - w, POOL_BUF):
            acc = acc + sp_ref[r, :, lo:lo + POOL_GC]
        pooled = acc / float(min(PAST_LEN + 1, w)) - u_g
        ypool_ref[:, lo:lo + POOL_GC] = _dot(pooled, wp_ref[g]) * ps_ref[:, lo:lo + POOL_GC]
    pool_ref[0:POOL_BUF - 1] = sp_ref[1:POOL_BUF]
    pool_ref[POOL_BUF - 1] = u

    for h in range(NH):
        col = slice(h * DH, (h + 1) * DH)
        dw = as_column(dw_r[h:h + 1])
        iw = as_column(iw_r[h:h + 1])
        q = proj[:, D_POOL + h * DH:D_POOL + (h + 1) * DH]
        k = proj[:, CUT_K + h * DH:CUT_K + (h + 1) * DH] * K_SCALE
        v = proj[:, CUT_V + h * DH:CUT_V + (h + 1) * DH]
        o = proj[:, CUT_V + D_MLSTM + h * DH:CUT_V + D_MLSTM + (h + 1) * DH]
        n_old = n_ref[:, h, :]
        s = jnp.sum(q * k, axis=-1, keepdims=True) * dw
        q_ref[:, col] = q
        kd_ref[:, col] = dw * k
        v_ref[:, col] = v
        iw_ref[:, col] = iw
        sv_ref[:, col] = s * v
        den_ref[:, col] = s + iw * jnp.sum(q * n_old, axis=-1, keepdims=True)
        floor_ref[:, col] = as_column(floor_r[h:h + 1])
        osig_ref[:, col] = jax.nn.sigmoid(o)
        nout_ref[:, h, :] = iw * n_old + dw * k


def _sample_proj(x, w_in_t, b_gate2d, pool_rows, n_state, m_rows, w_pool, pool_scale):
    nb = x.shape[0]
    wide = jax.ShapeDtypeStruct((nb, D_MLSTM), F32)
    vmem = pl.BlockSpec(memory_space=pltpu.VMEM)
    return pl.pallas_call(
        _sample_proj_kernel,
        name="sample_proj",
        in_specs=[vmem, vmem, pl.BlockSpec(memory_space=pltpu.SMEM), vmem, vmem, vmem, vmem, vmem],
        out_shape=([wide, jax.ShapeDtypeStruct(pool_rows.shape, F32)] + [wide] * 8
                   + [jax.ShapeDtypeStruct(n_state.shape, F32), jax.ShapeDtypeStruct(m_rows.shape, F32)]),
        compiler_params=pltpu.CompilerParams(vmem_limit_bytes=V7X_VMEM_LIMIT),
    )(x, w_in_t, b_gate2d, pool_rows, n_state, m_rows, w_pool, pool_scale)


def _sample_out_kernel(x_ref, ypool_ref, sv_ref, iw_ref, inter_ref, den_ref, floor_ref, osig_ref,
                       wo_ref, g1_ref, b1_ref, x1_ref):
    hh = (sv_ref[...] + iw_ref[...] * inter_ref[...]) / jnp.maximum(jnp.abs(den_ref[...]), floor_ref[...])
    mixin = jnp.concatenate([ypool_ref[...], osig_ref[...] * hh], axis=-1)
    x1_ref[...] = _layer_norm(ALPHA * x_ref[:, 0, :] + _dot(mixin, wo_ref[...]), g1_ref[...], b1_ref[...])


def _sample_out(x, ypool, sv, iw, inter, den, floor, osig, w_out, ln_g, ln_b):
    return pl.pallas_call(
        _sample_out_kernel,
        name="sample_out",
        out_shape=jax.ShapeDtypeStruct((x.shape[0], D_MODEL), F32),
        compiler_params=pltpu.CompilerParams(vmem_limit_bytes=V7X_VMEM_LIMIT),
    )(x, ypool, sv, iw, inter, den, floor, osig, w_out, ln_g, ln_b)


def kernel(x_prompt, x_sample, state_pool, state_C, state_n, state_m, w_in, b_gate, w_pool, pool_scale,
           w_out, ln1_g, ln1_b, w_ff1, b_ff1, w_ff2, b_ff2, ln2_g, ln2_b):
    assert w_in.shape[0] == DEPTH == 1
    bp, t_len, _ = x_prompt.shape
    bs = x_sample.shape[0]
    assert x_sample.shape[1] == 1

    w_t = jnp.transpose(w_in[0])
    wp = w_pool[0]
    ps = pool_scale[0].reshape(1, D_POOL)
    wo = w_out[0]
    g1, b1 = ln1_g[0].reshape(1, D_MODEL), ln1_b[0].reshape(1, D_MODEL)
    g2, b2 = ln2_g[0].reshape(1, D_MODEL), ln2_b[0].reshape(1, D_MODEL)
    wf1, wf2 = w_ff1[0], w_ff2[0]
    bf1, bf2 = b_ff1[0].reshape(1, D_FF), b_ff2[0].reshape(1, D_MODEL)

    (ypool, pool_s, q_s, kd_s, v_s, iw_s, sv_s, den_s, floor_s, osig_s, n_s, m_s_rows) = _sample_proj(
        x_sample, w_t, b_gate, jnp.transpose(state_pool[0], (1, 0, 2)),
        state_n[0], jnp.transpose(state_m[0]), wp, ps)

    x1_p, pool_p, c_p, n_p, m_rep, c_s, inter_blk = _prompt_mixer(
        x_prompt, w_t, b_gate, wp, ps, wo, g1, b1, state_C, q_s, kd_s, v_s, iw_s)
    m_p = m_rep[:, :, 0].reshape(DEPTH, bp, NH)

    x1_s = _sample_out(x_sample, ypool, sv_s, iw_s, inter_blk.reshape(bs, D_MLSTM), den_s, floor_s, osig_s,
                       wo, g1, b1)

    y_p, y_s = _ffn(x1_p.reshape(bp * t_len, D_MODEL), x1_s, wf1, bf1, wf2, bf2, g2, b2)

    return (y_p.reshape(bp, t_len, D_MODEL), y_s,
            pool_p, c_p, n_p, m_p,
            jnp.transpose(pool_s, (1, 0, 2))[None], c_s,
            n_s[None], jnp.transpose(m_s_rows)[None])
```

```python
import functools

import jax
import jax.numpy as jnp
from jax import lax
from jax.experimental import pallas as pl
from jax.experimental.pallas import tpu as pltpu

F32 = jnp.float32
BF16 = jnp.bfloat16

D_MODEL = 1024
D_POOL = D_MODEL // 2
D_MLSTM = D_MODEL - D_POOL
POOL_WINDOWS = (2, 4, 8, 16)
POOL_GC = D_POOL // len(POOL_WINDOWS)
POOL_BUF = max(POOL_WINDOWS) - 1
POOL_PAD = POOL_BUF + 1
NH = 4
DH = D_MLSTM // NH
D_FF = 4 * D_MODEL
DEPTH = 1
PAST_LEN = 16384
ALPHA = (2.0 * DEPTH) ** 0.25
LN_EPS = 1e-5
K_SCALE = DH ** -0.5

MLSTM_CHUNK = 128
MIX_TILE = 512
FFN_TILE = 512
FF_CHUNK = 2048
SAMPLE_SEQS = 4
PROJ_COLS = 256
LANES = 128
assert POOL_GC == LANES and DH == LANES
CUT_K = D_POOL + D_MLSTM
CUT_V = CUT_K + D_MLSTM
CUT_G = D_POOL + 4 * D_MLSTM
LB_Q = D_POOL // LANES
LB_K = CUT_K // LANES
LB_V = CUT_V // LANES
LB_O = LB_V + NH
LB_G = CUT_G // LANES
N_LB = LB_G + 1
V7X_VMEM_LIMIT = 60 * 1024 * 1024


def _layer_norm(y, g, b):
    mu = jnp.mean(y, axis=-1, keepdims=True)
    yc = y - mu
    var = jnp.mean(yc * yc, axis=-1, keepdims=True)
    return yc * lax.rsqrt(var + LN_EPS) * g + b


def _dot(a, b):
    return jnp.dot(a, b, preferred_element_type=F32)


def _dot_nt(a, b):
    return lax.dot_general(a, b, (((1,), (1,)), ((), ())), preferred_element_type=F32)


def _twice(row):
    return jnp.concatenate([row, row], axis=-1)


def _gate_bias_rows(bg_ref, width):
    row = lax.broadcasted_iota(jnp.int32, (2 * NH, width), 0)
    bias = jnp.zeros((2 * NH, width), F32)
    for g in range(2 * NH):
        bias = jnp.where(row == g, bg_ref[0, g], bias)
    return bias


def _mixer_kernel(xn_ref, xc_ref, wt_ref, bg_ref, wp_ref, ps_ref, wo_ref, g1_ref, b1_ref,
                  sc_ref, sq_ref, skd_ref, sv_ref, siw_ref,
                  x1_ref, pool_ref, c_ref, n_ref, m_ref, scout_ref, sinter_ref,
                  win_ref, wob_ref, xb_ref, pm_a, pm_b, ext_ref, mix_ref, caug_ref, mst_ref,
                  lhs_ref, ktw_ref, gbias_ref, *, tiles_per_seq):
    tq = MIX_TILE
    L = MLSTM_CHUNK
    step = pl.program_id(0)
    t_idx = lax.rem(step + tiles_per_seq - 1, tiles_per_seq)
    last_t = tiles_per_seq - 1
    parity = lax.rem(step, 2)

    def project(pm_next):
        xb_ref[...] = xn_ref[0].astype(BF16)

        def piece(c0, c1):
            res = _dot(xb_ref[...], win_ref[:, c0:c1])
            for i in range((c1 - c0) // LANES):
                pm_next[c0 // LANES + i] = res[:, i * LANES:(i + 1) * LANES]

        n_cols = N_LB * LANES
        return [functools.partial(piece, c0, min(c0 + PROJ_COLS, n_cols))
                for c0 in range(0, n_cols, PROJ_COLS)]

    @pl.when(step == 0)
    def _first_step():
        for j in range(CUT_G // PROJ_COLS):
            win_ref[:, j * PROJ_COLS:(j + 1) * PROJ_COLS] = (
                wt_ref[j * PROJ_COLS:(j + 1) * PROJ_COLS, :].T.astype(BF16))
        gate_rows = jnp.concatenate([wt_ref[CUT_G:CUT_G + 2 * NH, :],
                                     jnp.zeros((LANES - 2 * NH, D_MODEL), F32)], axis=0)
        win_ref[:, CUT_G:CUT_G + LANES] = gate_rows.T.astype(BF16)
        wob_ref[...] = wo_ref[...].astype(BF16)
        gbias_ref[...] = _gate_bias_rows(bg_ref, L)
        for piece in project(pm_a):
            piece()

    @pl.when(t_idx == 0)
    def _init():
        ext_ref[:, 0:POOL_PAD, :] = jnp.zeros((len(POOL_WINDOWS), POOL_PAD, POOL_GC), F32)
        caug_ref[...] = jnp.zeros(caug_ref.shape, F32)
        mst_ref[...] = jnp.zeros(mst_ref.shape, F32)

    tt = lax.broadcasted_iota(jnp.int32, (L, L), 0)
    ss = lax.broadcasted_iota(jnp.int32, (L, L), 1)
    causal = ss <= tt
    diag = ss == tt

    def run(pm_next, pm_cur):
        pieces = project(pm_next)
        n_chunks = tq // L
        n_slots = len(POOL_WINDOWS) + 2 * n_chunks * NH
        slot_of = [(k * n_slots) // len(pieces) for k in range(len(pieces))]
        slot = [0]

        def next_slot():
            for k, piece in enumerate(pieces):
                if slot_of[k] == slot[0]:
                    piece()
            slot[0] += 1

        pos = t_idx * tq + lax.broadcasted_iota(jnp.int32, (tq, POOL_GC), 0)
        for g, w in enumerate(POOL_WINDOWS):
            lo = g * POOL_GC
            u_g = pm_cur[g]
            ext_ref[g, POOL_PAD:POOL_PAD + tq, :] = u_g
            s = ext_ref[g]
            k = 1
            while k < w:
                s = s + pltpu.roll(s, k, axis=0)
                k *= 2
            cnt = jnp.minimum(pos + 1, w).astype(F32)
            pooled = s[POOL_PAD:, :] / cnt - u_g
            mix_ref[:, lo:lo + POOL_GC] = (_dot(pooled, wp_ref[g]) * ps_ref[:, lo:lo + POOL_GC]).astype(BF16)
            ext_ref[g, 0:POOL_PAD, :] = ext_ref[g, tq:tq + POOL_PAD, :]
            next_slot()

        lane = lax.broadcasted_iota(jnp.int32, (NH, L), 1)
        gate_bias = gbias_ref[...]
        m_prev = mst_ref[...]
        chunk_rows = []
        for c in range(n_chunks):
            rows = slice(c * L, (c + 1) * L)
            gates = pm_cur[LB_G, rows, :].T[0:2 * NH, :] + gate_bias
            lf = jax.nn.log_sigmoid(gates[NH:2 * NH])
            b = lf
            k = 1
            while k < L:
                b = b + jnp.where(lane >= k, pltpu.roll(b, k, axis=1), 0.0)
                k *= 2
            a = gates[0:NH] - b
            cmax = a
            k = 1
            while k < L:
                cmax = jnp.maximum(cmax, jnp.where(lane >= k, pltpu.roll(cmax, k, axis=1), -jnp.inf))
                k *= 2
            amax = jnp.max(a, axis=-1, keepdims=True)
            big_m = jnp.maximum(cmax, m_prev)
            mm = jnp.maximum(m_prev, amax)
            chunk_rows.append(dict(
                a=a, big_m=big_m,
                iw=jnp.exp(m_prev - big_m),
                floor=jnp.exp(-(b + big_m)),
                w_loc=jnp.exp(a - amax),
                g_state=jnp.exp(m_prev - mm),
                f_state=jnp.exp(amax - mm)))
            m_prev = jnp.sum(lf, axis=-1, keepdims=True) + mm
        mst_ref[...] = m_prev

        def as_column(row):
            return jnp.broadcast_to(row, (L, L)).T

        pair = (step - 1) // 2
        upper_half = lax.rem(step - 1, 2) == 1
        pad = jnp.zeros((DH - SAMPLE_SEQS, DH), F32)

        def step_rows(ref):
            rows8 = ref[pl.ds(pl.multiple_of(pair * 2 * SAMPLE_SEQS, 2 * SAMPLE_SEQS), 2 * SAMPLE_SEQS), :]
            return jnp.where(upper_half, rows8[SAMPLE_SEQS:], rows8[:SAMPLE_SEQS])

        s_q, s_kd, s_v, s_iw = (step_rows(r) for r in (sq_ref, skd_ref, sv_ref, siw_ref))

        def sample_item(h, j):
            col = slice(h * DH, (h + 1) * DH)
            c_old = sc_ref[0, j, h]
            sinter_ref[0, j:j + 1, col] = jnp.sum(sample_q_t[h][:, j:j + 1] * c_old, axis=0, keepdims=True)
            scout_ref[0, j, h] = s_iw[j:j + 1, col] * c_old + sample_kd_t[h][:, j:j + 1] * s_v[j:j + 1, col]

        sample_q_t = [jnp.concatenate([s_q[:, h * DH:(h + 1) * DH], pad], axis=0).T for h in range(NH)]
        sample_kd_t = [jnp.concatenate([s_kd[:, h * DH:(h + 1) * DH], pad], axis=0).T for h in range(NH)]
        sample_items = [functools.partial(sample_item, h, j) for h in range(NH) for j in range(SAMPLE_SEQS)]

        for c in range(n_chunks):
            rows = slice(c * L, (c + 1) * L)
            cr = chunk_rows[c]
            for h in range(NH):
                i = c * NH + h
                q = pm_cur[LB_Q + h, rows, :]
                qb = q.astype(BF16)
                kt = pm_cur[LB_K + h, rows, :].T * K_SCALE
                p = jnp.exp(jnp.where(causal, cr["a"][h:h + 1], -jnp.inf) - as_column(cr["big_m"][h:h + 1]))
                lhs_ref[i, :, 0:L] = (_dot(qb, kt.astype(BF16)) * p).astype(BF16)
                ktw_ref[i] = (kt * cr["w_loc"][h:h + 1]).astype(BF16)
                lhs_ref[i, :, L:L + DH] = (q * as_column(cr["iw"][h:h + 1])).astype(BF16)
                if i % 2 == 0:
                    sample_items[i // 2]()
                next_slot()

        ones_blk = jnp.ones((L, DH), BF16)
        for c in range(n_chunks):
            rows = slice(c * L, (c + 1) * L)
            cr = chunk_rows[c]
            for h in range(NH):
                i = c * NH + h
                caug = caug_ref[h]
                v_aug = jnp.concatenate([pm_cur[LB_V + h, rows, :].astype(BF16), ones_blk], axis=1)
                comb = _dot(lhs_ref[i], jnp.concatenate([v_aug, caug.astype(BF16)], axis=0))
                hh = comb[:, :DH] / jnp.maximum(jnp.abs(comb[:, DH:]), as_column(cr["floor"][h:h + 1]))
                o = pm_cur[LB_O + h, rows, :]
                mix_ref[rows, D_POOL + h * DH:D_POOL + (h + 1) * DH] = (
                    jax.nn.sigmoid(o) * hh).astype(BF16)
                caug_ref[h] = (_twice(cr["g_state"][h:h + 1]) * caug
                               + _twice(cr["f_state"][h:h + 1]) * _dot(ktw_ref[i], v_aug))
                if i % 2 == 0:
                    sample_items[n_chunks * NH // 2 + i // 2]()
                next_slot()

        mix = _dot(mix_ref[...], wob_ref[...])
        x1_ref[0] = _layer_norm(ALPHA * xc_ref[0] + mix, g1_ref[...], b1_ref[...])

    @pl.when((parity == 0) & (step > 0))
    def _even():
        run(pm_a, pm_b)

    @pl.when(parity == 1)
    def _odd():
        run(pm_b, pm_a)

    @pl.when((t_idx == last_t) & (step > 0))
    def _final_state():
        for g in range(len(POOL_WINDOWS)):
            pool_ref[0, 0, :, g * POOL_GC:(g + 1) * POOL_GC] = ext_ref[g, tq + 1:tq + POOL_PAD, :]
        for h in range(NH):
            caug = caug_ref[h]
            c_ref[0, 0, h] = caug[:, :DH]
            n_ref[0, 0, h:h + 1, :] = jnp.sum(jnp.where(diag, caug[:, DH:], 0.0),
                                              axis=0, keepdims=True)
        m_ref[0] = mst_ref[...]


def _prompt_mixer(x, w_in_t, b_gate2d, w_pool, pool_scale, w_out, ln_g, ln_b, state_c, sq, skd, sv, siw):
    bsz, t_len, _ = x.shape
    tq = MIX_TILE
    assert t_len % tq == 0 and tq % MLSTM_CHUNK == 0 and tq >= POOL_PAD
    nt = t_len // tq
    n_tiles = bsz * nt
    n_seq = sq.shape[0]
    assert n_seq == n_tiles * SAMPLE_SEQS and SAMPLE_SEQS * NH <= (tq // MLSTM_CHUNK) * NH
    n_items = (tq // MLSTM_CHUNK) * NH
    const2 = lambda i: (0, 0)
    nxt = lambda i: jnp.minimum(i, n_tiles - 1)
    cur = lambda i: jnp.maximum(i - 1, 0)
    return pl.pallas_call(
        functools.partial(_mixer_kernel, tiles_per_seq=nt),
        name="prompt_mixer",
        grid=(n_tiles + 1,),
        in_specs=[
            pl.BlockSpec((1, tq, D_MODEL), lambda i: (nxt(i) // nt, nxt(i) % nt, 0)),
            pl.BlockSpec((1, tq, D_MODEL), lambda i: (cur(i) // nt, cur(i) % nt, 0)),
            pl.BlockSpec(w_in_t.shape, const2, pipeline_mode=pl.Buffered(1)),
            pl.BlockSpec(memory_space=pltpu.SMEM),
            pl.BlockSpec(w_pool.shape, lambda i: (0, 0, 0)),
            pl.BlockSpec(pool_scale.shape, const2),
            pl.BlockSpec(w_out.shape, const2, pipeline_mode=pl.Buffered(1)),
            pl.BlockSpec(ln_g.shape, const2),
            pl.BlockSpec(ln_b.shape, const2),
            pl.BlockSpec((1, SAMPLE_SEQS, NH, DH, DH), lambda i: (0, cur(i), 0, 0, 0)),
            pl.BlockSpec(sq.shape, const2, pipeline_mode=pl.Buffered(1)),
            pl.BlockSpec(skd.shape, const2, pipeline_mode=pl.Buffered(1)),
            pl.BlockSpec(sv.shape, const2, pipeline_mode=pl.Buffered(1)),
            pl.BlockSpec(siw.shape, const2, pipeline_mode=pl.Buffered(1)),
        ],
        out_specs=[
            pl.BlockSpec((1, tq, D_MODEL), lambda i: (cur(i) // nt, cur(i) % nt, 0)),
            pl.BlockSpec((1, 1, POOL_BUF, D_POOL), lambda i: (0, cur(i) // nt, 0, 0)),
            pl.BlockSpec((1, 1, NH, DH, DH), lambda i: (0, cur(i) // nt, 0, 0, 0)),
            pl.BlockSpec((1, 1, NH, DH), lambda i: (0, cur(i) // nt, 0, 0)),
            pl.BlockSpec((1, NH, MLSTM_CHUNK), lambda i: (cur(i) // nt, 0, 0)),
            pl.BlockSpec((1, SAMPLE_SEQS, NH, DH, DH), lambda i: (0, cur(i), 0, 0, 0)),
            pl.BlockSpec((1, SAMPLE_SEQS, D_MLSTM), lambda i: (cur(i), 0, 0)),
        ],
        out_shape=[
            jax.ShapeDtypeStruct((bsz, t_len, D_MODEL), F32),
            jax.ShapeDtypeStruct((DEPTH, bsz, POOL_BUF, D_POOL), F32),
            jax.ShapeDtypeStruct((DEPTH, bsz, NH, DH, DH), F32),
            jax.ShapeDtypeStruct((DEPTH, bsz, NH, DH), F32),
            jax.ShapeDtypeStruct((bsz, NH, MLSTM_CHUNK), F32),
            jax.ShapeDtypeStruct(state_c.shape, F32),
            jax.ShapeDtypeStruct((n_tiles, SAMPLE_SEQS, D_MLSTM), F32),
        ],
        scratch_shapes=[
            pltpu.VMEM((D_MODEL, N_LB * LANES), BF16),
            pltpu.VMEM((D_MODEL, D_MODEL), BF16),
            pltpu.VMEM((tq, D_MODEL), BF16),
            pltpu.VMEM((N_LB, tq, LANES), F32),
            pltpu.VMEM((N_LB, tq, LANES), F32),
            pltpu.VMEM((len(POOL_WINDOWS), POOL_PAD + tq, POOL_GC), F32),
            pltpu.VMEM((tq, D_MODEL), BF16),
            pltpu.VMEM((NH, DH, 2 * DH), F32),
            pltpu.VMEM((NH, MLSTM_CHUNK), F32),
            pltpu.VMEM((n_items, MLSTM_CHUNK, MLSTM_CHUNK + DH), BF16),
            pltpu.VMEM((n_items, DH, MLSTM_CHUNK), BF16),
            pltpu.VMEM((2 * NH, MLSTM_CHUNK), F32),
        ],
        compiler_params=pltpu.CompilerParams(
            dimension_semantics=("arbitrary",),
            vmem_limit_bytes=V7X_VMEM_LIMIT),
    )(x, x, w_in_t, b_gate2d, w_pool, pool_scale, w_out, ln_g, ln_b, state_c, sq, skd, sv, siw)


def _ffn_kernel(x1p_ref, x1s_ref, w1_ref, b1_ref, w2_ref, b2_ref, g_ref, be_ref, yp_ref, ys_ref,
                pre_ref, *, n_prompt_tiles):
    def residual_plus_mlp(x1_ref):
        acc = None
        for c in range(D_FF // FF_CHUNK):
            cols = slice(c * FF_CHUNK, (c + 1) * FF_CHUNK)
            hid = jnp.maximum(_dot(x1_ref[...], w1_ref[:, cols]) + b1_ref[:, cols], 0.0)
            part = _dot(hid * hid, w2_ref[cols, :])
            acc = part if acc is None else acc + part
        return ALPHA * x1_ref[...] + (acc + b2_ref[...])

    step = pl.program_id(0)

    @pl.when(step == 0)
    def _no_tile_yet():
        pre_ref[...] = jnp.zeros(pre_ref.shape, F32)

    @pl.when(step < n_prompt_tiles)
    def _prompt():
        yp_ref[...] = _layer_norm(pre_ref[...], g_ref[...], be_ref[...])
        pre_ref[...] = residual_plus_mlp(x1p_ref)

    @pl.when(step == n_prompt_tiles)
    def _sample():
        yp_ref[...] = _layer_norm(pre_ref[...], g_ref[...], be_ref[...])
        ys_ref[:, 0, :] = _layer_norm(residual_plus_mlp(x1s_ref), g_ref[...], be_ref[...])


def _ffn(x1p, x1s, w1, b1, w2, b2, ln_g, ln_b):
    tile = FFN_TILE
    n_tok = x1p.shape[0]
    assert n_tok % tile == 0
    n_tiles = n_tok // tile
    const2 = lambda i: (0, 0)
    ptile = lambda i: (jnp.minimum(i, n_tiles - 1), 0)
    return pl.pallas_call(
        functools.partial(_ffn_kernel, n_prompt_tiles=n_tiles),
        name="ffn_ln2",
        grid=(n_tiles + 1,),
        in_specs=[
            pl.BlockSpec((tile, D_MODEL), ptile),
            pl.BlockSpec(x1s.shape, const2),
            pl.BlockSpec(w1.shape, const2, pipeline_mode=pl.Buffered(1)),
            pl.BlockSpec(b1.shape, const2),
            pl.BlockSpec(w2.shape, const2, pipeline_mode=pl.Buffered(1)),
            pl.BlockSpec(b2.shape, const2),
            pl.BlockSpec(ln_g.shape, const2),
            pl.BlockSpec(ln_b.shape, const2),
        ],
        out_specs=[pl.BlockSpec((tile, D_MODEL), lambda i: (jnp.maximum(i - 1, 0), 0)),
                   pl.BlockSpec((x1s.shape[0], 1, D_MODEL), lambda i: (0, 0, 0))],
        out_shape=[jax.ShapeDtypeStruct((n_tok, D_MODEL), F32),
                   jax.ShapeDtypeStruct((x1s.shape[0], 1, D_MODEL), F32)],
        scratch_shapes=[pltpu.VMEM((tile, D_MODEL), F32)],
        compiler_params=pltpu.CompilerParams(
            dimension_semantics=("arbitrary",),
            vmem_limit_bytes=V7X_VMEM_LIMIT),
    )(x1p, x1s, w1, b1, w2, b2, ln_g, ln_b)


def _sample_proj_kernel(x_ref, wt_ref, bg_ref, sp_ref, n_ref, m_ref, wp_ref, ps_ref,
                        ypool_ref, pool_ref, q_ref, kd_ref, v_ref, iw_ref, sv_ref, den_ref,
                        floor_ref, osig_ref, nout_ref, mout_ref):
    x = x_ref[:, 0, :]
    nb = x.shape[0]
    proj = _dot_nt(x, wt_ref[0:CUT_G, :])
    u = proj[:, 0:D_POOL]
    gates = _dot_nt(wt_ref[CUT_G:CUT_G + 2 * NH, :], x) + _gate_bias_rows(bg_ref, nb)
    ig_r = gates[0:NH]
    inter_r = jax.nn.log_sigmoid(gates[NH:2 * NH]) + m_ref[...]
    m_t_r = jnp.maximum(inter_r, ig_r)
    dw_r = jnp.exp(ig_r - m_t_r)
    iw_r = jnp.exp(inter_r - m_t_r)
    floor_r = jnp.exp(-m_t_r)
    mout_ref[...] = m_t_r

    def as_column(row):
        return jnp.broadcast_to(row, (DH, nb)).T

    for g, w in enumerate(POOL_WINDOWS):
        lo = g * POOL_GC
        u_g = u[:, lo:lo + POOL_GC]
        acc = u_g
        for r in range(POOL_PAD - w, POOL_BUF):
            acc = acc + sp_ref[r, :, lo:lo + POOL_GC]
        pooled = acc / float(min(PAST_LEN + 1, w)) - u_g
        ypool_ref[:, lo:lo + POOL_GC] = _dot(pooled, wp_ref[g]) * ps_ref[:, lo:lo + POOL_GC]
    pool_ref[0:POOL_BUF - 1] = sp_ref[1:POOL_BUF]
    pool_ref[POOL_BUF - 1] = u

    for h in range(NH):
        col = slice(h * DH, (h + 1) * DH)
        dw = as_column(dw_r[h:h + 1])
        iw = as_column(iw_r[h:h + 1])
        q = proj[:, D_POOL + h * DH:D_POOL + (h + 1) * DH]
        k = proj[:, CUT_K + h * DH:CUT_K + (h + 1) * DH] * K_SCALE
        v = proj[:, CUT_V + h * DH:CUT_V + (h + 1) * DH]
        o = proj[:, CUT_V + D_MLSTM + h * DH:CUT_V + D_MLSTM + (h + 1) * DH]
        n_old = n_ref[:, h, :]
        s = jnp.sum(q * k, axis=-1, keepdims=True) * dw
        q_ref[:, col] = q
        kd_ref[:, col] = dw * k
        v_ref[:, col] = v
        iw_ref[:, col] = iw
        sv_ref[:, col] = s * v
        den_ref[:, col] = s + iw * jnp.sum(q * n_old, axis=-1, keepdims=True)
        floor_ref[:, col] = as_column(floor_r[h:h + 1])
        osig_ref[:, col] = jax.nn.sigmoid(o)
        nout_ref[:, h, :] = iw * n_old + dw * k


def _sample_proj(x, w_in_t, b_gate2d, pool_rows, n_state, m_rows, w_pool, pool_scale):
    nb = x.shape[0]
    wide = jax.ShapeDtypeStruct((nb, D_MLSTM), F32)
    vmem = pl.BlockSpec(memory_space=pltpu.VMEM)
    return pl.pallas_call(
        _sample_proj_kernel,
        name="sample_proj",
        in_specs=[vmem, vmem, pl.BlockSpec(memory_space=pltpu.SMEM), vmem, vmem, vmem, vmem, vmem],
        out_shape=([wide, jax.ShapeDtypeStruct(pool_rows.shape, F32)] + [wide] * 8
                   + [jax.ShapeDtypeStruct(n_state.shape, F32), jax.ShapeDtypeStruct(m_rows.shape, F32)]),
        compiler_params=pltpu.CompilerParams(vmem_limit_bytes=V7X_VMEM_LIMIT),
    )(x, w_in_t, b_gate2d, pool_rows, n_state, m_rows, w_pool, pool_scale)


def _sample_out_kernel(x_ref, ypool_ref, sv_ref, iw_ref, inter_ref, den_ref, floor_ref, osig_ref,
                       wo_ref, g1_ref, b1_ref, x1_ref):
    hh = (sv_ref[...] + iw_ref[...] * inter_ref[...]) / jnp.maximum(jnp.abs(den_ref[...]), floor_ref[...])
    mixin = jnp.concatenate([ypool_ref[...], osig_ref[...] * hh], axis=-1)
    x1_ref[...] = _layer_norm(ALPHA * x_ref[:, 0, :] + _dot(mixin, wo_ref[...]), g1_ref[...], b1_ref[...])


def _sample_out(x, ypool, sv, iw, inter, den, floor, osig, w_out, ln_g, ln_b):
    return pl.pallas_call(
        _sample_out_kernel,
        name="sample_out",
        out_shape=jax.ShapeDtypeStruct((x.shape[0], D_MODEL), F32),
        compiler_params=pltpu.CompilerParams(vmem_limit_bytes=V7X_VMEM_LIMIT),
    )(x, ypool, sv, iw, inter, den, floor, osig, w_out, ln_g, ln_b)


def kernel(x_prompt, x_sample, state_pool, state_C, state_n, state_m, w_in, b_gate, w_pool, pool_scale,
           w_out, ln1_g, ln1_b, w_ff1, b_ff1, w_ff2, b_ff2, ln2_g, ln2_b):
    assert w_in.shape[0] == DEPTH == 1
    bp, t_len, _ = x_prompt.shape
    bs = x_sample.shape[0]
    assert x_sample.shape[1] == 1

    w_t = jnp.transpose(w_in[0])
    wp = w_pool[0]
    ps = pool_scale[0].reshape(1, D_POOL)
    wo = w_out[0]
    g1, b1 = ln1_g[0].reshape(1, D_MODEL), ln1_b[0].reshape(1, D_MODEL)
    g2, b2 = ln2_g[0].reshape(1, D_MODEL), ln2_b[0].reshape(1, D_MODEL)
    wf1, wf2 = w_ff1[0], w_ff2[0]
    bf1, bf2 = b_ff1[0].reshape(1, D_FF), b_ff2[0].reshape(1, D_MODEL)

    (ypool, pool_s, q_s, kd_s, v_s, iw_s, sv_s, den_s, floor_s, osig_s, n_s, m_s_rows) = _sample_proj(
        x_sample, w_t, b_gate, jnp.transpose(state_pool[0], (1, 0, 2)),
        state_n[0], jnp.transpose(state_m[0]), wp, ps)

    x1_p, pool_p, c_p, n_p, m_rep, c_s, inter_blk = _prompt_mixer(
        x_prompt, w_t, b_gate, wp, ps, wo, g1, b1, state_C, q_s, kd_s, v_s, iw_s)
    m_p = m_rep[:, :, 0].reshape(DEPTH, bp, NH)

    x1_s = _sample_out(x_sample, ypool, sv_s, iw_s, inter_blk.reshape(bs, D_MLSTM), den_s, floor_s, osig_s,
                       wo, g1, b1)

    y_p, y_s = _ffn(x1_p.reshape(bp * t_len, D_MODEL), x1_s, wf1, bf1, wf2, bf2, g2, b2)

    return (y_p.reshape(bp, t_len, D_MODEL), y_s,
            pool_p, c_p, n_p, m_p,
            jnp.transpose(pool_s, (1, 0, 2))[None], c_s,
            n_s[None], jnp.transpose(m_s_rows)[None])
```

```python
import functools

import jax
import jax.numpy as jnp
from jax import lax
from jax.experimental import pallas as pl
from jax.experimental.pallas import tpu as pltpu

F32 = jnp.float32
BF16 = jnp.bfloat16

D_MODEL = 1024
D_POOL = D_MODEL // 2
D_MLSTM = D_MODEL - D_POOL
POOL_WINDOWS = (2, 4, 8, 16)
POOL_GC = D_POOL // len(POOL_WINDOWS)
POOL_BUF = max(POOL_WINDOWS) - 1
POOL_PAD = POOL_BUF + 1
NH = 4
DH = D_MLSTM // NH
D_FF = 4 * D_MODEL
DEPTH = 1
PAST_LEN = 16384
ALPHA = (2.0 * DEPTH) ** 0.25
LN_EPS = 1e-5
K_SCALE = DH ** -0.5

MLSTM_CHUNK = 128
MIX_TILE = 512
FFN_TILE = 512
FF_CHUNK = 2048
SAMPLE_SEQS = 4
PROJ_COLS = 256
LANES = 128
assert POOL_GC == LANES and DH == LANES
CUT_K = D_POOL + D_MLSTM
CUT_V = CUT_K + D_MLSTM
CUT_G = D_POOL + 4 * D_MLSTM
LB_Q = D_POOL // LANES
LB_K = CUT_K // LANES
LB_V = CUT_V // LANES
LB_O = LB_V + NH
LB_G = CUT_G // LANES
N_LB = LB_G + 1
V7X_VMEM_LIMIT = 60 * 1024 * 1024


def _layer_norm(y, g, b):
    mu = jnp.mean(y, axis=-1, keepdims=True)
    yc = y - mu
    var = jnp.mean(yc * yc, axis=-1, keepdims=True)
    return yc * lax.rsqrt(var + LN_EPS) * g + b


def _dot(a, b):
    return jnp.dot(a, b, preferred_element_type=F32)


def _dot_nt(a, b):
    return lax.dot_general(a, b, (((1,), (1,)), ((), ())), preferred_element_type=F32)


def _twice(row):
    return jnp.concatenate([row, row], axis=-1)


def _gate_bias_rows(bg_ref, width):
    row = lax.broadcasted_iota(jnp.int32, (2 * NH, width), 0)
    bias = jnp.zeros((2 * NH, width), F32)
    for g in range(2 * NH):
        bias = jnp.where(row == g, bg_ref[0, g], bias)
    return bias


def _mixer_kernel(xn_ref, xc_ref, wt_ref, bg_ref, wp_ref, ps_ref, wo_ref, g1_ref, b1_ref,
                  sc_ref, sq_ref, skd_ref, sv_ref, siw_ref,
                  x1_ref, pool_ref, c_ref, n_ref, m_ref, scout_ref, sinter_ref,
                  win_ref, wob_ref, xb_ref, pm_a, pm_b, ext_ref, mix_ref, caug_ref, mst_ref,
                  lhs_ref, ktw_ref, gbias_ref, *, tiles_per_seq):
    tq = MIX_TILE
    L = MLSTM_CHUNK
    step = pl.program_id(0)
    t_idx = lax.rem(step + tiles_per_seq - 1, tiles_per_seq)
    last_t = tiles_per_seq - 1
    parity = lax.rem(step, 2)

    def project(pm_next):
        xb_ref[...] = xn_ref[0].astype(BF16)

        def piece(c0, c1):
            res = _dot(xb_ref[...], win_ref[:, c0:c1])
            for i in range((c1 - c0) // LANES):
                pm_next[c0 // LANES + i] = res[:, i * LANES:(i + 1) * LANES]

        n_cols = N_LB * LANES
        return [functools.partial(piece, c0, min(c0 + PROJ_COLS, n_cols))
                for c0 in range(0, n_cols, PROJ_COLS)]

    @pl.when(step == 0)
    def _first_step():
        for j in range(CUT_G // PROJ_COLS):
            win_ref[:, j * PROJ_COLS:(j + 1) * PROJ_COLS] = (
                wt_ref[j * PROJ_COLS:(j + 1) * PROJ_COLS, :].T.astype(BF16))
        gate_rows = jnp.concatenate([wt_ref[CUT_G:CUT_G + 2 * NH, :],
                                     jnp.zeros((LANES - 2 * NH, D_MODEL), F32)], axis=0)
        win_ref[:, CUT_G:CUT_G + LANES] = gate_rows.T.astype(BF16)
        wob_ref[...] = wo_ref[...].astype(BF16)
        gbias_ref[...] = _gate_bias_rows(bg_ref, L)
        for piece in project(pm_a):
            piece()

    @pl.when(t_idx == 0)
    def _init():
        ext_ref[:, 0:POOL_PAD, :] = jnp.zeros((len(POOL_WINDOWS), POOL_PAD, POOL_GC), F32)
        caug_ref[...] = jnp.zeros(caug_ref.shape, F32)
        mst_ref[...] = jnp.zeros(mst_ref.shape, F32)

    tt = lax.broadcasted_iota(jnp.int32, (L, L), 0)
    ss = lax.broadcasted_iota(jnp.int32, (L, L), 1)
    causal = ss <= tt
    diag = ss == tt

    def run(pm_next, pm_cur, sample_half):
        pieces = project(pm_next)
        n_chunks = tq // L
        n_slots = len(POOL_WINDOWS) + 2 * n_chunks * NH
        slot_of = [(k * n_slots) // len(pieces) for k in range(len(pieces))]
        slot = [0]

        def next_slot():
            for k, piece in enumerate(pieces):
                if slot_of[k] == slot[0]:
                    piece()
            slot[0] += 1

        pos = t_idx * tq + lax.broadcasted_iota(jnp.int32, (tq, POOL_GC), 0)
        for g, w in enumerate(POOL_WINDOWS):
            lo = g * POOL_GC
            u_g = pm_cur[g]
            ext_ref[g, POOL_PAD:POOL_PAD + tq, :] = u_g
            s = ext_ref[g]
            k = 1
            while k < w:
                s = s + pltpu.roll(s, k, axis=0)
                k *= 2
            cnt = jnp.minimum(pos + 1, w).astype(F32)
            pooled = s[POOL_PAD:, :] / cnt - u_g
            mix_ref[:, lo:lo + POOL_GC] = (_dot(pooled, wp_ref[g]) * ps_ref[:, lo:lo + POOL_GC]).astype(BF16)
            ext_ref[g, 0:POOL_PAD, :] = ext_ref[g, tq:tq + POOL_PAD, :]
            next_slot()

        lane = lax.broadcasted_iota(jnp.int32, (NH, L), 1)
        gate_bias = gbias_ref[...]
        m_prev = mst_ref[...]
        chunk_rows = []
        for c in range(n_chunks):
            rows = slice(c * L, (c + 1) * L)
            gates = pm_cur[LB_G, rows, :].T[0:2 * NH, :] + gate_bias
            lf = jax.nn.log_sigmoid(gates[NH:2 * NH])
            b = lf
            k = 1
            while k < L:
                b = b + jnp.where(lane >= k, pltpu.roll(b, k, axis=1), 0.0)
                k *= 2
            a = gates[0:NH] - b
            cmax = a
            k = 1
            while k < L:
                cmax = jnp.maximum(cmax, jnp.where(lane >= k, pltpu.roll(cmax, k, axis=1), -jnp.inf))
                k *= 2
            amax = jnp.max(a, axis=-1, keepdims=True)
            big_m = jnp.maximum(cmax, m_prev)
            mm = jnp.maximum(m_prev, amax)
            chunk_rows.append(dict(
                a=a, big_m=big_m,
                iw=jnp.exp(m_prev - big_m),
                floor=jnp.exp(-(b + big_m)),
                w_loc=jnp.exp(a - amax),
                g_state=jnp.exp(m_prev - mm),
                f_state=jnp.exp(amax - mm)))
            m_prev = jnp.sum(lf, axis=-1, keepdims=True) + mm
        mst_ref[...] = m_prev

        def as_column(row):
            return jnp.broadcast_to(row, (L, L)).T

        pair = (step - 1) // 2
        half_rows = slice(sample_half * SAMPLE_SEQS, (sample_half + 1) * SAMPLE_SEQS)
        pad = jnp.zeros((DH - SAMPLE_SEQS, DH), F32)

        def step_rows(ref):
            rows8 = ref[pl.ds(pl.multiple_of(pair * 2 * SAMPLE_SEQS, 2 * SAMPLE_SEQS), 2 * SAMPLE_SEQS), :]
            return rows8[half_rows]

        s_q, s_kd, s_v, s_iw = (step_rows(r) for r in (sq_ref, skd_ref, sv_ref, siw_ref))

        def sample_item(h, j):
            col = slice(h * DH, (h + 1) * DH)
            c_old = sc_ref[0, j, h]
            row = sample_half * SAMPLE_SEQS + j
            sinter_ref[0, row:row + 1, col] = jnp.sum(sample_q_t[h][:, j:j + 1] * c_old, axis=0, keepdims=True)
            scout_ref[0, j, h] = s_iw[j:j + 1, col] * c_old + sample_kd_t[h][:, j:j + 1] * s_v[j:j + 1, col]

        sample_q_t = [jnp.concatenate([s_q[:, h * DH:(h + 1) * DH], pad], axis=0).T for h in range(NH)]
        sample_kd_t = [jnp.concatenate([s_kd[:, h * DH:(h + 1) * DH], pad], axis=0).T for h in range(NH)]
        sample_items = [functools.partial(sample_item, h, j) for h in range(NH) for j in range(SAMPLE_SEQS)]

        for c in range(n_chunks):
            rows = slice(c * L, (c + 1) * L)
            cr = chunk_rows[c]
            for h in range(NH):
                i = c * NH + h
                q = pm_cur[LB_Q + h, rows, :]
                qb = q.astype(BF16)
                kt = pm_cur[LB_K + h, rows, :].T * K_SCALE
                p = jnp.exp(jnp.where(causal, cr["a"][h:h + 1], -jnp.inf) - as_column(cr["big_m"][h:h + 1]))
                lhs_ref[i, :, 0:L] = (_dot(qb, kt.astype(BF16)) * p).astype(BF16)
                ktw_ref[i] = (kt * cr["w_loc"][h:h + 1]).astype(BF16)
                lhs_ref[i, :, L:L + DH] = (q * as_column(cr["iw"][h:h + 1])).astype(BF16)
                if i % 2 == 0:
                    sample_items[i // 2]()
                next_slot()

        ones_blk = jnp.ones((L, DH), BF16)
        for c in range(n_chunks):
            rows = slice(c * L, (c + 1) * L)
            cr = chunk_rows[c]
            for h in range(NH):
                i = c * NH + h
                caug = caug_ref[h]
                v_aug = jnp.concatenate([pm_cur[LB_V + h, rows, :].astype(BF16), ones_blk], axis=1)
                comb = _dot(lhs_ref[i], jnp.concatenate([v_aug, caug.astype(BF16)], axis=0))
                hh = comb[:, :DH] / jnp.maximum(jnp.abs(comb[:, DH:]), as_column(cr["floor"][h:h + 1]))
                o = pm_cur[LB_O + h, rows, :]
                mix_ref[rows, D_POOL + h * DH:D_POOL + (h + 1) * DH] = (
                    jax.nn.sigmoid(o) * hh).astype(BF16)
                caug_ref[h] = (_twice(cr["g_state"][h:h + 1]) * caug
                               + _twice(cr["f_state"][h:h + 1]) * _dot(ktw_ref[i], v_aug))
                if i % 2 == 0:
                    sample_items[n_chunks * NH // 2 + i // 2]()
                next_slot()

        mix = _dot(mix_ref[...], wob_ref[...])
        x1_ref[0] = _layer_norm(ALPHA * xc_ref[0] + mix, g1_ref[...], b1_ref[...])

    @pl.when((parity == 0) & (step > 0))
    def _even():
        run(pm_a, pm_b, 1)

    @pl.when(parity == 1)
    def _odd():
        run(pm_b, pm_a, 0)

    @pl.when((t_idx == last_t) & (step > 0))
    def _final_state():
        for g in range(len(POOL_WINDOWS)):
            pool_ref[0, 0, :, g * POOL_GC:(g + 1) * POOL_GC] = ext_ref[g, tq + 1:tq + POOL_PAD, :]
        for h in range(NH):
            caug = caug_ref[h]
            c_ref[0, 0, h] = caug[:, :DH]
            n_ref[0, 0, h:h + 1, :] = jnp.sum(jnp.where(diag, caug[:, DH:], 0.0),
                                              axis=0, keepdims=True)
        m_ref[0] = mst_ref[...]


def _prompt_mixer(x, w_in_t, b_gate2d, w_pool, pool_scale, w_out, ln_g, ln_b, state_c, sq, skd, sv, siw):
    bsz, t_len, _ = x.shape
    tq = MIX_TILE
    assert t_len % tq == 0 and tq % MLSTM_CHUNK == 0 and tq >= POOL_PAD
    nt = t_len // tq
    n_tiles = bsz * nt
    n_seq = sq.shape[0]
    assert n_seq == n_tiles * SAMPLE_SEQS and n_tiles % 2 == 0
    assert SAMPLE_SEQS * NH == (tq // MLSTM_CHUNK) * NH
    n_items = (tq // MLSTM_CHUNK) * NH
    const2 = lambda i: (0, 0)
    nxt = lambda i: jnp.minimum(i, n_tiles - 1)
    cur = lambda i: jnp.maximum(i - 1, 0)
    return pl.pallas_call(
        functools.partial(_mixer_kernel, tiles_per_seq=nt),
        name="prompt_mixer",
        grid=(n_tiles + 1,),
        in_specs=[
            pl.BlockSpec((1, tq, D_MODEL), lambda i: (nxt(i) // nt, nxt(i) % nt, 0)),
            pl.BlockSpec((1, tq, D_MODEL), lambda i: (cur(i) // nt, cur(i) % nt, 0)),
            pl.BlockSpec(w_in_t.shape, const2, pipeline_mode=pl.Buffered(1)),
            pl.BlockSpec(memory_space=pltpu.SMEM),
            pl.BlockSpec(w_pool.shape, lambda i: (0, 0, 0)),
            pl.BlockSpec(pool_scale.shape, const2),
            pl.BlockSpec(w_out.shape, const2, pipeline_mode=pl.Buffered(1)),
            pl.BlockSpec(ln_g.shape, const2),
            pl.BlockSpec(ln_b.shape, const2),
            pl.BlockSpec((1, SAMPLE_SEQS, NH, DH, DH), lambda i: (0, cur(i), 0, 0, 0)),
            pl.BlockSpec(sq.shape, const2, pipeline_mode=pl.Buffered(1)),
            pl.BlockSpec(skd.shape, const2, pipeline_mode=pl.Buffered(1)),
            pl.BlockSpec(sv.shape, const2, pipeline_mode=pl.Buffered(1)),
            pl.BlockSpec(siw.shape, const2, pipeline_mode=pl.Buffered(1)),
        ],
        out_specs=[
            pl.BlockSpec((1, tq, D_MODEL), lambda i: (cur(i) // nt, cur(i) % nt, 0)),
            pl.BlockSpec((1, 1, POOL_BUF, D_POOL), lambda i: (0, cur(i) // nt, 0, 0)),
            pl.BlockSpec((1, 1, NH, DH, DH), lambda i: (0, cur(i) // nt, 0, 0, 0)),
            pl.BlockSpec((1, 1, NH, DH), lambda i: (0, cur(i) // nt, 0, 0)),
            pl.BlockSpec((1, NH, MLSTM_CHUNK), lambda i: (cur(i) // nt, 0, 0)),
            pl.BlockSpec((1, SAMPLE_SEQS, NH, DH, DH), lambda i: (0, cur(i), 0, 0, 0)),
            pl.BlockSpec((1, 2 * SAMPLE_SEQS, D_MLSTM), lambda i: (cur(i) // 2, 0, 0)),
        ],
        out_shape=[
            jax.ShapeDtypeStruct((bsz, t_len, D_MODEL), F32),
            jax.ShapeDtypeStruct((DEPTH, bsz, POOL_BUF, D_POOL), F32),
            jax.ShapeDtypeStruct((DEPTH, bsz, NH, DH, DH), F32),
            jax.ShapeDtypeStruct((DEPTH, bsz, NH, DH), F32),
            jax.ShapeDtypeStruct((bsz, NH, MLSTM_CHUNK), F32),
            jax.ShapeDtypeStruct(state_c.shape, F32),
            jax.ShapeDtypeStruct((n_tiles // 2, 2 * SAMPLE_SEQS, D_MLSTM), F32),
        ],
        scratch_shapes=[
            pltpu.VMEM((D_MODEL, N_LB * LANES), BF16),
            pltpu.VMEM((D_MODEL, D_MODEL), BF16),
            pltpu.VMEM((tq, D_MODEL), BF16),
            pltpu.VMEM((N_LB, tq, LANES), F32),
            pltpu.VMEM((N_LB, tq, LANES), F32),
            pltpu.VMEM((len(POOL_WINDOWS), POOL_PAD + tq, POOL_GC), F32),
            pltpu.VMEM((tq, D_MODEL), BF16),
            pltpu.VMEM((NH, DH, 2 * DH), F32),
            pltpu.VMEM((NH, MLSTM_CHUNK), F32),
            pltpu.VMEM((n_items, MLSTM_CHUNK, MLSTM_CHUNK + DH), BF16),
            pltpu.VMEM((n_items, DH, MLSTM_CHUNK), BF16),
            pltpu.VMEM((2 * NH, MLSTM_CHUNK), F32),
        ],
        compiler_params=pltpu.CompilerParams(
            dimension_semantics=("arbitrary",),
            vmem_limit_bytes=V7X_VMEM_LIMIT),
    )(x, x, w_in_t, b_gate2d, w_pool, pool_scale, w_out, ln_g, ln_b, state_c, sq, skd, sv, siw)


def _ffn_kernel(x1p_ref, x1s_ref, w1_ref, b1_ref, w2_ref, b2_ref, g_ref, be_ref, yp_ref, ys_ref,
                pre_ref, *, n_prompt_tiles):
    def residual_plus_mlp(x1_ref):
        acc = None
        for c in range(D_FF // FF_CHUNK):
            cols = slice(c * FF_CHUNK, (c + 1) * FF_CHUNK)
            hid = jnp.maximum(_dot(x1_ref[...], w1_ref[:, cols]) + b1_ref[:, cols], 0.0)
            part = _dot(hid * hid, w2_ref[cols, :])
            acc = part if acc is None else acc + part
        return ALPHA * x1_ref[...] + (acc + b2_ref[...])

    step = pl.program_id(0)

    @pl.when(step == 0)
    def _no_tile_yet():
        pre_ref[...] = jnp.zeros(pre_ref.shape, F32)

    @pl.when(step < n_prompt_tiles)
    def _prompt():
        yp_ref[...] = _layer_norm(pre_ref[...], g_ref[...], be_ref[...])
        pre_ref[...] = residual_plus_mlp(x1p_ref)

    @pl.when(step == n_prompt_tiles)
    def _sample():
        yp_ref[...] = _layer_norm(pre_ref[...], g_ref[...], be_ref[...])
        ys_ref[:, 0, :] = _layer_norm(residual_plus_mlp(x1s_ref), g_ref[...], be_ref[...])


def _ffn(x1p, x1s, w1, b1, w2, b2, ln_g, ln_b):
    tile = FFN_TILE
    n_tok = x1p.shape[0]
    assert n_tok % tile == 0
    n_tiles = n_tok // tile
    const2 = lambda i: (0, 0)
    ptile = lambda i: (jnp.minimum(i, n_tiles - 1), 0)
    return pl.pallas_call(
        functools.partial(_ffn_kernel, n_prompt_tiles=n_tiles),
        name="ffn_ln2",
        grid=(n_tiles + 1,),
        in_specs=[
            pl.BlockSpec((tile, D_MODEL), ptile),
            pl.BlockSpec(x1s.shape, const2),
            pl.BlockSpec(w1.shape, const2, pipeline_mode=pl.Buffered(1)),
            pl.BlockSpec(b1.shape, const2),
            pl.BlockSpec(w2.shape, const2, pipeline_mode=pl.Buffered(1)),
            pl.BlockSpec(b2.shape, const2),
            pl.BlockSpec(ln_g.shape, const2),
            pl.BlockSpec(ln_b.shape, const2),
        ],
        out_specs=[pl.BlockSpec((tile, D_MODEL), lambda i: (jnp.maximum(i - 1, 0), 0)),
                   pl.BlockSpec((x1s.shape[0], 1, D_MODEL), lambda i: (0, 0, 0))],
        out_shape=[jax.ShapeDtypeStruct((n_tok, D_MODEL), F32),
                   jax.ShapeDtypeStruct((x1s.shape[0], 1, D_MODEL), F32)],
        scratch_shapes=[pltpu.VMEM((tile, D_MODEL), F32)],
        compiler_params=pltpu.CompilerParams(
            dimension_semantics=("arbitrary",),
            vmem_limit_bytes=V7X_VMEM_LIMIT),
    )(x1p, x1s, w1, b1, w2, b2, ln_g, ln_b)


def _sample_proj_kernel(x_ref, wt_ref, bg_ref, sp_ref, n_ref, m_ref, wp_ref, ps_ref,
                        ypool_ref, pool_ref, q_ref, kd_ref, v_ref, iw_ref, sv_ref, den_ref,
                        floor_ref, osig_ref, nout_ref, mout_ref):
    x = x_ref[:, 0, :]
    nb = x.shape[0]
    proj = _dot_nt(x, wt_ref[0:CUT_G, :])
    u = proj[:, 0:D_POOL]
    gates = _dot_nt(wt_ref[CUT_G:CUT_G + 2 * NH, :], x) + _gate_bias_rows(bg_ref, nb)
    ig_r = gates[0:NH]
    inter_r = jax.nn.log_sigmoid(gates[NH:2 * NH]) + m_ref[...]
    m_t_r = jnp.maximum(inter_r, ig_r)
    dw_r = jnp.exp(ig_r - m_t_r)
    iw_r = jnp.exp(inter_r - m_t_r)
    floor_r = jnp.exp(-m_t_r)
    mout_ref[...] = m_t_r

    def as_column(row):
        return jnp.broadcast_to(row, (DH, nb)).T

    for g, w in enumerate(POOL_WINDOWS):
        lo = g * POOL_GC
        u_g = u[:, lo:lo + POOL_GC]
        acc = u_g
        for r in range(POOL_PAD - w, POOL_BUF):
            acc = acc + sp_ref[r, :, lo:lo + POOL_GC]
        pooled = acc / float(min(PAST_LEN + 1, w)) - u_g
        ypool_ref[:, lo:lo + POOL_GC] = _dot(pooled, wp_ref[g]) * ps_ref[:, lo:lo + POOL_GC]
    pool_ref[0:POOL_BUF - 1] = sp_ref[1:POOL_BUF]
    pool_ref[POOL_BUF - 1] = u

    for h in range(NH):
        col = slice(h * DH, (h + 1) * DH)
        dw = as_column(dw_r[h:h + 1])
        iw = as_column(iw_r[h:h + 1])
        q = proj[:, D_POOL + h * DH:D_POOL + (h + 1) * DH]
        k = proj[:, CUT_K + h * DH:CUT_K + (h + 1) * DH] * K_SCALE
        v = proj[:, CUT_V + h * DH:CUT_V + (h + 1) * DH]
        o = proj[:, CUT_V + D_MLSTM + h * DH:CUT_V + D_MLSTM + (h + 1) * DH]
        n_old = n_ref[:, h, :]
        s = jnp.sum(q * k, axis=-1, keepdims=True) * dw
        q_ref[:, col] = q
        kd_ref[:, col] = dw * k
        v_ref[:, col] = v
        iw_ref[:, col] = iw
        sv_ref[:, col] = s * v
        den_ref[:, col] = s + iw * jnp.sum(q * n_old, axis=-1, keepdims=True)
        floor_ref[:, col] = as_column(floor_r[h:h + 1])
        osig_ref[:, col] = jax.nn.sigmoid(o)
        nout_ref[:, h, :] = iw * n_old + dw * k


def _sample_proj(x, w_in_t, b_gate2d, pool_rows, n_state, m_rows, w_pool, pool_scale):
    nb = x.shape[0]
    wide = jax.ShapeDtypeStruct((nb, D_MLSTM), F32)
    vmem = pl.BlockSpec(memory_space=pltpu.VMEM)
    return pl.pallas_call(
        _sample_proj_kernel,
        name="sample_proj",
        in_specs=[vmem, vmem, pl.BlockSpec(memory_space=pltpu.SMEM), vmem, vmem, vmem, vmem, vmem],
        out_shape=([wide, jax.ShapeDtypeStruct(pool_rows.shape, F32)] + [wide] * 8
                   + [jax.ShapeDtypeStruct(n_state.shape, F32), jax.ShapeDtypeStruct(m_rows.shape, F32)]),
        compiler_params=pltpu.CompilerParams(vmem_limit_bytes=V7X_VMEM_LIMIT),
    )(x, w_in_t, b_gate2d, pool_rows, n_state, m_rows, w_pool, pool_scale)


def _sample_out_kernel(x_ref, ypool_ref, sv_ref, iw_ref, inter_ref, den_ref, floor_ref, osig_ref,
                       wo_ref, g1_ref, b1_ref, x1_ref):
    hh = (sv_ref[...] + iw_ref[...] * inter_ref[...]) / jnp.maximum(jnp.abs(den_ref[...]), floor_ref[...])
    mixin = jnp.concatenate([ypool_ref[...], osig_ref[...] * hh], axis=-1)
    x1_ref[...] = _layer_norm(ALPHA * x_ref[:, 0, :] + _dot(mixin, wo_ref[...]), g1_ref[...], b1_ref[...])


def _sample_out(x, ypool, sv, iw, inter, den, floor, osig, w_out, ln_g, ln_b):
    return pl.pallas_call(
        _sample_out_kernel,
        name="sample_out",
        out_shape=jax.ShapeDtypeStruct((x.shape[0], D_MODEL), F32),
        compiler_params=pltpu.CompilerParams(vmem_limit_bytes=V7X_VMEM_LIMIT),
    )(x, ypool, sv, iw, inter, den, floor, osig, w_out, ln_g, ln_b)


def kernel(x_prompt, x_sample, state_pool, state_C, state_n, state_m, w_in, b_gate, w_pool, pool_scale,
           w_out, ln1_g, ln1_b, w_ff1, b_ff1, w_ff2, b_ff2, ln2_g, ln2_b):
    assert w_in.shape[0] == DEPTH == 1
    bp, t_len, _ = x_prompt.shape
    bs = x_sample.shape[0]
    assert x_sample.shape[1] == 1

    w_t = jnp.transpose(w_in[0])
    wp = w_pool[0]
    ps = pool_scale[0].reshape(1, D_POOL)
    wo = w_out[0]
    g1, b1 = ln1_g[0].reshape(1, D_MODEL), ln1_b[0].reshape(1, D_MODEL)
    g2, b2 = ln2_g[0].reshape(1, D_MODEL), ln2_b[0].reshape(1, D_MODEL)
    wf1, wf2 = w_ff1[0], w_ff2[0]
    bf1, bf2 = b_ff1[0].reshape(1, D_FF), b_ff2[0].reshape(1, D_MODEL)

    (ypool, pool_s, q_s, kd_s, v_s, iw_s, sv_s, den_s, floor_s, osig_s, n_s, m_s_rows) = _sample_proj(
        x_sample, w_t, b_gate, jnp.transpose(state_pool[0], (1, 0, 2)),
        state_n[0], jnp.transpose(state_m[0]), wp, ps)

    x1_p, pool_p, c_p, n_p, m_rep, c_s, inter_blk = _prompt_mixer(
        x_prompt, w_t, b_gate, wp, ps, wo, g1, b1, state_C, q_s, kd_s, v_s, iw_s)
    m_p = m_rep[:, :, 0].reshape(DEPTH, bp, NH)

    x1_s = _sample_out(x_sample, ypool, sv_s, iw_s, inter_blk.reshape(bs, D_MLSTM), den_s, floor_s, osig_s,
                       wo, g1, b1)

    y_p, y_s = _ffn(x1_p.reshape(bp * t_len, D_MODEL), x1_s, wf1, bf1, wf2, bf2, g2, b2)

    return (y_p.reshape(bp, t_len, D_MODEL), y_s,
            pool_p, c_p, n_p, m_p,
            jnp.transpose(pool_s, (1, 0, 2))[None], c_s,
            n_s[None], jnp.transpose(m_s_rows)[None])
```

```python
import functools

import jax
import jax.numpy as jnp
from jax import lax
from jax.experimental import pallas as pl
from jax.experimental.pallas import tpu as pltpu

F32 = jnp.float32
BF16 = jnp.bfloat16

D_MODEL = 1024
D_POOL = D_MODEL // 2
D_MLSTM = D_MODEL - D_POOL
POOL_WINDOWS = (2, 4, 8, 16)
POOL_GC = D_POOL // len(POOL_WINDOWS)
POOL_BUF = max(POOL_WINDOWS) - 1
POOL_PAD = POOL_BUF + 1
NH = 4
DH = D_MLSTM // NH
D_FF = 4 * D_MODEL
DEPTH = 1
PAST_LEN = 16384
ALPHA = (2.0 * DEPTH) ** 0.25
LN_EPS = 1e-5
K_SCALE = DH ** -0.5

MLSTM_CHUNK = 128
MIX_TILE = 512
FFN_TILE = 512
FF_CHUNK = 2048
SAMPLE_SEQS = 4
PROJ_COLS = 256
LANES = 128
assert POOL_GC == LANES and DH == LANES
CUT_K = D_POOL + D_MLSTM
CUT_V = CUT_K + D_MLSTM
CUT_G = D_POOL + 4 * D_MLSTM
LB_Q = D_POOL // LANES
LB_K = CUT_K // LANES
LB_V = CUT_V // LANES
LB_O = LB_V + NH
LB_G = CUT_G // LANES
N_LB = LB_G + 1
V7X_VMEM_LIMIT = 60 * 1024 * 1024


def _layer_norm(y, g, b):
    mu = jnp.mean(y, axis=-1, keepdims=True)
    yc = y - mu
    var = jnp.mean(yc * yc, axis=-1, keepdims=True)
    return yc * lax.rsqrt(var + LN_EPS) * g + b


def _dot(a, b):
    return jnp.dot(a, b, preferred_element_type=F32)


def _dot_nt(a, b):
    return lax.dot_general(a, b, (((1,), (1,)), ((), ())), preferred_element_type=F32)


def _twice(row):
    return jnp.concatenate([row, row], axis=-1)


def _gate_bias_rows(bg_ref, width):
    row = lax.broadcasted_iota(jnp.int32, (2 * NH, width), 0)
    bias = jnp.zeros((2 * NH, width), F32)
    for g in range(2 * NH):
        bias = jnp.where(row == g, bg_ref[0, g], bias)
    return bias


def _mixer_kernel(xn_ref, xc_ref, wt_ref, bg_ref, wp_ref, ps_ref, wo_ref, g1_ref, b1_ref,
                  sc_ref, sq_ref, skd_ref, sv_ref, siw_ref,
                  x1_ref, pool_ref, c_ref, n_ref, m_ref, scout_ref, sinter_ref,
                  win_ref, wob_ref, xb_ref, pm_a, pm_b, ext_ref, mix_ref, caug_ref, mst_ref,
                  lhs_ref, ktw_ref, gbias_ref, *, tiles_per_seq):
    tq = MIX_TILE
    L = MLSTM_CHUNK
    step = pl.program_id(0)
    t_idx = lax.rem(step + tiles_per_seq - 1, tiles_per_seq)
    last_t = tiles_per_seq - 1
    parity = lax.rem(step, 2)

    def project(pm_next):
        xb_ref[...] = xn_ref[0].astype(BF16)

        def piece(c0, c1):
            res = _dot(xb_ref[...], win_ref[:, c0:c1])
            for i in range((c1 - c0) // LANES):
                pm_next[c0 // LANES + i] = res[:, i * LANES:(i + 1) * LANES]

        n_cols = N_LB * LANES
        return [functools.partial(piece, c0, min(c0 + PROJ_COLS, n_cols))
                for c0 in range(0, n_cols, PROJ_COLS)]

    @pl.when(step == 0)
    def _first_step():
        for j in range(CUT_G // PROJ_COLS):
            win_ref[:, j * PROJ_COLS:(j + 1) * PROJ_COLS] = (
                wt_ref[j * PROJ_COLS:(j + 1) * PROJ_COLS, :].T.astype(BF16))
        gate_rows = jnp.concatenate([wt_ref[CUT_G:CUT_G + 2 * NH, :],
                                     jnp.zeros((LANES - 2 * NH, D_MODEL), F32)], axis=0)
        win_ref[:, CUT_G:CUT_G + LANES] = gate_rows.T.astype(BF16)
        wob_ref[...] = wo_ref[...].astype(BF16)
        gbias_ref[...] = _gate_bias_rows(bg_ref, L)
        for piece in project(pm_a):
            piece()

    @pl.when(t_idx == 0)
    def _init():
        ext_ref[:, 0:POOL_PAD, :] = jnp.zeros((len(POOL_WINDOWS), POOL_PAD, POOL_GC), F32)
        caug_ref[...] = jnp.zeros(caug_ref.shape, F32)
        mst_ref[...] = jnp.zeros(mst_ref.shape, F32)

    tt = lax.broadcasted_iota(jnp.int32, (L, L), 0)
    ss = lax.broadcasted_iota(jnp.int32, (L, L), 1)
    causal = ss <= tt
    diag = ss == tt

    def run(pm_next, pm_cur, sample_half):
        pieces = project(pm_next)
        n_chunks = tq // L
        n_slots = len(POOL_WINDOWS) + 2 * n_chunks * NH
        slot_of = [(k * n_slots) // len(pieces) for k in range(len(pieces))]
        slot = [0]

        def next_slot():
            for k, piece in enumerate(pieces):
                if slot_of[k] == slot[0]:
                    piece()
            slot[0] += 1

        pos = t_idx * tq + lax.broadcasted_iota(jnp.int32, (tq, POOL_GC), 0)
        for g, w in enumerate(POOL_WINDOWS):
            lo = g * POOL_GC
            u_g = pm_cur[g]
            ext_ref[g, POOL_PAD:POOL_PAD + tq, :] = u_g
            s = ext_ref[g]
            k = 1
            while k < w:
                s = s + pltpu.roll(s, k, axis=0)
                k *= 2
            cnt = jnp.minimum(pos + 1, w).astype(F32)
            pooled = s[POOL_PAD:, :] / cnt - u_g
            mix_ref[:, lo:lo + POOL_GC] = (_dot(pooled, wp_ref[g]) * ps_ref[:, lo:lo + POOL_GC]).astype(BF16)
            ext_ref[g, 0:POOL_PAD, :] = ext_ref[g, tq:tq + POOL_PAD, :]
            next_slot()

        lane = lax.broadcasted_iota(jnp.int32, (NH, L), 1)
        gate_bias = gbias_ref[...]
        m_prev = mst_ref[...]
        chunk_rows = []
        for c in range(n_chunks):
            rows = slice(c * L, (c + 1) * L)
            gates = pm_cur[LB_G, rows, :].T[0:2 * NH, :] + gate_bias
            lf = jax.nn.log_sigmoid(gates[NH:2 * NH])
            b = lf
            k = 1
            while k < L:
                b = b + jnp.where(lane >= k, pltpu.roll(b, k, axis=1), 0.0)
                k *= 2
            a = gates[0:NH] - b
            cmax = a
            k = 1
            while k < L:
                cmax = jnp.maximum(cmax, jnp.where(lane >= k, pltpu.roll(cmax, k, axis=1), -jnp.inf))
                k *= 2
            amax = jnp.max(a, axis=-1, keepdims=True)
            big_m = jnp.maximum(cmax, m_prev)
            mm = jnp.maximum(m_prev, amax)
            chunk_rows.append(dict(
                a=a, big_m=big_m,
                iw=jnp.exp(m_prev - big_m),
                floor=jnp.exp(-(b + big_m)),
                w_loc=jnp.exp(a - amax),
                g_state=jnp.exp(m_prev - mm),
                f_state=jnp.exp(amax - mm)))
            m_prev = jnp.sum(lf, axis=-1, keepdims=True) + mm
        mst_ref[...] = m_prev

        def as_column(row):
            return jnp.broadcast_to(row, (L, L)).T

        pair = (step - 1) // 2
        half_rows = slice(sample_half * SAMPLE_SEQS, (sample_half + 1) * SAMPLE_SEQS)
        pad = jnp.zeros((DH - SAMPLE_SEQS, DH), F32)

        def step_rows(ref):
            rows8 = ref[pl.ds(pl.multiple_of(pair * 2 * SAMPLE_SEQS, 2 * SAMPLE_SEQS), 2 * SAMPLE_SEQS), :]
            return rows8[half_rows]

        s_q, s_kd, s_v, s_iw = (step_rows(r) for r in (sq_ref, skd_ref, sv_ref, siw_ref))

        def sample_item(h, j):
            col = slice(h * DH, (h + 1) * DH)
            c_old = sc_ref[0, j, h]
            row = sample_half * SAMPLE_SEQS + j
            sinter_ref[0, row:row + 1, col] = jnp.sum(sample_q_t[h][:, j:j + 1] * c_old, axis=0, keepdims=True)
            scout_ref[0, j, h] = s_iw[j:j + 1, col] * c_old + sample_kd_t[h][:, j:j + 1] * s_v[j:j + 1, col]

        sample_q_t = [jnp.concatenate([s_q[:, h * DH:(h + 1) * DH], pad], axis=0).T for h in range(NH)]
        sample_kd_t = [jnp.concatenate([s_kd[:, h * DH:(h + 1) * DH], pad], axis=0).T for h in range(NH)]
        sample_items = [functools.partial(sample_item, h, j) for h in range(NH) for j in range(SAMPLE_SEQS)]

        for c in range(n_chunks):
            rows = slice(c * L, (c + 1) * L)
            cr = chunk_rows[c]
            for h in range(NH):
                i = c * NH + h
                q = pm_cur[LB_Q + h, rows, :]
                qb = q.astype(BF16)
                kt = pm_cur[LB_K + h, rows, :].T * K_SCALE
                p = jnp.exp(jnp.where(causal, cr["a"][h:h + 1], -jnp.inf) - as_column(cr["big_m"][h:h + 1]))
                lhs_ref[i, :, 0:L] = (_dot(qb, kt.astype(BF16)) * p).astype(BF16)
                ktw_ref[i] = (kt * cr["w_loc"][h:h + 1]).astype(BF16)
                lhs_ref[i, :, L:L + DH] = (q * as_column(cr["iw"][h:h + 1])).astype(BF16)
                if i % 2 == 0:
                    sample_items[i // 2]()
                next_slot()

        ones_blk = jnp.ones((L, DH), BF16)
        for c in range(n_chunks):
            rows = slice(c * L, (c + 1) * L)
            cr = chunk_rows[c]
            for h in range(NH):
                i = c * NH + h
                caug = caug_ref[h]
                v_aug = jnp.concatenate([pm_cur[LB_V + h, rows, :].astype(BF16), ones_blk], axis=1)
                comb = _dot(lhs_ref[i], jnp.concatenate([v_aug, caug.astype(BF16)], axis=0))
                hh = comb[:, :DH] / jnp.maximum(jnp.abs(comb[:, DH:]), as_column(cr["floor"][h:h + 1]))
                o = pm_cur[LB_O + h, rows, :]
                mix_ref[rows, D_POOL + h * DH:D_POOL + (h + 1) * DH] = (
                    jax.nn.sigmoid(o) * hh).astype(BF16)
                caug_ref[h] = (_twice(cr["g_state"][h:h + 1]) * caug
                               + _twice(cr["f_state"][h:h + 1]) * _dot(ktw_ref[i], v_aug))
                if i % 2 == 0:
                    sample_items[n_chunks * NH // 2 + i // 2]()
                next_slot()

        mix = _dot(mix_ref[...], wob_ref[...])
        x1_ref[0] = _layer_norm(ALPHA * xc_ref[0] + mix, g1_ref[...], b1_ref[...])

    @pl.when((parity == 0) & (step > 0))
    def _even():
        run(pm_a, pm_b, 1)

    @pl.when(parity == 1)
    def _odd():
        run(pm_b, pm_a, 0)

    @pl.when((t_idx == last_t) & (step > 0))
    def _final_state():
        for g in range(len(POOL_WINDOWS)):
            pool_ref[0, 0, :, g * POOL_GC:(g + 1) * POOL_GC] = ext_ref[g, tq + 1:tq + POOL_PAD, :]
        for h in range(NH):
            caug = caug_ref[h]
            c_ref[0, 0, h] = caug[:, :DH]
            n_ref[0, 0, h:h + 1, :] = jnp.sum(jnp.where(diag, caug[:, DH:], 0.0),
                                              axis=0, keepdims=True)
        m_ref[0] = mst_ref[...]


def _prompt_mixer(x, w_in_t, b_gate2d, w_pool, pool_scale, w_out, ln_g, ln_b, state_c, sq, skd, sv, siw):
    bsz, t_len, _ = x.shape
    tq = MIX_TILE
    assert t_len % tq == 0 and tq % MLSTM_CHUNK == 0 and tq >= POOL_PAD
    nt = t_len // tq
    n_tiles = bsz * nt
    n_seq = sq.shape[0]
    assert n_seq == n_tiles * SAMPLE_SEQS and n_tiles % 2 == 0
    assert SAMPLE_SEQS * NH == (tq // MLSTM_CHUNK) * NH
    n_items = (tq // MLSTM_CHUNK) * NH
    const2 = lambda i: (0, 0)
    nxt = lambda i: jnp.minimum(i, n_tiles - 1)
    cur = lambda i: jnp.maximum(i - 1, 0)
    return pl.pallas_call(
        functools.partial(_mixer_kernel, tiles_per_seq=nt),
        name="prompt_mixer",
        grid=(n_tiles + 1,),
        in_specs=[
            pl.BlockSpec((1, tq, D_MODEL), lambda i: (nxt(i) // nt, nxt(i) % nt, 0)),
            pl.BlockSpec((1, tq, D_MODEL), lambda i: (cur(i) // nt, cur(i) % nt, 0)),
            pl.BlockSpec(w_in_t.shape, const2, pipeline_mode=pl.Buffered(1)),
            pl.BlockSpec(memory_space=pltpu.SMEM),
            pl.BlockSpec(w_pool.shape, lambda i: (0, 0, 0)),
            pl.BlockSpec(pool_scale.shape, const2),
            pl.BlockSpec(w_out.shape, const2, pipeline_mode=pl.Buffered(1)),
            pl.BlockSpec(ln_g.shape, const2),
            pl.BlockSpec(ln_b.shape, const2),
            pl.BlockSpec((1, SAMPLE_SEQS, NH, DH, DH), lambda i: (0, cur(i), 0, 0, 0)),
            pl.BlockSpec(sq.shape, const2, pipeline_mode=pl.Buffered(1)),
            pl.BlockSpec(skd.shape, const2, pipeline_mode=pl.Buffered(1)),
            pl.BlockSpec(sv.shape, const2, pipeline_mode=pl.Buffered(1)),
            pl.BlockSpec(siw.shape, const2, pipeline_mode=pl.Buffered(1)),
        ],
        out_specs=[
            pl.BlockSpec((1, tq, D_MODEL), lambda i: (cur(i) // nt, cur(i) % nt, 0)),
            pl.BlockSpec((1, 1, POOL_BUF, D_POOL), lambda i: (0, cur(i) // nt, 0, 0)),
            pl.BlockSpec((1, 1, NH, DH, DH), lambda i: (0, cur(i) // nt, 0, 0, 0)),
            pl.BlockSpec((1, 1, NH, DH), lambda i: (0, cur(i) // nt, 0, 0)),
            pl.BlockSpec((1, NH, MLSTM_CHUNK), lambda i: (cur(i) // nt, 0, 0)),
            pl.BlockSpec((1, SAMPLE_SEQS, NH, DH, DH), lambda i: (0, cur(i), 0, 0, 0)),
            pl.BlockSpec((1, 2 * SAMPLE_SEQS, D_MLSTM), lambda i: (cur(i) // 2, 0, 0)),
        ],
        out_shape=[
            jax.ShapeDtypeStruct((bsz, t_len, D_MODEL), F32),
            jax.ShapeDtypeStruct((DEPTH, bsz, POOL_BUF, D_POOL), F32),
            jax.ShapeDtypeStruct((DEPTH, bsz, NH, DH, DH), F32),
            jax.ShapeDtypeStruct((DEPTH, bsz, NH, DH), F32),
            jax.ShapeDtypeStruct((bsz, NH, MLSTM_CHUNK), F32),
            jax.ShapeDtypeStruct(state_c.shape, F32),
            jax.ShapeDtypeStruct((n_tiles // 2, 2 * SAMPLE_SEQS, D_MLSTM), F32),
        ],
        scratch_shapes=[
            pltpu.VMEM((D_MODEL, N_LB * LANES), BF16),
            pltpu.VMEM((D_MODEL, D_MODEL), BF16),
            pltpu.VMEM((tq, D_MODEL), BF16),
            pltpu.VMEM((N_LB, tq, LANES), F32),
            pltpu.VMEM((N_LB, tq, LANES), F32),
            pltpu.VMEM((len(POOL_WINDOWS), POOL_PAD + tq, POOL_GC), F32),
            pltpu.VMEM((tq, D_MODEL), BF16),
            pltpu.VMEM((NH, DH, 2 * DH), F32),
            pltpu.VMEM((NH, MLSTM_CHUNK), F32),
            pltpu.VMEM((n_items, MLSTM_CHUNK, MLSTM_CHUNK + DH), BF16),
            pltpu.VMEM((n_items, DH, MLSTM_CHUNK), BF16),
            pltpu.VMEM((2 * NH, MLSTM_CHUNK), F32),
        ],
        compiler_params=pltpu.CompilerParams(
            dimension_semantics=("arbitrary",),
            vmem_limit_bytes=V7X_VMEM_LIMIT),
    )(x, x, w_in_t, b_gate2d, w_pool, pool_scale, w_out, ln_g, ln_b, state_c, sq, skd, sv, siw)


def _ffn_kernel(x1p_ref, w1_ref, b1_ref, w2_ref, b2_ref, g_ref, be_ref,
                xs_ref, ypool_ref, sv_ref, siw_ref, sinter_ref, den_ref, floor_ref, osig_ref,
                wo_ref, g1_ref, bb1_ref,
                yp_ref, ys_ref, pre_ref, x1s_ref, *, n_prompt_tiles):
    def residual_plus_mlp(x1_ref):
        acc = None
        for c in range(D_FF // FF_CHUNK):
            cols = slice(c * FF_CHUNK, (c + 1) * FF_CHUNK)
            hid = jnp.maximum(_dot(x1_ref[...], w1_ref[:, cols]) + b1_ref[:, cols], 0.0)
            part = _dot(hid * hid, w2_ref[cols, :])
            acc = part if acc is None else acc + part
        return ALPHA * x1_ref[...] + (acc + b2_ref[...])

    step = pl.program_id(0)

    @pl.when(step == 0)
    def _no_tile_yet():
        pre_ref[...] = jnp.zeros(pre_ref.shape, F32)
        hh = (sv_ref[...] + siw_ref[...] * sinter_ref[...]) / jnp.maximum(jnp.abs(den_ref[...]), floor_ref[...])
        mixin = jnp.concatenate([ypool_ref[...], osig_ref[...] * hh], axis=-1)
        x1s_ref[...] = _layer_norm(ALPHA * xs_ref[:, 0, :] + _dot(mixin, wo_ref[...]),
                                   g1_ref[...], bb1_ref[...])

    @pl.when(step < n_prompt_tiles)
    def _prompt():
        yp_ref[...] = _layer_norm(pre_ref[...], g_ref[...], be_ref[...])
        pre_ref[...] = residual_plus_mlp(x1p_ref)

    @pl.when(step == n_prompt_tiles)
    def _sample():
        yp_ref[...] = _layer_norm(pre_ref[...], g_ref[...], be_ref[...])
        ys_ref[:, 0, :] = _layer_norm(residual_plus_mlp(x1s_ref), g_ref[...], be_ref[...])


def _ffn(x1p, w1, b1, w2, b2, ln_g, ln_b, xs, sample_rows, w_out, ln1_g, ln1_b):
    tile = FFN_TILE
    n_tok = x1p.shape[0]
    assert n_tok % tile == 0
    n_tiles = n_tok // tile
    n_seq = xs.shape[0]
    const2 = lambda i: (0, 0)
    ptile = lambda i: (jnp.minimum(i, n_tiles - 1), 0)
    once = lambda shape: pl.BlockSpec(shape, lambda i: (0,) * len(shape), pipeline_mode=pl.Buffered(1))
    return pl.pallas_call(
        functools.partial(_ffn_kernel, n_prompt_tiles=n_tiles),
        name="ffn_ln2",
        grid=(n_tiles + 1,),
        in_specs=[
            pl.BlockSpec((tile, D_MODEL), ptile),
            pl.BlockSpec(w1.shape, const2, pipeline_mode=pl.Buffered(1)),
            pl.BlockSpec(b1.shape, const2),
            pl.BlockSpec(w2.shape, const2, pipeline_mode=pl.Buffered(1)),
            pl.BlockSpec(b2.shape, const2),
            pl.BlockSpec(ln_g.shape, const2),
            pl.BlockSpec(ln_b.shape, const2),
            once(xs.shape),
        ] + [once(r.shape) for r in sample_rows] + [
            once(w_out.shape),
            pl.BlockSpec(ln1_g.shape, const2),
            pl.BlockSpec(ln1_b.shape, const2),
        ],
        out_specs=[pl.BlockSpec((tile, D_MODEL), lambda i: (jnp.maximum(i - 1, 0), 0)),
                   pl.BlockSpec((n_seq, 1, D_MODEL), lambda i: (0, 0, 0))],
        out_shape=[jax.ShapeDtypeStruct((n_tok, D_MODEL), F32),
                   jax.ShapeDtypeStruct((n_seq, 1, D_MODEL), F32)],
        scratch_shapes=[pltpu.VMEM((tile, D_MODEL), F32),
                        pltpu.VMEM((n_seq, D_MODEL), F32)],
        compiler_params=pltpu.CompilerParams(
            dimension_semantics=("arbitrary",),
            vmem_limit_bytes=V7X_VMEM_LIMIT),
    )(x1p, w1, b1, w2, b2, ln_g, ln_b, xs, *sample_rows, w_out, ln1_g, ln1_b)


def _sample_proj_kernel(x_ref, wt_ref, bg_ref, sp_ref, n_ref, m_ref, wp_ref, ps_ref,
                        ypool_ref, pool_ref, q_ref, kd_ref, v_ref, iw_ref, sv_ref, den_ref,
                        floor_ref, osig_ref, nout_ref, mout_ref):
    x = x_ref[:, 0, :]
    nb = x.shape[0]
    proj = _dot_nt(x, wt_ref[0:CUT_G, :])
    u = proj[:, 0:D_POOL]
    gates = _dot_nt(wt_ref[CUT_G:CUT_G + 2 * NH, :], x) + _gate_bias_rows(bg_ref, nb)
    ig_r = gates[0:NH]
    inter_r = jax.nn.log_sigmoid(gates[NH:2 * NH]) + m_ref[...]
    m_t_r = jnp.maximum(inter_r, ig_r)
    dw_r = jnp.exp(ig_r - m_t_r)
    iw_r = jnp.exp(inter_r - m_t_r)
    floor_r = jnp.exp(-m_t_r)
    mout_ref[...] = m_t_r

    def as_column(row):
        return jnp.broadcast_to(row, (DH, nb)).T

    for g, w in enumerate(POOL_WINDOWS):
        lo = g * POOL_GC
        u_g = u[:, lo:lo + POOL_GC]
        acc = u_g
        for r in range(POOL_PAD - w, POOL_BUF):
            acc = acc + sp_ref[r, :, lo:lo + POOL_GC]
        pooled = acc / float(min(PAST_LEN + 1, w)) - u_g
        ypool_ref[:, lo:lo + POOL_GC] = _dot(pooled, wp_ref[g]) * ps_ref[:, lo:lo + POOL_GC]
    pool_ref[0:POOL_BUF - 1] = sp_ref[1:POOL_BUF]
    pool_ref[POOL_BUF - 1] = u

    for h in range(NH):
        col = slice(h * DH, (h + 1) * DH)
        dw = as_column(dw_r[h:h + 1])
        iw = as_column(iw_r[h:h + 1])
        q = proj[:, D_POOL + h * DH:D_POOL + (h + 1) * DH]
        k = proj[:, CUT_K + h * DH:CUT_K + (h + 1) * DH] * K_SCALE
        v = proj[:, CUT_V + h * DH:CUT_V + (h + 1) * DH]
        o = proj[:, CUT_V + D_MLSTM + h * DH:CUT_V + D_MLSTM + (h + 1) * DH]
        n_old = n_ref[:, h, :]
        s = jnp.sum(q * k, axis=-1, keepdims=True) * dw
        q_ref[:, col] = q
        kd_ref[:, col] = dw * k
        v_ref[:, col] = v
        iw_ref[:, col] = iw
        sv_ref[:, col] = s * v
        den_ref[:, col] = s + iw * jnp.sum(q * n_old, axis=-1, keepdims=True)
        floor_ref[:, col] = as_column(floor_r[h:h + 1])
        osig_ref[:, col] = jax.nn.sigmoid(o)
        nout_ref[:, h, :] = iw * n_old + dw * k


def _sample_proj(x, w_in_t, b_gate2d, pool_rows, n_state, m_rows, w_pool, pool_scale):
    nb = x.shape[0]
    wide = jax.ShapeDtypeStruct((nb, D_MLSTM), F32)
    vmem = pl.BlockSpec(memory_space=pltpu.VMEM)
    return pl.pallas_call(
        _sample_proj_kernel,
        name="sample_proj",
        in_specs=[vmem, vmem, pl.BlockSpec(memory_space=pltpu.SMEM), vmem, vmem, vmem, vmem, vmem],
        out_shape=([wide, jax.ShapeDtypeStruct(pool_rows.shape, F32)] + [wide] * 8
                   + [jax.ShapeDtypeStruct(n_state.shape, F32), jax.ShapeDtypeStruct(m_rows.shape, F32)]),
        compiler_params=pltpu.CompilerParams(vmem_limit_bytes=V7X_VMEM_LIMIT),
    )(x, w_in_t, b_gate2d, pool_rows, n_state, m_rows, w_pool, pool_scale)


def kernel(x_prompt, x_sample, state_pool, state_C, state_n, state_m, w_in, b_gate, w_pool, pool_scale,
           w_out, ln1_g, ln1_b, w_ff1, b_ff1, w_ff2, b_ff2, ln2_g, ln2_b):
    assert w_in.shape[0] == DEPTH == 1
    bp, t_len, _ = x_prompt.shape
    bs = x_sample.shape[0]
    assert x_sample.shape[1] == 1

    w_t = jnp.transpose(w_in[0])
    wp = w_pool[0]
    ps = pool_scale[0].reshape(1, D_POOL)
    wo = w_out[0]
    g1, b1 = ln1_g[0].reshape(1, D_MODEL), ln1_b[0].reshape(1, D_MODEL)
    g2, b2 = ln2_g[0].reshape(1, D_MODEL), ln2_b[0].reshape(1, D_MODEL)
    wf1, wf2 = w_ff1[0], w_ff2[0]
    bf1, bf2 = b_ff1[0].reshape(1, D_FF), b_ff2[0].reshape(1, D_MODEL)

    (ypool, pool_s, q_s, kd_s, v_s, iw_s, sv_s, den_s, floor_s, osig_s, n_s, m_s_rows) = _sample_proj(
        x_sample, w_t, b_gate, jnp.transpose(state_pool[0], (1, 0, 2)),
        state_n[0], jnp.transpose(state_m[0]), wp, ps)

    x1_p, pool_p, c_p, n_p, m_rep, c_s, inter_blk = _prompt_mixer(
        x_prompt, w_t, b_gate, wp, ps, wo, g1, b1, state_C, q_s, kd_s, v_s, iw_s)
    m_p = m_rep[:, :, 0].reshape(DEPTH, bp, NH)

    sample_rows = (ypool, sv_s, iw_s, inter_blk.reshape(bs, D_MLSTM), den_s, floor_s, osig_s)
    y_p, y_s = _ffn(x1_p.reshape(bp * t_len, D_MODEL), wf1, bf1, wf2, bf2, g2, b2,
                    x_sample, sample_rows, wo, g1, b1)

    return (y_p.reshape(bp, t_len, D_MODEL), y_s,
            pool_p, c_p, n_p, m_p,
            jnp.transpose(pool_s, (1, 0, 2))[None], c_s,
            n_s[None], jnp.transpose(m_s_rows)[None])
```

```python
import functools

import jax
import jax.numpy as jnp
from jax import lax
from jax.experimental import pallas as pl
from jax.experimental.pallas import tpu as pltpu

F32 = jnp.float32
BF16 = jnp.bfloat16

D_MODEL = 1024
D_POOL = D_MODEL // 2
D_MLSTM = D_MODEL - D_POOL
POOL_WINDOWS = (2, 4, 8, 16)
POOL_GC = D_POOL // len(POOL_WINDOWS)
POOL_BUF = max(POOL_WINDOWS) - 1
POOL_PAD = POOL_BUF + 1
NH = 4
DH = D_MLSTM // NH
D_FF = 4 * D_MODEL
DEPTH = 1
PAST_LEN = 16384
ALPHA = (2.0 * DEPTH) ** 0.25
LN_EPS = 1e-5
K_SCALE = DH ** -0.5

MLSTM_CHUNK = 128
MIX_TILE = 512
FFN_TILE = 512
FF_CHUNK = 2048
SAMPLE_SEQS = 4
PROJ_COLS = 256
LANES = 128
assert POOL_GC == LANES and DH == LANES
CUT_K = D_POOL + D_MLSTM
CUT_V = CUT_K + D_MLSTM
CUT_G = D_POOL + 4 * D_MLSTM
LB_Q = D_POOL // LANES
LB_K = CUT_K // LANES
LB_V = CUT_V // LANES
LB_O = LB_V + NH
LB_G = CUT_G // LANES
N_LB = LB_G + 1
V7X_VMEM_LIMIT = 64 * 1024 * 1024


def _layer_norm(y, g, b):
    mu = jnp.mean(y, axis=-1, keepdims=True)
    yc = y - mu
    var = jnp.mean(yc * yc, axis=-1, keepdims=True)
    return yc * lax.rsqrt(var + LN_EPS) * g + b


def _dot(a, b):
    return jnp.dot(a, b, preferred_element_type=F32)


def _dot_nt(a, b):
    return lax.dot_general(a, b, (((1,), (1,)), ((), ())), preferred_element_type=F32)


def _twice(row):
    return jnp.concatenate([row, row], axis=-1)


def _gate_bias_rows(bg_ref, width):
    row = lax.broadcasted_iota(jnp.int32, (2 * NH, width), 0)
    bias = jnp.zeros((2 * NH, width), F32)
    for g in range(2 * NH):
        bias = jnp.where(row == g, bg_ref[0, g], bias)
    return bias


def _mixer_kernel(xn_ref, xc_ref, wt_ref, bg_ref, wp_ref, ps_ref, wo_ref, g1_ref, b1_ref,
                  sc_ref, sq_ref, skd_ref, sv_ref, siw_ref,
                  x1_ref, pool_ref, c_ref, n_ref, m_ref, scout_ref, sinter_ref,
                  win_ref, wob_ref, xb_ref, pm_a, pm_b, ext_ref, mix_a, mix_b, res_ref, caug_ref, mst_ref,
                  lhs_ref, ktw_ref, gbias_ref, *, tiles_per_seq, n_tiles):
    tq = MIX_TILE
    L = MLSTM_CHUNK
    step = pl.program_id(0)
    t_idx = lax.rem(step + tiles_per_seq - 1, tiles_per_seq)
    last_t = tiles_per_seq - 1
    parity = lax.rem(step, 2)

    def project(pm_next):
        xb_ref[...] = xn_ref[0].astype(BF16)

        def piece(c0, c1):
            res = _dot(xb_ref[...], win_ref[:, c0:c1])
            for i in range((c1 - c0) // LANES):
                pm_next[c0 // LANES + i] = res[:, i * LANES:(i + 1) * LANES]

        n_cols = N_LB * LANES
        return [functools.partial(piece, c0, min(c0 + PROJ_COLS, n_cols))
                for c0 in range(0, n_cols, PROJ_COLS)]

    @pl.when(step == 0)
    def _first_step():
        for j in range(CUT_G // PROJ_COLS):
            win_ref[:, j * PROJ_COLS:(j + 1) * PROJ_COLS] = (
                wt_ref[j * PROJ_COLS:(j + 1) * PROJ_COLS, :].T.astype(BF16))
        gate_rows = jnp.concatenate([wt_ref[CUT_G:CUT_G + 2 * NH, :],
                                     jnp.zeros((LANES - 2 * NH, D_MODEL), F32)], axis=0)
        win_ref[:, CUT_G:CUT_G + LANES] = gate_rows.T.astype(BF16)
        wob_ref[...] = wo_ref[...].astype(BF16)
        gbias_ref[...] = _gate_bias_rows(bg_ref, L)
        mix_b[...] = jnp.zeros(mix_b.shape, BF16)
        res_ref[...] = jnp.zeros(res_ref.shape, F32)
        for piece in project(pm_a):
            piece()

    @pl.when(t_idx == 0)
    def _init():
        ext_ref[:, 0:POOL_PAD, :] = jnp.zeros((len(POOL_WINDOWS), POOL_PAD, POOL_GC), F32)
        caug_ref[...] = jnp.zeros(caug_ref.shape, F32)
        mst_ref[...] = jnp.zeros(mst_ref.shape, F32)

    tt = lax.broadcasted_iota(jnp.int32, (L, L), 0)
    ss = lax.broadcasted_iota(jnp.int32, (L, L), 1)
    causal = ss <= tt
    diag = ss == tt

    def finish(mix_prev):
        x1_ref[0] = _layer_norm(res_ref[...] + _dot(mix_prev[...], wob_ref[...]), g1_ref[...], b1_ref[...])

    def run(pm_next, pm_cur, mix_cur, mix_prev, sample_half):
        finish(mix_prev)
        pieces = project(pm_next)
        n_chunks = tq // L
        n_slots = len(POOL_WINDOWS) + 2 * n_chunks * NH
        slot_of = [(k * n_slots) // len(pieces) for k in range(len(pieces))]
        slot = [0]

        def next_slot():
            for k, piece in enumerate(pieces):
                if slot_of[k] == slot[0]:
                    piece()
            slot[0] += 1

        pos = t_idx * tq + lax.broadcasted_iota(jnp.int32, (tq, POOL_GC), 0)
        for g, w in enumerate(POOL_WINDOWS):
            lo = g * POOL_GC
            u_g = pm_cur[g]
            ext_ref[g, POOL_PAD:POOL_PAD + tq, :] = u_g
            s = ext_ref[g]
            k = 1
            while k < w:
                s = s + pltpu.roll(s, k, axis=0)
                k *= 2
            cnt = jnp.minimum(pos + 1, w).astype(F32)
            pooled = s[POOL_PAD:, :] / cnt - u_g
            mix_cur[:, lo:lo + POOL_GC] = (_dot(pooled, wp_ref[g]) * ps_ref[:, lo:lo + POOL_GC]).astype(BF16)
            ext_ref[g, 0:POOL_PAD, :] = ext_ref[g, tq:tq + POOL_PAD, :]
            next_slot()

        lane = lax.broadcasted_iota(jnp.int32, (NH, L), 1)
        gate_bias = gbias_ref[...]
        m_prev = mst_ref[...]
        chunk_rows = []
        for c in range(n_chunks):
            rows = slice(c * L, (c + 1) * L)
            gates = pm_cur[LB_G, rows, :].T[0:2 * NH, :] + gate_bias
            lf = jax.nn.log_sigmoid(gates[NH:2 * NH])
            b = lf
            k = 1
            while k < L:
                b = b + jnp.where(lane >= k, pltpu.roll(b, k, axis=1), 0.0)
                k *= 2
            a = gates[0:NH] - b
            cmax = a
            k = 1
            while k < L:
                cmax = jnp.maximum(cmax, jnp.where(lane >= k, pltpu.roll(cmax, k, axis=1), -jnp.inf))
                k *= 2
            amax = jnp.max(a, axis=-1, keepdims=True)
            big_m = jnp.maximum(cmax, m_prev)
            mm = jnp.maximum(m_prev, amax)
            chunk_rows.append(dict(
                a=a, big_m=big_m,
                iw=jnp.exp(m_prev - big_m),
                floor=jnp.exp(-(b + big_m)),
                w_loc=jnp.exp(a - amax),
                g_state=jnp.exp(m_prev - mm),
                f_state=jnp.exp(amax - mm)))
            m_prev = jnp.sum(lf, axis=-1, keepdims=True) + mm
        mst_ref[...] = m_prev

        def as_column(row):
            return jnp.broadcast_to(row, (L, L)).T

        pair = (step - 1) // 2
        half_rows = slice(sample_half * SAMPLE_SEQS, (sample_half + 1) * SAMPLE_SEQS)
        pad = jnp.zeros((DH - SAMPLE_SEQS, DH), F32)

        def step_rows(ref):
            rows8 = ref[pl.ds(pl.multiple_of(pair * 2 * SAMPLE_SEQS, 2 * SAMPLE_SEQS), 2 * SAMPLE_SEQS), :]
            return rows8[half_rows]

        s_q, s_kd, s_v, s_iw = (step_rows(r) for r in (sq_ref, skd_ref, sv_ref, siw_ref))

        def sample_item(h, j):
            col = slice(h * DH, (h + 1) * DH)
            c_old = sc_ref[0, j, h]
            row = sample_half * SAMPLE_SEQS + j
            sinter_ref[0, row:row + 1, col] = jnp.sum(sample_q_t[h][:, j:j + 1] * c_old, axis=0, keepdims=True)
            scout_ref[0, j, h] = s_iw[j:j + 1, col] * c_old + sample_kd_t[h][:, j:j + 1] * s_v[j:j + 1, col]

        sample_q_t = [jnp.concatenate([s_q[:, h * DH:(h + 1) * DH], pad], axis=0).T for h in range(NH)]
        sample_kd_t = [jnp.concatenate([s_kd[:, h * DH:(h + 1) * DH], pad], axis=0).T for h in range(NH)]
        sample_items = [functools.partial(sample_item, h, j) for h in range(NH) for j in range(SAMPLE_SEQS)]

        for c in range(n_chunks):
            rows = slice(c * L, (c + 1) * L)
            cr = chunk_rows[c]
            for h in range(NH):
                i = c * NH + h
                q = pm_cur[LB_Q + h, rows, :]
                qb = q.astype(BF16)
                kt = pm_cur[LB_K + h, rows, :].T * K_SCALE
                p = jnp.exp(jnp.where(causal, cr["a"][h:h + 1], -jnp.inf) - as_column(cr["big_m"][h:h + 1]))
                lhs_ref[i, :, 0:L] = (_dot(qb, kt.astype(BF16)) * p).astype(BF16)
                ktw_ref[i] = (kt * cr["w_loc"][h:h + 1]).astype(BF16)
                lhs_ref[i, :, L:L + DH] = (q * as_column(cr["iw"][h:h + 1])).astype(BF16)
                if i % 2 == 0:
                    sample_items[i // 2]()
                next_slot()

        ones_blk = jnp.ones((L, DH), BF16)
        for c in range(n_chunks):
            rows = slice(c * L, (c + 1) * L)
            cr = chunk_rows[c]
            for h in range(NH):
                i = c * NH + h
                caug = caug_ref[h]
                v_aug = jnp.concatenate([pm_cur[LB_V + h, rows, :].astype(BF16), ones_blk], axis=1)
                comb = _dot(lhs_ref[i], jnp.concatenate([v_aug, caug.astype(BF16)], axis=0))
                hh = comb[:, :DH] / jnp.maximum(jnp.abs(comb[:, DH:]), as_column(cr["floor"][h:h + 1]))
                o = pm_cur[LB_O + h, rows, :]
                mix_cur[rows, D_POOL + h * DH:D_POOL + (h + 1) * DH] = (
                    jax.nn.sigmoid(o) * hh).astype(BF16)
                caug_ref[h] = (_twice(cr["g_state"][h:h + 1]) * caug
                               + _twice(cr["f_state"][h:h + 1]) * _dot(ktw_ref[i], v_aug))
                if i % 2 == 0:
                    sample_items[n_chunks * NH // 2 + i // 2]()
                next_slot()

        res_ref[...] = ALPHA * xc_ref[0]

    @pl.when((parity == 0) & (step > 0) & (step <= n_tiles))
    def _even():
        run(pm_a, pm_b, mix_b, mix_a, 1)

    @pl.when((parity == 1) & (step <= n_tiles))
    def _odd():
        run(pm_b, pm_a, mix_a, mix_b, 0)

    @pl.when(step == n_tiles + 1)
    def _tail():
        finish(mix_b if n_tiles % 2 == 0 else mix_a)

    @pl.when((t_idx == last_t) & (step > 0))
    def _final_state():
        for g in range(len(POOL_WINDOWS)):
            pool_ref[0, 0, :, g * POOL_GC:(g + 1) * POOL_GC] = ext_ref[g, tq + 1:tq + POOL_PAD, :]
        for h in range(NH):
            caug = caug_ref[h]
            c_ref[0, 0, h] = caug[:, :DH]
            n_ref[0, 0, h:h + 1, :] = jnp.sum(jnp.where(diag, caug[:, DH:], 0.0),
                                              axis=0, keepdims=True)
        m_ref[0] = mst_ref[...]


def _prompt_mixer(x, w_in_t, b_gate2d, w_pool, pool_scale, w_out, ln_g, ln_b, state_c, sq, skd, sv, siw):
    bsz, t_len, _ = x.shape
    tq = MIX_TILE
    assert t_len % tq == 0 and tq % MLSTM_CHUNK == 0 and tq >= POOL_PAD
    nt = t_len // tq
    n_tiles = bsz * nt
    n_seq = sq.shape[0]
    assert n_seq == n_tiles * SAMPLE_SEQS and n_tiles % 2 == 0
    assert SAMPLE_SEQS * NH == (tq // MLSTM_CHUNK) * NH
    n_items = (tq // MLSTM_CHUNK) * NH
    const2 = lambda i: (0, 0)
    nxt = lambda i: jnp.minimum(i, n_tiles - 1)
    cur = lambda i: jnp.clip(i - 1, 0, n_tiles - 1)
    fin = lambda i: jnp.clip(i - 2, 0, n_tiles - 1)
    return pl.pallas_call(
        functools.partial(_mixer_kernel, tiles_per_seq=nt, n_tiles=n_tiles),
        name="prompt_mixer",
        grid=(n_tiles + 2,),
        in_specs=[
            pl.BlockSpec((1, tq, D_MODEL), lambda i: (nxt(i) // nt, nxt(i) % nt, 0)),
            pl.BlockSpec((1, tq, D_MODEL), lambda i: (cur(i) // nt, cur(i) % nt, 0)),
            pl.BlockSpec(w_in_t.shape, const2, pipeline_mode=pl.Buffered(1)),
            pl.BlockSpec(memory_space=pltpu.SMEM),
            pl.BlockSpec(w_pool.shape, lambda i: (0, 0, 0)),
            pl.BlockSpec(pool_scale.shape, const2),
            pl.BlockSpec(w_out.shape, const2, pipeline_mode=pl.Buffered(1)),
            pl.BlockSpec(ln_g.shape, const2),
            pl.BlockSpec(ln_b.shape, const2),
            pl.BlockSpec((1, SAMPLE_SEQS, NH, DH, DH), lambda i: (0, cur(i), 0, 0, 0)),
            pl.BlockSpec(sq.shape, const2, pipeline_mode=pl.Buffered(1)),
            pl.BlockSpec(skd.shape, const2, pipeline_mode=pl.Buffered(1)),
            pl.BlockSpec(sv.shape, const2, pipeline_mode=pl.Buffered(1)),
            pl.BlockSpec(siw.shape, const2, pipeline_mode=pl.Buffered(1)),
        ],
        out_specs=[
            pl.BlockSpec((1, tq, D_MODEL), lambda i: (fin(i) // nt, fin(i) % nt, 0)),
            pl.BlockSpec((1, 1, POOL_BUF, D_POOL), lambda i: (0, cur(i) // nt, 0, 0)),
            pl.BlockSpec((1, 1, NH, DH, DH), lambda i: (0, cur(i) // nt, 0, 0, 0)),
            pl.BlockSpec((1, 1, NH, DH), lambda i: (0, cur(i) // nt, 0, 0)),
            pl.BlockSpec((1, NH, MLSTM_CHUNK), lambda i: (cur(i) // nt, 0, 0)),
            pl.BlockSpec((1, SAMPLE_SEQS, NH, DH, DH), lambda i: (0, cur(i), 0, 0, 0)),
            pl.BlockSpec((1, 2 * SAMPLE_SEQS, D_MLSTM), lambda i: (cur(i) // 2, 0, 0)),
        ],
        out_shape=[
            jax.ShapeDtypeStruct((bsz, t_len, D_MODEL), F32),
            jax.ShapeDtypeStruct((DEPTH, bsz, POOL_BUF, D_POOL), F32),
            jax.ShapeDtypeStruct((DEPTH, bsz, NH, DH, DH), F32),
            jax.ShapeDtypeStruct((DEPTH, bsz, NH, DH), F32),
            jax.ShapeDtypeStruct((bsz, NH, MLSTM_CHUNK), F32),
            jax.ShapeDtypeStruct(state_c.shape, F32),
            jax.ShapeDtypeStruct((n_tiles // 2, 2 * SAMPLE_SEQS, D_MLSTM), F32),
        ],
        scratch_shapes=[
            pltpu.VMEM((D_MODEL, N_LB * LANES), BF16),
            pltpu.VMEM((D_MODEL, D_MODEL), BF16),
            pltpu.VMEM((tq, D_MODEL), BF16),
            pltpu.VMEM((N_LB, tq, LANES), F32),
            pltpu.VMEM((N_LB, tq, LANES), F32),
            pltpu.VMEM((len(POOL_WINDOWS), POOL_PAD + tq, POOL_GC), F32),
            pltpu.VMEM((tq, D_MODEL), BF16),
            pltpu.VMEM((tq, D_MODEL), BF16),
            pltpu.VMEM((tq, D_MODEL), F32),
            pltpu.VMEM((NH, DH, 2 * DH), F32),
            pltpu.VMEM((NH, MLSTM_CHUNK), F32),
            pltpu.VMEM((n_items, MLSTM_CHUNK, MLSTM_CHUNK + DH), BF16),
            pltpu.VMEM((n_items, DH, MLSTM_CHUNK), BF16),
            pltpu.VMEM((2 * NH, MLSTM_CHUNK), F32),
        ],
        compiler_params=pltpu.CompilerParams(
            dimension_semantics=("arbitrary",),
            vmem_limit_bytes=V7X_VMEM_LIMIT),
    )(x, x, w_in_t, b_gate2d, w_pool, pool_scale, w_out, ln_g, ln_b, state_c, sq, skd, sv, siw)


def _ffn_kernel(x1p_ref, w1_ref, b1_ref, w2_ref, b2_ref, g_ref, be_ref,
                xs_ref, ypool_ref, sv_ref, siw_ref, sinter_ref, den_ref, floor_ref, osig_ref,
                wo_ref, g1_ref, bb1_ref,
                yp_ref, ys_ref, pre_ref, x1s_ref, *, n_prompt_tiles):
    def residual_plus_mlp(x1_ref):
        acc = None
        for c in range(D_FF // FF_CHUNK):
            cols = slice(c * FF_CHUNK, (c + 1) * FF_CHUNK)
            hid = jnp.maximum(_dot(x1_ref[...], w1_ref[:, cols]) + b1_ref[:, cols], 0.0)
            part = _dot(hid * hid, w2_ref[cols, :])
            acc = part if acc is None else acc + part
        return ALPHA * x1_ref[...] + (acc + b2_ref[...])

    step = pl.program_id(0)

    @pl.when(step == 0)
    def _no_tile_yet():
        pre_ref[...] = jnp.zeros(pre_ref.shape, F32)
        hh = (sv_ref[...] + siw_ref[...] * sinter_ref[...]) / jnp.maximum(jnp.abs(den_ref[...]), floor_ref[...])
        mixin = jnp.concatenate([ypool_ref[...], osig_ref[...] * hh], axis=-1)
        x1s_ref[...] = _layer_norm(ALPHA * xs_ref[:, 0, :] + _dot(mixin, wo_ref[...]),
                                   g1_ref[...], bb1_ref[...])

    @pl.when(step < n_prompt_tiles)
    def _prompt():
        yp_ref[...] = _layer_norm(pre_ref[...], g_ref[...], be_ref[...])
        pre_ref[...] = residual_plus_mlp(x1p_ref)

    @pl.when(step == n_prompt_tiles)
    def _sample():
        yp_ref[...] = _layer_norm(pre_ref[...], g_ref[...], be_ref[...])
        ys_ref[:, 0, :] = _layer_norm(residual_plus_mlp(x1s_ref), g_ref[...], be_ref[...])


def _ffn(x1p, w1, b1, w2, b2, ln_g, ln_b, xs, sample_rows, w_out, ln1_g, ln1_b):
    tile = FFN_TILE
    n_tok = x1p.shape[0]
    assert n_tok % tile == 0
    n_tiles = n_tok // tile
    n_seq = xs.shape[0]
    const2 = lambda i: (0, 0)
    ptile = lambda i: (jnp.minimum(i, n_tiles - 1), 0)
    once = lambda shape: pl.BlockSpec(shape, lambda i: (0,) * len(shape), pipeline_mode=pl.Buffered(1))
    return pl.pallas_call(
        functools.partial(_ffn_kernel, n_prompt_tiles=n_tiles),
        name="ffn_ln2",
        grid=(n_tiles + 1,),
        in_specs=[
            pl.BlockSpec((tile, D_MODEL), ptile),
            pl.BlockSpec(w1.shape, const2, pipeline_mode=pl.Buffered(1)),
            pl.BlockSpec(b1.shape, const2),
            pl.BlockSpec(w2.shape, const2, pipeline_mode=pl.Buffered(1)),
            pl.BlockSpec(b2.shape, const2),
            pl.BlockSpec(ln_g.shape, const2),
            pl.BlockSpec(ln_b.shape, const2),
            once(xs.shape),
        ] + [once(r.shape) for r in sample_rows] + [
            once(w_out.shape),
            pl.BlockSpec(ln1_g.shape, const2),
            pl.BlockSpec(ln1_b.shape, const2),
        ],
        out_specs=[pl.BlockSpec((tile, D_MODEL), lambda i: (jnp.maximum(i - 1, 0), 0)),
                   pl.BlockSpec((n_seq, 1, D_MODEL), lambda i: (0, 0, 0))],
        out_shape=[jax.ShapeDtypeStruct((n_tok, D_MODEL), F32),
                   jax.ShapeDtypeStruct((n_seq, 1, D_MODEL), F32)],
        scratch_shapes=[pltpu.VMEM((tile, D_MODEL), F32),
                        pltpu.VMEM((n_seq, D_MODEL), F32)],
        compiler_params=pltpu.CompilerParams(
            dimension_semantics=("arbitrary",),
            vmem_limit_bytes=V7X_VMEM_LIMIT),
    )(x1p, w1, b1, w2, b2, ln_g, ln_b, xs, *sample_rows, w_out, ln1_g, ln1_b)


def _sample_proj_kernel(x_ref, wt_ref, bg_ref, sp_ref, n_ref, m_ref, wp_ref, ps_ref,
                        ypool_ref, pool_ref, q_ref, kd_ref, v_ref, iw_ref, sv_ref, den_ref,
                        floor_ref, osig_ref, nout_ref, mout_ref):
    x = x_ref[:, 0, :]
    nb = x.shape[0]
    proj = _dot_nt(x, wt_ref[0:CUT_G, :])
    u = proj[:, 0:D_POOL]
    gates = _dot_nt(wt_ref[CUT_G:CUT_G + 2 * NH, :], x) + _gate_bias_rows(bg_ref, nb)
    ig_r = gates[0:NH]
    inter_r = jax.nn.log_sigmoid(gates[NH:2 * NH]) + m_ref[...]
    m_t_r = jnp.maximum(inter_r, ig_r)
    dw_r = jnp.exp(ig_r - m_t_r)
    iw_r = jnp.exp(inter_r - m_t_r)
    floor_r = jnp.exp(-m_t_r)
    mout_ref[...] = m_t_r

    def as_column(row):
        return jnp.broadcast_to(row, (DH, nb)).T

    for g, w in enumerate(POOL_WINDOWS):
        lo = g * POOL_GC
        u_g = u[:, lo:lo + POOL_GC]
        acc = u_g
        for r in range(POOL_PAD - w, POOL_BUF):
            acc = acc + sp_ref[r, :, lo:lo + POOL_GC]
        pooled = acc / float(min(PAST_LEN + 1, w)) - u_g
        ypool_ref[:, lo:lo + POOL_GC] = _dot(pooled, wp_ref[g]) * ps_ref[:, lo:lo + POOL_GC]
    pool_ref[0:POOL_BUF - 1] = sp_ref[1:POOL_BUF]
    pool_ref[POOL_BUF - 1] = u

    for h in range(NH):
        col = slice(h * DH, (h + 1) * DH)
        dw = as_column(dw_r[h:h + 1])
        iw = as_column(iw_r[h:h + 1])
        q = proj[:, D_POOL + h * DH:D_POOL + (h + 1) * DH]
        k = proj[:, CUT_K + h * DH:CUT_K + (h + 1) * DH] * K_SCALE
        v = proj[:, CUT_V + h * DH:CUT_V + (h + 1) * DH]
        o = proj[:, CUT_V + D_MLSTM + h * DH:CUT_V + D_MLSTM + (h + 1) * DH]
        n_old = n_ref[:, h, :]
        s = jnp.sum(q * k, axis=-1, keepdims=True) * dw
        q_ref[:, col] = q
        kd_ref[:, col] = dw * k
        v_ref[:, col] = v
        iw_ref[:, col] = iw
        sv_ref[:, col] = s * v
        den_ref[:, col] = s + iw * jnp.sum(q * n_old, axis=-1, keepdims=True)
        floor_ref[:, col] = as_column(floor_r[h:h + 1])
        osig_ref[:, col] = jax.nn.sigmoid(o)
        nout_ref[:, h, :] = iw * n_old + dw * k


def _sample_proj(x, w_in_t, b_gate2d, pool_rows, n_state, m_rows, w_pool, pool_scale):
    nb = x.shape[0]
    wide = jax.ShapeDtypeStruct((nb, D_MLSTM), F32)
    vmem = pl.BlockSpec(memory_space=pltpu.VMEM)
    return pl.pallas_call(
        _sample_proj_kernel,
        name="sample_proj",
        in_specs=[vmem, vmem, pl.BlockSpec(memory_space=pltpu.SMEM), vmem, vmem, vmem, vmem, vmem],
        out_shape=([wide, jax.ShapeDtypeStruct(pool_rows.shape, F32)] + [wide] * 8
                   + [jax.ShapeDtypeStruct(n_state.shape, F32), jax.ShapeDtypeStruct(m_rows.shape, F32)]),
        compiler_params=pltpu.CompilerParams(vmem_limit_bytes=V7X_VMEM_LIMIT),
    )(x, w_in_t, b_gate2d, pool_rows, n_state, m_rows, w_pool, pool_scale)


def kernel(x_prompt, x_sample, state_pool, state_C, state_n, state_m, w_in, b_gate, w_pool, pool_scale,
           w_out, ln1_g, ln1_b, w_ff1, b_ff1, w_ff2, b_ff2, ln2_g, ln2_b):
    assert w_in.shape[0] == DEPTH == 1
    bp, t_len, _ = x_prompt.shape
    bs = x_sample.shape[0]
    assert x_sample.shape[1] == 1

    w_t = jnp.transpose(w_in[0])
    wp = w_pool[0]
    ps = pool_scale[0].reshape(1, D_POOL)
    wo = w_out[0]
    g1, b1 = ln1_g[0].reshape(1, D_MODEL), ln1_b[0].reshape(1, D_MODEL)
    g2, b2 = ln2_g[0].reshape(1, D_MODEL), ln2_b[0].reshape(1, D_MODEL)
    wf1, wf2 = w_ff1[0], w_ff2[0]
    bf1, bf2 = b_ff1[0].reshape(1, D_FF), b_ff2[0].reshape(1, D_MODEL)

    (ypool, pool_s, q_s, kd_s, v_s, iw_s, sv_s, den_s, floor_s, osig_s, n_s, m_s_rows) = _sample_proj(
        x_sample, w_t, b_gate, jnp.transpose(state_pool[0], (1, 0, 2)),
        state_n[0], jnp.transpose(state_m[0]), wp, ps)

    x1_p, pool_p, c_p, n_p, m_rep, c_s, inter_blk = _prompt_mixer(
        x_prompt, w_t, b_gate, wp, ps, wo, g1, b1, state_C, q_s, kd_s, v_s, iw_s)
    m_p = m_rep[:, :, 0].reshape(DEPTH, bp, NH)

    sample_rows = (ypool, sv_s, iw_s, inter_blk.reshape(bs, D_MLSTM), den_s, floor_s, osig_s)
    y_p, y_s = _ffn(x1_p.reshape(bp * t_len, D_MODEL), wf1, bf1, wf2, bf2, g2, b2,
                    x_sample, sample_rows, wo, g1, b1)

    return (y_p.reshape(bp, t_len, D_MODEL), y_s,
            pool_p, c_p, n_p, m_p,
            jnp.transpose(pool_s, (1, 0, 2))[None], c_s,
            n_s[None], jnp.transpose(m_s_rows)[None])
```

```python
import functools

import jax
import jax.numpy as jnp
from jax import lax
from jax.experimental import pallas as pl
from jax.experimental.pallas import tpu as pltpu

F32 = jnp.float32
BF16 = jnp.bfloat16

D_MODEL = 1024
D_POOL = D_MODEL // 2
D_MLSTM = D_MODEL - D_POOL
POOL_WINDOWS = (2, 4, 8, 16)
POOL_GC = D_POOL // len(POOL_WINDOWS)
POOL_BUF = max(POOL_WINDOWS) - 1
POOL_PAD = POOL_BUF + 1
NH = 4
DH = D_MLSTM // NH
D_FF = 4 * D_MODEL
DEPTH = 1
PAST_LEN = 16384
ALPHA = (2.0 * DEPTH) ** 0.25
LN_EPS = 1e-5
K_SCALE = DH ** -0.5

MLSTM_CHUNK = 128
MIX_TILE = 512
FFN_TILE = 512
FF_CHUNK = 2048
SAMPLE_SEQS = 4
PROJ_COLS = 256
LANES = 128
assert POOL_GC == LANES and DH == LANES
CUT_K = D_POOL + D_MLSTM
CUT_V = CUT_K + D_MLSTM
CUT_G = D_POOL + 4 * D_MLSTM
LB_Q = D_POOL // LANES
LB_K = CUT_K // LANES
LB_V = CUT_V // LANES
LB_O = LB_V + NH
LB_G = CUT_G // LANES
N_LB = LB_G + 1
V7X_VMEM_LIMIT = 62 * 1024 * 1024


def _layer_norm(y, g, b):
    mu = jnp.mean(y, axis=-1, keepdims=True)
    yc = y - mu
    var = jnp.mean(yc * yc, axis=-1, keepdims=True)
    return yc * lax.rsqrt(var + LN_EPS) * g + b


def _dot(a, b):
    return jnp.dot(a, b, preferred_element_type=F32)


def _dot_nt(a, b):
    return lax.dot_general(a, b, (((1,), (1,)), ((), ())), preferred_element_type=F32)


def _twice(row):
    return jnp.concatenate([row, row], axis=-1)


def _gate_bias_rows(bg_ref, width):
    row = lax.broadcasted_iota(jnp.int32, (2 * NH, width), 0)
    bias = jnp.zeros((2 * NH, width), F32)
    for g in range(2 * NH):
        bias = jnp.where(row == g, bg_ref[0, g], bias)
    return bias


def _mixer_kernel(xn_ref, wt_ref, bg_ref, wp_ref, ps_ref, wo_ref, g1_ref, b1_ref,
                  sc_ref, sq_ref, skd_ref, sv_ref, siw_ref,
                  x1_ref, pool_ref, c_ref, n_ref, m_ref, scout_ref, sinter_ref,
                  win_ref, wob_ref, xb_ref, pm_a, pm_b, ext_ref, mix_a, mix_b, res_a, res_b, caug_ref, mst_ref,
                  lhs_ref, ktw_ref, gbias_ref, *, tiles_per_seq, n_tiles):
    tq = MIX_TILE
    L = MLSTM_CHUNK
    step = pl.program_id(0)
    t_idx = lax.rem(step + tiles_per_seq - 1, tiles_per_seq)
    last_t = tiles_per_seq - 1
    parity = lax.rem(step, 2)

    def project(pm_next):
        xb_ref[...] = xn_ref[0].astype(BF16)

        def piece(c0, c1):
            res = _dot(xb_ref[...], win_ref[:, c0:c1])
            for i in range((c1 - c0) // LANES):
                pm_next[c0 // LANES + i] = res[:, i * LANES:(i + 1) * LANES]

        n_cols = N_LB * LANES
        return [functools.partial(piece, c0, min(c0 + PROJ_COLS, n_cols))
                for c0 in range(0, n_cols, PROJ_COLS)]

    @pl.when(step == 0)
    def _first_step():
        for j in range(CUT_G // PROJ_COLS):
            win_ref[:, j * PROJ_COLS:(j + 1) * PROJ_COLS] = (
                wt_ref[j * PROJ_COLS:(j + 1) * PROJ_COLS, :].T.astype(BF16))
        gate_rows = jnp.concatenate([wt_ref[CUT_G:CUT_G + 2 * NH, :],
                                     jnp.zeros((LANES - 2 * NH, D_MODEL), F32)], axis=0)
        win_ref[:, CUT_G:CUT_G + LANES] = gate_rows.T.astype(BF16)
        wob_ref[...] = wo_ref[...].astype(BF16)
        gbias_ref[...] = _gate_bias_rows(bg_ref, L)
        mix_b[...] = jnp.zeros(mix_b.shape, BF16)
        res_b[...] = jnp.zeros(res_b.shape, F32)
        res_a[...] = ALPHA * xn_ref[0]
        for piece in project(pm_a):
            piece()

    @pl.when(t_idx == 0)
    def _init():
        ext_ref[:, 0:POOL_PAD, :] = jnp.zeros((len(POOL_WINDOWS), POOL_PAD, POOL_GC), F32)
        caug_ref[...] = jnp.zeros(caug_ref.shape, F32)
        mst_ref[...] = jnp.zeros(mst_ref.shape, F32)

    tt = lax.broadcasted_iota(jnp.int32, (L, L), 0)
    ss = lax.broadcasted_iota(jnp.int32, (L, L), 1)
    causal = ss <= tt
    diag = ss == tt

    def finish(mix_prev, res):
        x1_ref[0] = _layer_norm(res[...] + _dot(mix_prev[...], wob_ref[...]), g1_ref[...], b1_ref[...])

    def run(pm_next, pm_cur, mix_cur, mix_prev, res, sample_half):
        finish(mix_prev, res)
        pieces = project(pm_next)
        n_chunks = tq // L
        n_slots = len(POOL_WINDOWS) + 2 * n_chunks * NH
        slot_of = [(k * n_slots) // len(pieces) for k in range(len(pieces))]
        slot = [0]

        def next_slot():
            for k, piece in enumerate(pieces):
                if slot_of[k] == slot[0]:
                    piece()
            slot[0] += 1

        pos = t_idx * tq + lax.broadcasted_iota(jnp.int32, (tq, POOL_GC), 0)
        for g, w in enumerate(POOL_WINDOWS):
            lo = g * POOL_GC
            u_g = pm_cur[g]
            ext_ref[g, POOL_PAD:POOL_PAD + tq, :] = u_g
            s = ext_ref[g]
            k = 1
            while k < w:
                s = s + pltpu.roll(s, k, axis=0)
                k *= 2
            cnt = jnp.minimum(pos + 1, w).astype(F32)
            pooled = s[POOL_PAD:, :] / cnt - u_g
            mix_cur[:, lo:lo + POOL_GC] = (_dot(pooled, wp_ref[g]) * ps_ref[:, lo:lo + POOL_GC]).astype(BF16)
            ext_ref[g, 0:POOL_PAD, :] = ext_ref[g, tq:tq + POOL_PAD, :]
            next_slot()

        lane = lax.broadcasted_iota(jnp.int32, (NH, L), 1)
        gate_bias = gbias_ref[...]
        m_prev = mst_ref[...]
        chunk_rows = []
        for c in range(n_chunks):
            rows = slice(c * L, (c + 1) * L)
            gates = pm_cur[LB_G, rows, :].T[0:2 * NH, :] + gate_bias
            lf = jax.nn.log_sigmoid(gates[NH:2 * NH])
            b = lf
            k = 1
            while k < L:
                b = b + jnp.where(lane >= k, pltpu.roll(b, k, axis=1), 0.0)
                k *= 2
            a = gates[0:NH] - b
            cmax = a
            k = 1
            while k < L:
                cmax = jnp.maximum(cmax, jnp.where(lane >= k, pltpu.roll(cmax, k, axis=1), -jnp.inf))
                k *= 2
            amax = jnp.max(a, axis=-1, keepdims=True)
            big_m = jnp.maximum(cmax, m_prev)
            mm = jnp.maximum(m_prev, amax)
            chunk_rows.append(dict(
                a=a, big_m=big_m,
                iw=jnp.exp(m_prev - big_m),
                floor=jnp.exp(-(b + big_m)),
                w_loc=jnp.exp(a - amax),
                g_state=jnp.exp(m_prev - mm),
                f_state=jnp.exp(amax - mm)))
            m_prev = jnp.sum(lf, axis=-1, keepdims=True) + mm
        mst_ref[...] = m_prev

        def as_column(row):
            return jnp.broadcast_to(row, (L, L)).T

        pair = (step - 1) // 2
        half_rows = slice(sample_half * SAMPLE_SEQS, (sample_half + 1) * SAMPLE_SEQS)
        pad = jnp.zeros((DH - SAMPLE_SEQS, DH), F32)

        def step_rows(ref):
            rows8 = ref[pl.ds(pl.multiple_of(pair * 2 * SAMPLE_SEQS, 2 * SAMPLE_SEQS), 2 * SAMPLE_SEQS), :]
            return rows8[half_rows]

        s_q, s_kd, s_v, s_iw = (step_rows(r) for r in (sq_ref, skd_ref, sv_ref, siw_ref))

        def sample_item(h, j):
            col = slice(h * DH, (h + 1) * DH)
            c_old = sc_ref[0, j, h]
            row = sample_half * SAMPLE_SEQS + j
            sinter_ref[0, row:row + 1, col] = jnp.sum(sample_q_t[h][:, j:j + 1] * c_old, axis=0, keepdims=True)
            scout_ref[0, j, h] = s_iw[j:j + 1, col] * c_old + sample_kd_t[h][:, j:j + 1] * s_v[j:j + 1, col]

        sample_q_t = [jnp.concatenate([s_q[:, h * DH:(h + 1) * DH], pad], axis=0).T for h in range(NH)]
        sample_kd_t = [jnp.concatenate([s_kd[:, h * DH:(h + 1) * DH], pad], axis=0).T for h in range(NH)]
        sample_items = [functools.partial(sample_item, h, j) for h in range(NH) for j in range(SAMPLE_SEQS)]

        for c in range(n_chunks):
            rows = slice(c * L, (c + 1) * L)
            cr = chunk_rows[c]
            for h in range(NH):
                i = c * NH + h
                q = pm_cur[LB_Q + h, rows, :]
                qb = q.astype(BF16)
                kt = pm_cur[LB_K + h, rows, :].T * K_SCALE
                p = jnp.exp(jnp.where(causal, cr["a"][h:h + 1], -jnp.inf) - as_column(cr["big_m"][h:h + 1]))
                lhs_ref[i, :, 0:L] = (_dot(qb, kt.astype(BF16)) * p).astype(BF16)
                ktw_ref[i] = (kt * cr["w_loc"][h:h + 1]).astype(BF16)
                lhs_ref[i, :, L:L + DH] = (q * as_column(cr["iw"][h:h + 1])).astype(BF16)
                if i % 2 == 0:
                    sample_items[i // 2]()
                next_slot()

        ones_blk = jnp.ones((L, DH), BF16)
        for c in range(n_chunks):
            rows = slice(c * L, (c + 1) * L)
            cr = chunk_rows[c]
            for h in range(NH):
                i = c * NH + h
                caug = caug_ref[h]
                v_aug = jnp.concatenate([pm_cur[LB_V + h, rows, :].astype(BF16), ones_blk], axis=1)
                comb = _dot(lhs_ref[i], jnp.concatenate([v_aug, caug.astype(BF16)], axis=0))
                hh = comb[:, :DH] / jnp.maximum(jnp.abs(comb[:, DH:]), as_column(cr["floor"][h:h + 1]))
                o = pm_cur[LB_O + h, rows, :]
                mix_cur[rows, D_POOL + h * DH:D_POOL + (h + 1) * DH] = (
                    jax.nn.sigmoid(o) * hh).astype(BF16)
                caug_ref[h] = (_twice(cr["g_state"][h:h + 1]) * caug
                               + _twice(cr["f_state"][h:h + 1]) * _dot(ktw_ref[i], v_aug))
                if i % 2 == 0:
                    sample_items[n_chunks * NH // 2 + i // 2]()
                next_slot()

        res[...] = ALPHA * xn_ref[0]

    @pl.when((parity == 0) & (step > 0) & (step <= n_tiles))
    def _even():
        run(pm_a, pm_b, mix_b, mix_a, res_a, 1)

    @pl.when((parity == 1) & (step <= n_tiles))
    def _odd():
        run(pm_b, pm_a, mix_a, mix_b, res_b, 0)

    @pl.when(step == n_tiles + 1)
    def _tail():
        finish(*((mix_b, res_b) if n_tiles % 2 == 0 else (mix_a, res_a)))

    @pl.when((t_idx == last_t) & (step > 0))
    def _final_state():
        for g in range(len(POOL_WINDOWS)):
            pool_ref[0, 0, :, g * POOL_GC:(g + 1) * POOL_GC] = ext_ref[g, tq + 1:tq + POOL_PAD, :]
        for h in range(NH):
            caug = caug_ref[h]
            c_ref[0, 0, h] = caug[:, :DH]
            n_ref[0, 0, h:h + 1, :] = jnp.sum(jnp.where(diag, caug[:, DH:], 0.0),
                                              axis=0, keepdims=True)
        m_ref[0] = mst_ref[...]


def _prompt_mixer(x, w_in_t, b_gate2d, w_pool, pool_scale, w_out, ln_g, ln_b, state_c, sq, skd, sv, siw):
    bsz, t_len, _ = x.shape
    tq = MIX_TILE
    assert t_len % tq == 0 and tq % MLSTM_CHUNK == 0 and tq >= POOL_PAD
    nt = t_len // tq
    n_tiles = bsz * nt
    n_seq = sq.shape[0]
    assert n_seq == n_tiles * SAMPLE_SEQS and n_tiles % 2 == 0
    assert SAMPLE_SEQS * NH == (tq // MLSTM_CHUNK) * NH
    n_items = (tq // MLSTM_CHUNK) * NH
    const2 = lambda i: (0, 0)
    nxt = lambda i: jnp.minimum(i, n_tiles - 1)
    cur = lambda i: jnp.clip(i - 1, 0, n_tiles - 1)
    fin = lambda i: jnp.clip(i - 2, 0, n_tiles - 1)
    return pl.pallas_call(
        functools.partial(_mixer_kernel, tiles_per_seq=nt, n_tiles=n_tiles),
        name="prompt_mixer",
        grid=(n_tiles + 2,),
        in_specs=[
            pl.BlockSpec((1, tq, D_MODEL), lambda i: (nxt(i) // nt, nxt(i) % nt, 0)),
            pl.BlockSpec(w_in_t.shape, const2, pipeline_mode=pl.Buffered(1)),
            pl.BlockSpec(memory_space=pltpu.SMEM),
            pl.BlockSpec(w_pool.shape, lambda i: (0, 0, 0)),
            pl.BlockSpec(pool_scale.shape, const2),
            pl.BlockSpec(w_out.shape, const2, pipeline_mode=pl.Buffered(1)),
            pl.BlockSpec(ln_g.shape, const2),
            pl.BlockSpec(ln_b.shape, const2),
            pl.BlockSpec((1, SAMPLE_SEQS, NH, DH, DH), lambda i: (0, cur(i), 0, 0, 0)),
            pl.BlockSpec(sq.shape, const2, pipeline_mode=pl.Buffered(1)),
            pl.BlockSpec(skd.shape, const2, pipeline_mode=pl.Buffered(1)),
            pl.BlockSpec(sv.shape, const2, pipeline_mode=pl.Buffered(1)),
            pl.BlockSpec(siw.shape, const2, pipeline_mode=pl.Buffered(1)),
        ],
        out_specs=[
            pl.BlockSpec((1, tq, D_MODEL), lambda i: (fin(i) // nt, fin(i) % nt, 0)),
            pl.BlockSpec((1, 1, POOL_BUF, D_POOL), lambda i: (0, cur(i) // nt, 0, 0)),
            pl.BlockSpec((1, 1, NH, DH, DH), lambda i: (0, cur(i) // nt, 0, 0, 0)),
            pl.BlockSpec((1, 1, NH, DH), lambda i: (0, cur(i) // nt, 0, 0)),
            pl.BlockSpec((1, NH, MLSTM_CHUNK), lambda i: (cur(i) // nt, 0, 0)),
            pl.BlockSpec((1, SAMPLE_SEQS, NH, DH, DH), lambda i: (0, cur(i), 0, 0, 0)),
            pl.BlockSpec((1, 2 * SAMPLE_SEQS, D_MLSTM), lambda i: (cur(i) // 2, 0, 0)),
        ],
        out_shape=[
            jax.ShapeDtypeStruct((bsz, t_len, D_MODEL), F32),
            jax.ShapeDtypeStruct((DEPTH, bsz, POOL_BUF, D_POOL), F32),
            jax.ShapeDtypeStruct((DEPTH, bsz, NH, DH, DH), F32),
            jax.ShapeDtypeStruct((DEPTH, bsz, NH, DH), F32),
            jax.ShapeDtypeStruct((bsz, NH, MLSTM_CHUNK), F32),
            jax.ShapeDtypeStruct(state_c.shape, F32),
            jax.ShapeDtypeStruct((n_tiles // 2, 2 * SAMPLE_SEQS, D_MLSTM), F32),
        ],
        scratch_shapes=[
            pltpu.VMEM((D_MODEL, N_LB * LANES), BF16),
            pltpu.VMEM((D_MODEL, D_MODEL), BF16),
            pltpu.VMEM((tq, D_MODEL), BF16),
            pltpu.VMEM((N_LB, tq, LANES), F32),
            pltpu.VMEM((N_LB, tq, LANES), F32),
            pltpu.VMEM((len(POOL_WINDOWS), POOL_PAD + tq, POOL_GC), F32),
            pltpu.VMEM((tq, D_MODEL), BF16),
            pltpu.VMEM((tq, D_MODEL), BF16),
            pltpu.VMEM((tq, D_MODEL), F32),
            pltpu.VMEM((tq, D_MODEL), F32),
            pltpu.VMEM((NH, DH, 2 * DH), F32),
            pltpu.VMEM((NH, MLSTM_CHUNK), F32),
            pltpu.VMEM((n_items, MLSTM_CHUNK, MLSTM_CHUNK + DH), BF16),
            pltpu.VMEM((n_items, DH, MLSTM_CHUNK), BF16),
            pltpu.VMEM((2 * NH, MLSTM_CHUNK), F32),
        ],
        compiler_params=pltpu.CompilerParams(
            dimension_semantics=("arbitrary",),
            vmem_limit_bytes=V7X_VMEM_LIMIT),
    )(x, w_in_t, b_gate2d, w_pool, pool_scale, w_out, ln_g, ln_b, state_c, sq, skd, sv, siw)


def _ffn_kernel(x1p_ref, w1_ref, b1_ref, w2_ref, b2_ref, g_ref, be_ref,
                xs_ref, ypool_ref, sv_ref, siw_ref, sinter_ref, den_ref, floor_ref, osig_ref,
                wo_ref, g1_ref, bb1_ref,
                yp_ref, ys_ref, pre_ref, x1s_ref, *, n_prompt_tiles):
    def residual_plus_mlp(x1_ref):
        acc = None
        for c in range(D_FF // FF_CHUNK):
            cols = slice(c * FF_CHUNK, (c + 1) * FF_CHUNK)
            hid = jnp.maximum(_dot(x1_ref[...], w1_ref[:, cols]) + b1_ref[:, cols], 0.0)
            part = _dot(hid * hid, w2_ref[cols, :])
            acc = part if acc is None else acc + part
        return ALPHA * x1_ref[...] + (acc + b2_ref[...])

    step = pl.program_id(0)

    @pl.when(step == 0)
    def _no_tile_yet():
        pre_ref[...] = jnp.zeros(pre_ref.shape, F32)
        hh = (sv_ref[...] + siw_ref[...] * sinter_ref[...]) / jnp.maximum(jnp.abs(den_ref[...]), floor_ref[...])
        mixin = jnp.concatenate([ypool_ref[...], osig_ref[...] * hh], axis=-1)
        x1s_ref[...] = _layer_norm(ALPHA * xs_ref[:, 0, :] + _dot(mixin, wo_ref[...]),
                                   g1_ref[...], bb1_ref[...])

    @pl.when(step < n_prompt_tiles)
    def _prompt():
        yp_ref[...] = _layer_norm(pre_ref[...], g_ref[...], be_ref[...])
        pre_ref[...] = residual_plus_mlp(x1p_ref)

    @pl.when(step == n_prompt_tiles)
    def _sample():
        yp_ref[...] = _layer_norm(pre_ref[...], g_ref[...], be_ref[...])
        ys_ref[:, 0, :] = _layer_norm(residual_plus_mlp(x1s_ref), g_ref[...], be_ref[...])


def _ffn(x1p, w1, b1, w2, b2, ln_g, ln_b, xs, sample_rows, w_out, ln1_g, ln1_b):
    tile = FFN_TILE
    n_tok = x1p.shape[0]
    assert n_tok % tile == 0
    n_tiles = n_tok // tile
    n_seq = xs.shape[0]
    const2 = lambda i: (0, 0)
    ptile = lambda i: (jnp.minimum(i, n_tiles - 1), 0)
    once = lambda shape: pl.BlockSpec(shape, lambda i: (0,) * len(shape), pipeline_mode=pl.Buffered(1))
    return pl.pallas_call(
        functools.partial(_ffn_kernel, n_prompt_tiles=n_tiles),
        name="ffn_ln2",
        grid=(n_tiles + 1,),
        in_specs=[
            pl.BlockSpec((tile, D_MODEL), ptile),
            pl.BlockSpec(w1.shape, const2, pipeline_mode=pl.Buffered(1)),
            pl.BlockSpec(b1.shape, const2),
            pl.BlockSpec(w2.shape, const2, pipeline_mode=pl.Buffered(1)),
            pl.BlockSpec(b2.shape, const2),
            pl.BlockSpec(ln_g.shape, const2),
            pl.BlockSpec(ln_b.shape, const2),
            once(xs.shape),
        ] + [once(r.shape) for r in sample_rows] + [
            once(w_out.shape),
            pl.BlockSpec(ln1_g.shape, const2),
            pl.BlockSpec(ln1_b.shape, const2),
        ],
        out_specs=[pl.BlockSpec((tile, D_MODEL), lambda i: (jnp.maximum(i - 1, 0), 0)),
                   pl.BlockSpec((n_seq, 1, D_MODEL), lambda i: (0, 0, 0))],
        out_shape=[jax.ShapeDtypeStruct((n_tok, D_MODEL), F32),
                   jax.ShapeDtypeStruct((n_seq, 1, D_MODEL), F32)],
        scratch_shapes=[pltpu.VMEM((tile, D_MODEL), F32),
                        pltpu.VMEM((n_seq, D_MODEL), F32)],
        compiler_params=pltpu.CompilerParams(
            dimension_semantics=("arbitrary",),
            vmem_limit_bytes=V7X_VMEM_LIMIT),
    )(x1p, w1, b1, w2, b2, ln_g, ln_b, xs, *sample_rows, w_out, ln1_g, ln1_b)


def _sample_proj_kernel(x_ref, wt_ref, bg_ref, sp_ref, n_ref, m_ref, wp_ref, ps_ref,
                        ypool_ref, pool_ref, q_ref, kd_ref, v_ref, iw_ref, sv_ref, den_ref,
                        floor_ref, osig_ref, nout_ref, mout_ref):
    x = x_ref[:, 0, :]
    nb = x.shape[0]
    proj = _dot_nt(x, wt_ref[0:CUT_G, :])
    u = proj[:, 0:D_POOL]
    gates = _dot_nt(wt_ref[CUT_G:CUT_G + 2 * NH, :], x) + _gate_bias_rows(bg_ref, nb)
    ig_r = gates[0:NH]
    inter_r = jax.nn.log_sigmoid(gates[NH:2 * NH]) + m_ref[...]
    m_t_r = jnp.maximum(inter_r, ig_r)
    dw_r = jnp.exp(ig_r - m_t_r)
    iw_r = jnp.exp(inter_r - m_t_r)
    floor_r = jnp.exp(-m_t_r)
    mout_ref[...] = m_t_r

    def as_column(row):
        return jnp.broadcast_to(row, (DH, nb)).T

    for g, w in enumerate(POOL_WINDOWS):
        lo = g * POOL_GC
        u_g = u[:, lo:lo + POOL_GC]
        acc = u_g
        for r in range(POOL_PAD - w, POOL_BUF):
            acc = acc + sp_ref[r, :, lo:lo + POOL_GC]
        pooled = acc / float(min(PAST_LEN + 1, w)) - u_g
        ypool_ref[:, lo:lo + POOL_GC] = _dot(pooled, wp_ref[g]) * ps_ref[:, lo:lo + POOL_GC]
    pool_ref[0:POOL_BUF - 1] = sp_ref[1:POOL_BUF]
    pool_ref[POOL_BUF - 1] = u

    for h in range(NH):
        col = slice(h * DH, (h + 1) * DH)
        dw = as_column(dw_r[h:h + 1])
        iw = as_column(iw_r[h:h + 1])
        q = proj[:, D_POOL + h * DH:D_POOL + (h + 1) * DH]
        k = proj[:, CUT_K + h * DH:CUT_K + (h + 1) * DH] * K_SCALE
        v = proj[:, CUT_V + h * DH:CUT_V + (h + 1) * DH]
        o = proj[:, CUT_V + D_MLSTM + h * DH:CUT_V + D_MLSTM + (h + 1) * DH]
        n_old = n_ref[:, h, :]
        s = jnp.sum(q * k, axis=-1, keepdims=True) * dw
        q_ref[:, col] = q
        kd_ref[:, col] = dw * k
        v_ref[:, col] = v
        iw_ref[:, col] = iw
        sv_ref[:, col] = s * v
        den_ref[:, col] = s + iw * jnp.sum(q * n_old, axis=-1, keepdims=True)
        floor_ref[:, col] = as_column(floor_r[h:h + 1])
        osig_ref[:, col] = jax.nn.sigmoid(o)
        nout_ref[:, h, :] = iw * n_old + dw * k


def _sample_proj(x, w_in_t, b_gate2d, pool_rows, n_state, m_rows, w_pool, pool_scale):
    nb = x.shape[0]
    wide = jax.ShapeDtypeStruct((nb, D_MLSTM), F32)
    vmem = pl.BlockSpec(memory_space=pltpu.VMEM)
    return pl.pallas_call(
        _sample_proj_kernel,
        name="sample_proj",
        in_specs=[vmem, vmem, pl.BlockSpec(memory_space=pltpu.SMEM), vmem, vmem, vmem, vmem, vmem],
        out_shape=([wide, jax.ShapeDtypeStruct(pool_rows.shape, F32)] + [wide] * 8
                   + [jax.ShapeDtypeStruct(n_state.shape, F32), jax.ShapeDtypeStruct(m_rows.shape, F32)]),
        compiler_params=pltpu.CompilerParams(vmem_limit_bytes=V7X_VMEM_LIMIT),
    )(x, w_in_t, b_gate2d, pool_rows, n_state, m_rows, w_pool, pool_scale)


def kernel(x_prompt, x_sample, state_pool, state_C, state_n, state_m, w_in, b_gate, w_pool, pool_scale,
           w_out, ln1_g, ln1_b, w_ff1, b_ff1, w_ff2, b_ff2, ln2_g, ln2_b):
    assert w_in.shape[0] == DEPTH == 1
    bp, t_len, _ = x_prompt.shape
    bs = x_sample.shape[0]
    assert x_sample.shape[1] == 1

    w_t = jnp.transpose(w_in[0])
    wp = w_pool[0]
    ps = pool_scale[0].reshape(1, D_POOL)
    wo = w_out[0]
    g1, b1 = ln1_g[0].reshape(1, D_MODEL), ln1_b[0].reshape(1, D_MODEL)
    g2, b2 = ln2_g[0].reshape(1, D_MODEL), ln2_b[0].reshape(1, D_MODEL)
    wf1, wf2 = w_ff1[0], w_ff2[0]
    bf1, bf2 = b_ff1[0].reshape(1, D_FF), b_ff2[0].reshape(1, D_MODEL)

    (ypool, pool_s, q_s, kd_s, v_s, iw_s, sv_s, den_s, floor_s, osig_s, n_s, m_s_rows) = _sample_proj(
        x_sample, w_t, b_gate, jnp.transpose(state_pool[0], (1, 0, 2)),
        state_n[0], jnp.transpose(state_m[0]), wp, ps)

    x1_p, pool_p, c_p, n_p, m_rep, c_s, inter_blk = _prompt_mixer(
        x_prompt, w_t, b_gate, wp, ps, wo, g1, b1, state_C, q_s, kd_s, v_s, iw_s)
    m_p = m_rep[:, :, 0].reshape(DEPTH, bp, NH)

    sample_rows = (ypool, sv_s, iw_s, inter_blk.reshape(bs, D_MLSTM), den_s, floor_s, osig_s)
    y_p, y_s = _ffn(x1_p.reshape(bp * t_len, D_MODEL), wf1, bf1, wf2, bf2, g2, b2,
                    x_sample, sample_rows, wo, g1, b1)

    return (y_p.reshape(bp, t_len, D_MODEL), y_s,
            pool_p, c_p, n_p, m_p,
            jnp.transpose(pool_s, (1, 0, 2))[None], c_s,
            n_s[None], jnp.transpose(m_s_rows)[None])
```

```python
import functools

import jax
import jax.numpy as jnp
from jax import lax
from jax.experimental import pallas as pl
from jax.experimental.pallas import tpu as pltpu

F32 = jnp.float32
BF16 = jnp.bfloat16

D_MODEL = 1024
D_POOL = D_MODEL // 2
D_MLSTM = D_MODEL - D_POOL
POOL_WINDOWS = (2, 4, 8, 16)
POOL_GC = D_POOL // len(POOL_WINDOWS)
POOL_BUF = max(POOL_WINDOWS) - 1
POOL_PAD = POOL_BUF + 1
NH = 4
DH = D_MLSTM // NH
D_FF = 4 * D_MODEL
DEPTH = 1
PAST_LEN = 16384
ALPHA = (2.0 * DEPTH) ** 0.25
LN_EPS = 1e-5
K_SCALE = DH ** -0.5

MLSTM_CHUNK = 128
MIX_TILE = 512
FFN_TILE = 512
FF_CHUNK = 2048
SAMPLE_SEQS = 4
PROJ_COLS = 256
LANES = 128
assert POOL_GC == LANES and DH == LANES
CUT_K = D_POOL + D_MLSTM
CUT_V = CUT_K + D_MLSTM
CUT_G = D_POOL + 4 * D_MLSTM
LB_Q = D_POOL // LANES
LB_K = CUT_K // LANES
LB_V = CUT_V // LANES
LB_O = LB_V + NH
LB_G = CUT_G // LANES
N_LB = LB_G + 1
V7X_VMEM_LIMIT = 62 * 1024 * 1024


def _layer_norm(y, g, b):
    mu = jnp.mean(y, axis=-1, keepdims=True)
    yc = y - mu
    var = jnp.mean(yc * yc, axis=-1, keepdims=True)
    return yc * lax.rsqrt(var + LN_EPS) * g + b


def _dot(a, b):
    return jnp.dot(a, b, preferred_element_type=F32)


def _dot_nt(a, b):
    return lax.dot_general(a, b, (((1,), (1,)), ((), ())), preferred_element_type=F32)


def _twice(row):
    return jnp.concatenate([row, row], axis=-1)


def _gate_bias_rows(bg_ref, width):
    row = lax.broadcasted_iota(jnp.int32, (2 * NH, width), 0)
    bias = jnp.zeros((2 * NH, width), F32)
    for g in range(2 * NH):
        bias = jnp.where(row == g, bg_ref[0, g], bias)
    return bias


def _mixer_kernel(xn_ref, wt_ref, bg_ref, wp_ref, ps_ref, wo_ref, g1_ref, b1_ref,
                  sc_ref, sq_ref, skd_ref, sv_ref, siw_ref,
                  x1_ref, pool_ref, c_ref, n_ref, m_ref, scout_ref, sinter_ref,
                  win_ref, wob_ref, xb_ref, pm_a, pm_b, ext_ref, mix_a, mix_b, res_a, res_b, caug_ref, mst_ref,
                  lhs_ref, ktw_ref, gbias_ref, *, tiles_per_seq, n_tiles):
    tq = MIX_TILE
    L = MLSTM_CHUNK
    step = pl.program_id(0)
    t_idx = lax.rem(step + tiles_per_seq - 1, tiles_per_seq)
    last_t = tiles_per_seq - 1
    parity = lax.rem(step, 2)

    def project(pm_next):
        xb_ref[...] = xn_ref[0].astype(BF16)

        def piece(c0, c1):
            res = _dot(xb_ref[...], win_ref[:, c0:c1])
            for i in range((c1 - c0) // LANES):
                pm_next[c0 // LANES + i] = res[:, i * LANES:(i + 1) * LANES]

        n_cols = N_LB * LANES
        return [functools.partial(piece, c0, min(c0 + PROJ_COLS, n_cols))
                for c0 in range(0, n_cols, PROJ_COLS)]

    @pl.when(step == 0)
    def _first_step():
        for j in range(CUT_G // PROJ_COLS):
            win_ref[:, j * PROJ_COLS:(j + 1) * PROJ_COLS] = (
                wt_ref[j * PROJ_COLS:(j + 1) * PROJ_COLS, :].T.astype(BF16))
        gate_rows = jnp.concatenate([wt_ref[CUT_G:CUT_G + 2 * NH, :],
                                     jnp.zeros((LANES - 2 * NH, D_MODEL), F32)], axis=0)
        win_ref[:, CUT_G:CUT_G + LANES] = gate_rows.T.astype(BF16)
        wob_ref[...] = wo_ref[...].astype(BF16)
        gbias_ref[...] = _gate_bias_rows(bg_ref, L)
        mix_b[...] = jnp.zeros(mix_b.shape, BF16)
        res_b[...] = jnp.zeros(res_b.shape, F32)
        res_a[...] = ALPHA * xn_ref[0]
        for piece in project(pm_a):
            piece()

    @pl.when(t_idx == 0)
    def _init():
        ext_ref[:, 0:POOL_PAD, :] = jnp.zeros((len(POOL_WINDOWS), POOL_PAD, POOL_GC), F32)
        caug_ref[...] = jnp.zeros(caug_ref.shape, F32)
        mst_ref[...] = jnp.zeros(mst_ref.shape, F32)

    tt = lax.broadcasted_iota(jnp.int32, (L, L), 0)
    ss = lax.broadcasted_iota(jnp.int32, (L, L), 1)
    causal = ss <= tt
    diag = ss == tt

    def finish(mix_prev, res):
        x1_ref[0] = _layer_norm(res[...] + _dot(mix_prev[...], wob_ref[...]), g1_ref[...], b1_ref[...])

    def run(pm_next, pm_cur, mix_cur, mix_prev, res, sample_half):
        finish(mix_prev, res)
        pieces = project(pm_next)
        n_chunks = tq // L
        n_slots = len(POOL_WINDOWS) + 2 * n_chunks * NH
        slot_of = [(k * n_slots) // len(pieces) for k in range(len(pieces))]
        slot = [0]

        def next_slot():
            for k, piece in enumerate(pieces):
                if slot_of[k] == slot[0]:
                    piece()
            slot[0] += 1

        pos = t_idx * tq + lax.broadcasted_iota(jnp.int32, (tq, POOL_GC), 0)
        for g, w in enumerate(POOL_WINDOWS):
            lo = g * POOL_GC
            u_g = pm_cur[g]
            ext_ref[g, POOL_PAD:POOL_PAD + tq, :] = u_g
            s = ext_ref[g]
            k = 1
            while k < w:
                s = s + pltpu.roll(s, k, axis=0)
                k *= 2
            cnt = jnp.minimum(pos + 1, w).astype(F32)
            pooled = s[POOL_PAD:, :] / cnt - u_g
            mix_cur[:, lo:lo + POOL_GC] = (_dot(pooled, wp_ref[g]) * ps_ref[:, lo:lo + POOL_GC]).astype(BF16)
            ext_ref[g, 0:POOL_PAD, :] = ext_ref[g, tq:tq + POOL_PAD, :]
            next_slot()

        lane = lax.broadcasted_iota(jnp.int32, (NH, L), 1)
        gate_bias = gbias_ref[...]
        m_prev = mst_ref[...]
        chunk_rows = []
        for c in range(n_chunks):
            rows = slice(c * L, (c + 1) * L)
            gates = pm_cur[LB_G, rows, :].T[0:2 * NH, :] + gate_bias
            lf = jax.nn.log_sigmoid(gates[NH:2 * NH])
            b = lf
            k = 1
            while k < L:
                b = b + jnp.where(lane >= k, pltpu.roll(b, k, axis=1), 0.0)
                k *= 2
            a = gates[0:NH] - b
            cmax = a
            k = 1
            while k < L:
                cmax = jnp.maximum(cmax, jnp.where(lane >= k, pltpu.roll(cmax, k, axis=1), -jnp.inf))
                k *= 2
            amax = jnp.max(a, axis=-1, keepdims=True)
            big_m = jnp.maximum(cmax, m_prev)
            mm = jnp.maximum(m_prev, amax)
            chunk_rows.append(dict(
                a=a, big_m=big_m,
                iw=jnp.exp(m_prev - big_m),
                floor=jnp.exp(-(b + big_m)),
                w_loc=jnp.exp(a - amax),
                g_state=jnp.exp(m_prev - mm),
                f_state=jnp.exp(amax - mm)))
            m_prev = jnp.sum(lf, axis=-1, keepdims=True) + mm
        mst_ref[...] = m_prev

        def as_column(row):
            return jnp.broadcast_to(row, (L, L)).T

        pair = (step - 1) // 2
        half_rows = slice(sample_half * SAMPLE_SEQS, (sample_half + 1) * SAMPLE_SEQS)
        pad = jnp.zeros((DH - SAMPLE_SEQS, DH), F32)

        def step_rows(ref):
            rows8 = ref[pl.ds(pl.multiple_of(pair * 2 * SAMPLE_SEQS, 2 * SAMPLE_SEQS), 2 * SAMPLE_SEQS), :]
            return rows8[half_rows]

        s_q, s_kd, s_v, s_iw = (step_rows(r) for r in (sq_ref, skd_ref, sv_ref, siw_ref))

        def sample_item(h, j):
            col = slice(h * DH, (h + 1) * DH)
            c_old = sc_ref[0, j, h]
            row = sample_half * SAMPLE_SEQS + j
            sinter_ref[0, row:row + 1, col] = jnp.sum(sample_q_t[h][:, j:j + 1] * c_old, axis=0, keepdims=True)
            scout_ref[0, j, h] = s_iw[j:j + 1, col] * c_old + sample_kd_t[h][:, j:j + 1] * s_v[j:j + 1, col]

        sample_q_t = [jnp.concatenate([s_q[:, h * DH:(h + 1) * DH], pad], axis=0).T for h in range(NH)]
        sample_kd_t = [jnp.concatenate([s_kd[:, h * DH:(h + 1) * DH], pad], axis=0).T for h in range(NH)]
        sample_items = [functools.partial(sample_item, h, j) for h in range(NH) for j in range(SAMPLE_SEQS)]

        for c in range(n_chunks):
            rows = slice(c * L, (c + 1) * L)
            cr = chunk_rows[c]
            for h in range(NH):
                i = c * NH + h
                q = pm_cur[LB_Q + h, rows, :]
                qb = q.astype(BF16)
                kt = pm_cur[LB_K + h, rows, :].T * K_SCALE
                p = jnp.exp(jnp.where(causal, cr["a"][h:h + 1], -jnp.inf) - as_column(cr["big_m"][h:h + 1]))
                lhs_ref[i, :, 0:L] = (_dot(qb, kt.astype(BF16)) * p).astype(BF16)
                ktw_ref[i] = (kt * cr["w_loc"][h:h + 1]).astype(BF16)
                lhs_ref[i, :, L:L + DH] = (q * as_column(cr["iw"][h:h + 1])).astype(BF16)
                if i % 2 == 0:
                    sample_items[i // 2]()
                next_slot()

        ones_blk = jnp.ones((L, DH), BF16)
        for c in range(n_chunks):
            rows = slice(c * L, (c + 1) * L)
            cr = chunk_rows[c]
            for h in range(NH):
                i = c * NH + h
                caug = caug_ref[h]
                v_aug = jnp.concatenate([pm_cur[LB_V + h, rows, :].astype(BF16), ones_blk], axis=1)
                comb = _dot(lhs_ref[i], jnp.concatenate([v_aug, caug.astype(BF16)], axis=0))
                hh = comb[:, :DH] / jnp.maximum(jnp.abs(comb[:, DH:]), as_column(cr["floor"][h:h + 1]))
                o = pm_cur[LB_O + h, rows, :]
                mix_cur[rows, D_POOL + h * DH:D_POOL + (h + 1) * DH] = (
                    jax.nn.sigmoid(o) * hh).astype(BF16)
                caug_ref[h] = (_twice(cr["g_state"][h:h + 1]) * caug
                               + _twice(cr["f_state"][h:h + 1]) * _dot(ktw_ref[i], v_aug))
                if i % 2 == 0:
                    sample_items[n_chunks * NH // 2 + i // 2]()
                next_slot()

        res[...] = ALPHA * xn_ref[0]

    @pl.when((parity == 0) & (step > 0) & (step <= n_tiles))
    def _even():
        run(pm_a, pm_b, mix_b, mix_a, res_a, 1)

    @pl.when((parity == 1) & (step <= n_tiles))
    def _odd():
        run(pm_b, pm_a, mix_a, mix_b, res_b, 0)

    @pl.when(step == n_tiles + 1)
    def _tail():
        finish(*((mix_b, res_b) if n_tiles % 2 == 0 else (mix_a, res_a)))

    @pl.when((t_idx == last_t) & (step > 0))
    def _final_state():
        for g in range(len(POOL_WINDOWS)):
            pool_ref[0, 0, :, g * POOL_GC:(g + 1) * POOL_GC] = ext_ref[g, tq + 1:tq + POOL_PAD, :]
        for h in range(NH):
            caug = caug_ref[h]
            c_ref[0, 0, h] = caug[:, :DH]
            n_ref[0, 0, h:h + 1, :] = jnp.sum(jnp.where(diag, caug[:, DH:], 0.0),
                                              axis=0, keepdims=True)
        m_ref[0] = mst_ref[...]


def _prompt_mixer(x, w_in_t, b_gate2d, w_pool, pool_scale, w_out, ln_g, ln_b, state_c, sq, skd, sv, siw):
    bsz, t_len, _ = x.shape
    tq = MIX_TILE
    assert t_len % tq == 0 and tq % MLSTM_CHUNK == 0 and tq >= POOL_PAD
    nt = t_len // tq
    n_tiles = bsz * nt
    n_seq = sq.shape[0]
    assert n_seq == n_tiles * SAMPLE_SEQS and n_tiles % 2 == 0
    assert SAMPLE_SEQS * NH == (tq // MLSTM_CHUNK) * NH
    n_items = (tq // MLSTM_CHUNK) * NH
    const2 = lambda i: (0, 0)
    nxt = lambda i: jnp.minimum(i, n_tiles - 1)
    cur = lambda i: jnp.clip(i - 1, 0, n_tiles - 1)
    fin = lambda i: jnp.clip(i - 2, 0, n_tiles - 1)
    return pl.pallas_call(
        functools.partial(_mixer_kernel, tiles_per_seq=nt, n_tiles=n_tiles),
        name="prompt_mixer",
        grid=(n_tiles + 2,),
        in_specs=[
            pl.BlockSpec((1, tq, D_MODEL), lambda i: (nxt(i) // nt, nxt(i) % nt, 0)),
            pl.BlockSpec(w_in_t.shape, const2, pipeline_mode=pl.Buffered(1)),
            pl.BlockSpec(memory_space=pltpu.SMEM),
            pl.BlockSpec(w_pool.shape, lambda i: (0, 0, 0)),
            pl.BlockSpec(pool_scale.shape, const2),
            pl.BlockSpec(w_out.shape, const2, pipeline_mode=pl.Buffered(1)),
            pl.BlockSpec(ln_g.shape, const2),
            pl.BlockSpec(ln_b.shape, const2),
            pl.BlockSpec((1, SAMPLE_SEQS, NH, DH, DH), lambda i: (0, cur(i), 0, 0, 0)),
            pl.BlockSpec(sq.shape, const2, pipeline_mode=pl.Buffered(1)),
            pl.BlockSpec(skd.shape, const2, pipeline_mode=pl.Buffered(1)),
            pl.BlockSpec(sv.shape, const2, pipeline_mode=pl.Buffered(1)),
            pl.BlockSpec(siw.shape, const2, pipeline_mode=pl.Buffered(1)),
        ],
        out_specs=[
            pl.BlockSpec((1, tq, D_MODEL), lambda i: (fin(i) // nt, fin(i) % nt, 0)),
            pl.BlockSpec((1, 1, POOL_BUF, D_POOL), lambda i: (0, cur(i) // nt, 0, 0)),
            pl.BlockSpec((1, 1, NH, DH, DH), lambda i: (0, cur(i) // nt, 0, 0, 0)),
            pl.BlockSpec((1, 1, NH, DH), lambda i: (0, cur(i) // nt, 0, 0)),
            pl.BlockSpec((1, NH, MLSTM_CHUNK), lambda i: (cur(i) // nt, 0, 0)),
            pl.BlockSpec((1, SAMPLE_SEQS, NH, DH, DH), lambda i: (0, cur(i), 0, 0, 0)),
            pl.BlockSpec((1, 2 * SAMPLE_SEQS, D_MLSTM), lambda i: (cur(i) // 2, 0, 0)),
        ],
        out_shape=[
            jax.ShapeDtypeStruct((bsz, t_len, D_MODEL), F32),
            jax.ShapeDtypeStruct((DEPTH, bsz, POOL_BUF, D_POOL), F32),
            jax.ShapeDtypeStruct((DEPTH, bsz, NH, DH, DH), F32),
            jax.ShapeDtypeStruct((DEPTH, bsz, NH, DH), F32),
            jax.ShapeDtypeStruct((bsz, NH, MLSTM_CHUNK), F32),
            jax.ShapeDtypeStruct(state_c.shape, F32),
            jax.ShapeDtypeStruct((n_tiles // 2, 2 * SAMPLE_SEQS, D_MLSTM), F32),
        ],
        scratch_shapes=[
            pltpu.VMEM((D_MODEL, N_LB * LANES), BF16),
            pltpu.VMEM((D_MODEL, D_MODEL), BF16),
            pltpu.VMEM((tq, D_MODEL), BF16),
            pltpu.VMEM((N_LB, tq, LANES), F32),
            pltpu.VMEM((N_LB, tq, LANES), F32),
            pltpu.VMEM((len(POOL_WINDOWS), POOL_PAD + tq, POOL_GC), F32),
            pltpu.VMEM((tq, D_MODEL), BF16),
            pltpu.VMEM((tq, D_MODEL), BF16),
            pltpu.VMEM((tq, D_MODEL), F32),
            pltpu.VMEM((tq, D_MODEL), F32),
            pltpu.VMEM((NH, DH, 2 * DH), F32),
            pltpu.VMEM((NH, MLSTM_CHUNK), F32),
            pltpu.VMEM((n_items, MLSTM_CHUNK, MLSTM_CHUNK + DH), BF16),
            pltpu.VMEM((n_items, DH, MLSTM_CHUNK), BF16),
            pltpu.VMEM((2 * NH, MLSTM_CHUNK), F32),
        ],
        compiler_params=pltpu.CompilerParams(
            dimension_semantics=("arbitrary",),
            vmem_limit_bytes=V7X_VMEM_LIMIT),
    )(x, w_in_t, b_gate2d, w_pool, pool_scale, w_out, ln_g, ln_b, state_c, sq, skd, sv, siw)


def _ffn_kernel(x1p_hbm, w1_ref, b1_ref, w2_ref, b2_ref, g_ref, be_ref,
                xs_ref, ypool_ref, sv_ref, siw_ref, sinter_ref, den_ref, floor_ref, osig_ref,
                wo_ref, g1_ref, bb1_ref,
                yp_hbm, ys_ref, pre_ref, x1s_ref, last_sem, *, n_prompt_tiles):
    def residual_plus_mlp(x1_ref):
        acc = None
        for c in range(D_FF // FF_CHUNK):
            cols = slice(c * FF_CHUNK, (c + 1) * FF_CHUNK)
            hid = jnp.maximum(_dot(x1_ref[...], w1_ref[:, cols]) + b1_ref[:, cols], 0.0)
            part = _dot(hid * hid, w2_ref[cols, :])
            acc = part if acc is None else acc + part
        return ALPHA * x1_ref[...] + (acc + b2_ref[...])

    tile = pre_ref.shape[0]
    pre_ref[...] = jnp.zeros(pre_ref.shape, F32)
    hh = (sv_ref[...] + siw_ref[...] * sinter_ref[...]) / jnp.maximum(jnp.abs(den_ref[...]), floor_ref[...])
    mixin = jnp.concatenate([ypool_ref[...], osig_ref[...] * hh], axis=-1)
    x1s_ref[...] = _layer_norm(ALPHA * xs_ref[:, 0, :] + _dot(mixin, wo_ref[...]),
                               g1_ref[...], bb1_ref[...])

    def tile_step(x_ref, y_ref):
        y_ref[...] = _layer_norm(pre_ref[...], g_ref[...], be_ref[...])
        pre_ref[...] = residual_plus_mlp(x_ref)

    pltpu.emit_pipeline(
        tile_step,
        grid=(n_prompt_tiles,),
        in_specs=[pl.BlockSpec((tile, D_MODEL), lambda i: (i, 0))],
        out_specs=[pl.BlockSpec((tile, D_MODEL), lambda i: (jnp.maximum(i - 1, 0), 0))],
    )(x1p_hbm, yp_hbm)

    pre_ref[...] = _layer_norm(pre_ref[...], g_ref[...], be_ref[...])
    last_tile = pltpu.make_async_copy(
        pre_ref, yp_hbm.at[pl.ds((n_prompt_tiles - 1) * tile, tile), :], last_sem.at[0])
    last_tile.start()
    ys_ref[:, 0, :] = _layer_norm(residual_plus_mlp(x1s_ref), g_ref[...], be_ref[...])
    last_tile.wait()


def _ffn(x1p, w1, b1, w2, b2, ln_g, ln_b, xs, sample_rows, w_out, ln1_g, ln1_b):
    tile = FFN_TILE
    n_tok = x1p.shape[0]
    assert n_tok % tile == 0
    n_tiles = n_tok // tile
    n_seq = xs.shape[0]
    vmem = pl.BlockSpec(memory_space=pltpu.VMEM)
    hbm = pl.BlockSpec(memory_space=pl.ANY)
    return pl.pallas_call(
        functools.partial(_ffn_kernel, n_prompt_tiles=n_tiles),
        name="ffn_ln2",
        in_specs=[hbm] + [vmem] * (10 + len(sample_rows)),
        out_specs=[hbm, vmem],
        out_shape=[jax.ShapeDtypeStruct((n_tok, D_MODEL), F32),
                   jax.ShapeDtypeStruct((n_seq, 1, D_MODEL), F32)],
        scratch_shapes=[pltpu.VMEM((tile, D_MODEL), F32),
                        pltpu.VMEM((n_seq, D_MODEL), F32),
                        pltpu.SemaphoreType.DMA((1,))],
        compiler_params=pltpu.CompilerParams(vmem_limit_bytes=V7X_VMEM_LIMIT),
    )(x1p, w1, b1, w2, b2, ln_g, ln_b, xs, *sample_rows, w_out, ln1_g, ln1_b)


def _sample_proj_kernel(x_ref, wt_ref, bg_ref, sp_ref, n_ref, m_ref, wp_ref, ps_ref,
                        ypool_ref, pool_ref, q_ref, kd_ref, v_ref, iw_ref, sv_ref, den_ref,
                        floor_ref, osig_ref, nout_ref, mout_ref):
    x = x_ref[:, 0, :]
    nb = x.shape[0]
    proj = _dot_nt(x, wt_ref[0:CUT_G, :])
    u = proj[:, 0:D_POOL]
    gates = _dot_nt(wt_ref[CUT_G:CUT_G + 2 * NH, :], x) + _gate_bias_rows(bg_ref, nb)
    ig_r = gates[0:NH]
    inter_r = jax.nn.log_sigmoid(gates[NH:2 * NH]) + m_ref[...]
    m_t_r = jnp.maximum(inter_r, ig_r)
    dw_r = jnp.exp(ig_r - m_t_r)
    iw_r = jnp.exp(inter_r - m_t_r)
    floor_r = jnp.exp(-m_t_r)
    mout_ref[...] = m_t_r

    def as_column(row):
        return jnp.broadcast_to(row, (DH, nb)).T

    for g, w in enumerate(POOL_WINDOWS):
        lo = g * POOL_GC
        u_g = u[:, lo:lo + POOL_GC]
        acc = u_g
        for r in range(POOL_PAD - w, POOL_BUF):
            acc = acc + sp_ref[r, :, lo:lo + POOL_GC]
        pooled = acc / float(min(PAST_LEN + 1, w)) - u_g
        ypool_ref[:, lo:lo + POOL_GC] = _dot(pooled, wp_ref[g]) * ps_ref[:, lo:lo + POOL_GC]
    pool_ref[0:POOL_BUF - 1] = sp_ref[1:POOL_BUF]
    pool_ref[POOL_BUF - 1] = u

    for h in range(NH):
        col = slice(h * DH, (h + 1) * DH)
        dw = as_column(dw_r[h:h + 1])
        iw = as_column(iw_r[h:h + 1])
        q = proj[:, D_POOL + h * DH:D_POOL + (h + 1) * DH]
        k = proj[:, CUT_K + h * DH:CUT_K + (h + 1) * DH] * K_SCALE
        v = proj[:, CUT_V + h * DH:CUT_V + (h + 1) * DH]
        o = proj[:, CUT_V + D_MLSTM + h * DH:CUT_V + D_MLSTM + (h + 1) * DH]
        n_old = n_ref[:, h, :]
        s = jnp.sum(q * k, axis=-1, keepdims=True) * dw
        q_ref[:, col] = q
        kd_ref[:, col] = dw * k
        v_ref[:, col] = v
        iw_ref[:, col] = iw
        sv_ref[:, col] = s * v
        den_ref[:, col] = s + iw * jnp.sum(q * n_old, axis=-1, keepdims=True)
        floor_ref[:, col] = as_column(floor_r[h:h + 1])
        osig_ref[:, col] = jax.nn.sigmoid(o)
        nout_ref[:, h, :] = iw * n_old + dw * k


def _sample_proj(x, w_in_t, b_gate2d, pool_rows, n_state, m_rows, w_pool, pool_scale):
    nb = x.shape[0]
    wide = jax.ShapeDtypeStruct((nb, D_MLSTM), F32)
    vmem = pl.BlockSpec(memory_space=pltpu.VMEM)
    return pl.pallas_call(
        _sample_proj_kernel,
        name="sample_proj",
        in_specs=[vmem, vmem, pl.BlockSpec(memory_space=pltpu.SMEM), vmem, vmem, vmem, vmem, vmem],
        out_shape=([wide, jax.ShapeDtypeStruct(pool_rows.shape, F32)] + [wide] * 8
                   + [jax.ShapeDtypeStruct(n_state.shape, F32), jax.ShapeDtypeStruct(m_rows.shape, F32)]),
        compiler_params=pltpu.CompilerParams(vmem_limit_bytes=V7X_VMEM_LIMIT),
    )(x, w_in_t, b_gate2d, pool_rows, n_state, m_rows, w_pool, pool_scale)


def kernel(x_prompt, x_sample, state_pool, state_C, state_n, state_m, w_in, b_gate, w_pool, pool_scale,
           w_out, ln1_g, ln1_b, w_ff1, b_ff1, w_ff2, b_ff2, ln2_g, ln2_b):
    assert w_in.shape[0] == DEPTH == 1
    bp, t_len, _ = x_prompt.shape
    bs = x_sample.shape[0]
    assert x_sample.shape[1] == 1

    w_t = jnp.transpose(w_in[0])
    wp = w_pool[0]
    ps = pool_scale[0].reshape(1, D_POOL)
    wo = w_out[0]
    g1, b1 = ln1_g[0].reshape(1, D_MODEL), ln1_b[0].reshape(1, D_MODEL)
    g2, b2 = ln2_g[0].reshape(1, D_MODEL), ln2_b[0].reshape(1, D_MODEL)
    wf1, wf2 = w_ff1[0], w_ff2[0]
    bf1, bf2 = b_ff1[0].reshape(1, D_FF), b_ff2[0].reshape(1, D_MODEL)

    (ypool, pool_s, q_s, kd_s, v_s, iw_s, sv_s, den_s, floor_s, osig_s, n_s, m_s_rows) = _sample_proj(
        x_sample, w_t, b_gate, jnp.transpose(state_pool[0], (1, 0, 2)),
        state_n[0], jnp.transpose(state_m[0]), wp, ps)

    x1_p, pool_p, c_p, n_p, m_rep, c_s, inter_blk = _prompt_mixer(
        x_prompt, w_t, b_gate, wp, ps, wo, g1, b1, state_C, q_s, kd_s, v_s, iw_s)
    m_p = m_rep[:, :, 0].reshape(DEPTH, bp, NH)

    sample_rows = (ypool, sv_s, iw_s, inter_blk.reshape(bs, D_MLSTM), den_s, floor_s, osig_s)
    y_p, y_s = _ffn(x1_p.reshape(bp * t_len, D_MODEL), wf1, bf1, wf2, bf2, g2, b2,
                    x_sample, sample_rows, wo, g1, b1)

    return (y_p.reshape(bp, t_len, D_MODEL), y_s,
            pool_p, c_p, n_p, m_p,
            jnp.transpose(pool_s, (1, 0, 2))[None], c_s,
            n_s[None], jnp.transpose(m_s_rows)[None])
```

```python
import functools

import jax
import jax.numpy as jnp
from jax import lax
from jax.experimental import pallas as pl
from jax.experimental.pallas import tpu as pltpu

F32 = jnp.float32
BF16 = jnp.bfloat16

D_MODEL = 1024
D_POOL = D_MODEL // 2
D_MLSTM = D_MODEL - D_POOL
POOL_WINDOWS = (2, 4, 8, 16)
POOL_GC = D_POOL // len(POOL_WINDOWS)
POOL_BUF = max(POOL_WINDOWS) - 1
POOL_PAD = POOL_BUF + 1
NH = 4
DH = D_MLSTM // NH
D_FF = 4 * D_MODEL
DEPTH = 1
PAST_LEN = 16384
ALPHA = (2.0 * DEPTH) ** 0.25
LN_EPS = 1e-5
K_SCALE = DH ** -0.5

MLSTM_CHUNK = 128
MIX_TILE = 512
FFN_TILE = 512
FF_CHUNK = 2048
FIRST_FF_CHUNK = 1024
SAMPLE_SEQS = 4
PROJ_COLS = 256
LANES = 128
assert POOL_GC == LANES and DH == LANES
CUT_K = D_POOL + D_MLSTM
CUT_V = CUT_K + D_MLSTM
CUT_G = D_POOL + 4 * D_MLSTM
LB_Q = D_POOL // LANES
LB_K = CUT_K // LANES
LB_V = CUT_V // LANES
LB_O = LB_V + NH
LB_G = CUT_G // LANES
N_LB = LB_G + 1
V7X_VMEM_LIMIT = 62 * 1024 * 1024


def _layer_norm(y, g, b):
    mu = jnp.mean(y, axis=-1, keepdims=True)
    yc = y - mu
    var = jnp.mean(yc * yc, axis=-1, keepdims=True)
    return yc * lax.rsqrt(var + LN_EPS) * g + b


def _dot(a, b):
    return jnp.dot(a, b, preferred_element_type=F32)


def _dot_nt(a, b):
    return lax.dot_general(a, b, (((1,), (1,)), ((), ())), preferred_element_type=F32)


def _twice(row):
    return jnp.concatenate([row, row], axis=-1)


def _gate_bias_rows(bg_ref, width):
    row = lax.broadcasted_iota(jnp.int32, (2 * NH, width), 0)
    bias = jnp.zeros((2 * NH, width), F32)
    for g in range(2 * NH):
        bias = jnp.where(row == g, bg_ref[0, g], bias)
    return bias


def _mixer_kernel(xn_ref, wt_ref, bg_ref, wp_ref, ps_ref, wo_ref, g1_ref, b1_ref,
                  sc_ref, sq_ref, skd_ref, sv_ref, siw_ref,
                  x1_ref, pool_ref, c_ref, n_ref, m_ref, scout_ref, sinter_ref,
                  win_ref, wob_ref, xb_ref, pm_a, pm_b, ext_ref, mix_a, mix_b, res_a, res_b, caug_ref, mst_ref,
                  lhs_ref, ktw_ref, gbias_ref, *, tiles_per_seq, n_tiles):
    tq = MIX_TILE
    L = MLSTM_CHUNK
    step = pl.program_id(0)
    t_idx = lax.rem(step + tiles_per_seq - 1, tiles_per_seq)
    last_t = tiles_per_seq - 1
    parity = lax.rem(step, 2)

    def project(pm_next):
        xb_ref[...] = xn_ref[0].astype(BF16)

        def piece(c0, c1):
            res = _dot(xb_ref[...], win_ref[:, c0:c1])
            for i in range((c1 - c0) // LANES):
                pm_next[c0 // LANES + i] = res[:, i * LANES:(i + 1) * LANES]

        n_cols = N_LB * LANES
        return [functools.partial(piece, c0, min(c0 + PROJ_COLS, n_cols))
                for c0 in range(0, n_cols, PROJ_COLS)]

    @pl.when(step == 0)
    def _first_step():
        for j in range(CUT_G // PROJ_COLS):
            win_ref[:, j * PROJ_COLS:(j + 1) * PROJ_COLS] = (
                wt_ref[j * PROJ_COLS:(j + 1) * PROJ_COLS, :].T.astype(BF16))
        gate_rows = jnp.concatenate([wt_ref[CUT_G:CUT_G + 2 * NH, :],
                                     jnp.zeros((LANES - 2 * NH, D_MODEL), F32)], axis=0)
        win_ref[:, CUT_G:CUT_G + LANES] = gate_rows.T.astype(BF16)
        wob_ref[...] = wo_ref[...].astype(BF16)
        gbias_ref[...] = _gate_bias_rows(bg_ref, L)
        mix_b[...] = jnp.zeros(mix_b.shape, BF16)
        res_b[...] = jnp.zeros(res_b.shape, F32)
        res_a[...] = ALPHA * xn_ref[0]
        for piece in project(pm_a):
            piece()

    @pl.when(t_idx == 0)
    def _init():
        ext_ref[:, 0:POOL_PAD, :] = jnp.zeros((len(POOL_WINDOWS), POOL_PAD, POOL_GC), F32)
        caug_ref[...] = jnp.zeros(caug_ref.shape, F32)
        mst_ref[...] = jnp.zeros(mst_ref.shape, F32)

    tt = lax.broadcasted_iota(jnp.int32, (L, L), 0)
    ss = lax.broadcasted_iota(jnp.int32, (L, L), 1)
    causal = ss <= tt
    diag = ss == tt

    def finish(mix_prev, res):
        x1_ref[0] = _layer_norm(res[...] + _dot(mix_prev[...], wob_ref[...]), g1_ref[...], b1_ref[...])

    def run(pm_next, pm_cur, mix_cur, mix_prev, res, sample_half):
        finish(mix_prev, res)
        pieces = project(pm_next)
        n_chunks = tq // L
        n_slots = len(POOL_WINDOWS) + 2 * n_chunks * NH
        slot_of = [(k * n_slots) // len(pieces) for k in range(len(pieces))]
        slot = [0]

        def next_slot():
            for k, piece in enumerate(pieces):
                if slot_of[k] == slot[0]:
                    piece()
            slot[0] += 1

        pos = t_idx * tq + lax.broadcasted_iota(jnp.int32, (tq, POOL_GC), 0)
        for g, w in enumerate(POOL_WINDOWS):
            lo = g * POOL_GC
            u_g = pm_cur[g]
            ext_ref[g, POOL_PAD:POOL_PAD + tq, :] = u_g
            s = ext_ref[g]
            k = 1
            while k < w:
                s = s + pltpu.roll(s, k, axis=0)
                k *= 2
            cnt = jnp.minimum(pos + 1, w).astype(F32)
            pooled = s[POOL_PAD:, :] / cnt - u_g
            mix_cur[:, lo:lo + POOL_GC] = (_dot(pooled, wp_ref[g]) * ps_ref[:, lo:lo + POOL_GC]).astype(BF16)
            ext_ref[g, 0:POOL_PAD, :] = ext_ref[g, tq:tq + POOL_PAD, :]
            next_slot()

        lane = lax.broadcasted_iota(jnp.int32, (NH, L), 1)
        gate_bias = gbias_ref[...]
        m_prev = mst_ref[...]
        chunk_rows = []
        for c in range(n_chunks):
            rows = slice(c * L, (c + 1) * L)
            gates = pm_cur[LB_G, rows, :].T[0:2 * NH, :] + gate_bias
            lf = jax.nn.log_sigmoid(gates[NH:2 * NH])
            b = lf
            k = 1
            while k < L:
                b = b + jnp.where(lane >= k, pltpu.roll(b, k, axis=1), 0.0)
                k *= 2
            a = gates[0:NH] - b
            cmax = a
            k = 1
            while k < L:
                cmax = jnp.maximum(cmax, jnp.where(lane >= k, pltpu.roll(cmax, k, axis=1), -jnp.inf))
                k *= 2
            amax = jnp.max(a, axis=-1, keepdims=True)
            big_m = jnp.maximum(cmax, m_prev)
            mm = jnp.maximum(m_prev, amax)
            chunk_rows.append(dict(
                a=a, big_m=big_m,
                iw=jnp.exp(m_prev - big_m),
                floor=jnp.exp(-(b + big_m)),
                w_loc=jnp.exp(a - amax),
                g_state=jnp.exp(m_prev - mm),
                f_state=jnp.exp(amax - mm)))
            m_prev = jnp.sum(lf, axis=-1, keepdims=True) + mm
        mst_ref[...] = m_prev

        def as_column(row):
            return jnp.broadcast_to(row, (L, L)).T

        pair = (step - 1) // 2
        half_rows = slice(sample_half * SAMPLE_SEQS, (sample_half + 1) * SAMPLE_SEQS)
        pad = jnp.zeros((DH - SAMPLE_SEQS, DH), F32)

        def step_rows(ref):
            rows8 = ref[pl.ds(pl.multiple_of(pair * 2 * SAMPLE_SEQS, 2 * SAMPLE_SEQS), 2 * SAMPLE_SEQS), :]
            return rows8[half_rows]

        s_q, s_kd, s_v, s_iw = (step_rows(r) for r in (sq_ref, skd_ref, sv_ref, siw_ref))

        def sample_item(h, j):
            col = slice(h * DH, (h + 1) * DH)
            c_old = sc_ref[0, j, h]
            row = sample_half * SAMPLE_SEQS + j
            sinter_ref[0, row:row + 1, col] = jnp.sum(sample_q_t[h][:, j:j + 1] * c_old, axis=0, keepdims=True)
            scout_ref[0, j, h] = s_iw[j:j + 1, col] * c_old + sample_kd_t[h][:, j:j + 1] * s_v[j:j + 1, col]

        sample_q_t = [jnp.concatenate([s_q[:, h * DH:(h + 1) * DH], pad], axis=0).T for h in range(NH)]
        sample_kd_t = [jnp.concatenate([s_kd[:, h * DH:(h + 1) * DH], pad], axis=0).T for h in range(NH)]
        sample_items = [functools.partial(sample_item, h, j) for h in range(NH) for j in range(SAMPLE_SEQS)]

        for c in range(n_chunks):
            rows = slice(c * L, (c + 1) * L)
            cr = chunk_rows[c]
            for h in range(NH):
                i = c * NH + h
                q = pm_cur[LB_Q + h, rows, :]
                qb = q.astype(BF16)
                kt = pm_cur[LB_K + h, rows, :].T * K_SCALE
                p = jnp.exp(jnp.where(causal, cr["a"][h:h + 1], -jnp.inf) - as_column(cr["big_m"][h:h + 1]))
                lhs_ref[i, :, 0:L] = (_dot(qb, kt.astype(BF16)) * p).astype(BF16)
                ktw_ref[i] = (kt * cr["w_loc"][h:h + 1]).astype(BF16)
                lhs_ref[i, :, L:L + DH] = (q * as_column(cr["iw"][h:h + 1])).astype(BF16)
                if i % 2 == 0:
                    sample_items[i // 2]()
                next_slot()

        ones_blk = jnp.ones((L, DH), BF16)
        for c in range(n_chunks):
            rows = slice(c * L, (c + 1) * L)
            cr = chunk_rows[c]
            for h in range(NH):
                i = c * NH + h
                caug = caug_ref[h]
                v_aug = jnp.concatenate([pm_cur[LB_V + h, rows, :].astype(BF16), ones_blk], axis=1)
                comb = _dot(lhs_ref[i], jnp.concatenate([v_aug, caug.astype(BF16)], axis=0))
                hh = comb[:, :DH] / jnp.maximum(jnp.abs(comb[:, DH:]), as_column(cr["floor"][h:h + 1]))
                o = pm_cur[LB_O + h, rows, :]
                mix_cur[rows, D_POOL + h * DH:D_POOL + (h + 1) * DH] = (
                    jax.nn.sigmoid(o) * hh).astype(BF16)
                caug_ref[h] = (_twice(cr["g_state"][h:h + 1]) * caug
                               + _twice(cr["f_state"][h:h + 1]) * _dot(ktw_ref[i], v_aug))
                if i % 2 == 0:
                    sample_items[n_chunks * NH // 2 + i // 2]()
                next_slot()

        res[...] = ALPHA * xn_ref[0]

    @pl.when((parity == 0) & (step > 0) & (step <= n_tiles))
    def _even():
        run(pm_a, pm_b, mix_b, mix_a, res_a, 1)

    @pl.when((parity == 1) & (step <= n_tiles))
    def _odd():
        run(pm_b, pm_a, mix_a, mix_b, res_b, 0)

    @pl.when(step == n_tiles + 1)
    def _tail():
        finish(*((mix_b, res_b) if n_tiles % 2 == 0 else (mix_a, res_a)))

    @pl.when((t_idx == last_t) & (step > 0))
    def _final_state():
        for g in range(len(POOL_WINDOWS)):
            pool_ref[0, 0, :, g * POOL_GC:(g + 1) * POOL_GC] = ext_ref[g, tq + 1:tq + POOL_PAD, :]
        for h in range(NH):
            caug = caug_ref[h]
            c_ref[0, 0, h] = caug[:, :DH]
            n_ref[0, 0, h:h + 1, :] = jnp.sum(jnp.where(diag, caug[:, DH:], 0.0),
                                              axis=0, keepdims=True)
        m_ref[0] = mst_ref[...]


def _prompt_mixer(x, w_in_t, b_gate2d, w_pool, pool_scale, w_out, ln_g, ln_b, state_c, sq, skd, sv, siw):
    bsz, t_len, _ = x.shape
    tq = MIX_TILE
    assert t_len % tq == 0 and tq % MLSTM_CHUNK == 0 and tq >= POOL_PAD
    nt = t_len // tq
    n_tiles = bsz * nt
    n_seq = sq.shape[0]
    assert n_seq == n_tiles * SAMPLE_SEQS and n_tiles % 2 == 0
    assert SAMPLE_SEQS * NH == (tq // MLSTM_CHUNK) * NH
    n_items = (tq // MLSTM_CHUNK) * NH
    const2 = lambda i: (0, 0)
    nxt = lambda i: jnp.minimum(i, n_tiles - 1)
    cur = lambda i: jnp.clip(i - 1, 0, n_tiles - 1)
    fin = lambda i: jnp.clip(i - 2, 0, n_tiles - 1)
    return pl.pallas_call(
        functools.partial(_mixer_kernel, tiles_per_seq=nt, n_tiles=n_tiles),
        name="prompt_mixer",
        grid=(n_tiles + 2,),
        in_specs=[
            pl.BlockSpec((1, tq, D_MODEL), lambda i: (nxt(i) // nt, nxt(i) % nt, 0)),
            pl.BlockSpec(w_in_t.shape, const2, pipeline_mode=pl.Buffered(1)),
            pl.BlockSpec(memory_space=pltpu.SMEM),
            pl.BlockSpec(w_pool.shape, lambda i: (0, 0, 0)),
            pl.BlockSpec(pool_scale.shape, const2),
            pl.BlockSpec(w_out.shape, const2, pipeline_mode=pl.Buffered(1)),
            pl.BlockSpec(ln_g.shape, const2),
            pl.BlockSpec(ln_b.shape, const2),
            pl.BlockSpec((1, SAMPLE_SEQS, NH, DH, DH), lambda i: (0, cur(i), 0, 0, 0)),
            pl.BlockSpec(sq.shape, const2, pipeline_mode=pl.Buffered(1)),
            pl.BlockSpec(skd.shape, const2, pipeline_mode=pl.Buffered(1)),
            pl.BlockSpec(sv.shape, const2, pipeline_mode=pl.Buffered(1)),
            pl.BlockSpec(siw.shape, const2, pipeline_mode=pl.Buffered(1)),
        ],
        out_specs=[
            pl.BlockSpec((1, tq, D_MODEL), lambda i: (fin(i) // nt, fin(i) % nt, 0)),
            pl.BlockSpec((1, 1, POOL_BUF, D_POOL), lambda i: (0, cur(i) // nt, 0, 0)),
            pl.BlockSpec((1, 1, NH, DH, DH), lambda i: (0, cur(i) // nt, 0, 0, 0)),
            pl.BlockSpec((1, 1, NH, DH), lambda i: (0, cur(i) // nt, 0, 0)),
            pl.BlockSpec((1, NH, MLSTM_CHUNK), lambda i: (cur(i) // nt, 0, 0)),
            pl.BlockSpec((1, SAMPLE_SEQS, NH, DH, DH), lambda i: (0, cur(i), 0, 0, 0)),
            pl.BlockSpec((1, 2 * SAMPLE_SEQS, D_MLSTM), lambda i: (cur(i) // 2, 0, 0)),
        ],
        out_shape=[
            jax.ShapeDtypeStruct((bsz, t_len, D_MODEL), F32),
            jax.ShapeDtypeStruct((DEPTH, bsz, POOL_BUF, D_POOL), F32),
            jax.ShapeDtypeStruct((DEPTH, bsz, NH, DH, DH), F32),
            jax.ShapeDtypeStruct((DEPTH, bsz, NH, DH), F32),
            jax.ShapeDtypeStruct((bsz, NH, MLSTM_CHUNK), F32),
            jax.ShapeDtypeStruct(state_c.shape, F32),
            jax.ShapeDtypeStruct((n_tiles // 2, 2 * SAMPLE_SEQS, D_MLSTM), F32),
        ],
        scratch_shapes=[
            pltpu.VMEM((D_MODEL, N_LB * LANES), BF16),
            pltpu.VMEM((D_MODEL, D_MODEL), BF16),
            pltpu.VMEM((tq, D_MODEL), BF16),
            pltpu.VMEM((N_LB, tq, LANES), F32),
            pltpu.VMEM((N_LB, tq, LANES), F32),
            pltpu.VMEM((len(POOL_WINDOWS), POOL_PAD + tq, POOL_GC), F32),
            pltpu.VMEM((tq, D_MODEL), BF16),
            pltpu.VMEM((tq, D_MODEL), BF16),
            pltpu.VMEM((tq, D_MODEL), F32),
            pltpu.VMEM((tq, D_MODEL), F32),
            pltpu.VMEM((NH, DH, 2 * DH), F32),
            pltpu.VMEM((NH, MLSTM_CHUNK), F32),
            pltpu.VMEM((n_items, MLSTM_CHUNK, MLSTM_CHUNK + DH), BF16),
            pltpu.VMEM((n_items, DH, MLSTM_CHUNK), BF16),
            pltpu.VMEM((2 * NH, MLSTM_CHUNK), F32),
        ],
        compiler_params=pltpu.CompilerParams(
            dimension_semantics=("arbitrary",),
            vmem_limit_bytes=V7X_VMEM_LIMIT),
    )(x, w_in_t, b_gate2d, w_pool, pool_scale, w_out, ln_g, ln_b, state_c, sq, skd, sv, siw)


def _ffn_kernel(x1p_hbm, w1_hbm, b1_ref, w2_hbm, b2_ref, g_ref, be_ref,
                xs_ref, ypool_ref, sv_ref, siw_ref, sinter_ref, den_ref, floor_ref, osig_ref,
                wo_ref, g1_ref, bb1_ref,
                yp_hbm, ys_ref, pre_ref, x1s_ref, w1_ref, w2_ref, sems, *, n_prompt_tiles):
    tile = pre_ref.shape[0]
    half = D_FF // 2
    first_tile_in = pltpu.make_async_copy(x1p_hbm.at[pl.ds(0, tile), :], pre_ref, sems.at[0])
    w1_in = pltpu.make_async_copy(w1_hbm, w1_ref, sems.at[1])
    w2_in = [pltpu.make_async_copy(w2_hbm.at[pl.ds(k * half, half), :], w2_ref.at[pl.ds(k * half, half), :],
                                   sems.at[2 + k]) for k in range(2)]
    for copy in [first_tile_in, w1_in] + w2_in:
        copy.start()

    def residual_plus_mlp(x1_ref, chunk=FF_CHUNK, before_chunk=lambda c: None, before_second_dot=lambda c: None):
        acc = None
        for c in range(D_FF // chunk):
            cols = slice(c * chunk, (c + 1) * chunk)
            before_chunk(c)
            hid = jnp.maximum(_dot(x1_ref[...], w1_ref[:, cols]) + b1_ref[:, cols], 0.0)
            before_second_dot(c)
            part = _dot(hid * hid, w2_ref[cols, :])
            acc = part if acc is None else acc + part
        return ALPHA * x1_ref[...] + (acc + b2_ref[...])

    hh = (sv_ref[...] + siw_ref[...] * sinter_ref[...]) / jnp.maximum(jnp.abs(den_ref[...]), floor_ref[...])
    mixin = jnp.concatenate([ypool_ref[...], osig_ref[...] * hh], axis=-1)
    x1s_ref[...] = _layer_norm(ALPHA * xs_ref[:, 0, :] + _dot(mixin, wo_ref[...]),
                               g1_ref[...], bb1_ref[...])

    def wait_inputs(c):
        if c == 0:
            first_tile_in.wait()
            w1_in.wait()

    def wait_w2_half(c):
        if (c * FIRST_FF_CHUNK) % half == 0:
            w2_in[(c * FIRST_FF_CHUNK) // half].wait()

    pre_ref[...] = residual_plus_mlp(pre_ref, FIRST_FF_CHUNK, wait_inputs, wait_w2_half)

    def tile_step(x_ref, y_ref):
        y_ref[...] = _layer_norm(pre_ref[...], g_ref[...], be_ref[...])
        pre_ref[...] = residual_plus_mlp(x_ref)

    pltpu.emit_pipeline(
        tile_step,
        grid=(n_prompt_tiles - 1,),
        in_specs=[pl.BlockSpec((tile, D_MODEL), lambda i: (i + 1, 0))],
        out_specs=[pl.BlockSpec((tile, D_MODEL), lambda i: (i, 0))],
    )(x1p_hbm, yp_hbm)

    pre_ref[...] = _layer_norm(pre_ref[...], g_ref[...], be_ref[...])
    last_tile_out = pltpu.make_async_copy(
        pre_ref, yp_hbm.at[pl.ds((n_prompt_tiles - 1) * tile, tile), :], sems.at[4])
    last_tile_out.start()
    ys_ref[:, 0, :] = _layer_norm(residual_plus_mlp(x1s_ref), g_ref[...], be_ref[...])
    last_tile_out.wait()


def _ffn(x1p, w1, b1, w2, b2, ln_g, ln_b, xs, sample_rows, w_out, ln1_g, ln1_b):
    tile = FFN_TILE
    n_tok = x1p.shape[0]
    assert n_tok % tile == 0
    n_tiles = n_tok // tile
    n_seq = xs.shape[0]
    vmem = pl.BlockSpec(memory_space=pltpu.VMEM)
    hbm = pl.BlockSpec(memory_space=pl.ANY)
    return pl.pallas_call(
        functools.partial(_ffn_kernel, n_prompt_tiles=n_tiles),
        name="ffn_ln2",
        in_specs=[hbm, hbm, vmem, hbm] + [vmem] * (7 + len(sample_rows)),
        out_specs=[hbm, vmem],
        out_shape=[jax.ShapeDtypeStruct((n_tok, D_MODEL), F32),
                   jax.ShapeDtypeStruct((n_seq, 1, D_MODEL), F32)],
        scratch_shapes=[pltpu.VMEM((tile, D_MODEL), F32),
                        pltpu.VMEM((n_seq, D_MODEL), F32),
                        pltpu.VMEM(w1.shape, F32),
                        pltpu.VMEM(w2.shape, F32),
                        pltpu.SemaphoreType.DMA((5,))],
        compiler_params=pltpu.CompilerParams(vmem_limit_bytes=V7X_VMEM_LIMIT),
    )(x1p, w1, b1, w2, b2, ln_g, ln_b, xs, *sample_rows, w_out, ln1_g, ln1_b)


def _sample_proj_kernel(x_ref, wt_ref, bg_ref, sp_ref, n_ref, m_ref, wp_ref, ps_ref,
                        ypool_ref, pool_ref, q_ref, kd_ref, v_ref, iw_ref, sv_ref, den_ref,
                        floor_ref, osig_ref, nout_ref, mout_ref):
    x = x_ref[:, 0, :]
    nb = x.shape[0]
    proj = _dot_nt(x, wt_ref[0:CUT_G, :])
    u = proj[:, 0:D_POOL]
    gates = _dot_nt(wt_ref[CUT_G:CUT_G + 2 * NH, :], x) + _gate_bias_rows(bg_ref, nb)
    ig_r = gates[0:NH]
    inter_r = jax.nn.log_sigmoid(gates[NH:2 * NH]) + m_ref[...]
    m_t_r = jnp.maximum(inter_r, ig_r)
    dw_r = jnp.exp(ig_r - m_t_r)
    iw_r = jnp.exp(inter_r - m_t_r)
    floor_r = jnp.exp(-m_t_r)
    mout_ref[...] = m_t_r

    def as_column(row):
        return jnp.broadcast_to(row, (DH, nb)).T

    for g, w in enumerate(POOL_WINDOWS):
        lo = g * POOL_GC
        u_g = u[:, lo:lo + POOL_GC]
        acc = u_g
        for r in range(POOL_PAD - w, POOL_BUF):
            acc = acc + sp_ref[r, :, lo:lo + POOL_GC]
        pooled = acc / float(min(PAST_LEN + 1, w)) - u_g
        ypool_ref[:, lo:lo + POOL_GC] = _dot(pooled, wp_ref[g]) * ps_ref[:, lo:lo + POOL_GC]
    pool_ref[0:POOL_BUF - 1] = sp_ref[1:POOL_BUF]
    pool_ref[POOL_BUF - 1] = u

    for h in range(NH):
        col = slice(h * DH, (h + 1) * DH)
        dw = as_column(dw_r[h:h + 1])
        iw = as_column(iw_r[h:h + 1])
        q = proj[:, D_POOL + h * DH:D_POOL + (h + 1) * DH]
        k = proj[:, CUT_K + h * DH:CUT_K + (h + 1) * DH] * K_SCALE
        v = proj[:, CUT_V + h * DH:CUT_V + (h + 1) * DH]
        o = proj[:, CUT_V + D_MLSTM + h * DH:CUT_V + D_MLSTM + (h + 1) * DH]
        n_old = n_ref[:, h, :]
        s = jnp.sum(q * k, axis=-1, keepdims=True) * dw
        q_ref[:, col] = q
        kd_ref[:, col] = dw * k
        v_ref[:, col] = v
        iw_ref[:, col] = iw
        sv_ref[:, col] = s * v
        den_ref[:, col] = s + iw * jnp.sum(q * n_old, axis=-1, keepdims=True)
        floor_ref[:, col] = as_column(floor_r[h:h + 1])
        osig_ref[:, col] = jax.nn.sigmoid(o)
        nout_ref[:, h, :] = iw * n_old + dw * k


def _sample_proj(x, w_in_t, b_gate2d, pool_rows, n_state, m_rows, w_pool, pool_scale):
    nb = x.shape[0]
    wide = jax.ShapeDtypeStruct((nb, D_MLSTM), F32)
    vmem = pl.BlockSpec(memory_space=pltpu.VMEM)
    return pl.pallas_call(
        _sample_proj_kernel,
        name="sample_proj",
        in_specs=[vmem, vmem, pl.BlockSpec(memory_space=pltpu.SMEM), vmem, vmem, vmem, vmem, vmem],
        out_shape=([wide, jax.ShapeDtypeStruct(pool_rows.shape, F32)] + [wide] * 8
                   + [jax.ShapeDtypeStruct(n_state.shape, F32), jax.ShapeDtypeStruct(m_rows.shape, F32)]),
        compiler_params=pltpu.CompilerParams(vmem_limit_bytes=V7X_VMEM_LIMIT),
    )(x, w_in_t, b_gate2d, pool_rows, n_state, m_rows, w_pool, pool_scale)


def kernel(x_prompt, x_sample, state_pool, state_C, state_n, state_m, w_in, b_gate, w_pool, pool_scale,
           w_out, ln1_g, ln1_b, w_ff1, b_ff1, w_ff2, b_ff2, ln2_g, ln2_b):
    assert w_in.shape[0] == DEPTH == 1
    bp, t_len, _ = x_prompt.shape
    bs = x_sample.shape[0]
    assert x_sample.shape[1] == 1

    w_t = jnp.transpose(w_in[0])
    wp = w_pool[0]
    ps = pool_scale[0].reshape(1, D_POOL)
    wo = w_out[0]
    g1, b1 = ln1_g[0].reshape(1, D_MODEL), ln1_b[0].reshape(1, D_MODEL)
    g2, b2 = ln2_g[0].reshape(1, D_MODEL), ln2_b[0].reshape(1, D_MODEL)
    wf1, wf2 = w_ff1[0], w_ff2[0]
    bf1, bf2 = b_ff1[0].reshape(1, D_FF), b_ff2[0].reshape(1, D_MODEL)

    (ypool, pool_s, q_s, kd_s, v_s, iw_s, sv_s, den_s, floor_s, osig_s, n_s, m_s_rows) = _sample_proj(
        x_sample, w_t, b_gate, jnp.transpose(state_pool[0], (1, 0, 2)),
        state_n[0], jnp.transpose(state_m[0]), wp, ps)

    x1_p, pool_p, c_p, n_p, m_rep, c_s, inter_blk = _prompt_mixer(
        x_prompt, w_t, b_gate, wp, ps, wo, g1, b1, state_C, q_s, kd_s, v_s, iw_s)
    m_p = m_rep[:, :, 0].reshape(DEPTH, bp, NH)

    sample_rows = (ypool, sv_s, iw_s, inter_blk.reshape(bs, D_MLSTM), den_s, floor_s, osig_s)
    y_p, y_s = _ffn(x1_p.reshape(bp * t_len, D_MODEL), wf1, bf1, wf2, bf2, g2, b2,
                    x_sample, sample_rows, wo, g1, b1)

    return (y_p.reshape(bp, t_len, D_MODEL), y_s,
            pool_p, c_p, n_p, m_p,
            jnp.transpose(pool_s, (1, 0, 2))[None], c_s,
            n_s[None], jnp.transpose(m_s_rows)[None])
```

```python
import functools

import jax
import jax.numpy as jnp
from jax import lax
from jax.experimental import pallas as pl
from jax.experimental.pallas import tpu as pltpu

F32 = jnp.float32
BF16 = jnp.bfloat16

D_MODEL = 1024
D_POOL = D_MODEL // 2
D_MLSTM = D_MODEL - D_POOL
POOL_WINDOWS = (2, 4, 8, 16)
POOL_GC = D_POOL // len(POOL_WINDOWS)
POOL_BUF = max(POOL_WINDOWS) - 1
POOL_PAD = POOL_BUF + 1
NH = 4
DH = D_MLSTM // NH
D_FF = 4 * D_MODEL
DEPTH = 1
PAST_LEN = 16384
ALPHA = (2.0 * DEPTH) ** 0.25
LN_EPS = 1e-5
K_SCALE = DH ** -0.5

MLSTM_CHUNK = 128
MIX_TILE = 512
FFN_TILE = 512
FF_CHUNK = 2048
SAMPLE_SEQS = 4
PROJ_COLS = 256
LANES = 128
assert POOL_GC == LANES and DH == LANES
CUT_K = D_POOL + D_MLSTM
CUT_V = CUT_K + D_MLSTM
CUT_G = D_POOL + 4 * D_MLSTM
LB_Q = D_POOL // LANES
LB_K = CUT_K // LANES
LB_V = CUT_V // LANES
LB_O = LB_V + NH
LB_G = CUT_G // LANES
N_LB = LB_G + 1
V7X_VMEM_LIMIT = 62 * 1024 * 1024


def _layer_norm(y, g, b):
    mu = jnp.mean(y, axis=-1, keepdims=True)
    yc = y - mu
    var = jnp.mean(yc * yc, axis=-1, keepdims=True)
    return yc * lax.rsqrt(var + LN_EPS) * g + b


def _dot(a, b):
    return jnp.dot(a, b, preferred_element_type=F32)


def _dot_nt(a, b):
    return lax.dot_general(a, b, (((1,), (1,)), ((), ())), preferred_element_type=F32)


def _twice(row):
    return jnp.concatenate([row, row], axis=-1)


def _gate_bias_rows(bg_ref, width):
    row = lax.broadcasted_iota(jnp.int32, (2 * NH, width), 0)
    bias = jnp.zeros((2 * NH, width), F32)
    for g in range(2 * NH):
        bias = jnp.where(row == g, bg_ref[0, g], bias)
    return bias


def _mixer_kernel(xn_ref, wt_ref, bg_ref, wp_ref, ps_ref, wo_ref, g1_ref, b1_ref,
                  sc_ref, sq_ref, skd_ref, sv_ref, siw_ref,
                  x1_ref, pool_ref, c_ref, n_ref, m_ref, scout_ref, sinter_ref,
                  win_ref, wob_ref, xb_ref, pm_a, pm_b, ext_ref, mix_a, mix_b, res_a, res_b, caug_ref,
                  lhs_ref, ktw_ref, mst_ref, gbias_ref, *, tiles_per_seq, n_tiles):
    tq = MIX_TILE
    L = MLSTM_CHUNK
    step = pl.program_id(0)
    t_idx = lax.rem(step + tiles_per_seq - 1, tiles_per_seq)
    last_t = tiles_per_seq - 1
    parity = lax.rem(step, 2)

    def project(pm_next):
        xb_ref[...] = xn_ref[0].astype(BF16)

        def piece(c0, c1):
            res = _dot(xb_ref[...], win_ref[:, c0:c1])
            for i in range((c1 - c0) // LANES):
                pm_next[c0 // LANES + i] = res[:, i * LANES:(i + 1) * LANES]

        n_cols = N_LB * LANES
        return [functools.partial(piece, c0, min(c0 + PROJ_COLS, n_cols))
                for c0 in range(0, n_cols, PROJ_COLS)]

    @pl.when(step == 0)
    def _first_step():
        for j in range(CUT_G // PROJ_COLS):
            win_ref[:, j * PROJ_COLS:(j + 1) * PROJ_COLS] = (
                wt_ref[j * PROJ_COLS:(j + 1) * PROJ_COLS, :].T.astype(BF16))
        gate_rows = jnp.concatenate([wt_ref[CUT_G:CUT_G + 2 * NH, :],
                                     jnp.zeros((LANES - 2 * NH, D_MODEL), F32)], axis=0)
        win_ref[:, CUT_G:CUT_G + LANES] = gate_rows.T.astype(BF16)
        wob_ref[...] = wo_ref[...].astype(BF16)
        gbias_ref[...] = _gate_bias_rows(bg_ref, L)
        mix_b[...] = jnp.zeros(mix_b.shape, BF16)
        res_b[...] = jnp.zeros(res_b.shape, F32)
        res_a[...] = ALPHA * xn_ref[0]
        for piece in project(pm_a):
            piece()

    @pl.when(t_idx == 0)
    def _init():
        ext_ref[:, 0:POOL_PAD, :] = jnp.zeros((len(POOL_WINDOWS), POOL_PAD, POOL_GC), F32)
        caug_ref[...] = jnp.zeros(caug_ref.shape, F32)
        mst_ref[...] = jnp.zeros(mst_ref.shape, F32)

    tt = lax.broadcasted_iota(jnp.int32, (L, L), 0)
    ss = lax.broadcasted_iota(jnp.int32, (L, L), 1)
    causal = ss <= tt
    diag = ss == tt

    def finish(mix_prev, res):
        x1_ref[0] = _layer_norm(res[...] + _dot(mix_prev[...], wob_ref[...]), g1_ref[...], b1_ref[...])

    def run(pm_next, pm_cur, mix_cur, mix_prev, res, sample_half):
        finish(mix_prev, res)
        pieces = project(pm_next)
        n_chunks = tq // L
        n_slots = len(POOL_WINDOWS) + 2 * n_chunks * NH
        slot_of = [(k * n_slots) // len(pieces) for k in range(len(pieces))]
        slot = [0]

        def next_slot():
            for k, piece in enumerate(pieces):
                if slot_of[k] == slot[0]:
                    piece()
            slot[0] += 1

        pos = t_idx * tq + lax.broadcasted_iota(jnp.int32, (tq, POOL_GC), 0)
        for g, w in enumerate(POOL_WINDOWS):
            lo = g * POOL_GC
            u_g = pm_cur[g]
            ext_ref[g, POOL_PAD:POOL_PAD + tq, :] = u_g
            s = ext_ref[g]
            k = 1
            while k < w:
                s = s + pltpu.roll(s, k, axis=0)
                k *= 2
            cnt = jnp.minimum(pos + 1, w).astype(F32)
            pooled = s[POOL_PAD:, :] / cnt - u_g
            mix_cur[:, lo:lo + POOL_GC] = (_dot(pooled, wp_ref[g]) * ps_ref[:, lo:lo + POOL_GC]).astype(BF16)
            ext_ref[g, 0:POOL_PAD, :] = ext_ref[g, tq:tq + POOL_PAD, :]
            next_slot()

        lane = lax.broadcasted_iota(jnp.int32, (NH, L), 1)
        gate_bias = gbias_ref[...]
        m_prev = mst_ref[...]
        chunk_rows = []
        for c in range(n_chunks):
            rows = slice(c * L, (c + 1) * L)
            gates = pm_cur[LB_G, rows, :].T[0:2 * NH, :] + gate_bias
            lf = jax.nn.log_sigmoid(gates[NH:2 * NH])
            b = lf
            k = 1
            while k < L:
                b = b + jnp.where(lane >= k, pltpu.roll(b, k, axis=1), 0.0)
                k *= 2
            a = gates[0:NH] - b
            cmax = a
            k = 1
            while k < L:
                cmax = jnp.maximum(cmax, jnp.where(lane >= k, pltpu.roll(cmax, k, axis=1), -jnp.inf))
                k *= 2
            amax = jnp.max(a, axis=-1, keepdims=True)
            big_m = jnp.maximum(cmax, m_prev)
            mm = jnp.maximum(m_prev, amax)
            chunk_rows.append(dict(
                a=a, big_m=big_m,
                iw=jnp.exp(m_prev - big_m),
                floor=jnp.exp(-(b + big_m)),
                w_loc=jnp.exp(a - amax),
                g_state=jnp.exp(m_prev - mm),
                f_state=jnp.exp(amax - mm)))
            m_prev = jnp.sum(lf, axis=-1, keepdims=True) + mm
        mst_ref[...] = m_prev

        def as_column(row):
            return jnp.broadcast_to(row, (L, L)).T

        pair = (step - 1) // 2
        half_rows = slice(sample_half * SAMPLE_SEQS, (sample_half + 1) * SAMPLE_SEQS)
        pad = jnp.zeros((DH - SAMPLE_SEQS, DH), F32)

        def step_rows(ref):
            rows8 = ref[pl.ds(pl.multiple_of(pair * 2 * SAMPLE_SEQS, 2 * SAMPLE_SEQS), 2 * SAMPLE_SEQS), :]
            return rows8[half_rows]

        s_q, s_kd, s_v, s_iw = (step_rows(r) for r in (sq_ref, skd_ref, sv_ref, siw_ref))

        def sample_item(h, j):
            col = slice(h * DH, (h + 1) * DH)
            c_old = sc_ref[0, j, h]
            row = sample_half * SAMPLE_SEQS + j
            sinter_ref[0, row:row + 1, col] = jnp.sum(sample_q_t[h][:, j:j + 1] * c_old, axis=0, keepdims=True)
            scout_ref[0, j, h] = s_iw[j:j + 1, col] * c_old + sample_kd_t[h][:, j:j + 1] * s_v[j:j + 1, col]

        sample_q_t = [jnp.concatenate([s_q[:, h * DH:(h + 1) * DH], pad], axis=0).T for h in range(NH)]
        sample_kd_t = [jnp.concatenate([s_kd[:, h * DH:(h + 1) * DH], pad], axis=0).T for h in range(NH)]
        sample_items = [functools.partial(sample_item, h, j) for h in range(NH) for j in range(SAMPLE_SEQS)]

        for c in range(n_chunks):
            rows = slice(c * L, (c + 1) * L)
            cr = chunk_rows[c]
            for h in range(NH):
                i = c * NH + h
                q = pm_cur[LB_Q + h, rows, :]
                qb = q.astype(BF16)
                kt = pm_cur[LB_K + h, rows, :].T * K_SCALE
                p = jnp.exp(jnp.where(causal, cr["a"][h:h + 1], -jnp.inf) - as_column(cr["big_m"][h:h + 1]))
                lhs_ref[i, :, 0:L] = (_dot(qb, kt.astype(BF16)) * p).astype(BF16)
                ktw_ref[i] = (kt * cr["w_loc"][h:h + 1]).astype(BF16)
                lhs_ref[i, :, L:L + DH] = (q * as_column(cr["iw"][h:h + 1])).astype(BF16)
                if i % 2 == 0:
                    sample_items[i // 2]()
                next_slot()

        ones_blk = jnp.ones((L, DH), BF16)
        for c in range(n_chunks):
            rows = slice(c * L, (c + 1) * L)
            cr = chunk_rows[c]
            for h in range(NH):
                i = c * NH + h
                caug = caug_ref[h]
                v_aug = jnp.concatenate([pm_cur[LB_V + h, rows, :].astype(BF16), ones_blk], axis=1)
                comb = _dot(lhs_ref[i], jnp.concatenate([v_aug, caug.astype(BF16)], axis=0))
                hh = comb[:, :DH] / jnp.maximum(jnp.abs(comb[:, DH:]), as_column(cr["floor"][h:h + 1]))
                o = pm_cur[LB_O + h, rows, :]
                mix_cur[rows, D_POOL + h * DH:D_POOL + (h + 1) * DH] = (
                    jax.nn.sigmoid(o) * hh).astype(BF16)
                caug_ref[h] = (_twice(cr["g_state"][h:h + 1]) * caug
                               + _twice(cr["f_state"][h:h + 1]) * _dot(ktw_ref[i], v_aug))
                if i % 2 == 0:
                    sample_items[n_chunks * NH // 2 + i // 2]()
                next_slot()

        res[...] = ALPHA * xn_ref[0]

    @pl.when((parity == 0) & (step > 0) & (step <= n_tiles))
    def _even():
        run(pm_a, pm_b, mix_b, mix_a, res_a, 1)

    @pl.when((parity == 1) & (step <= n_tiles))
    def _odd():
        run(pm_b, pm_a, mix_a, mix_b, res_b, 0)

    @pl.when(step == n_tiles + 1)
    def _tail():
        finish(*((mix_b, res_b) if n_tiles % 2 == 0 else (mix_a, res_a)))

    @pl.when((t_idx == last_t) & (step > 0))
    def _final_state():
        for g in range(len(POOL_WINDOWS)):
            pool_ref[0, 0, :, g * POOL_GC:(g + 1) * POOL_GC] = ext_ref[g, tq + 1:tq + POOL_PAD, :]
        for h in range(NH):
            caug = caug_ref[h]
            c_ref[0, 0, h] = caug[:, :DH]
            n_ref[0, 0, h:h + 1, :] = jnp.sum(jnp.where(diag, caug[:, DH:], 0.0),
                                              axis=0, keepdims=True)
        m_ref[0] = mst_ref[...]


def _prompt_mixer(x, w_in_t, b_gate2d, w_pool, pool_scale, w_out, ln_g, ln_b, state_c, sq, skd, sv, siw):
    bsz, t_len, _ = x.shape
    tq = MIX_TILE
    assert t_len % tq == 0 and tq % MLSTM_CHUNK == 0 and tq >= POOL_PAD
    nt = t_len // tq
    n_tiles = bsz * nt
    n_seq = sq.shape[0]
    assert n_seq == n_tiles * SAMPLE_SEQS and n_tiles % 2 == 0
    assert SAMPLE_SEQS * NH == (tq // MLSTM_CHUNK) * NH
    n_items = (tq // MLSTM_CHUNK) * NH
    const2 = lambda i: (0, 0)
    nxt = lambda i: jnp.minimum(i, n_tiles - 1)
    cur = lambda i: jnp.clip(i - 1, 0, n_tiles - 1)
    fin = lambda i: jnp.clip(i - 2, 0, n_tiles - 1)
    return pl.pallas_call(
        functools.partial(_mixer_kernel, tiles_per_seq=nt, n_tiles=n_tiles),
        name="prompt_mixer",
        grid=(n_tiles + 2,),
        in_specs=[
            pl.BlockSpec((1, tq, D_MODEL), lambda i: (nxt(i) // nt, nxt(i) % nt, 0)),
            pl.BlockSpec(w_in_t.shape, const2, pipeline_mode=pl.Buffered(1)),
            pl.BlockSpec(memory_space=pltpu.SMEM),
            pl.BlockSpec(w_pool.shape, lambda i: (0, 0, 0)),
            pl.BlockSpec(pool_scale.shape, const2),
            pl.BlockSpec(w_out.shape, const2, pipeline_mode=pl.Buffered(1)),
            pl.BlockSpec(ln_g.shape, const2),
            pl.BlockSpec(ln_b.shape, const2),
            pl.BlockSpec((1, SAMPLE_SEQS, NH, DH, DH), lambda i: (0, cur(i), 0, 0, 0)),
            pl.BlockSpec(sq.shape, const2, pipeline_mode=pl.Buffered(1)),
            pl.BlockSpec(skd.shape, const2, pipeline_mode=pl.Buffered(1)),
            pl.BlockSpec(sv.shape, const2, pipeline_mode=pl.Buffered(1)),
            pl.BlockSpec(siw.shape, const2, pipeline_mode=pl.Buffered(1)),
        ],
        out_specs=[
            pl.BlockSpec((1, tq, D_MODEL), lambda i: (fin(i) // nt, fin(i) % nt, 0)),
            pl.BlockSpec((1, 1, POOL_BUF, D_POOL), lambda i: (0, cur(i) // nt, 0, 0)),
            pl.BlockSpec((1, 1, NH, DH, DH), lambda i: (0, cur(i) // nt, 0, 0, 0)),
            pl.BlockSpec((1, 1, NH, DH), lambda i: (0, cur(i) // nt, 0, 0)),
            pl.BlockSpec((1, NH, MLSTM_CHUNK), lambda i: (cur(i) // nt, 0, 0)),
            pl.BlockSpec((1, SAMPLE_SEQS, NH, DH, DH), lambda i: (0, cur(i), 0, 0, 0)),
            pl.BlockSpec((1, 2 * SAMPLE_SEQS, D_MLSTM), lambda i: (cur(i) // 2, 0, 0)),
        ],
        out_shape=[
            jax.ShapeDtypeStruct((bsz, t_len, D_MODEL), F32),
            jax.ShapeDtypeStruct((DEPTH, bsz, POOL_BUF, D_POOL), F32),
            jax.ShapeDtypeStruct((DEPTH, bsz, NH, DH, DH), F32),
            jax.ShapeDtypeStruct((DEPTH, bsz, NH, DH), F32),
            jax.ShapeDtypeStruct((bsz, NH, MLSTM_CHUNK), F32),
            jax.ShapeDtypeStruct(state_c.shape, F32),
            jax.ShapeDtypeStruct((n_tiles // 2, 2 * SAMPLE_SEQS, D_MLSTM), F32),
        ],
        scratch_shapes=[
            pltpu.VMEM((D_MODEL, N_LB * LANES), BF16),
            pltpu.VMEM((D_MODEL, D_MODEL), BF16),
            pltpu.VMEM((tq, D_MODEL), BF16),
            pltpu.VMEM((N_LB, tq, LANES), F32),
            pltpu.VMEM((N_LB, tq, LANES), F32),
            pltpu.VMEM((len(POOL_WINDOWS), POOL_PAD + tq, POOL_GC), F32),
            pltpu.VMEM((tq, D_MODEL), BF16),
            pltpu.VMEM((tq, D_MODEL), BF16),
            pltpu.VMEM((tq, D_MODEL), F32),
            pltpu.VMEM((tq, D_MODEL), F32),
            pltpu.VMEM((NH, DH, 2 * DH), F32),
            pltpu.VMEM((n_items, MLSTM_CHUNK, MLSTM_CHUNK + DH), BF16),
            pltpu.VMEM((n_items, DH, MLSTM_CHUNK), BF16),
            pltpu.VMEM((NH, MLSTM_CHUNK), F32),
            pltpu.VMEM((2 * NH, MLSTM_CHUNK), F32),
        ],
        compiler_params=pltpu.CompilerParams(
            dimension_semantics=("arbitrary",),
            vmem_limit_bytes=V7X_VMEM_LIMIT),
    )(x, w_in_t, b_gate2d, w_pool, pool_scale, w_out, ln_g, ln_b, state_c, sq, skd, sv, siw)


def _ffn_kernel(x1p_ref, w1_ref, b1_ref, w2_ref, b2_ref, g_ref, be_ref,
                xs_ref, ypool_ref, sv_ref, siw_ref, sinter_ref, den_ref, floor_ref, osig_ref,
                wo_ref, g1_ref, bb1_ref,
                yp_ref, ys_ref, pre_ref, x1s_ref, *, n_prompt_tiles):
    def residual_plus_mlp(x1_ref):
        acc = None
        for c in range(D_FF // FF_CHUNK):
            cols = slice(c * FF_CHUNK, (c + 1) * FF_CHUNK)
            hid = jnp.maximum(_dot(x1_ref[...], w1_ref[:, cols]) + b1_ref[:, cols], 0.0)
            part = _dot(hid * hid, w2_ref[cols, :])
            acc = part if acc is None else acc + part
        return ALPHA * x1_ref[...] + (acc + b2_ref[...])

    step = pl.program_id(0)

    @pl.when(step == 0)
    def _no_tile_yet():
        pre_ref[...] = jnp.zeros(pre_ref.shape, F32)
        hh = (sv_ref[...] + siw_ref[...] * sinter_ref[...]) / jnp.maximum(jnp.abs(den_ref[...]), floor_ref[...])
        mixin = jnp.concatenate([ypool_ref[...], osig_ref[...] * hh], axis=-1)
        x1s_ref[...] = _layer_norm(ALPHA * xs_ref[:, 0, :] + _dot(mixin, wo_ref[...]),
                                   g1_ref[...], bb1_ref[...])

    @pl.when(step < n_prompt_tiles)
    def _prompt():
        yp_ref[...] = _layer_norm(pre_ref[...], g_ref[...], be_ref[...])
        pre_ref[...] = residual_plus_mlp(x1p_ref)

    @pl.when(step == n_prompt_tiles)
    def _sample():
        yp_ref[...] = _layer_norm(pre_ref[...], g_ref[...], be_ref[...])
        ys_ref[:, 0, :] = _layer_norm(residual_plus_mlp(x1s_ref), g_ref[...], be_ref[...])


def _ffn(x1p, w1, b1, w2, b2, ln_g, ln_b, xs, sample_rows, w_out, ln1_g, ln1_b):
    tile = FFN_TILE
    n_tok = x1p.shape[0]
    assert n_tok % tile == 0
    n_tiles = n_tok // tile
    n_seq = xs.shape[0]
    const2 = lambda i: (0, 0)
    ptile = lambda i: (jnp.minimum(i, n_tiles - 1), 0)
    once = lambda shape: pl.BlockSpec(shape, lambda i: (0,) * len(shape), pipeline_mode=pl.Buffered(1))
    return pl.pallas_call(
        functools.partial(_ffn_kernel, n_prompt_tiles=n_tiles),
        name="ffn_ln2",
        grid=(n_tiles + 1,),
        in_specs=[
            pl.BlockSpec((tile, D_MODEL), ptile),
            pl.BlockSpec(w1.shape, const2, pipeline_mode=pl.Buffered(1)),
            pl.BlockSpec(b1.shape, const2),
            pl.BlockSpec(w2.shape, const2, pipeline_mode=pl.Buffered(1)),
            pl.BlockSpec(b2.shape, const2),
            pl.BlockSpec(ln_g.shape, const2),
            pl.BlockSpec(ln_b.shape, const2),
            once(xs.shape),
        ] + [once(r.shape) for r in sample_rows] + [
            once(w_out.shape),
            pl.BlockSpec(ln1_g.shape, const2),
            pl.BlockSpec(ln1_b.shape, const2),
        ],
        out_specs=[pl.BlockSpec((tile, D_MODEL), lambda i: (jnp.maximum(i - 1, 0), 0)),
                   pl.BlockSpec((n_seq, 1, D_MODEL), lambda i: (0, 0, 0))],
        out_shape=[jax.ShapeDtypeStruct((n_tok, D_MODEL), F32),
                   jax.ShapeDtypeStruct((n_seq, 1, D_MODEL), F32)],
        scratch_shapes=[pltpu.VMEM((tile, D_MODEL), F32),
                        pltpu.VMEM((n_seq, D_MODEL), F32)],
        compiler_params=pltpu.CompilerParams(
            dimension_semantics=("arbitrary",),
            vmem_limit_bytes=V7X_VMEM_LIMIT),
    )(x1p, w1, b1, w2, b2, ln_g, ln_b, xs, *sample_rows, w_out, ln1_g, ln1_b)


def _sample_proj_kernel(x_ref, wt_ref, bg_ref, sp_ref, n_ref, m_ref, wp_ref, ps_ref,
                        ypool_ref, pool_ref, q_ref, kd_ref, v_ref, iw_ref, sv_ref, den_ref,
                        floor_ref, osig_ref, nout_ref, mout_ref):
    x = x_ref[:, 0, :]
    nb = x.shape[0]
    proj = _dot_nt(x, wt_ref[0:CUT_G, :])
    u = proj[:, 0:D_POOL]
    gates = _dot_nt(wt_ref[CUT_G:CUT_G + 2 * NH, :], x) + _gate_bias_rows(bg_ref, nb)
    ig_r = gates[0:NH]
    inter_r = jax.nn.log_sigmoid(gates[NH:2 * NH]) + m_ref[...]
    m_t_r = jnp.maximum(inter_r, ig_r)
    dw_r = jnp.exp(ig_r - m_t_r)
    iw_r = jnp.exp(inter_r - m_t_r)
    floor_r = jnp.exp(-m_t_r)
    mout_ref[...] = m_t_r

    def as_column(row):
        return jnp.broadcast_to(row, (DH, nb)).T

    for g, w in enumerate(POOL_WINDOWS):
        lo = g * POOL_GC
        u_g = u[:, lo:lo + POOL_GC]
        acc = u_g
        for r in range(POOL_PAD - w, POOL_BUF):
            acc = acc + sp_ref[r, :, lo:lo + POOL_GC]
        pooled = acc / float(min(PAST_LEN + 1, w)) - u_g
        ypool_ref[:, lo:lo + POOL_GC] = _dot(pooled, wp_ref[g]) * ps_ref[:, lo:lo + POOL_GC]
    pool_ref[0:POOL_BUF - 1] = sp_ref[1:POOL_BUF]
    pool_ref[POOL_BUF - 1] = u

    for h in range(NH):
        col = slice(h * DH, (h + 1) * DH)
        dw = as_column(dw_r[h:h + 1])
        iw = as_column(iw_r[h:h + 1])
        q = proj[:, D_POOL + h * DH:D_POOL + (h + 1) * DH]
        k = proj[:, CUT_K + h * DH:CUT_K + (h + 1) * DH] * K_SCALE
        v = proj[:, CUT_V + h * DH:CUT_V + (h + 1) * DH]
        o = proj[:, CUT_V + D_MLSTM + h * DH:CUT_V + D_MLSTM + (h + 1) * DH]
        n_old = n_ref[:, h, :]
        s = jnp.sum(q * k, axis=-1, keepdims=True) * dw
        q_ref[:, col] = q
        kd_ref[:, col] = dw * k
        v_ref[:, col] = v
        iw_ref[:, col] = iw
        sv_ref[:, col] = s * v
        den_ref[:, col] = s + iw * jnp.sum(q * n_old, axis=-1, keepdims=True)
        floor_ref[:, col] = as_column(floor_r[h:h + 1])
        osig_ref[:, col] = jax.nn.sigmoid(o)
        nout_ref[:, h, :] = iw * n_old + dw * k


def _sample_proj(x, w_in_t, b_gate2d, pool_rows, n_state, m_rows, w_pool, pool_scale):
    nb = x.shape[0]
    wide = jax.ShapeDtypeStruct((nb, D_MLSTM), F32)
    vmem = pl.BlockSpec(memory_space=pltpu.VMEM)
    return pl.pallas_call(
        _sample_proj_kernel,
        name="sample_proj",
        in_specs=[vmem, vmem, pl.BlockSpec(memory_space=pltpu.SMEM), vmem, vmem, vmem, vmem, vmem],
        out_shape=([wide, jax.ShapeDtypeStruct(pool_rows.shape, F32)] + [wide] * 8
                   + [jax.ShapeDtypeStruct(n_state.shape, F32), jax.ShapeDtypeStruct(m_rows.shape, F32)]),
        compiler_params=pltpu.CompilerParams(vmem_limit_bytes=V7X_VMEM_LIMIT),
    )(x, w_in_t, b_gate2d, pool_rows, n_state, m_rows, w_pool, pool_scale)


def kernel(x_prompt, x_sample, state_pool, state_C, state_n, state_m, w_in, b_gate, w_pool, pool_scale,
           w_out, ln1_g, ln1_b, w_ff1, b_ff1, w_ff2, b_ff2, ln2_g, ln2_b):
    assert w_in.shape[0] == DEPTH == 1
    bp, t_len, _ = x_prompt.shape
    bs = x_sample.shape[0]
    assert x_sample.shape[1] == 1

    w_t = jnp.transpose(w_in[0])
    wp = w_pool[0]
    ps = pool_scale[0].reshape(1, D_POOL)
    wo = w_out[0]
    g1, b1 = ln1_g[0].reshape(1, D_MODEL), ln1_b[0].reshape(1, D_MODEL)
    g2, b2 = ln2_g[0].reshape(1, D_MODEL), ln2_b[0].reshape(1, D_MODEL)
    wf1, wf2 = w_ff1[0], w_ff2[0]
    bf1, bf2 = b_ff1[0].reshape(1, D_FF), b_ff2[0].reshape(1, D_MODEL)

    (ypool, pool_s, q_s, kd_s, v_s, iw_s, sv_s, den_s, floor_s, osig_s, n_s, m_s_rows) = _sample_proj(
        x_sample, w_t, b_gate, jnp.transpose(state_pool[0], (1, 0, 2)),
        state_n[0], jnp.transpose(state_m[0]), wp, ps)

    x1_p, pool_p, c_p, n_p, m_rep, c_s, inter_blk = _prompt_mixer(
        x_prompt, w_t, b_gate, wp, ps, wo, g1, b1, state_C, q_s, kd_s, v_s, iw_s)
    m_p = m_rep[:, :, 0].reshape(DEPTH, bp, NH)

    sample_rows = (ypool, sv_s, iw_s, inter_blk.reshape(bs, D_MLSTM), den_s, floor_s, osig_s)
    y_p, y_s = _ffn(x1_p.reshape(bp * t_len, D_MODEL), wf1, bf1, wf2, bf2, g2, b2,
                    x_sample, sample_rows, wo, g1, b1)

    return (y_p.reshape(bp, t_len, D_MODEL), y_s,
            pool_p, c_p, n_p, m_p,
            jnp.transpose(pool_s, (1, 0, 2))[None], c_s,
            n_s[None], jnp.transpose(m_s_rows)[None])
```

```python
import functools

import jax
import jax.numpy as jnp
from jax import lax
from jax.experimental import pallas as pl
from jax.experimental.pallas import tpu as pltpu

F32 = jnp.float32
BF16 = jnp.bfloat16

D_MODEL = 1024
D_POOL = D_MODEL // 2
D_MLSTM = D_MODEL - D_POOL
POOL_WINDOWS = (2, 4, 8, 16)
POOL_GC = D_POOL // len(POOL_WINDOWS)
POOL_BUF = max(POOL_WINDOWS) - 1
POOL_PAD = POOL_BUF + 1
NH = 4
DH = D_MLSTM // NH
D_FF = 4 * D_MODEL
DEPTH = 1
PAST_LEN = 16384
ALPHA = (2.0 * DEPTH) ** 0.25
LN_EPS = 1e-5
K_SCALE = DH ** -0.5

MLSTM_CHUNK = 128
MIX_TILE = 512
FFN_TILE = 512
FF_CHUNK = 2048
SAMPLE_SEQS = 4
PROJ_COLS = 256
LANES = 128
assert POOL_GC == LANES and DH == LANES
CUT_K = D_POOL + D_MLSTM
CUT_V = CUT_K + D_MLSTM
CUT_G = D_POOL + 4 * D_MLSTM
LB_Q = D_POOL // LANES
LB_K = CUT_K // LANES
LB_V = CUT_V // LANES
LB_O = LB_V + NH
LB_G = CUT_G // LANES
N_LB = LB_G + 1
V7X_VMEM_LIMIT = 62 * 1024 * 1024


def _layer_norm(y, g, b):
    mu = jnp.mean(y, axis=-1, keepdims=True)
    yc = y - mu
    var = jnp.mean(yc * yc, axis=-1, keepdims=True)
    return yc * lax.rsqrt(var + LN_EPS) * g + b


def _dot(a, b):
    return jnp.dot(a, b, preferred_element_type=F32)


def _dot_nt(a, b):
    return lax.dot_general(a, b, (((1,), (1,)), ((), ())), preferred_element_type=F32)


def _twice(row):
    return jnp.concatenate([row, row], axis=-1)


def _gate_bias_rows(bg_ref, width):
    row = lax.broadcasted_iota(jnp.int32, (2 * NH, width), 0)
    bias = jnp.zeros((2 * NH, width), F32)
    for g in range(2 * NH):
        bias = jnp.where(row == g, bg_ref[0, g], bias)
    return bias


def _mixer_kernel(xn_ref, wt_ref, bg_ref, wp_ref, ps_ref, wo_ref, g1_ref, b1_ref,
                  sc_ref, sq_ref, skd_ref, sv_ref, siw_ref,
                  x1_ref, pool_ref, c_ref, n_ref, m_ref, scout_ref, sinter_ref,
                  win_ref, wob_ref, xb_ref, pm_a, pm_b, ext_ref, mix_a, mix_b, res_a, res_b, caug_ref,
                  lhs_ref, ktw_ref, mst_ref, gbias_ref, *, tiles_per_seq, n_tiles):
    tq = MIX_TILE
    L = MLSTM_CHUNK
    step = pl.program_id(0)
    t_idx = lax.rem(step + tiles_per_seq - 1, tiles_per_seq)
    last_t = tiles_per_seq - 1
    parity = lax.rem(step, 2)

    def project(pm_next):
        xb_ref[...] = xn_ref[0].astype(BF16)

        def piece(c0, c1):
            res = _dot(xb_ref[...], win_ref[:, c0:c1])
            for i in range((c1 - c0) // LANES):
                pm_next[c0 // LANES + i] = res[:, i * LANES:(i + 1) * LANES]

        n_cols = N_LB * LANES
        return [functools.partial(piece, c0, min(c0 + PROJ_COLS, n_cols))
                for c0 in range(0, n_cols, PROJ_COLS)]

    @pl.when(step == 0)
    def _first_step():
        for j in range(CUT_G // PROJ_COLS):
            win_ref[:, j * PROJ_COLS:(j + 1) * PROJ_COLS] = (
                wt_ref[j * PROJ_COLS:(j + 1) * PROJ_COLS, :].T.astype(BF16))
        gate_rows = jnp.concatenate([wt_ref[CUT_G:CUT_G + 2 * NH, :],
                                     jnp.zeros((LANES - 2 * NH, D_MODEL), F32)], axis=0)
        win_ref[:, CUT_G:CUT_G + LANES] = gate_rows.T.astype(BF16)
        wob_ref[...] = wo_ref[...].astype(BF16)
        gbias_ref[...] = _gate_bias_rows(bg_ref, L)
        mix_b[...] = jnp.zeros(mix_b.shape, BF16)
        res_b[...] = jnp.zeros(res_b.shape, F32)
        res_a[...] = ALPHA * xn_ref[0]
        for piece in project(pm_a):
            piece()

    @pl.when(t_idx == 0)
    def _init():
        ext_ref[:, 0:POOL_PAD, :] = jnp.zeros((len(POOL_WINDOWS), POOL_PAD, POOL_GC), F32)
        caug_ref[...] = jnp.zeros(caug_ref.shape, F32)
        mst_ref[...] = jnp.zeros(mst_ref.shape, F32)

    tt = lax.broadcasted_iota(jnp.int32, (L, L), 0)
    ss = lax.broadcasted_iota(jnp.int32, (L, L), 1)
    causal = ss <= tt
    diag = ss == tt

    def finish(mix_prev, res):
        x1_ref[0] = _layer_norm(res[...] + _dot(mix_prev[...], wob_ref[...]), g1_ref[...], b1_ref[...])

    def run(pm_next, pm_cur, mix_cur, mix_prev, res, sample_half):
        finish(mix_prev, res)
        pieces = project(pm_next)
        n_chunks = tq // L
        n_slots = len(POOL_WINDOWS) + 2 * n_chunks * NH
        slot_of = [(k * n_slots) // len(pieces) for k in range(len(pieces))]
        slot = [0]

        def next_slot():
            for k, piece in enumerate(pieces):
                if slot_of[k] == slot[0]:
                    piece()
            slot[0] += 1

        pos = t_idx * tq + lax.broadcasted_iota(jnp.int32, (tq, POOL_GC), 0)
        for g, w in enumerate(POOL_WINDOWS):
            lo = g * POOL_GC
            u_g = pm_cur[g]
            ext_ref[g, POOL_PAD:POOL_PAD + tq, :] = u_g
            s = ext_ref[g]
            k = 1
            while k < w:
                s = s + pltpu.roll(s, k, axis=0)
                k *= 2
            cnt = jnp.minimum(pos + 1, w).astype(F32)
            pooled = s[POOL_PAD:, :] / cnt - u_g
            y_g = _dot(pooled.astype(BF16), wp_ref[g].astype(BF16))
            mix_cur[:, lo:lo + POOL_GC] = (y_g * ps_ref[:, lo:lo + POOL_GC]).astype(BF16)
            ext_ref[g, 0:POOL_PAD, :] = ext_ref[g, tq:tq + POOL_PAD, :]
            next_slot()

        lane = lax.broadcasted_iota(jnp.int32, (NH, L), 1)
        gate_bias = gbias_ref[...]
        m_prev = mst_ref[...]
        chunk_rows = []
        for c in range(n_chunks):
            rows = slice(c * L, (c + 1) * L)
            gates = pm_cur[LB_G, rows, :].T[0:2 * NH, :] + gate_bias
            lf = jax.nn.log_sigmoid(gates[NH:2 * NH])
            b = lf
            k = 1
            while k < L:
                b = b + jnp.where(lane >= k, pltpu.roll(b, k, axis=1), 0.0)
                k *= 2
            a = gates[0:NH] - b
            cmax = a
            k = 1
            while k < L:
                cmax = jnp.maximum(cmax, jnp.where(lane >= k, pltpu.roll(cmax, k, axis=1), -jnp.inf))
                k *= 2
            amax = jnp.max(a, axis=-1, keepdims=True)
            big_m = jnp.maximum(cmax, m_prev)
            mm = jnp.maximum(m_prev, amax)
            chunk_rows.append(dict(
                a=a, big_m=big_m,
                iw=jnp.exp(m_prev - big_m),
                floor=jnp.exp(-(b + big_m)),
                w_loc=jnp.exp(a - amax),
                g_state=jnp.exp(m_prev - mm),
                f_state=jnp.exp(amax - mm)))
            m_prev = jnp.sum(lf, axis=-1, keepdims=True) + mm
        mst_ref[...] = m_prev

        def as_column(row):
            return jnp.broadcast_to(row, (L, L)).T

        pair = (step - 1) // 2
        half_rows = slice(sample_half * SAMPLE_SEQS, (sample_half + 1) * SAMPLE_SEQS)
        pad = jnp.zeros((DH - SAMPLE_SEQS, DH), F32)

        def step_rows(ref):
            rows8 = ref[pl.ds(pl.multiple_of(pair * 2 * SAMPLE_SEQS, 2 * SAMPLE_SEQS), 2 * SAMPLE_SEQS), :]
            return rows8[half_rows]

        s_q, s_kd, s_v, s_iw = (step_rows(r) for r in (sq_ref, skd_ref, sv_ref, siw_ref))

        def sample_item(h, j):
            col = slice(h * DH, (h + 1) * DH)
            c_old = sc_ref[0, j, h]
            row = sample_half * SAMPLE_SEQS + j
            sinter_ref[0, row:row + 1, col] = jnp.sum(sample_q_t[h][:, j:j + 1] * c_old, axis=0, keepdims=True)
            scout_ref[0, j, h] = s_iw[j:j + 1, col] * c_old + sample_kd_t[h][:, j:j + 1] * s_v[j:j + 1, col]

        sample_q_t = [jnp.concatenate([s_q[:, h * DH:(h + 1) * DH], pad], axis=0).T for h in range(NH)]
        sample_kd_t = [jnp.concatenate([s_kd[:, h * DH:(h + 1) * DH], pad], axis=0).T for h in range(NH)]
        sample_items = [functools.partial(sample_item, h, j) for h in range(NH) for j in range(SAMPLE_SEQS)]

        for c in range(n_chunks):
            rows = slice(c * L, (c + 1) * L)
            cr = chunk_rows[c]
            for h in range(NH):
                i = c * NH + h
                q = pm_cur[LB_Q + h, rows, :]
                qb = q.astype(BF16)
                kt = pm_cur[LB_K + h, rows, :].T * K_SCALE
                p = jnp.exp(jnp.where(causal, cr["a"][h:h + 1], -jnp.inf) - as_column(cr["big_m"][h:h + 1]))
                lhs_ref[i, :, 0:L] = (_dot(qb, kt.astype(BF16)) * p).astype(BF16)
                ktw_ref[i] = (kt * cr["w_loc"][h:h + 1]).astype(BF16)
                lhs_ref[i, :, L:L + DH] = (q * as_column(cr["iw"][h:h + 1])).astype(BF16)
                if i % 2 == 0:
                    sample_items[i // 2]()
                next_slot()

        ones_blk = jnp.ones((L, DH), BF16)
        for c in range(n_chunks):
            rows = slice(c * L, (c + 1) * L)
            cr = chunk_rows[c]
            for h in range(NH):
                i = c * NH + h
                caug = caug_ref[h]
                v_aug = jnp.concatenate([pm_cur[LB_V + h, rows, :].astype(BF16), ones_blk], axis=1)
                comb = _dot(lhs_ref[i], jnp.concatenate([v_aug, caug.astype(BF16)], axis=0))
                hh = comb[:, :DH] / jnp.maximum(jnp.abs(comb[:, DH:]), as_column(cr["floor"][h:h + 1]))
                o = pm_cur[LB_O + h, rows, :]
                mix_cur[rows, D_POOL + h * DH:D_POOL + (h + 1) * DH] = (
                    jax.nn.sigmoid(o) * hh).astype(BF16)
                caug_ref[h] = (_twice(cr["g_state"][h:h + 1]) * caug
                               + _twice(cr["f_state"][h:h + 1]) * _dot(ktw_ref[i], v_aug))
                if i % 2 == 0:
                    sample_items[n_chunks * NH // 2 + i // 2]()
                next_slot()

        res[...] = ALPHA * xn_ref[0]

    @pl.when((parity == 0) & (step > 0) & (step <= n_tiles))
    def _even():
        run(pm_a, pm_b, mix_b, mix_a, res_a, 1)

    @pl.when((parity == 1) & (step <= n_tiles))
    def _odd():
        run(pm_b, pm_a, mix_a, mix_b, res_b, 0)

    @pl.when(step == n_tiles + 1)
    def _tail():
        finish(*((mix_b, res_b) if n_tiles % 2 == 0 else (mix_a, res_a)))

    @pl.when((t_idx == last_t) & (step > 0))
    def _final_state():
        for g in range(len(POOL_WINDOWS)):
            pool_ref[0, 0, :, g * POOL_GC:(g + 1) * POOL_GC] = ext_ref[g, tq + 1:tq + POOL_PAD, :]
        for h in range(NH):
            caug = caug_ref[h]
            c_ref[0, 0, h] = caug[:, :DH]
            n_ref[0, 0, h:h + 1, :] = jnp.sum(jnp.where(diag, caug[:, DH:], 0.0),
                                              axis=0, keepdims=True)
        m_ref[0] = mst_ref[...]


def _prompt_mixer(x, w_in_t, b_gate2d, w_pool, pool_scale, w_out, ln_g, ln_b, state_c, sq, skd, sv, siw):
    bsz, t_len, _ = x.shape
    tq = MIX_TILE
    assert t_len % tq == 0 and tq % MLSTM_CHUNK == 0 and tq >= POOL_PAD
    nt = t_len // tq
    n_tiles = bsz * nt
    n_seq = sq.shape[0]
    assert n_seq == n_tiles * SAMPLE_SEQS and n_tiles % 2 == 0
    assert SAMPLE_SEQS * NH == (tq // MLSTM_CHUNK) * NH
    n_items = (tq // MLSTM_CHUNK) * NH
    const2 = lambda i: (0, 0)
    nxt = lambda i: jnp.minimum(i, n_tiles - 1)
    cur = lambda i: jnp.clip(i - 1, 0, n_tiles - 1)
    fin = lambda i: jnp.clip(i - 2, 0, n_tiles - 1)
    return pl.pallas_call(
        functools.partial(_mixer_kernel, tiles_per_seq=nt, n_tiles=n_tiles),
        name="prompt_mixer",
        grid=(n_tiles + 2,),
        in_specs=[
            pl.BlockSpec((1, tq, D_MODEL), lambda i: (nxt(i) // nt, nxt(i) % nt, 0)),
            pl.BlockSpec(w_in_t.shape, const2, pipeline_mode=pl.Buffered(1)),
            pl.BlockSpec(memory_space=pltpu.SMEM),
            pl.BlockSpec(w_pool.shape, lambda i: (0, 0, 0)),
            pl.BlockSpec(pool_scale.shape, const2),
            pl.BlockSpec(w_out.shape, const2, pipeline_mode=pl.Buffered(1)),
            pl.BlockSpec(ln_g.shape, const2),
            pl.BlockSpec(ln_b.shape, const2),
            pl.BlockSpec((1, SAMPLE_SEQS, NH, DH, DH), lambda i: (0, cur(i), 0, 0, 0)),
            pl.BlockSpec(sq.shape, const2, pipeline_mode=pl.Buffered(1)),
            pl.BlockSpec(skd.shape, const2, pipeline_mode=pl.Buffered(1)),
            pl.BlockSpec(sv.shape, const2, pipeline_mode=pl.Buffered(1)),
            pl.BlockSpec(siw.shape, const2, pipeline_mode=pl.Buffered(1)),
        ],
        out_specs=[
            pl.BlockSpec((1, tq, D_MODEL), lambda i: (fin(i) // nt, fin(i) % nt, 0)),
            pl.BlockSpec((1, 1, POOL_BUF, D_POOL), lambda i: (0, cur(i) // nt, 0, 0)),
            pl.BlockSpec((1, 1, NH, DH, DH), lambda i: (0, cur(i) // nt, 0, 0, 0)),
            pl.BlockSpec((1, 1, NH, DH), lambda i: (0, cur(i) // nt, 0, 0)),
            pl.BlockSpec((1, NH, MLSTM_CHUNK), lambda i: (cur(i) // nt, 0, 0)),
            pl.BlockSpec((1, SAMPLE_SEQS, NH, DH, DH), lambda i: (0, cur(i), 0, 0, 0)),
            pl.BlockSpec((1, 2 * SAMPLE_SEQS, D_MLSTM), lambda i: (cur(i) // 2, 0, 0)),
        ],
        out_shape=[
            jax.ShapeDtypeStruct((bsz, t_len, D_MODEL), F32),
            jax.ShapeDtypeStruct((DEPTH, bsz, POOL_BUF, D_POOL), F32),
            jax.ShapeDtypeStruct((DEPTH, bsz, NH, DH, DH), F32),
            jax.ShapeDtypeStruct((DEPTH, bsz, NH, DH), F32),
            jax.ShapeDtypeStruct((bsz, NH, MLSTM_CHUNK), F32),
            jax.ShapeDtypeStruct(state_c.shape, F32),
            jax.ShapeDtypeStruct((n_tiles // 2, 2 * SAMPLE_SEQS, D_MLSTM), F32),
        ],
        scratch_shapes=[
            pltpu.VMEM((D_MODEL, N_LB * LANES), BF16),
            pltpu.VMEM((D_MODEL, D_MODEL), BF16),
            pltpu.VMEM((tq, D_MODEL), BF16),
            pltpu.VMEM((N_LB, tq, LANES), F32),
            pltpu.VMEM((N_LB, tq, LANES), F32),
            pltpu.VMEM((len(POOL_WINDOWS), POOL_PAD + tq, POOL_GC), F32),
            pltpu.VMEM((tq, D_MODEL), BF16),
            pltpu.VMEM((tq, D_MODEL), BF16),
            pltpu.VMEM((tq, D_MODEL), F32),
            pltpu.VMEM((tq, D_MODEL), F32),
            pltpu.VMEM((NH, DH, 2 * DH), F32),
            pltpu.VMEM((n_items, MLSTM_CHUNK, MLSTM_CHUNK + DH), BF16),
            pltpu.VMEM((n_items, DH, MLSTM_CHUNK), BF16),
            pltpu.VMEM((NH, MLSTM_CHUNK), F32),
            pltpu.VMEM((2 * NH, MLSTM_CHUNK), F32),
        ],
        compiler_params=pltpu.CompilerParams(
            dimension_semantics=("arbitrary",),
            vmem_limit_bytes=V7X_VMEM_LIMIT),
    )(x, w_in_t, b_gate2d, w_pool, pool_scale, w_out, ln_g, ln_b, state_c, sq, skd, sv, siw)


def _ffn_kernel(x1p_ref, w1_ref, b1_ref, w2_ref, b2_ref, g_ref, be_ref,
                xs_ref, ypool_ref, sv_ref, siw_ref, sinter_ref, den_ref, floor_ref, osig_ref,
                wo_ref, g1_ref, bb1_ref,
                yp_ref, ys_ref, pre_ref, x1s_ref, *, n_prompt_tiles):
    def residual_plus_mlp(x1_ref):
        acc = None
        for c in range(D_FF // FF_CHUNK):
            cols = slice(c * FF_CHUNK, (c + 1) * FF_CHUNK)
            hid = jnp.maximum(_dot(x1_ref[...], w1_ref[:, cols]) + b1_ref[:, cols], 0.0)
            part = _dot(hid * hid, w2_ref[cols, :])
            acc = part if acc is None else acc + part
        return ALPHA * x1_ref[...] + (acc + b2_ref[...])

    step = pl.program_id(0)

    @pl.when(step == 0)
    def _no_tile_yet():
        pre_ref[...] = jnp.zeros(pre_ref.shape, F32)
        hh = (sv_ref[...] + siw_ref[...] * sinter_ref[...]) / jnp.maximum(jnp.abs(den_ref[...]), floor_ref[...])
        mixin = jnp.concatenate([ypool_ref[...], osig_ref[...] * hh], axis=-1)
        x1s_ref[...] = _layer_norm(ALPHA * xs_ref[:, 0, :] + _dot(mixin, wo_ref[...]),
                                   g1_ref[...], bb1_ref[...])

    @pl.when(step < n_prompt_tiles)
    def _prompt():
        yp_ref[...] = _layer_norm(pre_ref[...], g_ref[...], be_ref[...])
        pre_ref[...] = residual_plus_mlp(x1p_ref)

    @pl.when(step == n_prompt_tiles)
    def _sample():
        yp_ref[...] = _layer_norm(pre_ref[...], g_ref[...], be_ref[...])
        ys_ref[:, 0, :] = _layer_norm(residual_plus_mlp(x1s_ref), g_ref[...], be_ref[...])


def _ffn(x1p, w1, b1, w2, b2, ln_g, ln_b, xs, sample_rows, w_out, ln1_g, ln1_b):
    tile = FFN_TILE
    n_tok = x1p.shape[0]
    assert n_tok % tile == 0
    n_tiles = n_tok // tile
    n_seq = xs.shape[0]
    const2 = lambda i: (0, 0)
    ptile = lambda i: (jnp.minimum(i, n_tiles - 1), 0)
    once = lambda shape: pl.BlockSpec(shape, lambda i: (0,) * len(shape), pipeline_mode=pl.Buffered(1))
    return pl.pallas_call(
        functools.partial(_ffn_kernel, n_prompt_tiles=n_tiles),
        name="ffn_ln2",
        grid=(n_tiles + 1,),
        in_specs=[
            pl.BlockSpec((tile, D_MODEL), ptile),
            pl.BlockSpec(w1.shape, const2, pipeline_mode=pl.Buffered(1)),
            pl.BlockSpec(b1.shape, const2),
            pl.BlockSpec(w2.shape, const2, pipeline_mode=pl.Buffered(1)),
            pl.BlockSpec(b2.shape, const2),
            pl.BlockSpec(ln_g.shape, const2),
            pl.BlockSpec(ln_b.shape, const2),
            once(xs.shape),
        ] + [once(r.shape) for r in sample_rows] + [
            once(w_out.shape),
            pl.BlockSpec(ln1_g.shape, const2),
            pl.BlockSpec(ln1_b.shape, const2),
        ],
        out_specs=[pl.BlockSpec((tile, D_MODEL), lambda i: (jnp.maximum(i - 1, 0), 0)),
                   pl.BlockSpec((n_seq, 1, D_MODEL), lambda i: (0, 0, 0))],
        out_shape=[jax.ShapeDtypeStruct((n_tok, D_MODEL), F32),
                   jax.ShapeDtypeStruct((n_seq, 1, D_MODEL), F32)],
        scratch_shapes=[pltpu.VMEM((tile, D_MODEL), F32),
                        pltpu.VMEM((n_seq, D_MODEL), F32)],
        compiler_params=pltpu.CompilerParams(
            dimension_semantics=("arbitrary",),
            vmem_limit_bytes=V7X_VMEM_LIMIT),
    )(x1p, w1, b1, w2, b2, ln_g, ln_b, xs, *sample_rows, w_out, ln1_g, ln1_b)


def _sample_proj_kernel(x_ref, wt_ref, bg_ref, sp_ref, n_ref, m_ref, wp_ref, ps_ref,
                        ypool_ref, pool_ref, q_ref, kd_ref, v_ref, iw_ref, sv_ref, den_ref,
                        floor_ref, osig_ref, nout_ref, mout_ref):
    x = x_ref[:, 0, :]
    nb = x.shape[0]
    proj = _dot_nt(x, wt_ref[0:CUT_G, :])
    u = proj[:, 0:D_POOL]
    gates = _dot_nt(wt_ref[CUT_G:CUT_G + 2 * NH, :], x) + _gate_bias_rows(bg_ref, nb)
    ig_r = gates[0:NH]
    inter_r = jax.nn.log_sigmoid(gates[NH:2 * NH]) + m_ref[...]
    m_t_r = jnp.maximum(inter_r, ig_r)
    dw_r = jnp.exp(ig_r - m_t_r)
    iw_r = jnp.exp(inter_r - m_t_r)
    floor_r = jnp.exp(-m_t_r)
    mout_ref[...] = m_t_r

    def as_column(row):
        return jnp.broadcast_to(row, (DH, nb)).T

    for g, w in enumerate(POOL_WINDOWS):
        lo = g * POOL_GC
        u_g = u[:, lo:lo + POOL_GC]
        acc = u_g
        for r in range(POOL_PAD - w, POOL_BUF):
            acc = acc + sp_ref[r, :, lo:lo + POOL_GC]
        pooled = acc / float(min(PAST_LEN + 1, w)) - u_g
        ypool_ref[:, lo:lo + POOL_GC] = _dot(pooled, wp_ref[g]) * ps_ref[:, lo:lo + POOL_GC]
    pool_ref[0:POOL_BUF - 1] = sp_ref[1:POOL_BUF]
    pool_ref[POOL_BUF - 1] = u

    for h in range(NH):
        col = slice(h * DH, (h + 1) * DH)
        dw = as_column(dw_r[h:h + 1])
        iw = as_column(iw_r[h:h + 1])
        q = proj[:, D_POOL + h * DH:D_POOL + (h + 1) * DH]
        k = proj[:, CUT_K + h * DH:CUT_K + (h + 1) * DH] * K_SCALE
        v = proj[:, CUT_V + h * DH:CUT_V + (h + 1) * DH]
        o = proj[:, CUT_V + D_MLSTM + h * DH:CUT_V + D_MLSTM + (h + 1) * DH]
        n_old = n_ref[:, h, :]
        s = jnp.sum(q * k, axis=-1, keepdims=True) * dw
        q_ref[:, col] = q
        kd_ref[:, col] = dw * k
        v_ref[:, col] = v
        iw_ref[:, col] = iw
        sv_ref[:, col] = s * v
        den_ref[:, col] = s + iw * jnp.sum(q * n_old, axis=-1, keepdims=True)
        floor_ref[:, col] = as_column(floor_r[h:h + 1])
        osig_ref[:, col] = jax.nn.sigmoid(o)
        nout_ref[:, h, :] = iw * n_old + dw * k


def _sample_proj(x, w_in_t, b_gate2d, pool_rows, n_state, m_rows, w_pool, pool_scale):
    nb = x.shape[0]
    wide = jax.ShapeDtypeStruct((nb, D_MLSTM), F32)
    vmem = pl.BlockSpec(memory_space=pltpu.VMEM)
    return pl.pallas_call(
        _sample_proj_kernel,
        name="sample_proj",
        in_specs=[vmem, vmem, pl.BlockSpec(memory_space=pltpu.SMEM), vmem, vmem, vmem, vmem, vmem],
        out_shape=([wide, jax.ShapeDtypeStruct(pool_rows.shape, F32)] + [wide] * 8
                   + [jax.ShapeDtypeStruct(n_state.shape, F32), jax.ShapeDtypeStruct(m_rows.shape, F32)]),
        compiler_params=pltpu.CompilerParams(vmem_limit_bytes=V7X_VMEM_LIMIT),
    )(x, w_in_t, b_gate2d, pool_rows, n_state, m_rows, w_pool, pool_scale)


def kernel(x_prompt, x_sample, state_pool, state_C, state_n, state_m, w_in, b_gate, w_pool, pool_scale,
           w_out, ln1_g, ln1_b, w_ff1, b_ff1, w_ff2, b_ff2, ln2_g, ln2_b):
    assert w_in.shape[0] == DEPTH == 1
    bp, t_len, _ = x_prompt.shape
    bs = x_sample.shape[0]
    assert x_sample.shape[1] == 1

    w_t = jnp.transpose(w_in[0])
    wp = w_pool[0]
    ps = pool_scale[0].reshape(1, D_POOL)
    wo = w_out[0]
    g1, b1 = ln1_g[0].reshape(1, D_MODEL), ln1_b[0].reshape(1, D_MODEL)
    g2, b2 = ln2_g[0].reshape(1, D_MODEL), ln2_b[0].reshape(1, D_MODEL)
    wf1, wf2 = w_ff1[0], w_ff2[0]
    bf1, bf2 = b_ff1[0].reshape(1, D_FF), b_ff2[0].reshape(1, D_MODEL)

    (ypool, pool_s, q_s, kd_s, v_s, iw_s, sv_s, den_s, floor_s, osig_s, n_s, m_s_rows) = _sample_proj(
        x_sample, w_t, b_gate, jnp.transpose(state_pool[0], (1, 0, 2)),
        state_n[0], jnp.transpose(state_m[0]), wp, ps)

    x1_p, pool_p, c_p, n_p, m_rep, c_s, inter_blk = _prompt_mixer(
        x_prompt, w_t, b_gate, wp, ps, wo, g1, b1, state_C, q_s, kd_s, v_s, iw_s)
    m_p = m_rep[:, :, 0].reshape(DEPTH, bp, NH)

    sample_rows = (ypool, sv_s, iw_s, inter_blk.reshape(bs, D_MLSTM), den_s, floor_s, osig_s)
    y_p, y_s = _ffn(x1_p.reshape(bp * t_len, D_MODEL), wf1, bf1, wf2, bf2, g2, b2,
                    x_sample, sample_rows, wo, g1, b1)

    return (y_p.reshape(bp, t_len, D_MODEL), y_s,
            pool_p, c_p, n_p, m_p,
            jnp.transpose(pool_s, (1, 0, 2))[None], c_s,
            n_s[None], jnp.transpose(m_s_rows)[None])
```
